```python
import jax, jax.numpy as jnp
from jax import lax
import numpy as np

D_MODEL = 2048
BATCH = 8
SEQ = 2048
DEPTH = 2

MEM_LEN = 256
D_MIX = D_MODEL
DN_HEADS = 8
DN_HEAD_DIM = 128
DN_WIDTH = DN_HEADS * DN_HEAD_DIM
POOL_WINDOWS = (2, 4, 8, 16)
N_POOL_GROUPS = len(POOL_WINDOWS)
POOL_WIDTH = D_MIX - DN_WIDTH
POOL_GROUP_DIM = POOL_WIDTH // N_POOL_GROUPS
CONV_WIDTH = 4
CHUNK = 64
D_FF = ((8 * D_MODEL // 3 + 255) // 256) * 256
X_HEADS = 4
X_HEAD_DIM = D_MODEL // X_HEADS
EPS = 1e-6
IN_COLS = 4 * DN_WIDTH + 2 * DN_HEADS + POOL_WIDTH

kernel_name = 'hybrid_deltanet_pool_macaron'


def rmsnorm(x, g):
    xf = x.astype(jnp.float32)
    y = xf * lax.rsqrt(jnp.mean(xf * xf, axis=-1, keepdims=True) + EPS)
    return (y * g.astype(jnp.float32)).astype(x.dtype)


def l2norm(x):
    return x * lax.rsqrt(jnp.sum(x * x, axis=-1, keepdims=True) + EPS)


def swiglu(x, w_gate, w_up, w_down):
    return (jax.nn.silu(x @ w_gate) * (x @ w_up)) @ w_down


def causal_depthwise_conv(x, w):
    c = x.shape[-1]
    return lax.conv_general_dilated(
        x, w[:, None, :], window_strides=(1,), padding=[(CONV_WIDTH - 1, 0)],
        dimension_numbers=('NWC', 'WIO', 'NWC'), feature_group_count=c)


def gated_delta_rule(q, k, v, g, beta):
    b_, s_, h_, dk = q.shape
    dv = v.shape[-1]
    n_chunks = s_ // CHUNK

    def chunks(t):
        t = t.reshape((b_, n_chunks, CHUNK, h_) + t.shape[3:])
        return jnp.moveaxis(t, 3, 1)

    q, k, v, g, beta = chunks(q), chunks(k), chunks(v), chunks(g), chunks(beta)
    gc = jnp.cumsum(g, axis=-1)
    causal = jnp.tril(jnp.ones((CHUNK, CHUNK), dtype=bool))
    strict = jnp.tril(jnp.ones((CHUNK, CHUNK), dtype=bool), -1)
    decay_mat = jnp.exp(jnp.where(causal, gc[..., :, None] - gc[..., None, :], -jnp.inf))
    kb = k * beta[..., None]
    a_mat = jnp.where(strict, jnp.einsum('bhnid,bhnjd->bhnij', kb, k) * decay_mat, 0.0)
    rhs = jnp.concatenate([v * beta[..., None], kb * jnp.exp(gc)[..., None]], axis=-1)
    sol = lax.linalg.triangular_solve(a_mat, rhs, left_side=True, lower=True, unit_diagonal=True)
    u, w = sol[..., :dv], sol[..., dv:]
    qk = jnp.einsum('bhnid,bhnjd->bhnij', q, k) * decay_mat
    q_dec = q * jnp.exp(gc)[..., None]
    k_dec = k * jnp.exp(gc[..., -1:] - gc)[..., None]
    g_last = jnp.exp(gc[..., -1])

    def step(state, inp):
        u_i, w_i, qk_i, qd_i, kd_i, gl_i = inp
        v_new = u_i - jnp.einsum('bhck,bhkv->bhcv', w_i, state)
        o_i = jnp.einsum('bhck,bhkv->bhcv', qd_i, state) + jnp.einsum('bhcj,bhjv->bhcv', qk_i, v_new)
        state = state * gl_i[..., None, None] + jnp.einsum('bhck,bhcv->bhkv', kd_i, v_new)
        return state, o_i

    xs = tuple(jnp.moveaxis(t, 2, 0) for t in (u, w, qk, q_dec, k_dec, g_last))
    s0 = jnp.zeros((b_, h_, dk, dv), jnp.float32)
    _, o = lax.scan(step, s0, xs)
    return jnp.transpose(o, (1, 0, 3, 2, 4)).reshape(b_, s_, h_, dv)


def deltanet_branch(qkv, z, a, b, conv_w, a_log, dt_bias, out_norm_g):
    bsz, s_, _ = qkv.shape
    qkv = jax.nn.silu(causal_depthwise_conv(qkv, conv_w)).astype(jnp.float32)
    q, k, v = jnp.split(qkv, 3, axis=-1)
    shp = (bsz, s_, DN_HEADS, DN_HEAD_DIM)
    q = l2norm(q.reshape(shp)) * (DN_HEAD_DIM ** -0.5)
    k = l2norm(k.reshape(shp))
    v = v.reshape(shp)
    g = -jnp.exp(a_log.astype(jnp.float32)) * jax.nn.softplus(a.astype(jnp.float32) + dt_bias.astype(jnp.float32))
    beta = jax.nn.sigmoid(b.astype(jnp.float32))
    o = gated_delta_rule(q, k, v, g, beta)
    o = rmsnorm(o, out_norm_g) * jax.nn.silu(z.reshape(shp).astype(jnp.float32))
    return o.reshape(bsz, s_, DN_WIDTH).astype(z.dtype)


def pool_branch(p, pool_w, pool_scale):
    bsz, s_, _ = p.shape
    pf = p.astype(jnp.float32).reshape(bsz, s_, N_POOL_GROUPS, POOL_GROUP_DIM)
    csum = jnp.concatenate([jnp.zeros((bsz, 1, N_POOL_GROUPS, POOL_GROUP_DIM), jnp.float32),
                            jnp.cumsum(pf, axis=1)], axis=1)
    pos1 = jnp.arange(1, s_ + 1)
    outs = []
    for gi, win in enumerate(POOL_WINDOWS):
        c_g = csum[:, :, gi]
        lower = jnp.concatenate([jnp.zeros((bsz, win - 1, POOL_GROUP_DIM), jnp.float32),
                                 c_g[:, :s_ + 1 - win]], axis=1)
        count = jnp.minimum(pos1, win).astype(jnp.float32)[None, :, None]
        outs.append((c_g[:, 1:] - lower) / count)
    pooled = jnp.stack(outs, axis=2) - pf
    mixed = jnp.einsum('bsgc,gcd->bsgd', pooled.astype(p.dtype), pool_w)
    return mixed.reshape(bsz, s_, POOL_WIDTH) * pool_scale


def memory_cross_attention(h, mem_n, wq, wkv, wo):
    bsz, s_, _ = h.shape
    m_ = mem_n.shape[1]
    q = (h @ wq).reshape(bsz, s_, X_HEADS, X_HEAD_DIM)
    k, v = jnp.split(mem_n @ wkv, 2, axis=-1)
    k = k.reshape(bsz, m_, X_HEADS, X_HEAD_DIM)
    v = v.reshape(bsz, m_, X_HEADS, X_HEAD_DIM)
    s = jnp.einsum('bshd,bmhd->bhsm', q, k).astype(jnp.float32) * (X_HEAD_DIM ** -0.5)
    p = jax.nn.softmax(s, axis=-1).astype(v.dtype)
    o = jnp.einsum('bhsm,bmhd->bshd', p, v).reshape(bsz, s_, D_MODEL)
    return o @ wo


def _fwd_setup_inputs(seed: int = 0) -> dict:
    key = jax.random.key(seed)
    ks = jax.random.split(key, 26)
    f32 = jnp.float32

    def dense(k, fan_in, fan_out):
        return jax.random.normal(k, (DEPTH, fan_in, fan_out), f32) * fan_in ** -0.5

    def gain(k, n, lead=(DEPTH,)):
        return 1.0 + 0.01 * jax.random.normal(k, lead + (n,), f32)

    dt = jnp.exp(jax.random.uniform(ks[8], (DEPTH, DN_HEADS), f32, np.log(1e-3), np.log(1e-1)))
    return {
        'x': jax.random.normal(ks[0], (BATCH, SEQ, D_MODEL), f32),
        'mem': jax.random.normal(ks[1], (BATCH, MEM_LEN, D_MODEL), f32),
        'ffn1_norm': gain(ks[2], D_MODEL),
        'ffn1_w_gate': dense(ks[3], D_MODEL, D_FF),
        'ffn1_w_up': dense(ks[4], D_MODEL, D_FF),
        'ffn1_w_down': dense(ks[5], D_FF, D_MODEL),
        'mix_norm': gain(ks[6], D_MODEL),
        'w_in': dense(ks[7], D_MODEL, IN_COLS),
        'conv_w': jax.random.normal(ks[9], (DEPTH, CONV_WIDTH, 3 * DN_WIDTH), f32) * CONV_WIDTH ** -0.5,
        'a_log': jnp.log(jax.random.uniform(ks[10], (DEPTH, DN_HEADS), f32, 1.0, 16.0)),
        'dt_bias': dt + jnp.log(-jnp.expm1(-dt)),
        'dn_out_norm': gain(ks[11], DN_HEAD_DIM),
        'pool_w': jax.random.normal(ks[12], (DEPTH, N_POOL_GROUPS, POOL_GROUP_DIM, POOL_GROUP_DIM), f32) * POOL_GROUP_DIM ** -0.5,
        'pool_scale': 1.0 + 0.05 * jax.random.normal(ks[13], (DEPTH, POOL_WIDTH), f32),
        'w_out': dense(ks[14], D_MIX, D_MODEL),
        'xattn_norm': gain(ks[15], D_MODEL),
        'mem_norm': gain(ks[16], D_MODEL),
        'xattn_wq': dense(ks[17], D_MODEL, D_MODEL),
        'xattn_wkv': dense(ks[18], D_MODEL, 2 * D_MODEL),
        'xattn_wo': dense(ks[19], D_MODEL, D_MODEL),
        'ffn2_norm': gain(ks[20], D_MODEL),
        'ffn2_w_gate': dense(ks[21], D_MODEL, D_FF),
        'ffn2_w_up': dense(ks[22], D_MODEL, D_FF),
        'ffn2_w_down': dense(ks[23], D_FF, D_MODEL),
        'final_norm': gain(ks[24], D_MODEL, lead=()),
    }


def _fwd_reference(x, mem, ffn1_norm, ffn1_w_gate, ffn1_w_up, ffn1_w_down, mix_norm, w_in, conv_w,
              a_log, dt_bias, dn_out_norm, pool_w, pool_scale, w_out, xattn_norm, mem_norm,
              xattn_wq, xattn_wkv, xattn_wo, ffn2_norm, ffn2_w_gate, ffn2_w_up, ffn2_w_down,
              final_norm):
    o_q = 0
    o_z = 3 * DN_WIDTH
    o_a = 4 * DN_WIDTH
    o_b = o_a + DN_HEADS
    o_p = o_b + DN_HEADS
    for l in range(DEPTH):
        h = x + 0.5 * swiglu(rmsnorm(x, ffn1_norm[l]), ffn1_w_gate[l], ffn1_w_up[l], ffn1_w_down[l])
        proj = rmsnorm(h, mix_norm[l]) @ w_in[l]
        y_dn = deltanet_branch(proj[..., o_q:o_z], proj[..., o_z:o_a], proj[..., o_a:o_b],
                               proj[..., o_b:o_p], conv_w[l], a_log[l], dt_bias[l], dn_out_norm[l])
        y_pool = pool_branch(proj[..., o_p:], pool_w[l], pool_scale[l])
        h = h + jnp.concatenate([y_dn, y_pool], axis=-1) @ w_out[l]
        h = h + memory_cross_attention(rmsnorm(h, xattn_norm[l]), rmsnorm(mem, mem_norm[l]),
                                       xattn_wq[l], xattn_wkv[l], xattn_wo[l])
        x = h + 0.5 * swiglu(rmsnorm(h, ffn2_norm[l]), ffn2_w_gate[l], ffn2_w_up[l], ffn2_w_down[l])
    return rmsnorm(x, final_norm)


import jax as _jax
import jax.numpy as _jnp

TWIN_FORMAT = 'train_step'
FWD_PARAMS = ['x', 'mem', 'ffn1_norm', 'ffn1_w_gate', 'ffn1_w_up', 'ffn1_w_down', 'mix_norm', 'w_in', 'conv_w', 'a_log', 'dt_bias', 'dn_out_norm', 'pool_w', 'pool_scale', 'w_out', 'xattn_norm', 'mem_norm', 'xattn_wq', 'xattn_wkv', 'xattn_wo', 'ffn2_norm', 'ffn2_w_gate', 'ffn2_w_up', 'ffn2_w_down', 'final_norm']
TWIN_WEIGHTS = ['ffn1_norm', 'ffn1_w_gate', 'ffn1_w_up', 'ffn1_w_down', 'mix_norm', 'w_in', 'conv_w', 'a_log', 'dt_bias', 'dn_out_norm', 'pool_w', 'pool_scale', 'w_out', 'xattn_norm', 'mem_norm', 'xattn_wq', 'xattn_wkv', 'xattn_wo', 'ffn2_norm', 'ffn2_w_gate', 'ffn2_w_up', 'ffn2_w_down', 'final_norm']
TWIN_DIFF_INPUT = 'x'
TWIN_INPUTS = ['x', 'mem', 'ffn1_norm', 'ffn1_w_gate', 'ffn1_w_up', 'ffn1_w_down', 'mix_norm', 'w_in', 'conv_w', 'a_log', 'dt_bias', 'dn_out_norm', 'pool_w', 'pool_scale', 'w_out', 'xattn_norm', 'mem_norm', 'xattn_wq', 'xattn_wkv', 'xattn_wo', 'ffn2_norm', 'ffn2_w_gate', 'ffn2_w_up', 'ffn2_w_down', 'final_norm', 'loss_target', 'm_ffn1_norm', 'm_ffn1_w_gate', 'm_ffn1_w_up', 'm_ffn1_w_down', 'm_mix_norm', 'm_w_in', 'm_conv_w', 'm_a_log', 'm_dt_bias', 'm_dn_out_norm', 'm_pool_w', 'm_pool_scale', 'm_w_out', 'm_xattn_norm', 'm_mem_norm', 'm_xattn_wq', 'm_xattn_wkv', 'm_xattn_wo', 'm_ffn2_norm', 'm_ffn2_w_gate', 'm_ffn2_w_up', 'm_ffn2_w_down', 'm_final_norm', 'v_ffn1_norm', 'v_ffn1_w_gate', 'v_ffn1_w_up', 'v_ffn1_w_down', 'v_mix_norm', 'v_w_in', 'v_conv_w', 'v_a_log', 'v_dt_bias', 'v_dn_out_norm', 'v_pool_w', 'v_pool_scale', 'v_w_out', 'v_xattn_norm', 'v_mem_norm', 'v_xattn_wq', 'v_xattn_wkv', 'v_xattn_wo', 'v_ffn2_norm', 'v_ffn2_w_gate', 'v_ffn2_w_up', 'v_ffn2_w_down', 'v_final_norm']
TWIN_OUTPUTS = ['loss', 'grad_x', 'grad_ffn1_norm', 'grad_ffn1_w_gate', 'grad_ffn1_w_up', 'grad_ffn1_w_down', 'grad_mix_norm', 'grad_w_in', 'grad_conv_w', 'grad_a_log', 'grad_dt_bias', 'grad_dn_out_norm', 'grad_pool_w', 'grad_pool_scale', 'grad_w_out', 'grad_xattn_norm', 'grad_mem_norm', 'grad_xattn_wq', 'grad_xattn_wkv', 'grad_xattn_wo', 'grad_ffn2_norm', 'grad_ffn2_w_gate', 'grad_ffn2_w_up', 'grad_ffn2_w_down', 'grad_final_norm', 'delta_ffn1_norm', 'delta_ffn1_w_gate', 'delta_ffn1_w_up', 'delta_ffn1_w_down', 'delta_mix_norm', 'delta_w_in', 'delta_conv_w', 'delta_a_log', 'delta_dt_bias', 'delta_dn_out_norm', 'delta_pool_w', 'delta_pool_scale', 'delta_w_out', 'delta_xattn_norm', 'delta_mem_norm', 'delta_xattn_wq', 'delta_xattn_wkv', 'delta_xattn_wo', 'delta_ffn2_norm', 'delta_ffn2_w_gate', 'delta_ffn2_w_up', 'delta_ffn2_w_down', 'delta_final_norm', 'new_m_ffn1_norm', 'new_m_ffn1_w_gate', 'new_m_ffn1_w_up', 'new_m_ffn1_w_down', 'new_m_mix_norm', 'new_m_w_in', 'new_m_conv_w', 'new_m_a_log', 'new_m_dt_bias', 'new_m_dn_out_norm', 'new_m_pool_w', 'new_m_pool_scale', 'new_m_w_out', 'new_m_xattn_norm', 'new_m_mem_norm', 'new_m_xattn_wq', 'new_m_xattn_wkv', 'new_m_xattn_wo', 'new_m_ffn2_norm', 'new_m_ffn2_w_gate', 'new_m_ffn2_w_up', 'new_m_ffn2_w_down', 'new_m_final_norm', 'new_v_ffn1_norm', 'new_v_ffn1_w_gate', 'new_v_ffn1_w_up', 'new_v_ffn1_w_down', 'new_v_mix_norm', 'new_v_w_in', 'new_v_conv_w', 'new_v_a_log', 'new_v_dt_bias', 'new_v_dn_out_norm', 'new_v_pool_w', 'new_v_pool_scale', 'new_v_w_out', 'new_v_xattn_norm', 'new_v_mem_norm', 'new_v_xattn_wq', 'new_v_xattn_wkv', 'new_v_xattn_wo', 'new_v_ffn2_norm', 'new_v_ffn2_w_gate', 'new_v_ffn2_w_up', 'new_v_ffn2_w_down', 'new_v_final_norm']
TWIN_LEAF_KINDS = {'loss': 'loss', 'grad_x': 'grad_x', 'grad_ffn1_norm': 'grad_w', 'grad_ffn1_w_gate': 'grad_w', 'grad_ffn1_w_up': 'grad_w', 'grad_ffn1_w_down': 'grad_w', 'grad_mix_norm': 'grad_w', 'grad_w_in': 'grad_w', 'grad_conv_w': 'grad_w', 'grad_a_log': 'grad_w', 'grad_dt_bias': 'grad_w', 'grad_dn_out_norm': 'grad_w', 'grad_pool_w': 'grad_w', 'grad_pool_scale': 'grad_w', 'grad_w_out': 'grad_w', 'grad_xattn_norm': 'grad_w', 'grad_mem_norm': 'grad_w', 'grad_xattn_wq': 'grad_w', 'grad_xattn_wkv': 'grad_w', 'grad_xattn_wo': 'grad_w', 'grad_ffn2_norm': 'grad_w', 'grad_ffn2_w_gate': 'grad_w', 'grad_ffn2_w_up': 'grad_w', 'grad_ffn2_w_down': 'grad_w', 'grad_final_norm': 'grad_w', 'delta_ffn1_norm': 'delta_w', 'delta_ffn1_w_gate': 'delta_w', 'delta_ffn1_w_up': 'delta_w', 'delta_ffn1_w_down': 'delta_w', 'delta_mix_norm': 'delta_w', 'delta_w_in': 'delta_w', 'delta_conv_w': 'delta_w', 'delta_a_log': 'delta_w', 'delta_dt_bias': 'delta_w', 'delta_dn_out_norm': 'delta_w', 'delta_pool_w': 'delta_w', 'delta_pool_scale': 'delta_w', 'delta_w_out': 'delta_w', 'delta_xattn_norm': 'delta_w', 'delta_mem_norm': 'delta_w', 'delta_xattn_wq': 'delta_w', 'delta_xattn_wkv': 'delta_w', 'delta_xattn_wo': 'delta_w', 'delta_ffn2_norm': 'delta_w', 'delta_ffn2_w_gate': 'delta_w', 'delta_ffn2_w_up': 'delta_w', 'delta_ffn2_w_down': 'delta_w', 'delta_final_norm': 'delta_w', 'new_m_ffn1_norm': 'new_m', 'new_m_ffn1_w_gate': 'new_m', 'new_m_ffn1_w_up': 'new_m', 'new_m_ffn1_w_down': 'new_m', 'new_m_mix_norm': 'new_m', 'new_m_w_in': 'new_m', 'new_m_conv_w': 'new_m', 'new_m_a_log': 'new_m', 'new_m_dt_bias': 'new_m', 'new_m_dn_out_norm': 'new_m', 'new_m_pool_w': 'new_m', 'new_m_pool_scale': 'new_m', 'new_m_w_out': 'new_m', 'new_m_xattn_norm': 'new_m', 'new_m_mem_norm': 'new_m', 'new_m_xattn_wq': 'new_m', 'new_m_xattn_wkv': 'new_m', 'new_m_xattn_wo': 'new_m', 'new_m_ffn2_norm': 'new_m', 'new_m_ffn2_w_gate': 'new_m', 'new_m_ffn2_w_up': 'new_m', 'new_m_ffn2_w_down': 'new_m', 'new_m_final_norm': 'new_m', 'new_v_ffn1_norm': 'new_v', 'new_v_ffn1_w_gate': 'new_v', 'new_v_ffn1_w_up': 'new_v', 'new_v_ffn1_w_down': 'new_v', 'new_v_mix_norm': 'new_v', 'new_v_w_in': 'new_v', 'new_v_conv_w': 'new_v', 'new_v_a_log': 'new_v', 'new_v_dt_bias': 'new_v', 'new_v_dn_out_norm': 'new_v', 'new_v_pool_w': 'new_v', 'new_v_pool_scale': 'new_v', 'new_v_w_out': 'new_v', 'new_v_xattn_norm': 'new_v', 'new_v_mem_norm': 'new_v', 'new_v_xattn_wq': 'new_v', 'new_v_xattn_wkv': 'new_v', 'new_v_xattn_wo': 'new_v', 'new_v_ffn2_norm': 'new_v', 'new_v_ffn2_w_gate': 'new_v', 'new_v_ffn2_w_up': 'new_v', 'new_v_ffn2_w_down': 'new_v', 'new_v_final_norm': 'new_v'}


def _forward(args):
    return _fwd_reference(*[args[k] for k in FWD_PARAMS])


def _output_shape():
    out = _jax.eval_shape(lambda: _forward(_fwd_setup_inputs(0)))
    return out.shape, out.dtype

N_MICROBATCH = 1
ADAM_LR = 0.001
ADAM_B1 = 0.9
ADAM_B2 = 0.999
ADAM_EPS = 1e-08
ADAM_WD = 0.01
ADAM_STEP = 10
PER_EXAMPLE_BATCH_AXIS = {'x': 0, 'mem': 0, 'loss_target': 0}
SHARED_INPUTS = []
_WEIGHT_DTYPES = {'ffn1_norm': _jnp.float32, 'ffn1_w_gate': _jnp.float32, 'ffn1_w_up': _jnp.float32, 'ffn1_w_down': _jnp.float32, 'mix_norm': _jnp.float32, 'w_in': _jnp.float32, 'conv_w': _jnp.float32, 'a_log': _jnp.float32, 'dt_bias': _jnp.float32, 'dn_out_norm': _jnp.float32, 'pool_w': _jnp.float32, 'pool_scale': _jnp.float32, 'w_out': _jnp.float32, 'xattn_norm': _jnp.float32, 'mem_norm': _jnp.float32, 'xattn_wq': _jnp.float32, 'xattn_wkv': _jnp.float32, 'xattn_wo': _jnp.float32, 'ffn2_norm': _jnp.float32, 'ffn2_w_gate': _jnp.float32, 'ffn2_w_up': _jnp.float32, 'ffn2_w_down': _jnp.float32, 'final_norm': _jnp.float32}
MOMENT_SCALE = {'ffn1_norm': 2.794667e-02, 'ffn1_w_gate': 1.182163e-02, 'ffn1_w_up': 1.146349e-02, 'ffn1_w_down': 1.898133e-02, 'mix_norm': 4.653948e-02, 'w_in': 2.891103e-02, 'conv_w': 2.243561e-02, 'a_log': 7.593799e-02, 'dt_bias': 7.383775e-02, 'dn_out_norm': 8.051583e-02, 'pool_w': 4.320232e-02, 'pool_scale': 4.348804e-02, 'w_out': 3.650511e-02, 'xattn_norm': 5.077615e-03, 'mem_norm': 7.295736e-03, 'xattn_wq': 4.994082e-03, 'xattn_wkv': 5.026337e-03, 'xattn_wo': 5.042535e-03, 'ffn2_norm': 2.019818e-02, 'ffn2_w_gate': 8.674861e-03, 'ffn2_w_up': 8.406597e-03, 'ffn2_w_down': 1.392402e-02, 'final_norm': 7.999691e+00}


def _to_microbatches(a, axis):
    t = _jnp.moveaxis(a, axis, 0)
    t = t.reshape((N_MICROBATCH, t.shape[0] // N_MICROBATCH) + t.shape[1:])
    return _jnp.moveaxis(t, 1, axis + 1)


def setup_inputs(seed: int = 0) -> dict:
    inp = _fwd_setup_inputs(seed)
    key = _jax.random.fold_in(_jax.random.key(seed), 7919)
    shape, _ = _output_shape()
    out = dict(inp)
    out["loss_target"] = _jax.random.normal(_jax.random.fold_in(key, 0), shape, _jnp.float32)
    for i, name in enumerate(TWIN_WEIGHTS):
        w = inp[name].astype(_jnp.float32)
        if MOMENT_SCALE is None:
            s = _jnp.sqrt(_jnp.mean(_jnp.square(w)) + 1e-30)
        else:
            s = MOMENT_SCALE[name]
        km, kv = _jax.random.split(_jax.random.fold_in(key, i + 1))
        out[name] = w
        out["m_" + name] = s * _jax.random.normal(km, w.shape, _jnp.float32)
        out["v_" + name] = (s * s) * _jax.random.uniform(kv, w.shape, _jnp.float32, 0.5, 1.5)
    if N_MICROBATCH > 1:
        for name, axis in PER_EXAMPLE_BATCH_AXIS.items():
            out[name] = _to_microbatches(out[name], axis)
    return {'x': out['x'], 'mem': out['mem'], 'ffn1_norm': out['ffn1_norm'], 'ffn1_w_gate': out['ffn1_w_gate'], 'ffn1_w_up': out['ffn1_w_up'], 'ffn1_w_down': out['ffn1_w_down'], 'mix_norm': out['mix_norm'], 'w_in': out['w_in'], 'conv_w': out['conv_w'], 'a_log': out['a_log'], 'dt_bias': out['dt_bias'], 'dn_out_norm': out['dn_out_norm'], 'pool_w': out['pool_w'], 'pool_scale': out['pool_scale'], 'w_out': out['w_out'], 'xattn_norm': out['xattn_norm'], 'mem_norm': out['mem_norm'], 'xattn_wq': out['xattn_wq'], 'xattn_wkv': out['xattn_wkv'], 'xattn_wo': out['xattn_wo'], 'ffn2_norm': out['ffn2_norm'], 'ffn2_w_gate': out['ffn2_w_gate'], 'ffn2_w_up': out['ffn2_w_up'], 'ffn2_w_down': out['ffn2_w_down'], 'final_norm': out['final_norm'], 'loss_target': out['loss_target'], 'm_ffn1_norm': out['m_ffn1_norm'], 'm_ffn1_w_gate': out['m_ffn1_w_gate'], 'm_ffn1_w_up': out['m_ffn1_w_up'], 'm_ffn1_w_down': out['m_ffn1_w_down'], 'm_mix_norm': out['m_mix_norm'], 'm_w_in': out['m_w_in'], 'm_conv_w': out['m_conv_w'], 'm_a_log': out['m_a_log'], 'm_dt_bias': out['m_dt_bias'], 'm_dn_out_norm': out['m_dn_out_norm'], 'm_pool_w': out['m_pool_w'], 'm_pool_scale': out['m_pool_scale'], 'm_w_out': out['m_w_out'], 'm_xattn_norm': out['m_xattn_norm'], 'm_mem_norm': out['m_mem_norm'], 'm_xattn_wq': out['m_xattn_wq'], 'm_xattn_wkv': out['m_xattn_wkv'], 'm_xattn_wo': out['m_xattn_wo'], 'm_ffn2_norm': out['m_ffn2_norm'], 'm_ffn2_w_gate': out['m_ffn2_w_gate'], 'm_ffn2_w_up': out['m_ffn2_w_up'], 'm_ffn2_w_down': out['m_ffn2_w_down'], 'm_final_norm': out['m_final_norm'], 'v_ffn1_norm': out['v_ffn1_norm'], 'v_ffn1_w_gate': out['v_ffn1_w_gate'], 'v_ffn1_w_up': out['v_ffn1_w_up'], 'v_ffn1_w_down': out['v_ffn1_w_down'], 'v_mix_norm': out['v_mix_norm'], 'v_w_in': out['v_w_in'], 'v_conv_w': out['v_conv_w'], 'v_a_log': out['v_a_log'], 'v_dt_bias': out['v_dt_bias'], 'v_dn_out_norm': out['v_dn_out_norm'], 'v_pool_w': out['v_pool_w'], 'v_pool_scale': out['v_pool_scale'], 'v_w_out': out['v_w_out'], 'v_xattn_norm': out['v_xattn_norm'], 'v_mem_norm': out['v_mem_norm'], 'v_xattn_wq': out['v_xattn_wq'], 'v_xattn_wkv': out['v_xattn_wkv'], 'v_xattn_wo': out['v_xattn_wo'], 'v_ffn2_norm': out['v_ffn2_norm'], 'v_ffn2_w_gate': out['v_ffn2_w_gate'], 'v_ffn2_w_up': out['v_ffn2_w_up'], 'v_ffn2_w_down': out['v_ffn2_w_down'], 'v_final_norm': out['v_final_norm']}


def _loss(weights, diff, rest, loss_target):
    with _jax.named_scope("forward"):
        args = {**rest, TWIN_DIFF_INPUT: diff, **{k: w.astype(_WEIGHT_DTYPES[k]) for k, w in weights.items()}}
        y = _forward(args)
    with _jax.named_scope("loss_head"):
        err = _jnp.square(y.astype(_jnp.float32) - loss_target)
        return 0.5 * _jnp.sum(_jnp.mean(err, axis=-1)) if err.ndim else 0.5 * err


def _adamw(w, g, m, v):
    m = ADAM_B1 * m + (1.0 - ADAM_B1) * g
    v = ADAM_B2 * v + (1.0 - ADAM_B2) * _jnp.square(g)
    m_hat = m / (1.0 - ADAM_B1 ** ADAM_STEP)
    v_hat = v / (1.0 - ADAM_B2 ** ADAM_STEP)
    delta = -ADAM_LR * (m_hat / (_jnp.sqrt(v_hat) + ADAM_EPS) + ADAM_WD * w)
    return delta, m, v


def reference(x, mem, ffn1_norm, ffn1_w_gate, ffn1_w_up, ffn1_w_down, mix_norm, w_in, conv_w, a_log, dt_bias, dn_out_norm, pool_w, pool_scale, w_out, xattn_norm, mem_norm, xattn_wq, xattn_wkv, xattn_wo, ffn2_norm, ffn2_w_gate, ffn2_w_up, ffn2_w_down, final_norm, loss_target, m_ffn1_norm, m_ffn1_w_gate, m_ffn1_w_up, m_ffn1_w_down, m_mix_norm, m_w_in, m_conv_w, m_a_log, m_dt_bias, m_dn_out_norm, m_pool_w, m_pool_scale, m_w_out, m_xattn_norm, m_mem_norm, m_xattn_wq, m_xattn_wkv, m_xattn_wo, m_ffn2_norm, m_ffn2_w_gate, m_ffn2_w_up, m_ffn2_w_down, m_final_norm, v_ffn1_norm, v_ffn1_w_gate, v_ffn1_w_up, v_ffn1_w_down, v_mix_norm, v_w_in, v_conv_w, v_a_log, v_dt_bias, v_dn_out_norm, v_pool_w, v_pool_scale, v_w_out, v_xattn_norm, v_mem_norm, v_xattn_wq, v_xattn_wkv, v_xattn_wo, v_ffn2_norm, v_ffn2_w_gate, v_ffn2_w_up, v_ffn2_w_down, v_final_norm):
    given = dict(x=x, mem=mem, ffn1_norm=ffn1_norm, ffn1_w_gate=ffn1_w_gate, ffn1_w_up=ffn1_w_up, ffn1_w_down=ffn1_w_down, mix_norm=mix_norm, w_in=w_in, conv_w=conv_w, a_log=a_log, dt_bias=dt_bias, dn_out_norm=dn_out_norm, pool_w=pool_w, pool_scale=pool_scale, w_out=w_out, xattn_norm=xattn_norm, mem_norm=mem_norm, xattn_wq=xattn_wq, xattn_wkv=xattn_wkv, xattn_wo=xattn_wo, ffn2_norm=ffn2_norm, ffn2_w_gate=ffn2_w_gate, ffn2_w_up=ffn2_w_up, ffn2_w_down=ffn2_w_down, final_norm=final_norm, loss_target=loss_target, m_ffn1_norm=m_ffn1_norm, m_ffn1_w_gate=m_ffn1_w_gate, m_ffn1_w_up=m_ffn1_w_up, m_ffn1_w_down=m_ffn1_w_down, m_mix_norm=m_mix_norm, m_w_in=m_w_in, m_conv_w=m_conv_w, m_a_log=m_a_log, m_dt_bias=m_dt_bias, m_dn_out_norm=m_dn_out_norm, m_pool_w=m_pool_w, m_pool_scale=m_pool_scale, m_w_out=m_w_out, m_xattn_norm=m_xattn_norm, m_mem_norm=m_mem_norm, m_xattn_wq=m_xattn_wq, m_xattn_wkv=m_xattn_wkv, m_xattn_wo=m_xattn_wo, m_ffn2_norm=m_ffn2_norm, m_ffn2_w_gate=m_ffn2_w_gate, m_ffn2_w_up=m_ffn2_w_up, m_ffn2_w_down=m_ffn2_w_down, m_final_norm=m_final_norm, v_ffn1_norm=v_ffn1_norm, v_ffn1_w_gate=v_ffn1_w_gate, v_ffn1_w_up=v_ffn1_w_up, v_ffn1_w_down=v_ffn1_w_down, v_mix_norm=v_mix_norm, v_w_in=v_w_in, v_conv_w=v_conv_w, v_a_log=v_a_log, v_dt_bias=v_dt_bias, v_dn_out_norm=v_dn_out_norm, v_pool_w=v_pool_w, v_pool_scale=v_pool_scale, v_w_out=v_w_out, v_xattn_norm=v_xattn_norm, v_mem_norm=v_mem_norm, v_xattn_wq=v_xattn_wq, v_xattn_wkv=v_xattn_wkv, v_xattn_wo=v_xattn_wo, v_ffn2_norm=v_ffn2_norm, v_ffn2_w_gate=v_ffn2_w_gate, v_ffn2_w_up=v_ffn2_w_up, v_ffn2_w_down=v_ffn2_w_down, v_final_norm=v_final_norm)
    weights = {n: given[n] for n in TWIN_WEIGHTS}
    shared = {n: given[n] for n in SHARED_INPUTS}
    per_example = {n: given[n] for n in ['x', 'mem']}
    grad_fn = _jax.value_and_grad(_loss, argnums=(0, 1))

    def one_microbatch(ex, loss_target):
        ex = dict(ex)
        diff = ex.pop(TWIN_DIFF_INPUT)
        return grad_fn(weights, diff, {**shared, **ex}, loss_target)

    if N_MICROBATCH == 1:
        loss, (grad_w, grad_x) = one_microbatch(per_example, given["loss_target"])
    else:
        def body(carry, xs):
            loss_sum, grad_sum = carry
            l_k, (gw_k, gx_k) = one_microbatch(xs[0], xs[1])
            with _jax.named_scope("update"):
                return (loss_sum + l_k, _jax.tree.map(_jnp.add, grad_sum, gw_k)), gx_k

        init = (_jnp.zeros((), _jnp.float32), _jax.tree.map(_jnp.zeros_like, weights))
        (loss, grad_w), grad_x = _jax.lax.scan(body, init, (per_example, given["loss_target"]))
    with _jax.named_scope("update"):
        delta_w, new_m, new_v = {}, {}, {}
        for n in TWIN_WEIGHTS:
            delta_w[n], new_m[n], new_v[n] = _adamw(weights[n], grad_w[n], given["m_" + n], given["v_" + n])
    return (loss, grad_x, *[grad_w[n] for n in TWIN_WEIGHTS], *[delta_w[n] for n in TWIN_WEIGHTS],
            *[new_m[n] for n in TWIN_WEIGHTS], *[new_v[n] for n in TWIN_WEIGHTS])
```

```python
import functools

import jax
import jax.numpy as jnp
from jax import lax
from jax.experimental import pallas as pl
from jax.experimental.pallas import tpu as pltpu

F32, BF16 = jnp.float32, jnp.bfloat16
HI = lax.Precision.HIGHEST
MESH = pl.DeviceIdType.MESH

EPS = 1e-6
DN_HEADS = 8
HEAD_DIM = 128
X_HEADS = 4
POOL_WINDOWS = (2, 4, 8, 16)
CONV_WIDTH = 4
CHUNK = 64
N_SHARDS = 4
AB_PAD = 128
ADAM_LR, ADAM_B1, ADAM_B2, ADAM_EPS, ADAM_WD, ADAM_STEP = 0.001, 0.9, 0.999, 1e-08, 0.01, 10
VMEM_LIMIT = 56 << 20
ROW_TILE = 256


def _tile(n, pref, unit=128):
    best = None
    for t in range(unit, min(n, pref) + 1, unit):
        if n % t == 0:
            best = t
    return best if best is not None else n


def _params():
    return pltpu.CompilerParams(vmem_limit_bytes=VMEM_LIMIT)


def _dg(a, b, ca, cb, hi):
    dims = (((ca,), (cb,)), ((), ()))
    if hi:
        return lax.dot_general(a.astype(F32), b.astype(F32), dims, precision=HI, preferred_element_type=F32)
    return lax.dot_general(a.astype(BF16), b.astype(BF16), dims, preferred_element_type=F32)


@functools.partial(jax.custom_vjp, nondiff_argnums=(2, 3, 4))
def mmul(a, b, ca, cb, hi):
    return _dg(a, b, ca, cb, hi)


def _mmul_fwd(a, b, ca, cb, hi):
    return _dg(a, b, ca, cb, hi), (a, b)


def _mmul_bwd(ca, cb, hi, res, g):
    a, b = res
    da = _dg(g, b, 1, 1 - cb, hi) if ca == 1 else _dg(b, g, 1 - cb, 1, hi)
    db = _dg(a, g, 1 - ca, 0, hi) if cb == 0 else _dg(g, a, 0, 1 - ca, hi)
    return da.astype(a.dtype), db.astype(b.dtype)


mmul.defvjp(_mmul_fwd, _mmul_bwd)


@functools.partial(jax.custom_vjp, nondiff_argnums=(1,))
def shift_down(x, s):
    t = lax.broadcasted_iota(jnp.int32, x.shape, 0)
    return jnp.where(t >= s, pltpu.roll(x, s, 0), 0.0)


def _shift_up(x, s):
    n = x.shape[0]
    t = lax.broadcasted_iota(jnp.int32, x.shape, 0)
    return jnp.where(t < n - s, pltpu.roll(x, n - s, 0), 0.0)


shift_down.defvjp(lambda x, s: (shift_down(x, s), None), lambda s, _, g: (_shift_up(g, s),))


def sigmoid(x):
    return 0.5 * (jnp.tanh(0.5 * x) + 1.0)


def silu(x):
    return x * sigmoid(x)


@jax.custom_vjp
def softplus(x):
    u = jnp.exp(-jnp.abs(x))
    w = 1.0 + u
    log1p = jnp.where(w == 1.0, u, jnp.log(w) * u / jnp.where(w == 1.0, 1.0, w - 1.0))
    return jnp.maximum(x, 0.0) + log1p


softplus.defvjp(lambda x: (softplus(x), x), lambda x, g: (g * sigmoid(x),))


def rms(x, g):
    x = x.astype(F32)
    return x * lax.rsqrt(jnp.mean(x * x, axis=-1, keepdims=True) + EPS) * g


def swiglu(gate, up):
    return silu(gate) * up


def vjp_of(fn, n_in, diff, has_pids=False):
    def g(*args):
        pids = None
        if has_pids:
            pids, args = args[0], args[1:]
        ins, cots = list(args[:n_in]), args[n_in:]

        def f(*d):
            full = list(ins)
            for i, v in zip(diff, d):
                full[i] = v
            return fn(pids, *full) if has_pids else fn(*full)

        out, pull = jax.vjp(f, *[ins[i].astype(F32) for i in diff])
        if isinstance(out, (tuple, list)):
            return pull(tuple(c.astype(o.dtype) for c, o in zip(cots, out)))
        return pull(cots[0].astype(out.dtype))
    return g


def tile_call(name, fn, grid, ins, outs, with_pids=False):
    n_in = len(ins)

    def body(*refs):
        pids = tuple(pl.program_id(a) for a in range(len(grid)))
        vals = [r[...] for r in refs[:n_in]]
        res = fn(pids, *vals) if with_pids else fn(*vals)
        if not isinstance(res, (tuple, list)):
            res = (res,)
        for r, o, spec in zip(res, refs[n_in:], outs):
            acc = spec[4]
            if acc is None:
                o[...] = r.astype(o.dtype)
            else:
                first = functools.reduce(jnp.logical_and, [pids[a] == 0 for a in acc])

                @pl.when(first)
                def _():
                    o[...] = r.astype(o.dtype)

                @pl.when(jnp.logical_not(first))
                def _():
                    o[...] += r.astype(o.dtype)

    res = pl.pallas_call(
        body, name=name, grid=grid,
        in_specs=[pl.BlockSpec(b, im) for _, b, im in ins],
        out_specs=[pl.BlockSpec(s[2], s[3]) for s in outs],
        out_shape=[jax.ShapeDtypeStruct(s[0], s[1]) for s in outs],
        compiler_params=_params(),
    )(*[a for a, _, _ in ins])
    return res


def mm_call(name, grid, ins, pairs, n_acc, acc_shape, outs, epilogue, extras=()):
    n_in, n_ex, nk = len(ins), len(extras), grid[2]

    def body(*refs):
        in_refs, ex_refs = refs[:n_in], refs[n_in:n_in + n_ex]
        out_refs = refs[n_in + n_ex:n_in + n_ex + len(outs)]
        accs = refs[n_in + n_ex + len(outs):]
        k = pl.program_id(2)

        @pl.when(k == 0)
        def _():
            for a in accs:
                a[...] = jnp.zeros_like(a)

        for ia, ib, ca, cb, ai in pairs:
            accs[ai][...] += _dg(in_refs[ia][...], in_refs[ib][...], ca, cb, False)

        @pl.when(k == nk - 1)
        def _():
            res = epilogue([a[...] for a in accs], *[r[...] for r in ex_refs])
            if not isinstance(res, (tuple, list)):
                res = (res,)
            for r, o in zip(res, out_refs):
                o[...] = r.astype(o.dtype)

    return pl.pallas_call(
        body, name=name, grid=grid,
        in_specs=[pl.BlockSpec(b, im) for _, b, im in list(ins) + list(extras)],
        out_specs=[pl.BlockSpec(s[2], s[3]) for s in outs],
        out_shape=[jax.ShapeDtypeStruct(s[0], s[1]) for s in outs],
        scratch_shapes=[pltpu.VMEM(acc_shape, F32) for _ in range(n_acc)],
        compiler_params=_params(),
    )(*[a for a, _, _ in list(ins) + list(extras)])


def mm2(name, a, b, ca, cb, res=None, scale=None, out_dtype=F32, tm=512, tn=1024, tk=512):
    m, kk, n = a.shape[1 - ca], a.shape[ca], b.shape[1 - cb]
    tm, tn, tk = _tile(m, tm), _tile(n, tn), _tile(kk, tk)
    a_spec = ((tm, tk), lambda i, j, k: (i, k)) if ca == 1 else ((tk, tm), lambda i, j, k: (k, i))
    b_spec = ((tk, tn), lambda i, j, k: (k, j)) if cb == 0 else ((tn, tk), lambda i, j, k: (j, k))
    extras = [] if res is None else [(res, (tm, tn), lambda i, j, k: (i, j))]

    def epi(accs, *ex):
        r = accs[0] if scale is None else accs[0] * scale
        return r + ex[0] if ex else r

    return mm_call(name, (m // tm, n // tn, kk // tk), [(a,) + a_spec, (b,) + b_spec], [(0, 1, ca, cb, 0)], 1, (tm, tn),
                   [((m, n), out_dtype, (tm, tn), lambda i, j, k: (i, j))], epi, extras)[0]


def rows_call(name, fn, rows, consts, outs, acc_outs=(), tr=ROW_TILE):
    s = rows[0].shape[0]
    tr = _tile(s, tr, 8)
    ins = [(r, (tr, r.shape[1]), lambda i: (i, 0)) for r in rows]
    ins += [(c, c.shape, (lambda nd: (lambda i: (0,) * nd))(c.ndim)) for c in consts]
    o = [((s, c), dt, (tr, c), lambda i: (i, 0), None) for c, dt in outs]
    o += [(shp, F32, shp, (lambda nd: (lambda i: (0,) * nd))(len(shp)), (0,)) for shp in acc_outs]
    return tile_call(name, fn, (s // tr,), ins, o)


def _lane_pick(x, h):
    lane = lax.broadcasted_iota(jnp.int32, x.shape, x.ndim - 1)
    return jnp.sum(jnp.where(lane == h, x, 0.0), axis=-1, keepdims=True)


def gateprep_fn(ab, alog, dtb):
    t = ab.shape[0]
    gs, bs = [], []
    for h in range(DN_HEADS):
        a_h = _lane_pick(ab, h)
        b_h = _lane_pick(ab, DN_HEADS + h)
        g_h = -jnp.exp(_lane_pick(alog, h)) * softplus(a_h + _lane_pick(dtb, h))
        gs.append(jnp.broadcast_to(g_h, (t, HEAD_DIM)))
        bs.append(jnp.broadcast_to(sigmoid(b_h), (t, HEAD_DIM)))
    return jnp.concatenate(gs, axis=1), jnp.concatenate(bs, axis=1)


def conv_fn(pids, x, w):
    kind = pids[0] // DN_HEADS
    y = x * w[CONV_WIDTH - 1:CONV_WIDTH]
    for i in range(CONV_WIDTH - 1):
        y = y + shift_down(x, CONV_WIDTH - 1 - i) * w[i:i + 1]
    y = silu(y)
    n = y * lax.rsqrt(jnp.sum(y * y, axis=-1, keepdims=True) + EPS)
    n = n * jnp.where(kind == 0, HEAD_DIM ** -0.5, 1.0)
    return jnp.where(kind == 2, y, n)


def intra_head(q, k, v, g, b):
    c = q.shape[0]
    r = lax.broadcasted_iota(jnp.int32, (c, c), 0)
    cc = lax.broadcasted_iota(jnp.int32, (c, c), 1)
    tril = (r >= cc).astype(F32)
    gc = mmul(tril, g, 1, 0, True)
    m = gc[:, :c]
    decay = jnp.exp(jnp.where(r >= cc, m - m.T, -1e30))
    kb = k * b
    a = jnp.where(r > cc, mmul(kb, k, 1, 1, True) * decay, 0.0)
    x = -a
    t = jnp.where(r == cc, 1.0, 0.0) + x
    p = 2
    while p < c:
        x = mmul(x, x, 1, 0, True)
        t = t + mmul(t, x, 1, 0, True)
        p *= 2
    e = jnp.exp(gc)
    u = mmul(t, v * b, 1, 0, True)
    w = mmul(t, kb * e, 1, 0, True)
    qk = mmul(q, k, 1, 1, True) * decay
    gl = gc[c - 1:c, :]
    kd = k * jnp.exp(gl - gc)
    return u, w, qk, q * e, kd, jnp.broadcast_to(jnp.exp(gl), (8, HEAD_DIM))


def _heads(x, h):
    return x[:, h * HEAD_DIM:(h + 1) * HEAD_DIM]


def intra_fn(q, k, v, g, b):
    outs = [intra_head(_heads(q, h), _heads(k, h), _heads(v, h), _heads(g, h), _heads(b, h)) for h in range(DN_HEADS)]
    cat = lambda i: jnp.concatenate([o[i] for o in outs], axis=1)
    return cat(0), cat(1), jnp.stack([o[2] for o in outs], axis=0), cat(3), cat(4), cat(5)


def intra_bwd_fn(q, k, v, g, b, du, dw, dqk, dqd, dkd, dgl):
    res = []
    for h in range(DN_HEADS):
        hs = lambda x: _heads(x, h)
        res.append(vjp_of(intra_head, 5, (0, 1, 2, 3, 4))(hs(q), hs(k), hs(v), hs(g), hs(b),
                                                         hs(du), hs(dw), dqk[h], hs(dqd), hs(dkd), hs(dgl)))
    return tuple(jnp.concatenate([r[i] for r in res], axis=1) for i in range(5))


def scan_step(s, u, w, qk, qd, kd, gl):
    v_new = u - mmul(w, s, 1, 0, True)
    o = mmul(qd, s, 1, 0, True) + mmul(qk, v_new, 1, 0, True)
    return s * gl[0:1, :] + mmul(kd, v_new, 0, 0, True), o


def outgate_fn(o, z, g):
    return rms(o, g) * silu(z)


def pool_fn(pids, p):
    gid = pids[0]
    s = p.shape[0]
    t1 = (lax.broadcasted_iota(jnp.int32, p.shape, 0) + 1).astype(F32)
    acc, win, out = p, 1, None
    for gi, target in enumerate(POOL_WINDOWS):
        while win < target:
            acc = acc + shift_down(acc, win)
            win *= 2
        cand = acc / jnp.minimum(t1, float(target))
        out = cand if out is None else jnp.where(gid == gi, cand, out)
    return out - p


def poolmix_fn(pooled, pw, scale):
    return mmul(pooled, pw, 1, 0, False) * scale


def attn_fn(q, k, v):
    s = mmul(q, k, 1, 1, False) * (q.shape[1] ** -0.5)
    s = s - jnp.max(s, axis=-1, keepdims=True)
    e = jnp.exp(s)
    p = e / jnp.sum(e, axis=-1, keepdims=True)
    return mmul(p, v, 1, 0, False)


def rms_fwd(name, x, g):
    return rows_call(name, lambda a, b: rms(a, b), [x], [g], [(x.shape[1], BF16)])[0]


def rms_bwd(name, x, g, dy, dres):
    def fn(a, d, r, b):
        dx, dg = vjp_of(rms, 2, (0, 1))(a, b, d)
        return dx + r, dg
    return rows_call(name, fn, [x, dy, dres], [g], [(x.shape[1], F32)], [g.shape])


def ffn_fwd(tag, x, g, wg, wu, wd, l):
    s, d = x.shape
    fj = wg.shape[-1]
    f = N_SHARDS * fj
    xn = rms_fwd(tag + "_norm", x, g)
    tm, tk = _tile(s, 512), _tile(d, 512)
    w_spec = ((None, None, tk, fj), lambda i, j, k: (j, l, k, 0))
    o_spec = ((tm, fj), lambda i, j, k: (i, j))
    gate, up, act = mm_call(
        tag + "_gu", (s // tm, N_SHARDS, d // tk),
        [(xn, (tm, tk), lambda i, j, k: (i, k)), (wg,) + w_spec, (wu,) + w_spec],
        [(0, 1, 1, 0, 0), (0, 2, 1, 0, 1)], 2, (tm, fj),
        [((s, f), F32) + o_spec, ((s, f), F32) + o_spec, ((s, f), BF16) + o_spec],
        lambda accs: (accs[0], accs[1], swiglu(accs[0], accs[1])))
    tn = _tile(d, 1024)
    out = mm_call(
        tag + "_down", (s // tm, d // tn, N_SHARDS),
        [(act, (tm, fj), lambda i, j, k: (i, k)), (wd, (None, None, fj, tn), lambda i, j, k: (k, l, 0, j))],
        [(0, 1, 1, 0, 0)], 1, (tm, tn),
        [((s, d), F32, (tm, tn), lambda i, j, k: (i, j))],
        lambda accs, r: r + 0.5 * accs[0], [(x, (tm, tn), lambda i, j, k: (i, j))])[0]
    return out, (x, xn, gate, up, act)


def ffn_bwd(tag, saved, g, wg, wu, wd, l, dout):
    x, xn, gate, up, act = saved
    s, d = x.shape
    fj = wg.shape[-1]
    f = N_SHARDS * fj
    tm, tk = _tile(s, 512), _tile(d, 512)
    o_spec = ((tm, fj), lambda i, j, k: (i, j))

    def epi(accs, ga, u):
        dgate, dup = vjp_of(swiglu, 2, (0, 1))(ga, u, 0.5 * accs[0])
        return dgate, dup

    dgate, dup = mm_call(
        tag + "_dact", (s // tm, N_SHARDS, d // tk),
        [(dout, (tm, tk), lambda i, j, k: (i, k)), (wd, (None, None, fj, tk), lambda i, j, k: (j, l, 0, k))],
        [(0, 1, 1, 1, 0)], 1, (tm, fj),
        [((s, f), BF16) + o_spec, ((s, f), BF16) + o_spec], epi,
        [(gate,) + o_spec, (up,) + o_spec])
    ts, tn = _tile(s, 512), _tile(d, 512)
    dwd = mm_call(
        tag + "_dwd", (N_SHARDS, d // tn, s // ts),
        [(act, (ts, fj), lambda i, j, k: (k, i)), (dout, (ts, tn), lambda i, j, k: (k, j))],
        [(0, 1, 0, 0, 0)], 1, (fj, tn),
        [((N_SHARDS, fj, d), F32, (None, fj, tn), lambda i, j, k: (i, 0, j))],
        lambda accs: 0.5 * accs[0])[0]
    td = _tile(d, 512)
    g_spec = ((ts, fj), lambda i, j, k: (k, j))
    w_out = ((N_SHARDS, d, fj), F32, (None, td, fj), lambda i, j, k: (j, i, 0))
    dwg, dwu = mm_call(
        tag + "_dwgu", (d // td, N_SHARDS, s // ts),
        [(xn, (ts, td), lambda i, j, k: (k, i)), (dgate,) + g_spec, (dup,) + g_spec],
        [(0, 1, 0, 0, 0), (0, 2, 0, 0, 1)], 2, (td, fj), [w_out, w_out], lambda accs: (accs[0], accs[1]))
    tn = _tile(d, 1024)
    a_spec = ((tm, fj), lambda i, j, k: (i, k))
    wt_spec = ((None, None, tn, fj), lambda i, j, k: (k, l, j, 0))
    dxn = mm_call(
        tag + "_dxn", (s // tm, d // tn, N_SHARDS),
        [(dgate,) + a_spec, (wg,) + wt_spec, (dup,) + a_spec, (wu,) + wt_spec],
        [(0, 1, 1, 1, 0), (2, 3, 1, 1, 0)], 1, (tm, tn),
        [((s, d), F32, (tm, tn), lambda i, j, k: (i, j))], lambda accs: accs[0])[0]
    dx, dg = rms_bwd(tag + "_dnorm", x, g, dxn, dout)
    return dx, dg, dwg, dwu, dwd


def _hspec(tc, width=HEAD_DIM):
    return (tc, width)


def mixer_fwd(tag, h, wts):
    s, d = h.shape
    dnw = DN_HEADS * HEAD_DIM
    pw_ = d - dnw
    gdim = pw_ // len(POOL_WINDOWS)
    nc = s // CHUNK
    hn = rms_fwd(tag + "_norm", h, wts["mix_norm"])
    qkv = mm2(tag + "_qkv", hn, wts["w_qkv"], 1, 0)
    z = mm2(tag + "_z", hn, wts["w_z"], 1, 0)
    ab = mm2(tag + "_ab", hn, wts["w_ab"], 1, 0)
    p = mm2(tag + "_p", hn, wts["w_p"], 1, 0)
    qkvn = tile_call(tag + "_conv", conv_fn, (3 * DN_HEADS,),
                     [(qkv, (s, HEAD_DIM), lambda i: (0, i)), (wts["conv_w"], (CONV_WIDTH, HEAD_DIM), lambda i: (0, i))],
                     [((s, 3 * dnw), F32, (s, HEAD_DIM), lambda i: (0, i), None)], with_pids=True)[0]
    g_bc, b_bc = rows_call(tag + "_gates", gateprep_fn, [ab], [wts["a_log"], wts["dt_bias"]], [(dnw, F32), (dnw, F32)])
    cw = (CHUNK, dnw)
    u, w, qk, qd, kd, gl = tile_call(
        tag + "_intra", intra_fn, (nc,),
        [(qkvn, cw, lambda n: (n, 0)), (qkvn, cw, lambda n: (n, 1)), (qkvn, cw, lambda n: (n, 2)),
         (g_bc, cw, lambda n: (n, 0)), (b_bc, cw, lambda n: (n, 0))],
        [((s, dnw), F32, cw, lambda n: (n, 0), None), ((s, dnw), F32, cw, lambda n: (n, 0), None),
         ((DN_HEADS, s, CHUNK), F32, (DN_HEADS, CHUNK, CHUNK), lambda n: (0, n, 0), None),
         ((s, dnw), F32, cw, lambda n: (n, 0), None), ((s, dnw), F32, cw, lambda n: (n, 0), None),
         ((nc * 8, dnw), F32, (8, dnw), lambda n: (n, 0), None)])
    o, states = scan_fwd(tag + "_scan", u, w, qk, qd, kd, gl)
    y_dn = tile_call(
        tag + "_outgate", outgate_fn, (DN_HEADS, s // ROW_TILE),
        [(o, (ROW_TILE, HEAD_DIM), lambda hh, i: (i, hh)), (z, (ROW_TILE, HEAD_DIM), lambda hh, i: (i, hh)),
         (wts["dn_out_norm"], (1, HEAD_DIM), lambda hh, i: (0, 0))],
        [((s, dnw), BF16, (ROW_TILE, HEAD_DIM), lambda hh, i: (i, hh), None)])[0]
    ng = len(POOL_WINDOWS)
    pooled = tile_call(tag + "_pool", pool_fn, (ng,), [(p, (s, gdim), lambda i: (0, i))],
                       [((s, pw_), BF16, (s, gdim), lambda i: (0, i), None)], with_pids=True)[0]
    tp = _tile(s, 512)
    y_pool = tile_call(
        tag + "_poolmix", poolmix_fn, (ng, s // tp),
        [(pooled, (tp, gdim), lambda gi, i: (i, gi)), (wts["pool_w"], (None, gdim, gdim), lambda gi, i: (gi, 0, 0)),
         (wts["pool_scale"], (1, gdim), lambda gi, i: (0, gi))],
        [((s, pw_), BF16, (tp, gdim), lambda gi, i: (i, gi), None)])[0]
    h1 = mm2(tag + "_out_a", y_dn, wts["w_out_a"], 1, 0, res=h)
    h2 = mm2(tag + "_out_b", y_pool, wts["w_out_b"], 1, 0, res=h1)
    saved = (h, hn, qkv, z, ab, p, qkvn, g_bc, b_bc, u, w, qk, qd, kd, gl, o, states, y_dn, pooled, y_pool)
    return h2, saved


def scan_fwd(name, u, w, qk, qd, kd, gl):
    s, dnw = u.shape
    nc = s // CHUNK
    cw = (CHUNK, dnw)

    def body(u_r, w_r, qk_r, qd_r, kd_r, gl_r, o_r, st_r, state):
        @pl.when(pl.program_id(0) == 0)
        def _():
            state[...] = jnp.zeros_like(state)

        st_r[...] = state[...]
        outs = []
        for h in range(DN_HEADS):
            hs = slice(h * HEAD_DIM, (h + 1) * HEAD_DIM)
            s_new, o_h = scan_step(state[hs, :], u_r[:, hs], w_r[:, hs], qk_r[h], qd_r[:, hs], kd_r[:, hs], gl_r[:, hs])
            state[hs, :] = s_new
            outs.append(o_h)
        o_r[...] = jnp.concatenate(outs, axis=1)

    row = lambda n: (n, 0)
    return pl.pallas_call(
        body, name=name, grid=(nc,),
        in_specs=[pl.BlockSpec(cw, row), pl.BlockSpec(cw, row), pl.BlockSpec((DN_HEADS, CHUNK, CHUNK), lambda n: (0, n, 0)),
                  pl.BlockSpec(cw, row), pl.BlockSpec(cw, row), pl.BlockSpec((8, dnw), row)],
        out_specs=[pl.BlockSpec(cw, row), pl.BlockSpec((None, dnw, HEAD_DIM), lambda n: (n, 0, 0))],
        out_shape=[jax.ShapeDtypeStruct((s, dnw), F32), jax.ShapeDtypeStruct((nc, dnw, HEAD_DIM), F32)],
        scratch_shapes=[pltpu.VMEM((dnw, HEAD_DIM), F32)],
        compiler_params=_params(),
    )(u, w, qk, qd, kd, gl)


def scan_bwd(name, states, u, w, qk, qd, kd, gl, do):
    s, dnw = u.shape
    nc = s // CHUNK
    cw = (CHUNK, dnw)

    def body(st_r, u_r, w_r, qk_r, qd_r, kd_r, gl_r, do_r, du_r, dw_r, dqk_r, dqd_r, dkd_r, dgl_r, dstate):
        @pl.when(pl.program_id(0) == 0)
        def _():
            dstate[...] = jnp.zeros_like(dstate)

        res = []
        for h in range(DN_HEADS):
            hs = slice(h * HEAD_DIM, (h + 1) * HEAD_DIM)
            r = vjp_of(scan_step, 7, tuple(range(7)))(
                st_r[hs, :], u_r[:, hs], w_r[:, hs], qk_r[h], qd_r[:, hs], kd_r[:, hs], gl_r[:, hs],
                dstate[hs, :], do_r[:, hs])
            dstate[hs, :] = r[0]
            res.append(r)
        cat = lambda i: jnp.concatenate([r[i] for r in res], axis=1)
        du_r[...] = cat(1)
        dw_r[...] = cat(2)
        dqk_r[...] = jnp.stack([r[3] for r in res], axis=0)
        dqd_r[...] = cat(4)
        dkd_r[...] = cat(5)
        dgl_r[...] = cat(6)

    row = lambda n: (nc - 1 - n, 0)
    qk_spec = pl.BlockSpec((DN_HEADS, CHUNK, CHUNK), lambda n: (0, nc - 1 - n, 0))
    return pl.pallas_call(
        body, name=name, grid=(nc,),
        in_specs=[pl.BlockSpec((None, dnw, HEAD_DIM), lambda n: (nc - 1 - n, 0, 0)), pl.BlockSpec(cw, row), pl.BlockSpec(cw, row),
                  qk_spec, pl.BlockSpec(cw, row), pl.BlockSpec(cw, row), pl.BlockSpec((8, dnw), row), pl.BlockSpec(cw, row)],
        out_specs=[pl.BlockSpec(cw, row), pl.BlockSpec(cw, row), qk_spec, pl.BlockSpec(cw, row), pl.BlockSpec(cw, row),
                   pl.BlockSpec((8, dnw), row)],
        out_shape=[jax.ShapeDtypeStruct((s, dnw), F32), jax.ShapeDtypeStruct((s, dnw), F32),
                   jax.ShapeDtypeStruct((DN_HEADS, s, CHUNK), F32), jax.ShapeDtypeStruct((s, dnw), F32),
                   jax.ShapeDtypeStruct((s, dnw), F32), jax.ShapeDtypeStruct((nc * 8, dnw), F32)],
        scratch_shapes=[pltpu.VMEM((dnw, HEAD_DIM), F32)],
        compiler_params=_params(),
    )(states, u, w, qk, qd, kd, gl, do)


def mixer_bwd(tag, saved, wts, dout):
    h, hn, qkv, z, ab, p, qkvn, g_bc, b_bc, u, w, qk, qd, kd, gl, o, states, y_dn, pooled, y_pool = saved
    s, d = h.shape
    dnw = DN_HEADS * HEAD_DIM
    pw_ = d - dnw
    ng = len(POOL_WINDOWS)
    gdim = pw_ // ng
    nc = s // CHUNK
    gr = {}
    d_ydn = mm2(tag + "_dydn", dout, wts["w_out_a"], 1, 1)
    d_ypool = mm2(tag + "_dypool", dout, wts["w_out_b"], 1, 1)
    gr["w_out_a"] = mm2(tag + "_dwout_a", y_dn, dout, 0, 0)
    gr["w_out_b"] = mm2(tag + "_dwout_b", y_pool, dout, 0, 0)
    tp = _tile(s, 512)
    d_pooled, gr["pool_w"], gr["pool_scale"] = tile_call(
        tag + "_dpoolmix", vjp_of(poolmix_fn, 3, (0, 1, 2)), (ng, s // tp),
        [(pooled, (tp, gdim), lambda gi, i: (i, gi)), (wts["pool_w"], (None, gdim, gdim), lambda gi, i: (gi, 0, 0)),
         (wts["pool_scale"], (1, gdim), lambda gi, i: (0, gi)), (d_ypool, (tp, gdim), lambda gi, i: (i, gi))],
        [((s, pw_), F32, (tp, gdim), lambda gi, i: (i, gi), None),
         ((ng, gdim, gdim), F32, (None, gdim, gdim), lambda gi, i: (gi, 0, 0), (1,)),
         ((1, pw_), F32, (1, gdim), lambda gi, i: (0, gi), (1,))])
    d_p = tile_call(tag + "_dpool", vjp_of(pool_fn, 1, (0,), True), (ng,),
                    [(p, (s, gdim), lambda i: (0, i)), (d_pooled, (s, gdim), lambda i: (0, i))],
                    [((s, pw_), BF16, (s, gdim), lambda i: (0, i), None)], with_pids=True)[0]
    hb = (ROW_TILE, HEAD_DIM)
    d_o, d_z, gr["dn_out_norm"] = tile_call(
        tag + "_doutgate", vjp_of(outgate_fn, 3, (0, 1, 2)), (DN_HEADS, s // ROW_TILE),
        [(o, hb, lambda hh, i: (i, hh)), (z, hb, lambda hh, i: (i, hh)), (wts["dn_out_norm"], (1, HEAD_DIM), lambda hh, i: (0, 0)),
         (d_ydn, hb, lambda hh, i: (i, hh))],
        [((s, dnw), F32, hb, lambda hh, i: (i, hh), None), ((s, dnw), BF16, hb, lambda hh, i: (i, hh), None),
         ((1, HEAD_DIM), F32, (1, HEAD_DIM), lambda hh, i: (0, 0), (0, 1))])
    du, dw, dqk, dqd, dkd, dgl = scan_bwd(tag + "_dscan", states, u, w, qk, qd, kd, gl, d_o)
    cw = (CHUNK, dnw)
    row = lambda n: (n, 0)
    dq, dk, dv, dg_bc, db_bc = tile_call(
        tag + "_dintra", intra_bwd_fn, (nc,),
        [(qkvn, cw, lambda n: (n, 0)), (qkvn, cw, lambda n: (n, 1)), (qkvn, cw, lambda n: (n, 2)),
         (g_bc, cw, row), (b_bc, cw, row), (du, cw, row), (dw, cw, row),
         (dqk, (DN_HEADS, CHUNK, CHUNK), lambda n: (0, n, 0)), (dqd, cw, row), (dkd, cw, row), (dgl, (8, dnw), row)],
        [((s, dnw), F32, cw, row, None)] * 5)
    d_ab, gr["a_log"], gr["dt_bias"] = rows_call(
        tag + "_dgates", lambda a, dg, db, al, dt: vjp_of(gateprep_fn, 3, (0, 1, 2))(a, al, dt, dg, db),
        [ab, dg_bc, db_bc], [wts["a_log"], wts["dt_bias"]], [(AB_PAD, BF16)], [(1, AB_PAD), (1, AB_PAD)])
    d_qkvn = jnp.concatenate([dq, dk, dv], axis=1)
    d_qkv, gr["conv_w"] = tile_call(
        tag + "_dconv", vjp_of(conv_fn, 2, (0, 1), True), (3 * DN_HEADS,),
        [(qkv, (s, HEAD_DIM), lambda i: (0, i)), (wts["conv_w"], (CONV_WIDTH, HEAD_DIM), lambda i: (0, i)),
         (d_qkvn, (s, HEAD_DIM), lambda i: (0, i))],
        [((s, 3 * dnw), BF16, (s, HEAD_DIM), lambda i: (0, i), None),
         ((CONV_WIDTH, 3 * dnw), F32, (CONV_WIDTH, HEAD_DIM), lambda i: (0, i), None)], with_pids=True)
    gr["w_qkv"] = mm2(tag + "_dwqkv", hn, d_qkv, 0, 0)
    gr["w_z"] = mm2(tag + "_dwz", hn, d_z, 0, 0)
    gr["w_ab"] = mm2(tag + "_dwab", hn, d_ab, 0, 0)
    gr["w_p"] = mm2(tag + "_dwp", hn, d_p, 0, 0)
    d_hn = mm2(tag + "_dhn1", d_qkv, wts["w_qkv"], 1, 1)
    d_hn = mm2(tag + "_dhn2", d_z, wts["w_z"], 1, 1, res=d_hn)
    d_hn = mm2(tag + "_dhn3", d_ab, wts["w_ab"], 1, 1, res=d_hn)
    d_hn = mm2(tag + "_dhn4", d_p, wts["w_p"], 1, 1, res=d_hn)
    dh, gr["mix_norm"] = rms_bwd(tag + "_dnorm", h, wts["mix_norm"], d_hn, dout)
    return dh, gr


def xattn_fwd(tag, h, memn, wts):
    s, d = h.shape
    m = memn.shape[0]
    dh_ = d // X_HEADS
    hn = rms_fwd(tag + "_norm", h, wts["xattn_norm"])
    q = mm2(tag + "_q", hn, wts["wq"], 1, 0, out_dtype=BF16)
    kv = mm2(tag + "_kv", memn, wts["wkv"], 1, 0, out_dtype=BF16)
    tq = _tile(s, 512)
    o = tile_call(
        tag + "_attn", attn_fn, (X_HEADS, s // tq),
        [(q, (tq, dh_), lambda hh, i: (i, hh)), (kv, (m, dh_), lambda hh, i: (0, hh)), (kv, (m, dh_), lambda hh, i: (0, X_HEADS + hh))],
        [((s, d), BF16, (tq, dh_), lambda hh, i: (i, hh), None)])[0]
    out = mm2(tag + "_o", o, wts["wo"], 1, 0, res=h)
    return out, (h, hn, q, kv, o)


def xattn_bwd(tag, saved, memn, mem, wts, dout):
    h, hn, q, kv, o = saved
    s, d = h.shape
    m = memn.shape[0]
    dh_ = d // X_HEADS
    gr = {}
    d_o = mm2(tag + "_do", dout, wts["wo"], 1, 1)
    gr["wo"] = mm2(tag + "_dwo", o, dout, 0, 0)
    tq = _tile(s, 512)
    dq, dk, dv = tile_call(
        tag + "_dattn", vjp_of(attn_fn, 3, (0, 1, 2)), (X_HEADS, s // tq),
        [(q, (tq, dh_), lambda hh, i: (i, hh)), (kv, (m, dh_), lambda hh, i: (0, hh)), (kv, (m, dh_), lambda hh, i: (0, X_HEADS + hh)),
         (d_o, (tq, dh_), lambda hh, i: (i, hh))],
        [((s, d), BF16, (tq, dh_), lambda hh, i: (i, hh), None),
         ((m, d), F32, (m, dh_), lambda hh, i: (0, hh), (1,)), ((m, d), F32, (m, dh_), lambda hh, i: (0, hh), (1,))])
    dkv = jnp.concatenate([dk, dv], axis=1).astype(BF16)
    gr["wq"] = mm2(tag + "_dwq", hn, dq, 0, 0)
    gr["wkv"] = mm2(tag + "_dwkv", memn, dkv, 0, 0)
    d_memn = mm2(tag + "_dmemn", dkv, wts["wkv"], 1, 1)
    gr["mem_norm"] = rows_call(
        tag + "_dmemnorm", lambda a, dy, b: vjp_of(rms, 2, (1,))(a, b, dy)[0], [mem, d_memn], [wts["mem_norm"]], [],
        [wts["mem_norm"].shape], tr=128)[0]
    d_hn = mm2(tag + "_dhn", dq, wts["wq"], 1, 1)
    dh, gr["xattn_norm"] = rms_bwd(tag + "_dnorm", h, wts["xattn_norm"], d_hn, dout)
    return dh, gr


def final_loss(x, g, tgt):
    d = x.shape[1]

    def fn(a, t, b):
        def f(aa, bb):
            return 0.5 * jnp.sum(jnp.square(rms(aa, bb) - t)) / d
        loss, (dx, dg) = jax.value_and_grad(f, (0, 1))(a, b)
        lane = lax.broadcasted_iota(jnp.int32, (1, 128), 1)
        return dx, dg, jnp.where(lane == 0, loss, 0.0)
    return rows_call("final_loss", fn, [x, tgt], [g], [(d, F32)], [g.shape, (1, 128)])


SHARDED = ("ffn1_w_gate", "ffn1_w_up", "ffn1_w_down", "w_in", "conv_w", "pool_w", "w_out", "xattn_wq", "xattn_wkv",
           "xattn_wo", "ffn2_w_gate", "ffn2_w_up", "ffn2_w_down")
REPLICATED = ("ffn1_norm", "mix_norm", "a_log", "dt_bias", "dn_out_norm", "pool_scale", "xattn_norm", "mem_norm",
              "ffn2_norm", "final_norm")
WEIGHTS = ("ffn1_norm", "ffn1_w_gate", "ffn1_w_up", "ffn1_w_down", "mix_norm", "w_in", "conv_w", "a_log", "dt_bias",
           "dn_out_norm", "pool_w", "pool_scale", "w_out", "xattn_norm", "mem_norm", "xattn_wq", "xattn_wkv", "xattn_wo",
           "ffn2_norm", "ffn2_w_gate", "ffn2_w_up", "ffn2_w_down", "final_norm")


def _lane_pad(v, width=128):
    return jnp.pad(v, (0, width - v.shape[0]))[None, :]


def layer_weights(gath, rep, l, d):
    dnw = DN_HEADS * HEAD_DIM
    row = lambda name: rep[name][l][None, :].astype(F32)
    cols = lambda g: jnp.transpose(g[:, l], (1, 0, 2)).reshape(g.shape[2], -1)
    rows_ = lambda g: g[:, l].reshape(-1, g.shape[3])
    w_in = cols(gath["w_in"])
    o_ab = 4 * dnw
    w_out = rows_(gath["w_out"])
    pw = gath["pool_w"][:, l]
    gdim = pw.shape[-1]
    pw = jnp.transpose(pw.reshape(N_SHARDS, len(POOL_WINDOWS), gdim // N_SHARDS, gdim), (1, 0, 2, 3))
    mixer = dict(
        mix_norm=row("mix_norm"), w_qkv=w_in[:, :3 * dnw], w_z=w_in[:, 3 * dnw:o_ab],
        w_ab=jnp.pad(w_in[:, o_ab:o_ab + 2 * DN_HEADS], ((0, 0), (0, AB_PAD - 2 * DN_HEADS))),
        w_p=w_in[:, o_ab + 2 * DN_HEADS:], conv_w=cols(gath["conv_w"]).astype(F32),
        a_log=_lane_pad(rep["a_log"][l].astype(F32)), dt_bias=_lane_pad(rep["dt_bias"][l].astype(F32)),
        dn_out_norm=row("dn_out_norm"), pool_w=pw.reshape(len(POOL_WINDOWS), gdim, gdim), pool_scale=row("pool_scale"),
        w_out_a=w_out[:dnw], w_out_b=w_out[dnw:])
    xattn = dict(xattn_norm=row("xattn_norm"), mem_norm=row("mem_norm"), wq=rows_(gath["xattn_wq"]),
                 wkv=cols(gath["xattn_wkv"]), wo=rows_(gath["xattn_wo"]))
    return dict(ffn1_norm=row("ffn1_norm"), ffn2_norm=row("ffn2_norm"), mixer=mixer, xattn=xattn)


def _col_shards(g):
    k, n = g.shape
    return jnp.transpose(g.reshape(k, N_SHARDS, n // N_SHARDS), (1, 0, 2))


def model_grads(x, mem, tgt, gath, rep):
    s, d = x.shape
    depth = rep["ffn1_norm"].shape[0]
    dnw = DN_HEADS * HEAD_DIM
    wl = [layer_weights(gath, rep, l, d) for l in range(depth)]
    saved = []
    h = x
    for l in range(depth):
        w = wl[l]
        t = "l%d" % l
        memn = rms_fwd(t + "_memnorm", mem, w["xattn"]["mem_norm"])
        h, s1 = ffn_fwd(t + "_ffn1", h, w["ffn1_norm"], gath["ffn1_w_gate"], gath["ffn1_w_up"], gath["ffn1_w_down"], l)
        h, s2 = mixer_fwd(t + "_mix", h, w["mixer"])
        h, s3 = xattn_fwd(t + "_xattn", h, memn, w["xattn"])
        h, s4 = ffn_fwd(t + "_ffn2", h, w["ffn2_norm"], gath["ffn2_w_gate"], gath["ffn2_w_up"], gath["ffn2_w_down"], l)
        saved.append((memn, s1, s2, s3, s4))
    dh, d_final, loss_row = final_loss(h, rep["final_norm"][None, :].astype(F32), tgt)
    big, small = [None] * depth, [None] * depth
    for l in reversed(range(depth)):
        w = wl[l]
        t = "l%d" % l
        memn, s1, s2, s3, s4 = saved[l]
        gb, gs = {}, {}
        dh, gs["ffn2_norm"], gb["ffn2_w_gate"], gb["ffn2_w_up"], gb["ffn2_w_down"] = ffn_bwd(
            t + "_ffn2", s4, w["ffn2_norm"], gath["ffn2_w_gate"], gath["ffn2_w_up"], gath["ffn2_w_down"], l, dh)
        dh, gx = xattn_bwd(t + "_xattn", s3, memn, mem, w["xattn"], dh)
        dh, gm = mixer_bwd(t + "_mix", s2, w["mixer"], dh)
        dh, gs["ffn1_norm"], gb["ffn1_w_gate"], gb["ffn1_w_up"], gb["ffn1_w_down"] = ffn_bwd(
            t + "_ffn1", s1, w["ffn1_norm"], gath["ffn1_w_gate"], gath["ffn1_w_up"], gath["ffn1_w_down"], l, dh)
        for n in ("xattn_norm", "mem_norm"):
            gs[n] = gx[n]
        for n in ("mix_norm", "a_log", "dt_bias", "dn_out_norm", "pool_scale"):
            gs[n] = gm[n]
        gb["w_in"] = _col_shards(jnp.concatenate([gm["w_qkv"], gm["w_z"], gm["w_ab"][:, :2 * DN_HEADS], gm["w_p"]], axis=1))
        gb["conv_w"] = _col_shards(gm["conv_w"])
        gdim = gm["pool_w"].shape[-1]
        gb["pool_w"] = jnp.transpose(gm["pool_w"].reshape(len(POOL_WINDOWS), N_SHARDS, gdim // N_SHARDS, gdim),
                                     (1, 0, 2, 3)).reshape(N_SHARDS, gdim, gdim)
        gb["w_out"] = jnp.concatenate([gm["w_out_a"], gm["w_out_b"]], axis=0).reshape(N_SHARDS, d // N_SHARDS, d)
        gb["xattn_wq"] = gx["wq"].reshape(N_SHARDS, d // N_SHARDS, d)
        gb["xattn_wo"] = gx["wo"].reshape(N_SHARDS, d // N_SHARDS, d)
        gb["xattn_wkv"] = _col_shards(gx["wkv"])
        big[l], small[l] = gb, gs
    return loss_row, dh, big, small, d_final


HBM = pl.BlockSpec(memory_space=pltpu.HBM)


def _place():
    x, y, c = lax.axis_index("x"), lax.axis_index("y"), lax.axis_index("c")
    chips = [(1 - x, y), (x, 1 - y), (1 - x, 1 - y)]
    return x, y, c, 2 * x + y, chips, [2 * px + py for px, py in chips]


def gather_weights(shards):
    n = len(shards)

    def body(*refs):
        ins, outs = refs[:n], refs[n:2 * n]
        ssem, rsem, lsem = refs[2 * n:]
        x, y, c, j_own, chips, js = _place()
        sib = (x, y, 1 - c)

        def ici(t, k, to):
            return pltpu.make_async_remote_copy(
                src_ref=ins[t].at[c], dst_ref=outs[t].at[j_own, c], send_sem=ssem.at[6 * t + k], recv_sem=rsem.at[6 * t + k],
                device_id=to, device_id_type=MESH)

        def landed(t, k, layer, sem_k):
            return pltpu.make_async_remote_copy(
                src_ref=outs[t].at[js[k], layer], dst_ref=outs[t].at[js[k], layer], send_sem=ssem.at[6 * t + sem_k],
                recv_sem=rsem.at[6 * t + sem_k], device_id=sib, device_id_type=MESH)

        local, sends = [], []
        for t in range(n):
            cp = pltpu.make_async_copy(ins[t], outs[t].at[j_own], lsem.at[t])
            cp.start()
            local.append(cp)
            for k, (px, py) in enumerate(chips):
                cp = ici(t, k, (px, py, c))
                cp.start()
                sends.append(cp)
        for t in range(n):
            for k in range(3):
                landed(t, k, c, k).wait_recv()
                cp = landed(t, k, c, 3 + k)
                cp.start()
                sends.append(cp)
        for t in range(n):
            for k in range(3):
                landed(t, k, 1 - c, 3 + k).wait_recv()
        for cp in sends:
            cp.wait_send()
        for cp in local:
            cp.wait()

    return pl.pallas_call(
        body, name="gather_weights",
        in_specs=[HBM] * n, out_specs=[HBM] * n,
        out_shape=[jax.ShapeDtypeStruct((N_SHARDS,) + a.shape, a.dtype) for a in shards],
        scratch_shapes=[pltpu.SemaphoreType.DMA((6 * n,)), pltpu.SemaphoreType.DMA((6 * n,)), pltpu.SemaphoreType.DMA((n,))],
    )(*shards)


def exchange_layers(g0s, g1s):
    n = len(g0s)

    def body(*refs):
        g0, g1, outs = refs[:n], refs[n:2 * n], refs[2 * n:3 * n]
        ssem, rsem = refs[3 * n:]
        x, y, c, _, _, _ = _place()

        def copy(t, src):
            return pltpu.make_async_remote_copy(src_ref=src[t], dst_ref=outs[t], send_sem=ssem.at[t], recv_sem=rsem.at[t],
                                                device_id=(x, y, 1 - c), device_id_type=MESH)

        @pl.when(c == 0)
        def _():
            for t in range(n):
                copy(t, g1).start()

        @pl.when(c == 1)
        def _():
            for t in range(n):
                copy(t, g0).start()

        for t in range(n):
            copy(t, g0).wait()

    return pl.pallas_call(
        body, name="exchange_layers", in_specs=[HBM] * (2 * n), out_specs=[HBM] * n,
        out_shape=[jax.ShapeDtypeStruct(a.shape, a.dtype) for a in g0s],
        scratch_shapes=[pltpu.SemaphoreType.DMA((n,)), pltpu.SemaphoreType.DMA((n,))],
    )(*g0s, *g1s)


def exchange_shards(parts):
    n = len(parts)

    def body(*refs):
        ins, outs = refs[:n], refs[n:2 * n]
        ssem, rsem, lsem = refs[2 * n:]
        x, y, c, j_own, chips, js = _place()
        local, sends = [], []
        for t in range(n):
            cp = pltpu.make_async_copy(ins[t].at[j_own], outs[t].at[j_own], lsem.at[t])
            cp.start()
            local.append(cp)
            for k, (px, py) in enumerate(chips):
                cp = pltpu.make_async_remote_copy(
                    src_ref=ins[t].at[js[k]], dst_ref=outs[t].at[j_own], send_sem=ssem.at[3 * t + k], recv_sem=rsem.at[3 * t + k],
                    device_id=(px, py, c), device_id_type=MESH)
                cp.start()
                sends.append(cp)
        for t in range(n):
            for k, (px, py) in enumerate(chips):
                pltpu.make_async_remote_copy(
                    src_ref=ins[t].at[js[k]], dst_ref=outs[t].at[js[k]], send_sem=ssem.at[3 * t + k], recv_sem=rsem.at[3 * t + k],
                    device_id=(px, py, c), device_id_type=MESH).wait_recv()
        for cp in sends:
            cp.wait_send()
        for cp in local:
            cp.wait()

    return pl.pallas_call(
        body, name="exchange_shards", in_specs=[HBM] * n, out_specs=[HBM] * n,
        out_shape=[jax.ShapeDtypeStruct(a.shape, a.dtype) for a in parts],
        scratch_shapes=[pltpu.SemaphoreType.DMA((3 * n,)), pltpu.SemaphoreType.DMA((3 * n,)), pltpu.SemaphoreType.DMA((n,))],
    )(*parts)


def share_layers(fins):
    n = len(fins)

    def body(*refs):
        ins, outs = refs[:n], refs[n:2 * n]
        ssem, rsem, lsem = refs[2 * n:]
        x, y, c, _, _, _ = _place()
        local, sends = [], []
        for t in range(n):
            cp = pltpu.make_async_copy(ins[t], outs[t].at[c], lsem.at[t])
            cp.start()
            local.append(cp)
            cp = pltpu.make_async_remote_copy(src_ref=ins[t], dst_ref=outs[t].at[c], send_sem=ssem.at[t], recv_sem=rsem.at[t],
                                              device_id=(x, y, 1 - c), device_id_type=MESH)
            cp.start()
            sends.append(cp)
        for t in range(n):
            pltpu.make_async_remote_copy(src_ref=ins[t], dst_ref=outs[t].at[1 - c], send_sem=ssem.at[t], recv_sem=rsem.at[t],
                                         device_id=(x, y, 1 - c), device_id_type=MESH).wait_recv()
        for cp in sends:
            cp.wait_send()
        for cp in local:
            cp.wait()

    return pl.pallas_call(
        body, name="share_layers", in_specs=[HBM] * n, out_specs=[HBM] * n,
        out_shape=[jax.ShapeDtypeStruct((2,) + a.shape, a.dtype) for a in fins],
        scratch_shapes=[pltpu.SemaphoreType.DMA((n,)), pltpu.SemaphoreType.DMA((n,)), pltpu.SemaphoreType.DMA((n,))],
    )(*fins)


def allreduce_small(buf):
    r = buf.shape[0]
    n_dev = 8

    def body(in_ref, out_ref, gath, ssem, rsem):
        x, y, c = lax.axis_index("x"), lax.axis_index("y"), lax.axis_index("c")
        flip = lambda v, bit: 1 - v if bit else v
        me = 4 * x + 2 * y + c
        gath[me] = in_ref[...]
        peers = [(flip(x, k >> 2 & 1), flip(y, k >> 1 & 1), flip(c, k & 1)) for k in range(1, n_dev)]
        sends = []
        for k, peer in enumerate(peers):
            cp = pltpu.make_async_remote_copy(src_ref=in_ref, dst_ref=gath.at[me], send_sem=ssem.at[k], recv_sem=rsem.at[k],
                                              device_id=peer, device_id_type=MESH)
            cp.start()
            sends.append(cp)
        for k, (px, py, pc) in enumerate(peers):
            pltpu.make_async_remote_copy(src_ref=in_ref, dst_ref=gath.at[4 * px + 2 * py + pc], send_sem=ssem.at[k],
                                         recv_sem=rsem.at[k], device_id=(px, py, pc), device_id_type=MESH).wait_recv()
        for cp in sends:
            cp.wait_send()
        acc = gath[0]
        for i in range(1, n_dev):
            acc = acc + gath[i]
        out_ref[...] = acc

    return pl.pallas_call(
        body, name="allreduce_small",
        in_specs=[pl.BlockSpec(memory_space=pltpu.VMEM)], out_specs=pl.BlockSpec(memory_space=pltpu.VMEM),
        out_shape=jax.ShapeDtypeStruct(buf.shape, F32),
        scratch_shapes=[pltpu.VMEM((n_dev, r, 128), F32), pltpu.SemaphoreType.DMA((n_dev - 1,)), pltpu.SemaphoreType.DMA((n_dev - 1,))],
    )(buf)


def _rtile(r, pref=256):
    return _tile(r, pref, 16)


def chip_partial(name, g0, g1, recv):
    _, r, cdim = g0.shape
    tr = _rtile(r)
    spec = ((None, tr, cdim), lambda j, i: (j, i, 0))

    def fn(a, b, rv):
        mine = jnp.where(lax.axis_index("c") == 0, a, b)
        return mine + rv
    return tile_call(name, fn, (N_SHARDS, r // tr), [(g0,) + spec, (g1,) + spec, (recv,) + spec],
                     [(g0.shape, BF16) + spec + (None,)])[0]


def sum_chips(name, parts):
    _, r, cdim = parts.shape
    tr = _rtile(r)
    return tile_call(name, lambda p: (p[0].astype(F32) + p[1].astype(F32)) + (p[2].astype(F32) + p[3].astype(F32)),
                     (r // tr,), [(parts, (N_SHARDS, tr, cdim), lambda i: (0, i, 0))],
                     [((r, cdim), F32, (tr, cdim), lambda i: (i, 0), None)])[0]


def adamw_fn(w, g, m, v):
    m = ADAM_B1 * m + (1.0 - ADAM_B1) * g
    v = ADAM_B2 * v + (1.0 - ADAM_B2) * jnp.square(g)
    m_hat = m / (1.0 - ADAM_B1 ** ADAM_STEP)
    v_hat = v / (1.0 - ADAM_B2 ** ADAM_STEP)
    delta = -ADAM_LR * (m_hat / (jnp.sqrt(v_hat) + ADAM_EPS) + ADAM_WD * w)
    return delta, m, v


def adamw(name, w, g, m, v):
    nl, r, cdim = w.shape
    tr = _rtile(r, 128)
    spec = ((None, tr, cdim), lambda l, i: (l, i, 0))
    return tile_call(name, adamw_fn, (nl, r // tr), [(a,) + spec for a in (w, g, m, v)],
                     [(w.shape, F32) + spec + (None,)] * 3)


def _as3(a):
    return a.reshape(a.shape[0], -1, a.shape[-1])


def _pack_rows(vals):
    rows = []
    for v in vals:
        v = v.reshape(-1).astype(F32)
        pad = (-v.shape[0]) % 128
        rows.append(jnp.pad(v, (0, pad)).reshape(-1, 128))
    out = jnp.concatenate(rows, axis=0)
    return jnp.pad(out, ((0, (-out.shape[0]) % 8), (0, 0)))


def _unpack_rows(buf, like):
    outs, r = [], 0
    for a in like:
        n = a.size
        nr = -(-n // 128)
        outs.append(buf[r:r + nr].reshape(-1)[:n].reshape(a.shape))
        r += nr
    return outs


def kernel(x, mem, ffn1_norm, ffn1_w_gate, ffn1_w_up, ffn1_w_down, mix_norm, w_in, conv_w, a_log, dt_bias, dn_out_norm, pool_w, pool_scale, w_out, xattn_norm, mem_norm, xattn_wq, xattn_wkv, xattn_wo, ffn2_norm, ffn2_w_gate, ffn2_w_up, ffn2_w_down, final_norm, loss_target, m_ffn1_norm, m_ffn1_w_gate, m_ffn1_w_up, m_ffn1_w_down, m_mix_norm, m_w_in, m_conv_w, m_a_log, m_dt_bias, m_dn_out_norm, m_pool_w, m_pool_scale, m_w_out, m_xattn_norm, m_mem_norm, m_xattn_wq, m_xattn_wkv, m_xattn_wo, m_ffn2_norm, m_ffn2_w_gate, m_ffn2_w_up, m_ffn2_w_down, m_final_norm, v_ffn1_norm, v_ffn1_w_gate, v_ffn1_w_up, v_ffn1_w_down, v_mix_norm, v_w_in, v_conv_w, v_a_log, v_dt_bias, v_dn_out_norm, v_pool_w, v_pool_scale, v_w_out, v_xattn_norm, v_mem_norm, v_xattn_wq, v_xattn_wkv, v_xattn_wo, v_ffn2_norm, v_ffn2_w_gate, v_ffn2_w_up, v_ffn2_w_down, v_final_norm):
    given = dict(locals())
    w = {n: given[n] for n in WEIGHTS}
    m = {n: given["m_" + n] for n in WEIGHTS}
    v = {n: given["v_" + n] for n in WEIGHTS}
    shards = [_as3(w[n]).astype(F32 if n == "conv_w" else BF16) for n in SHARDED]
    gath = dict(zip(SHARDED, gather_weights(shards)))
    rep = {n: w[n] for n in REPLICATED}
    loss_row, dx, big, small, d_final = model_grads(x[0], mem[0], loss_target[0], gath, rep)
    recv = exchange_layers([big[0][n] for n in SHARDED], [big[1][n] for n in SHARDED])
    parts = [chip_partial("partial_" + n, big[0][n], big[1][n], r) for n, r in zip(SHARDED, recv)]
    parts = exchange_shards(parts)
    fins = [sum_chips("sum_" + n, p) for n, p in zip(SHARDED, parts)]
    grads = dict(zip(SHARDED, share_layers(fins)))
    out_g, out_d, out_m, out_v = {}, {}, {}, {}
    for n in SHARDED:
        d_, m_, v_ = adamw("adamw_" + n, _as3(w[n]), grads[n], _as3(m[n]), _as3(v[n]))
        out_g[n], out_d[n], out_m[n], out_v[n] = (a.reshape(w[n].shape) for a in (grads[n], d_, m_, v_))
    rep_names = [n for n in REPLICATED if n != "final_norm"]
    g_rep = {n: jnp.stack([small[l][n][0, :w[n].shape[1]] for l in range(len(small))]) for n in rep_names}
    g_rep["final_norm"] = d_final[0]
    like = [w[n] for n in REPLICATED] + [jnp.zeros((1,), F32)]
    summed = allreduce_small(_pack_rows([g_rep[n] for n in REPLICATED] + [loss_row[0, :1]]))
    pk = lambda tree: _pack_rows([tree[n] for n in REPLICATED] + [jnp.zeros((1,), F32)])
    wp, mp, vp = pk(w), pk(m), pk(v)
    dp, mp2, vp2 = adamw("adamw_small", wp[None], summed[None], mp[None], vp[None])
    for buf, dst in ((summed, out_g), (dp[0], out_d), (mp2[0], out_m), (vp2[0], out_v)):
        for n, a in zip(REPLICATED, _unpack_rows(buf, like)):
            dst[n] = a
    loss = _unpack_rows(summed, like)[-1][0]
    return (loss, dx[None], *[out_g[n] for n in WEIGHTS], *[out_d[n] for n in WEIGHTS],
            *[out_m[n] for n in WEIGHTS], *[out_v[n] for n in WEIGHTS])
```

```python
import functools

import jax
import jax.numpy as jnp
from jax import lax
from jax.experimental import pallas as pl
from jax.experimental.pallas import tpu as pltpu

F32, BF16 = jnp.float32, jnp.bfloat16
HI = lax.Precision.HIGHEST
MESH = pl.DeviceIdType.MESH

EPS = 1e-6
DN_HEADS = 8
HEAD_DIM = 128
X_HEADS = 4
POOL_WINDOWS = (2, 4, 8, 16)
CONV_WIDTH = 4
CHUNK = 64
N_SHARDS = 4
AB_PAD = 128
ADAM_LR, ADAM_B1, ADAM_B2, ADAM_EPS, ADAM_WD, ADAM_STEP = 0.001, 0.9, 0.999, 1e-08, 0.01, 10
VMEM_LIMIT = 56 << 20
ROW_TILE = 256


def _tile(n, pref, unit=128):
    best = None
    for t in range(unit, min(n, pref) + 1, unit):
        if n % t == 0:
            best = t
    return best if best is not None else n


def _params():
    return pltpu.CompilerParams(vmem_limit_bytes=VMEM_LIMIT)


def _split(a):
    a = a.astype(F32)
    head = a.astype(BF16)
    return head, (a - head.astype(F32)).astype(BF16)


def _dg(a, b, ca, cb, hi):
    dims = (((ca,), (cb,)), ((), ()))
    dot = lambda u, v: lax.dot_general(u, v, dims, preferred_element_type=F32)
    if hi:
        (a0, a1), (b0, b1) = _split(a), _split(b)
        return dot(a0, b0) + (dot(a0, b1) + dot(a1, b0))
    return dot(a.astype(BF16), b.astype(BF16))


@functools.partial(jax.custom_vjp, nondiff_argnums=(2, 3, 4))
def mmul(a, b, ca, cb, hi):
    return _dg(a, b, ca, cb, hi)


def _mmul_fwd(a, b, ca, cb, hi):
    return _dg(a, b, ca, cb, hi), (a, b)


def _mmul_bwd(ca, cb, hi, res, g):
    a, b = res
    da = _dg(g, b, 1, 1 - cb, hi) if ca == 1 else _dg(b, g, 1 - cb, 1, hi)
    db = _dg(a, g, 1 - ca, 0, hi) if cb == 0 else _dg(g, a, 0, 1 - ca, hi)
    return da.astype(a.dtype), db.astype(b.dtype)


mmul.defvjp(_mmul_fwd, _mmul_bwd)


@functools.partial(jax.custom_vjp, nondiff_argnums=(1,))
def shift_down(x, s):
    t = lax.broadcasted_iota(jnp.int32, x.shape, 0)
    return jnp.where(t >= s, pltpu.roll(x, s, 0), 0.0)


def _shift_up(x, s):
    n = x.shape[0]
    t = lax.broadcasted_iota(jnp.int32, x.shape, 0)
    return jnp.where(t < n - s, pltpu.roll(x, n - s, 0), 0.0)


shift_down.defvjp(lambda x, s: (shift_down(x, s), None), lambda s, _, g: (_shift_up(g, s),))


def sigmoid(x):
    return 0.5 * (jnp.tanh(0.5 * x) + 1.0)


def silu(x):
    return x * sigmoid(x)


@jax.custom_vjp
def softplus(x):
    u = jnp.exp(-jnp.abs(x))
    w = 1.0 + u
    log1p = jnp.where(w == 1.0, u, jnp.log(w) * u / jnp.where(w == 1.0, 1.0, w - 1.0))
    return jnp.maximum(x, 0.0) + log1p


softplus.defvjp(lambda x: (softplus(x), x), lambda x, g: (g * sigmoid(x),))


def rms(x, g):
    x = x.astype(F32)
    return x * lax.rsqrt(jnp.mean(x * x, axis=-1, keepdims=True) + EPS) * g


def swiglu(gate, up):
    return silu(gate) * up


def vjp_of(fn, n_in, diff, has_pids=False):
    def g(*args):
        pids = None
        if has_pids:
            pids, args = args[0], args[1:]
        ins, cots = list(args[:n_in]), args[n_in:]

        def f(*d):
            full = list(ins)
            for i, v in zip(diff, d):
                full[i] = v
            return fn(pids, *full) if has_pids else fn(*full)

        out, pull = jax.vjp(f, *[ins[i].astype(F32) for i in diff])
        if isinstance(out, (tuple, list)):
            return pull(tuple(c.astype(o.dtype) for c, o in zip(cots, out)))
        return pull(cots[0].astype(out.dtype))
    return g


def tile_call(name, fn, grid, ins, outs, with_pids=False, prefetch=None):
    n_in = len(ins)

    def body(*refs):
        if prefetch is not None:
            refs = refs[1:]
        pids = tuple(pl.program_id(a) for a in range(len(grid)))
        vals = [r[...] for r in refs[:n_in]]
        res = fn(pids, *vals) if with_pids else fn(*vals)
        if not isinstance(res, (tuple, list)):
            res = (res,)
        for r, o, spec in zip(res, refs[n_in:], outs):
            acc = spec[4]
            if acc is None:
                o[...] = r.astype(o.dtype)
            else:
                first = functools.reduce(jnp.logical_and, [pids[a] == 0 for a in acc])

                @pl.when(first)
                def _():
                    o[...] = r.astype(o.dtype)

                @pl.when(jnp.logical_not(first))
                def _():
                    o[...] += r.astype(o.dtype)

    in_specs = [pl.BlockSpec(b, im) for _, b, im in ins]
    out_specs = [pl.BlockSpec(s[2], s[3]) for s in outs]
    out_shape = [jax.ShapeDtypeStruct(s[0], s[1]) for s in outs]
    if prefetch is None:
        return pl.pallas_call(body, name=name, grid=grid, in_specs=in_specs, out_specs=out_specs, out_shape=out_shape,
                              compiler_params=_params())(*[a for a, _, _ in ins])
    spec = pltpu.PrefetchScalarGridSpec(num_scalar_prefetch=1, grid=grid, in_specs=in_specs, out_specs=out_specs)
    return pl.pallas_call(body, name=name, grid_spec=spec, out_shape=out_shape,
                          compiler_params=_params())(prefetch, *[a for a, _, _ in ins])


def mm_call(name, grid, ins, pairs, n_acc, acc_shape, outs, epilogue, extras=()):
    n_in, n_ex, nk = len(ins), len(extras), grid[2]

    def finish(accs, ex_refs, out_refs):
        res = epilogue(accs, *[r[...] for r in ex_refs])
        if not isinstance(res, (tuple, list)):
            res = (res,)
        for r, o in zip(res, out_refs):
            o[...] = r.astype(o.dtype)

    def body(*refs):
        in_refs, ex_refs = refs[:n_in], refs[n_in:n_in + n_ex]
        out_refs = refs[n_in + n_ex:n_in + n_ex + len(outs)]
        accs = refs[n_in + n_ex + len(outs):]
        if nk == 1:
            vals = [None] * n_acc
            for ia, ib, ca, cb, ai in pairs:
                d = _dg(in_refs[ia][...], in_refs[ib][...], ca, cb, False)
                vals[ai] = d if vals[ai] is None else vals[ai] + d
            finish(vals, ex_refs, out_refs)
            return
        k = pl.program_id(2)

        @pl.when(k == 0)
        def _():
            for a in accs:
                a[...] = jnp.zeros_like(a)

        for ia, ib, ca, cb, ai in pairs:
            accs[ai][...] += _dg(in_refs[ia][...], in_refs[ib][...], ca, cb, False)

        @pl.when(k == nk - 1)
        def _():
            finish([a[...] for a in accs], ex_refs, out_refs)

    return pl.pallas_call(
        body, name=name, grid=grid,
        in_specs=[pl.BlockSpec(b, im) for _, b, im in list(ins) + list(extras)],
        out_specs=[pl.BlockSpec(s[2], s[3]) for s in outs],
        out_shape=[jax.ShapeDtypeStruct(s[0], s[1]) for s in outs],
        scratch_shapes=[pltpu.VMEM(acc_shape, F32) for _ in range(n_acc if nk > 1 else 0)],
        compiler_params=_params(),
    )(*[a for a, _, _ in list(ins) + list(extras)])


MM_VMEM_BUDGET = 40 << 20


def _mm_tiles(m, n, kk, sa, sb, so, has_res):
    tk = _tile(kk, 2048)
    best = None
    for tm in (1024, 512, 256, 128):
        for tn in (1024, 512, 256, 128):
            tm_, tn_ = _tile(m, tm), _tile(n, tn)
            need = 2 * (tm_ * tk * sa + tk * tn_ * sb + tm_ * tn_ * so) + (tm_ * tn_ * 4 if tk < kk else 0)
            need += 2 * tm_ * tn_ * 4 if has_res else 0
            if need <= MM_VMEM_BUDGET and (best is None or tm_ * tn_ > best[0] * best[1]):
                best = (tm_, tn_)
    return best + (tk,)


def mm2(name, a, b, ca, cb, res=None, scale=None, out_dtype=F32):
    m, kk, n = a.shape[1 - ca], a.shape[ca], b.shape[1 - cb]
    tm, tn, tk = _mm_tiles(m, n, kk, a.dtype.itemsize, b.dtype.itemsize, jnp.dtype(out_dtype).itemsize, res is not None)
    a_spec = ((tm, tk), lambda i, j, k: (i, k)) if ca == 1 else ((tk, tm), lambda i, j, k: (k, i))
    b_spec = ((tk, tn), lambda i, j, k: (k, j)) if cb == 0 else ((tn, tk), lambda i, j, k: (j, k))
    extras = [] if res is None else [(res, (tm, tn), lambda i, j, k: (i, j))]

    def epi(accs, *ex):
        r = accs[0] if scale is None else accs[0] * scale
        return r + ex[0] if ex else r

    return mm_call(name, (m // tm, n // tn, kk // tk), [(a,) + a_spec, (b,) + b_spec], [(0, 1, ca, cb, 0)], 1, (tm, tn),
                   [((m, n), out_dtype, (tm, tn), lambda i, j, k: (i, j))], epi, extras)[0]


def rows_call(name, fn, rows, consts, outs, acc_outs=(), tr=ROW_TILE):
    s = rows[0].shape[0]
    tr = _tile(s, tr, 8)
    ins = [(r, (tr, r.shape[1]), lambda i: (i, 0)) for r in rows]
    ins += [(c, c.shape, (lambda nd: (lambda i: (0,) * nd))(c.ndim)) for c in consts]
    o = [((s, c), dt, (tr, c), lambda i: (i, 0), None) for c, dt in outs]
    o += [(shp, F32, shp, (lambda nd: (lambda i: (0,) * nd))(len(shp)), (0,)) for shp in acc_outs]
    return tile_call(name, fn, (s // tr,), ins, o)


def _lane_pick(x, h):
    lane = lax.broadcasted_iota(jnp.int32, x.shape, x.ndim - 1)
    return jnp.sum(jnp.where(lane == h, x, 0.0), axis=-1, keepdims=True)


def gateprep_fn(ab, alog, dtb):
    t = ab.shape[0]
    gs, bs = [], []
    for h in range(DN_HEADS):
        a_h = _lane_pick(ab, h)
        b_h = _lane_pick(ab, DN_HEADS + h)
        g_h = -jnp.exp(_lane_pick(alog, h)) * softplus(a_h + _lane_pick(dtb, h))
        gs.append(jnp.broadcast_to(g_h, (t, HEAD_DIM)))
        bs.append(jnp.broadcast_to(sigmoid(b_h), (t, HEAD_DIM)))
    return jnp.concatenate(gs, axis=1), jnp.concatenate(bs, axis=1)


def conv_fn(pids, x, w):
    kind = pids[0] // DN_HEADS
    y = x * w[CONV_WIDTH - 1:CONV_WIDTH]
    for i in range(CONV_WIDTH - 1):
        y = y + shift_down(x, CONV_WIDTH - 1 - i) * w[i:i + 1]
    y = silu(y)
    n = y * lax.rsqrt(jnp.sum(y * y, axis=-1, keepdims=True) + EPS)
    n = n * jnp.where(kind == 0, HEAD_DIM ** -0.5, 1.0)
    return jnp.where(kind == 2, y, n)


def unit_lower_inverse(a):
    c = a.shape[0]
    r = lax.broadcasted_iota(jnp.int32, (c, c), 0)
    cc = lax.broadcasted_iota(jnp.int32, (c, c), 1)
    x = -a
    t = jnp.where(r == cc, 1.0, 0.0) + x
    p = 2
    while p < c:
        x = _dg(x, x, 1, 0, True)
        t = t + _dg(t, x, 1, 0, True)
        p *= 2
    return t


@jax.custom_vjp
def known_inverse(a, t):
    return t


known_inverse.defvjp(lambda a, t: (t, t),
                     lambda t, g: (-_dg(_dg(t, g, 0, 0, True), t, 1, 1, True), jnp.zeros_like(t)))


def intra_head(q, k, v, g, b, t_known):
    c = q.shape[0]
    r = lax.broadcasted_iota(jnp.int32, (c, c), 0)
    cc = lax.broadcasted_iota(jnp.int32, (c, c), 1)
    tril = (r >= cc).astype(F32)
    gc = mmul(tril, g, 1, 0, True)
    m = gc[:, :c]
    decay = jnp.exp(jnp.where(r >= cc, m - m.T, -1e30))
    kb = k * b
    a = jnp.where(r > cc, mmul(kb, k, 1, 1, True) * decay, 0.0)
    t = unit_lower_inverse(a) if t_known is None else known_inverse(a, t_known)
    e = jnp.exp(gc)
    u = mmul(t, v * b, 1, 0, True)
    w = mmul(t, kb * e, 1, 0, True)
    qk = mmul(q, k, 1, 1, True) * decay
    gl = gc[c - 1:c, :]
    kd = k * jnp.exp(gl - gc)
    outs = (u, w, qk, q * e, kd, jnp.broadcast_to(jnp.exp(gl), (8, HEAD_DIM)))
    return outs + (t,) if t_known is None else outs


def _heads(x, h):
    return x[:, h * HEAD_DIM:(h + 1) * HEAD_DIM]


def intra_fn(q, k, v, g, b):
    outs = [intra_head(_heads(q, h), _heads(k, h), _heads(v, h), _heads(g, h), _heads(b, h), None) for h in range(DN_HEADS)]
    cat = lambda i: jnp.concatenate([o[i] for o in outs], axis=1)
    stack = lambda i: jnp.stack([o[i] for o in outs], axis=0)
    return cat(0), cat(1), stack(2), cat(3), cat(4), cat(5), stack(6)


def intra_bwd_fn(q, k, v, g, b, tinv, du, dw, dqk, dqd, dkd, dgl):
    res = []
    for h in range(DN_HEADS):
        hs = lambda x: _heads(x, h)
        res.append(vjp_of(intra_head, 6, (0, 1, 2, 3, 4))(hs(q), hs(k), hs(v), hs(g), hs(b), tinv[h],
                                                         hs(du), hs(dw), dqk[h], hs(dqd), hs(dkd), hs(dgl)))
    return tuple(jnp.concatenate([r[i] for r in res], axis=1) for i in range(5))


def scan_step(s, u, w, qk, qd, kd, gl):
    v_new = u - mmul(w, s, 1, 0, True)
    o = mmul(qd, s, 1, 0, True) + mmul(qk, v_new, 1, 0, True)
    return s * gl[0:1, :] + mmul(kd, v_new, 0, 0, True), o


def outgate_fn(o, z, g):
    return rms(o, g) * silu(z)


def pool_fn(pids, p):
    gid = pids[0]
    s = p.shape[0]
    t1 = (lax.broadcasted_iota(jnp.int32, p.shape, 0) + 1).astype(F32)
    acc, win, out = p, 1, None
    for gi, target in enumerate(POOL_WINDOWS):
        while win < target:
            acc = acc + shift_down(acc, win)
            win *= 2
        cand = acc / jnp.minimum(t1, float(target))
        out = cand if out is None else jnp.where(gid == gi, cand, out)
    return out - p


def poolmix_fn(pooled, pw, scale):
    return mmul(pooled, pw, 1, 0, False) * scale


def attn_fn(q, k, v):
    s = mmul(q, k, 1, 1, False) * (q.shape[1] ** -0.5)
    s = s - jnp.max(s, axis=-1, keepdims=True)
    e = jnp.exp(s)
    p = e / jnp.sum(e, axis=-1, keepdims=True)
    return mmul(p, v, 1, 0, False)


def rms_fwd(name, x, g):
    return rows_call(name, lambda a, b: rms(a, b), [x], [g], [(x.shape[1], BF16)])[0]


def rms_bwd(name, x, g, dy, dres):
    def fn(a, d, r, b):
        dx, dg = vjp_of(rms, 2, (0, 1))(a, b, d)
        return dx + r, dx + r, dg
    return rows_call(name, fn, [x, dy, dres], [g], [(x.shape[1], F32), (x.shape[1], BF16)], [g.shape])


def ffn_fwd(tag, x, g, wg, wu, wd, l):
    s, d = x.shape
    fj = wg.shape[-1]
    f = N_SHARDS * fj
    xn = rms_fwd(tag + "_norm", x, g)
    tm = _tile(s, 256)
    w_spec = ((None, None, d, fj), lambda j, i, k: (j, l, 0, 0))
    o_spec = ((tm, fj), lambda j, i, k: (i, j))
    gate, up, act = mm_call(
        tag + "_gu", (N_SHARDS, s // tm, 1),
        [(xn, (tm, d), lambda j, i, k: (i, 0)), (wg,) + w_spec, (wu,) + w_spec],
        [(0, 1, 1, 0, 0), (0, 2, 1, 0, 1)], 2, (tm, fj),
        [((s, f), F32) + o_spec, ((s, f), F32) + o_spec, ((s, f), BF16) + o_spec],
        lambda accs: (accs[0], accs[1], swiglu(accs[0], accs[1])))
    tm, tn = _tile(s, 1024), _tile(d, 1024)
    out = mm_call(
        tag + "_down", (s // tm, d // tn, N_SHARDS),
        [(act, (tm, fj), lambda i, j, k: (i, k)), (wd, (None, None, fj, tn), lambda i, j, k: (k, l, 0, j))],
        [(0, 1, 1, 0, 0)], 1, (tm, tn),
        [((s, d), F32, (tm, tn), lambda i, j, k: (i, j))],
        lambda accs, r: r + 0.5 * accs[0], [(x, (tm, tn), lambda i, j, k: (i, j))])[0]
    return out, (x, xn, gate, up, act)


def ffn_bwd(tag, saved, g, wg, wu, wd, l, dout, dout_b):
    x, xn, gate, up, act = saved
    s, d = x.shape
    fj = wg.shape[-1]
    f = N_SHARDS * fj
    tm = _tile(s, 512)
    o_spec = ((tm, fj), lambda j, i, k: (i, j))

    def epi(accs, ga, u):
        dgate, dup = vjp_of(swiglu, 2, (0, 1))(ga, u, 0.5 * accs[0])
        return dgate, dup

    dgate, dup = mm_call(
        tag + "_dact", (N_SHARDS, s // tm, 1),
        [(dout_b, (tm, d), lambda j, i, k: (i, 0)), (wd, (None, None, fj, d), lambda j, i, k: (j, l, 0, 0))],
        [(0, 1, 1, 1, 0)], 1, (tm, fj),
        [((s, f), BF16) + o_spec, ((s, f), BF16) + o_spec], epi,
        [(gate,) + o_spec, (up,) + o_spec])
    tn = _tile(d, 1024)
    dwd = mm_call(
        tag + "_dwd", (N_SHARDS, d // tn, 1),
        [(act, (s, fj), lambda j, i, k: (0, j)), (dout_b, (s, tn), lambda j, i, k: (0, i))],
        [(0, 1, 0, 0, 0)], 1, (fj, tn),
        [((N_SHARDS, fj, d), F32, (None, fj, tn), lambda j, i, k: (j, 0, i))],
        lambda accs: 0.5 * accs[0])[0]
    td = _tile(d, 512)
    g_spec = ((s, fj), lambda j, i, k: (0, j))
    w_out = ((N_SHARDS, d, fj), F32, (None, td, fj), lambda j, i, k: (j, i, 0))
    dwg, dwu = mm_call(
        tag + "_dwgu", (N_SHARDS, d // td, 1),
        [(xn, (s, td), lambda j, i, k: (0, i)), (dgate,) + g_spec, (dup,) + g_spec],
        [(0, 1, 0, 0, 0), (0, 2, 0, 0, 1)], 2, (td, fj), [w_out, w_out], lambda accs: (accs[0], accs[1]))
    tm, tn = _tile(s, 1024), _tile(d, 1024)
    a_spec = ((tm, fj), lambda i, j, k: (i, k))
    wt_spec = ((None, None, tn, fj), lambda i, j, k: (k, l, j, 0))
    dxn = mm_call(
        tag + "_dxn", (s // tm, d // tn, N_SHARDS),
        [(dgate,) + a_spec, (wg,) + wt_spec, (dup,) + a_spec, (wu,) + wt_spec],
        [(0, 1, 1, 1, 0), (2, 3, 1, 1, 0)], 1, (tm, tn),
        [((s, d), F32, (tm, tn), lambda i, j, k: (i, j))], lambda accs: accs[0])[0]
    dx, dx_b, dg = rms_bwd(tag + "_dnorm", x, g, dxn, dout)
    return dx, dx_b, dg, dwg, dwu, dwd


def _hspec(tc, width=HEAD_DIM):
    return (tc, width)


def mixer_fwd(tag, h, wts):
    s, d = h.shape
    dnw = DN_HEADS * HEAD_DIM
    pw_ = d - dnw
    gdim = pw_ // len(POOL_WINDOWS)
    nc = s // CHUNK
    hn = rms_fwd(tag + "_norm", h, wts["mix_norm"])
    qkv = mm2(tag + "_qkv", hn, wts["w_qkv"], 1, 0)
    z = mm2(tag + "_z", hn, wts["w_z"], 1, 0)
    ab = mm2(tag + "_ab", hn, wts["w_ab"], 1, 0)
    p = mm2(tag + "_p", hn, wts["w_p"], 1, 0)
    qkvn = tile_call(tag + "_conv", conv_fn, (3 * DN_HEADS,),
                     [(qkv, (s, HEAD_DIM), lambda i: (0, i)), (wts["conv_w"], (CONV_WIDTH, HEAD_DIM), lambda i: (0, i))],
                     [((s, 3 * dnw), F32, (s, HEAD_DIM), lambda i: (0, i), None)], with_pids=True)[0]
    g_bc, b_bc = rows_call(tag + "_gates", gateprep_fn, [ab], [wts["a_log"], wts["dt_bias"]], [(dnw, F32), (dnw, F32)])
    cw = (CHUNK, dnw)
    sq = ((DN_HEADS, s, CHUNK), F32, (DN_HEADS, CHUNK, CHUNK), lambda n: (0, n, 0), None)
    u, w, qk, qd, kd, gl, tinv = tile_call(
        tag + "_intra", intra_fn, (nc,),
        [(qkvn, cw, lambda n: (n, 0)), (qkvn, cw, lambda n: (n, 1)), (qkvn, cw, lambda n: (n, 2)),
         (g_bc, cw, lambda n: (n, 0)), (b_bc, cw, lambda n: (n, 0))],
        [((s, dnw), F32, cw, lambda n: (n, 0), None), ((s, dnw), F32, cw, lambda n: (n, 0), None), sq,
         ((s, dnw), F32, cw, lambda n: (n, 0), None), ((s, dnw), F32, cw, lambda n: (n, 0), None),
         ((nc * 8, dnw), F32, (8, dnw), lambda n: (n, 0), None), sq])
    o, states = scan_fwd(tag + "_scan", u, w, qk, qd, kd, gl)
    y_dn = tile_call(
        tag + "_outgate", outgate_fn, (DN_HEADS, s // ROW_TILE),
        [(o, (ROW_TILE, HEAD_DIM), lambda hh, i: (i, hh)), (z, (ROW_TILE, HEAD_DIM), lambda hh, i: (i, hh)),
         (wts["dn_out_norm"], (1, HEAD_DIM), lambda hh, i: (0, 0))],
        [((s, dnw), BF16, (ROW_TILE, HEAD_DIM), lambda hh, i: (i, hh), None)])[0]
    ng = len(POOL_WINDOWS)
    pooled = tile_call(tag + "_pool", pool_fn, (ng,), [(p, (s, gdim), lambda i: (0, i))],
                       [((s, pw_), BF16, (s, gdim), lambda i: (0, i), None)], with_pids=True)[0]
    tp = _tile(s, 512)
    y_pool = tile_call(
        tag + "_poolmix", poolmix_fn, (ng, s // tp),
        [(pooled, (tp, gdim), lambda gi, i: (i, gi)), (wts["pool_w"], (None, gdim, gdim), lambda gi, i: (gi, 0, 0)),
         (wts["pool_scale"], (1, gdim), lambda gi, i: (0, gi))],
        [((s, pw_), BF16, (tp, gdim), lambda gi, i: (i, gi), None)])[0]
    h1 = mm2(tag + "_out_a", y_dn, wts["w_out_a"], 1, 0, res=h)
    h2 = mm2(tag + "_out_b", y_pool, wts["w_out_b"], 1, 0, res=h1)
    saved = (h, hn, qkv, z, ab, p, qkvn, g_bc, b_bc, u, w, qk, qd, kd, gl, tinv, o, states, y_dn, pooled, y_pool)
    return h2, saved


def scan_fwd(name, u, w, qk, qd, kd, gl):
    s, dnw = u.shape
    nc = s // CHUNK
    cw = (CHUNK, dnw)

    def body(u_r, w_r, qk_r, qd_r, kd_r, gl_r, o_r, st_r, state):
        @pl.when(pl.program_id(0) == 0)
        def _():
            state[...] = jnp.zeros_like(state)

        st_r[...] = state[...]
        outs = []
        for h in range(DN_HEADS):
            hs = slice(h * HEAD_DIM, (h + 1) * HEAD_DIM)
            s_new, o_h = scan_step(state[hs, :], u_r[:, hs], w_r[:, hs], qk_r[h], qd_r[:, hs], kd_r[:, hs], gl_r[:, hs])
            state[hs, :] = s_new
            outs.append(o_h)
        o_r[...] = jnp.concatenate(outs, axis=1)

    row = lambda n: (n, 0)
    return pl.pallas_call(
        body, name=name, grid=(nc,),
        in_specs=[pl.BlockSpec(cw, row), pl.BlockSpec(cw, row), pl.BlockSpec((DN_HEADS, CHUNK, CHUNK), lambda n: (0, n, 0)),
                  pl.BlockSpec(cw, row), pl.BlockSpec(cw, row), pl.BlockSpec((8, dnw), row)],
        out_specs=[pl.BlockSpec(cw, row), pl.BlockSpec((None, dnw, HEAD_DIM), lambda n: (n, 0, 0))],
        out_shape=[jax.ShapeDtypeStruct((s, dnw), F32), jax.ShapeDtypeStruct((nc, dnw, HEAD_DIM), F32)],
        scratch_shapes=[pltpu.VMEM((dnw, HEAD_DIM), F32)],
        compiler_params=_params(),
    )(u, w, qk, qd, kd, gl)


def scan_bwd(name, states, u, w, qk, qd, kd, gl, do):
    s, dnw = u.shape
    nc = s // CHUNK
    cw = (CHUNK, dnw)

    def body(st_r, u_r, w_r, qk_r, qd_r, kd_r, gl_r, do_r, du_r, dw_r, dqk_r, dqd_r, dkd_r, dgl_r, dstate):
        @pl.when(pl.program_id(0) == 0)
        def _():
            dstate[...] = jnp.zeros_like(dstate)

        res = []
        for h in range(DN_HEADS):
            hs = slice(h * HEAD_DIM, (h + 1) * HEAD_DIM)
            r = vjp_of(scan_step, 7, tuple(range(7)))(
                st_r[hs, :], u_r[:, hs], w_r[:, hs], qk_r[h], qd_r[:, hs], kd_r[:, hs], gl_r[:, hs],
                dstate[hs, :], do_r[:, hs])
            dstate[hs, :] = r[0]
            res.append(r)
        cat = lambda i: jnp.concatenate([r[i] for r in res], axis=1)
        du_r[...] = cat(1)
        dw_r[...] = cat(2)
        dqk_r[...] = jnp.stack([r[3] for r in res], axis=0)
        dqd_r[...] = cat(4)
        dkd_r[...] = cat(5)
        dgl_r[...] = cat(6)

    row = lambda n: (nc - 1 - n, 0)
    qk_spec = pl.BlockSpec((DN_HEADS, CHUNK, CHUNK), lambda n: (0, nc - 1 - n, 0))
    return pl.pallas_call(
        body, name=name, grid=(nc,),
        in_specs=[pl.BlockSpec((None, dnw, HEAD_DIM), lambda n: (nc - 1 - n, 0, 0)), pl.BlockSpec(cw, row), pl.BlockSpec(cw, row),
                  qk_spec, pl.BlockSpec(cw, row), pl.BlockSpec(cw, row), pl.BlockSpec((8, dnw), row), pl.BlockSpec(cw, row)],
        out_specs=[pl.BlockSpec(cw, row), pl.BlockSpec(cw, row), qk_spec, pl.BlockSpec(cw, row), pl.BlockSpec(cw, row),
                   pl.BlockSpec((8, dnw), row)],
        out_shape=[jax.ShapeDtypeStruct((s, dnw), F32), jax.ShapeDtypeStruct((s, dnw), F32),
                   jax.ShapeDtypeStruct((DN_HEADS, s, CHUNK), F32), jax.ShapeDtypeStruct((s, dnw), F32),
                   jax.ShapeDtypeStruct((s, dnw), F32), jax.ShapeDtypeStruct((nc * 8, dnw), F32)],
        scratch_shapes=[pltpu.VMEM((dnw, HEAD_DIM), F32)],
        compiler_params=_params(),
    )(states, u, w, qk, qd, kd, gl, do)


def mixer_bwd(tag, saved, wts, dout, dout_b):
    h, hn, qkv, z, ab, p, qkvn, g_bc, b_bc, u, w, qk, qd, kd, gl, tinv, o, states, y_dn, pooled, y_pool = saved
    s, d = h.shape
    dnw = DN_HEADS * HEAD_DIM
    pw_ = d - dnw
    ng = len(POOL_WINDOWS)
    gdim = pw_ // ng
    nc = s // CHUNK
    gr = {}
    d_ydn = mm2(tag + "_dydn", dout_b, wts["w_out_a"], 1, 1)
    d_ypool = mm2(tag + "_dypool", dout_b, wts["w_out_b"], 1, 1)
    gr["w_out_a"] = mm2(tag + "_dwout_a", y_dn, dout_b, 0, 0)
    gr["w_out_b"] = mm2(tag + "_dwout_b", y_pool, dout_b, 0, 0)
    tp = _tile(s, 512)
    d_pooled, gr["pool_w"], gr["pool_scale"] = tile_call(
        tag + "_dpoolmix", vjp_of(poolmix_fn, 3, (0, 1, 2)), (ng, s // tp),
        [(pooled, (tp, gdim), lambda gi, i: (i, gi)), (wts["pool_w"], (None, gdim, gdim), lambda gi, i: (gi, 0, 0)),
         (wts["pool_scale"], (1, gdim), lambda gi, i: (0, gi)), (d_ypool, (tp, gdim), lambda gi, i: (i, gi))],
        [((s, pw_), F32, (tp, gdim), lambda gi, i: (i, gi), None),
         ((ng, gdim, gdim), F32, (None, gdim, gdim), lambda gi, i: (gi, 0, 0), (1,)),
         ((1, pw_), F32, (1, gdim), lambda gi, i: (0, gi), (1,))])
    d_p = tile_call(tag + "_dpool", vjp_of(pool_fn, 1, (0,), True), (ng,),
                    [(p, (s, gdim), lambda i: (0, i)), (d_pooled, (s, gdim), lambda i: (0, i))],
                    [((s, pw_), BF16, (s, gdim), lambda i: (0, i), None)], with_pids=True)[0]
    hb = (ROW_TILE, HEAD_DIM)
    d_o, d_z, gr["dn_out_norm"] = tile_call(
        tag + "_doutgate", vjp_of(outgate_fn, 3, (0, 1, 2)), (DN_HEADS, s // ROW_TILE),
        [(o, hb, lambda hh, i: (i, hh)), (z, hb, lambda hh, i: (i, hh)), (wts["dn_out_norm"], (1, HEAD_DIM), lambda hh, i: (0, 0)),
         (d_ydn, hb, lambda hh, i: (i, hh))],
        [((s, dnw), F32, hb, lambda hh, i: (i, hh), None), ((s, dnw), BF16, hb, lambda hh, i: (i, hh), None),
         ((1, HEAD_DIM), F32, (1, HEAD_DIM), lambda hh, i: (0, 0), (0, 1))])
    du, dw, dqk, dqd, dkd, dgl = scan_bwd(tag + "_dscan", states, u, w, qk, qd, kd, gl, d_o)
    cw = (CHUNK, dnw)
    row = lambda n: (n, 0)
    dq, dk, dv, dg_bc, db_bc = tile_call(
        tag + "_dintra", intra_bwd_fn, (nc,),
        [(qkvn, cw, lambda n: (n, 0)), (qkvn, cw, lambda n: (n, 1)), (qkvn, cw, lambda n: (n, 2)),
         (g_bc, cw, row), (b_bc, cw, row), (tinv, (DN_HEADS, CHUNK, CHUNK), lambda n: (0, n, 0)), (du, cw, row), (dw, cw, row),
         (dqk, (DN_HEADS, CHUNK, CHUNK), lambda n: (0, n, 0)), (dqd, cw, row), (dkd, cw, row), (dgl, (8, dnw), row)],
        [((s, dnw), F32, cw, row, None)] * 5)
    d_ab, gr["a_log"], gr["dt_bias"] = rows_call(
        tag + "_dgates", lambda a, dg, db, al, dt: vjp_of(gateprep_fn, 3, (0, 1, 2))(a, al, dt, dg, db),
        [ab, dg_bc, db_bc], [wts["a_log"], wts["dt_bias"]], [(AB_PAD, BF16)], [(1, AB_PAD), (1, AB_PAD)])
    d_qkvn = jnp.concatenate([dq, dk, dv], axis=1)
    d_qkv, gr["conv_w"] = tile_call(
        tag + "_dconv", vjp_of(conv_fn, 2, (0, 1), True), (3 * DN_HEADS,),
        [(qkv, (s, HEAD_DIM), lambda i: (0, i)), (wts["conv_w"], (CONV_WIDTH, HEAD_DIM), lambda i: (0, i)),
         (d_qkvn, (s, HEAD_DIM), lambda i: (0, i))],
        [((s, 3 * dnw), BF16, (s, HEAD_DIM), lambda i: (0, i), None),
         ((CONV_WIDTH, 3 * dnw), F32, (CONV_WIDTH, HEAD_DIM), lambda i: (0, i), None)], with_pids=True)
    gr["w_qkv"] = mm2(tag + "_dwqkv", hn, d_qkv, 0, 0)
    gr["w_z"] = mm2(tag + "_dwz", hn, d_z, 0, 0)
    gr["w_ab"] = mm2(tag + "_dwab", hn, d_ab, 0, 0)
    gr["w_p"] = mm2(tag + "_dwp", hn, d_p, 0, 0)
    d_hn = mm2(tag + "_dhn1", d_qkv, wts["w_qkv"], 1, 1)
    d_hn = mm2(tag + "_dhn2", d_z, wts["w_z"], 1, 1, res=d_hn)
    d_hn = mm2(tag + "_dhn3", d_ab, wts["w_ab"], 1, 1, res=d_hn)
    d_hn = mm2(tag + "_dhn4", d_p, wts["w_p"], 1, 1, res=d_hn)
    dh, dh_b, gr["mix_norm"] = rms_bwd(tag + "_dnorm", h, wts["mix_norm"], d_hn, dout)
    return dh, dh_b, gr


def xattn_fwd(tag, h, memn, wts):
    s, d = h.shape
    m = memn.shape[0]
    dh_ = d // X_HEADS
    hn = rms_fwd(tag + "_norm", h, wts["xattn_norm"])
    q = mm2(tag + "_q", hn, wts["wq"], 1, 0, out_dtype=BF16)
    kv = mm2(tag + "_kv", memn, wts["wkv"], 1, 0, out_dtype=BF16)
    tq = _tile(s, 512)
    o = tile_call(
        tag + "_attn", attn_fn, (X_HEADS, s // tq),
        [(q, (tq, dh_), lambda hh, i: (i, hh)), (kv, (m, dh_), lambda hh, i: (0, hh)), (kv, (m, dh_), lambda hh, i: (0, X_HEADS + hh))],
        [((s, d), BF16, (tq, dh_), lambda hh, i: (i, hh), None)])[0]
    out = mm2(tag + "_o", o, wts["wo"], 1, 0, res=h)
    return out, (h, hn, q, kv, o)


def xattn_bwd(tag, saved, memn, mem, wts, dout, dout_b):
    h, hn, q, kv, o = saved
    s, d = h.shape
    m = memn.shape[0]
    dh_ = d // X_HEADS
    gr = {}
    d_o = mm2(tag + "_do", dout_b, wts["wo"], 1, 1, out_dtype=BF16)
    gr["wo"] = mm2(tag + "_dwo", o, dout_b, 0, 0)
    tq = _tile(s, 512)
    dq, dk, dv = tile_call(
        tag + "_dattn", vjp_of(attn_fn, 3, (0, 1, 2)), (X_HEADS, s // tq),
        [(q, (tq, dh_), lambda hh, i: (i, hh)), (kv, (m, dh_), lambda hh, i: (0, hh)), (kv, (m, dh_), lambda hh, i: (0, X_HEADS + hh)),
         (d_o, (tq, dh_), lambda hh, i: (i, hh))],
        [((s, d), BF16, (tq, dh_), lambda hh, i: (i, hh), None),
         ((m, d), F32, (m, dh_), lambda hh, i: (0, hh), (1,)), ((m, d), F32, (m, dh_), lambda hh, i: (0, hh), (1,))])
    dkv = jnp.concatenate([dk, dv], axis=1).astype(BF16)
    gr["wq"] = mm2(tag + "_dwq", hn, dq, 0, 0)
    gr["wkv"] = mm2(tag + "_dwkv", memn, dkv, 0, 0)
    d_memn = mm2(tag + "_dmemn", dkv, wts["wkv"], 1, 1)
    gr["mem_norm"] = rows_call(
        tag + "_dmemnorm", lambda a, dy, b: vjp_of(rms, 2, (1,))(a, b, dy)[0], [mem, d_memn], [wts["mem_norm"]], [],
        [wts["mem_norm"].shape], tr=128)[0]
    d_hn = mm2(tag + "_dhn", dq, wts["wq"], 1, 1)
    dh, dh_b, gr["xattn_norm"] = rms_bwd(tag + "_dnorm", h, wts["xattn_norm"], d_hn, dout)
    return dh, dh_b, gr


def final_loss(x, g, tgt):
    d = x.shape[1]

    def fn(a, t, b):
        def f(aa, bb):
            return 0.5 * jnp.sum(jnp.square(rms(aa, bb) - t)) / d
        loss, (dx, dg) = jax.value_and_grad(f, (0, 1))(a, b)
        lane = lax.broadcasted_iota(jnp.int32, (1, 128), 1)
        return dx, dx, dg, jnp.where(lane == 0, loss, 0.0)
    return rows_call("final_loss", fn, [x, tgt], [g], [(d, F32), (d, BF16)], [g.shape, (1, 128)])


SHARDED = ("ffn1_w_gate", "ffn1_w_up", "ffn1_w_down", "w_in", "conv_w", "pool_w", "w_out", "xattn_wq", "xattn_wkv",
           "xattn_wo", "ffn2_w_gate", "ffn2_w_up", "ffn2_w_down")
REPLICATED = ("ffn1_norm", "mix_norm", "a_log", "dt_bias", "dn_out_norm", "pool_scale", "xattn_norm", "mem_norm",
              "ffn2_norm", "final_norm")
WEIGHTS = ("ffn1_norm", "ffn1_w_gate", "ffn1_w_up", "ffn1_w_down", "mix_norm", "w_in", "conv_w", "a_log", "dt_bias",
           "dn_out_norm", "pool_w", "pool_scale", "w_out", "xattn_norm", "mem_norm", "xattn_wq", "xattn_wkv", "xattn_wo",
           "ffn2_norm", "ffn2_w_gate", "ffn2_w_up", "ffn2_w_down", "final_norm")


def _lane_pad(v, width=128):
    return jnp.pad(v, (0, width - v.shape[0]))[None, :]


def layer_weights(gath, rep, l, d):
    dnw = DN_HEADS * HEAD_DIM
    row = lambda name: rep[name][l][None, :].astype(F32)
    cols = lambda g: jnp.transpose(g[:, l], (1, 0, 2)).reshape(g.shape[2], -1)
    rows_ = lambda g: g[:, l].reshape(-1, g.shape[3])
    w_in = cols(gath["w_in"])
    o_ab = 4 * dnw
    w_out = rows_(gath["w_out"])
    pw = gath["pool_w"][:, l]
    gdim = pw.shape[-1]
    pw = jnp.transpose(pw.reshape(N_SHARDS, len(POOL_WINDOWS), gdim // N_SHARDS, gdim), (1, 0, 2, 3))
    mixer = dict(
        mix_norm=row("mix_norm"), w_qkv=w_in[:, :3 * dnw], w_z=w_in[:, 3 * dnw:o_ab],
        w_ab=jnp.pad(w_in[:, o_ab:o_ab + 2 * DN_HEADS], ((0, 0), (0, AB_PAD - 2 * DN_HEADS))),
        w_p=w_in[:, o_ab + 2 * DN_HEADS:], conv_w=cols(gath["conv_w"]).astype(F32),
        a_log=_lane_pad(rep["a_log"][l].astype(F32)), dt_bias=_lane_pad(rep["dt_bias"][l].astype(F32)),
        dn_out_norm=row("dn_out_norm"), pool_w=pw.reshape(len(POOL_WINDOWS), gdim, gdim), pool_scale=row("pool_scale"),
        w_out_a=w_out[:dnw], w_out_b=w_out[dnw:])
    xattn = dict(xattn_norm=row("xattn_norm"), mem_norm=row("mem_norm"), wq=rows_(gath["xattn_wq"]),
                 wkv=cols(gath["xattn_wkv"]), wo=rows_(gath["xattn_wo"]))
    return dict(ffn1_norm=row("ffn1_norm"), ffn2_norm=row("ffn2_norm"), mixer=mixer, xattn=xattn)


def _col_shards(g):
    k, n = g.shape
    return jnp.transpose(g.reshape(k, N_SHARDS, n // N_SHARDS), (1, 0, 2))


def model_grads(x, mem, tgt, gath, rep):
    s, d = x.shape
    depth = rep["ffn1_norm"].shape[0]
    dnw = DN_HEADS * HEAD_DIM
    wl = [layer_weights(gath, rep, l, d) for l in range(depth)]
    saved = []
    h = x
    for l in range(depth):
        w = wl[l]
        t = "l%d" % l
        memn = rms_fwd(t + "_memnorm", mem, w["xattn"]["mem_norm"])
        h, s1 = ffn_fwd(t + "_ffn1", h, w["ffn1_norm"], gath["ffn1_w_gate"], gath["ffn1_w_up"], gath["ffn1_w_down"], l)
        h, s2 = mixer_fwd(t + "_mix", h, w["mixer"])
        h, s3 = xattn_fwd(t + "_xattn", h, memn, w["xattn"])
        h, s4 = ffn_fwd(t + "_ffn2", h, w["ffn2_norm"], gath["ffn2_w_gate"], gath["ffn2_w_up"], gath["ffn2_w_down"], l)
        saved.append((memn, s1, s2, s3, s4))
    dh, dh_b, d_final, loss_row = final_loss(h, rep["final_norm"][None, :].astype(F32), tgt)
    big, small = [None] * depth, [None] * depth
    for l in reversed(range(depth)):
        w = wl[l]
        t = "l%d" % l
        memn, s1, s2, s3, s4 = saved[l]
        gb, gs = {}, {}
        dh, dh_b, gs["ffn2_norm"], gb["ffn2_w_gate"], gb["ffn2_w_up"], gb["ffn2_w_down"] = ffn_bwd(
            t + "_ffn2", s4, w["ffn2_norm"], gath["ffn2_w_gate"], gath["ffn2_w_up"], gath["ffn2_w_down"], l, dh, dh_b)
        dh, dh_b, gx = xattn_bwd(t + "_xattn", s3, memn, mem, w["xattn"], dh, dh_b)
        dh, dh_b, gm = mixer_bwd(t + "_mix", s2, w["mixer"], dh, dh_b)
        dh, dh_b, gs["ffn1_norm"], gb["ffn1_w_gate"], gb["ffn1_w_up"], gb["ffn1_w_down"] = ffn_bwd(
            t + "_ffn1", s1, w["ffn1_norm"], gath["ffn1_w_gate"], gath["ffn1_w_up"], gath["ffn1_w_down"], l, dh, dh_b)
        for n in ("xattn_norm", "mem_norm"):
            gs[n] = gx[n]
        for n in ("mix_norm", "a_log", "dt_bias", "dn_out_norm", "pool_scale"):
            gs[n] = gm[n]
        gb["w_in"] = _col_shards(jnp.concatenate([gm["w_qkv"], gm["w_z"], gm["w_ab"][:, :2 * DN_HEADS], gm["w_p"]], axis=1))
        gb["conv_w"] = _col_shards(gm["conv_w"])
        gdim = gm["pool_w"].shape[-1]
        gb["pool_w"] = jnp.transpose(gm["pool_w"].reshape(len(POOL_WINDOWS), N_SHARDS, gdim // N_SHARDS, gdim),
                                     (1, 0, 2, 3)).reshape(N_SHARDS, gdim, gdim)
        gb["w_out"] = jnp.concatenate([gm["w_out_a"], gm["w_out_b"]], axis=0).reshape(N_SHARDS, d // N_SHARDS, d)
        gb["xattn_wq"] = gx["wq"].reshape(N_SHARDS, d // N_SHARDS, d)
        gb["xattn_wo"] = gx["wo"].reshape(N_SHARDS, d // N_SHARDS, d)
        gb["xattn_wkv"] = _col_shards(gx["wkv"])
        big[l], small[l] = gb, gs
    return loss_row, dh, big, small, d_final


HBM = pl.BlockSpec(memory_space=pltpu.HBM)


def _place():
    x, y, c = lax.axis_index("x"), lax.axis_index("y"), lax.axis_index("c")
    chips = [(1 - x, y), (x, 1 - y), (1 - x, 1 - y)]
    return x, y, c, 2 * x + y, chips, [2 * px + py for px, py in chips]


def place_shard(name, w, where, dtype):
    nl, r, cdim = w.shape
    tr = _rtile(r)
    return tile_call(name, lambda a: a, (nl, r // tr), [(w, (None, tr, cdim), lambda l, i, p: (l, i, 0))],
                     [((N_SHARDS, nl, r, cdim), dtype, (None, None, tr, cdim), lambda l, i, p: (p[0], l, i, 0), None)],
                     prefetch=where)[0]


def gather_weights(bufs):
    n = len(bufs)

    def body(*refs):
        outs = refs[n:2 * n]
        ssem, rsem = refs[2 * n:]
        x, y, c, j_own, chips, js = _place()

        def copy(t, j, layer, sem, to):
            blk = outs[t].at[j, layer]
            return pltpu.make_async_remote_copy(src_ref=blk, dst_ref=blk, send_sem=ssem.at[6 * t + sem],
                                                recv_sem=rsem.at[6 * t + sem], device_id=to, device_id_type=MESH)

        sib = (x, y, 1 - c)
        sends = []
        for t in range(n):
            for k, (px, py) in enumerate(chips):
                sends.append(copy(t, j_own, c, k, (px, py, c)))
                sends[-1].start()
        for t in range(n):
            for k in range(3):
                copy(t, js[k], c, k, sib).wait_recv()
                sends.append(copy(t, js[k], c, 3 + k, sib))
                sends[-1].start()
        for t in range(n):
            for k in range(3):
                copy(t, js[k], 1 - c, 3 + k, sib).wait_recv()
        for cp in sends:
            cp.wait_send()

    return pl.pallas_call(
        body, name="gather_weights",
        in_specs=[HBM] * n, out_specs=[HBM] * n, input_output_aliases={t: t for t in range(n)},
        out_shape=[jax.ShapeDtypeStruct(a.shape, a.dtype) for a in bufs],
        scratch_shapes=[pltpu.SemaphoreType.DMA((6 * n,)), pltpu.SemaphoreType.DMA((6 * n,))],
    )(*bufs)


def exchange_layers(g0s, g1s):
    n = len(g0s)

    def body(*refs):
        g0, g1, outs = refs[:n], refs[n:2 * n], refs[2 * n:3 * n]
        ssem, rsem = refs[3 * n:]
        x, y, c, _, _, _ = _place()

        def copy(t, src):
            return pltpu.make_async_remote_copy(src_ref=src[t], dst_ref=outs[t], send_sem=ssem.at[t], recv_sem=rsem.at[t],
                                                device_id=(x, y, 1 - c), device_id_type=MESH)

        @pl.when(c == 0)
        def _():
            for t in range(n):
                copy(t, g1).start()

        @pl.when(c == 1)
        def _():
            for t in range(n):
                copy(t, g0).start()

        for t in range(n):
            copy(t, g0).wait()

    return pl.pallas_call(
        body, name="exchange_layers", in_specs=[HBM] * (2 * n), out_specs=[HBM] * n,
        out_shape=[jax.ShapeDtypeStruct(a.shape, a.dtype) for a in g0s],
        scratch_shapes=[pltpu.SemaphoreType.DMA((n,)), pltpu.SemaphoreType.DMA((n,))],
    )(*g0s, *g1s)


def exchange_shards(parts):
    n = len(parts)

    def body(*refs):
        ins, outs = refs[:n], refs[n:2 * n]
        ssem, rsem = refs[2 * n:]
        x, y, c, j_own, chips, js = _place()

        def copy(t, k):
            px, py = chips[k]
            return pltpu.make_async_remote_copy(
                src_ref=ins[t].at[js[k]], dst_ref=outs[t].at[k], send_sem=ssem.at[3 * t + k], recv_sem=rsem.at[3 * t + k],
                device_id=(px, py, c), device_id_type=MESH)

        for t in range(n):
            for k in range(3):
                copy(t, k).start()
        for t in range(n):
            for k in range(3):
                copy(t, k).wait()

    return pl.pallas_call(
        body, name="exchange_shards", in_specs=[HBM] * n, out_specs=[HBM] * n,
        out_shape=[jax.ShapeDtypeStruct((3,) + a.shape[1:], a.dtype) for a in parts],
        scratch_shapes=[pltpu.SemaphoreType.DMA((3 * n,)), pltpu.SemaphoreType.DMA((3 * n,))],
    )(*parts)


def share_layers(bufs):
    n = len(bufs)

    def body(*refs):
        outs = refs[n:2 * n]
        ssem, rsem = refs[2 * n:]
        x, y, c, _, _, _ = _place()

        def copy(t, layer):
            return pltpu.make_async_remote_copy(src_ref=outs[t].at[layer], dst_ref=outs[t].at[layer], send_sem=ssem.at[t],
                                                recv_sem=rsem.at[t], device_id=(x, y, 1 - c), device_id_type=MESH)

        for t in range(n):
            copy(t, c).start()
        for t in range(n):
            copy(t, 1 - c).wait_recv()
        for t in range(n):
            copy(t, c).wait_send()

    return pl.pallas_call(
        body, name="share_layers", in_specs=[HBM] * n, out_specs=[HBM] * n, input_output_aliases={t: t for t in range(n)},
        out_shape=[jax.ShapeDtypeStruct(a.shape, a.dtype) for a in bufs],
        scratch_shapes=[pltpu.SemaphoreType.DMA((n,)), pltpu.SemaphoreType.DMA((n,))],
    )(*bufs)


def allreduce_small(buf):
    r = buf.shape[0]
    n_dev = 8

    def body(in_ref, out_ref, gath, ssem, rsem):
        x, y, c = lax.axis_index("x"), lax.axis_index("y"), lax.axis_index("c")
        flip = lambda v, bit: 1 - v if bit else v
        me = 4 * x + 2 * y + c
        gath[me] = in_ref[...]
        peers = [(flip(x, k >> 2 & 1), flip(y, k >> 1 & 1), flip(c, k & 1)) for k in range(1, n_dev)]
        sends = []
        for k, peer in enumerate(peers):
            cp = pltpu.make_async_remote_copy(src_ref=in_ref, dst_ref=gath.at[me], send_sem=ssem.at[k], recv_sem=rsem.at[k],
                                              device_id=peer, device_id_type=MESH)
            cp.start()
            sends.append(cp)
        for k, (px, py, pc) in enumerate(peers):
            pltpu.make_async_remote_copy(src_ref=in_ref, dst_ref=gath.at[4 * px + 2 * py + pc], send_sem=ssem.at[k],
                                         recv_sem=rsem.at[k], device_id=(px, py, pc), device_id_type=MESH).wait_recv()
        for cp in sends:
            cp.wait_send()
        acc = gath[0]
        for i in range(1, n_dev):
            acc = acc + gath[i]
        out_ref[...] = acc

    return pl.pallas_call(
        body, name="allreduce_small",
        in_specs=[pl.BlockSpec(memory_space=pltpu.VMEM)], out_specs=pl.BlockSpec(memory_space=pltpu.VMEM),
        out_shape=jax.ShapeDtypeStruct(buf.shape, F32),
        scratch_shapes=[pltpu.VMEM((n_dev, r, 128), F32), pltpu.SemaphoreType.DMA((n_dev - 1,)), pltpu.SemaphoreType.DMA((n_dev - 1,))],
    )(buf)


def _rtile(r, pref=256):
    return _tile(r, pref, 16)


def chip_partial(name, g0, g1, recv):
    _, r, cdim = g0.shape
    tr = _rtile(r)
    spec = ((None, tr, cdim), lambda j, i: (j, i, 0))

    def fn(a, b, rv):
        mine = jnp.where(lax.axis_index("c") == 0, a, b)
        return mine + rv
    return tile_call(name, fn, (N_SHARDS, r // tr), [(g0,) + spec, (g1,) + spec, (recv,) + spec],
                     [(g0.shape, BF16) + spec + (None,)])[0]


def sum_chips(name, parts, recv, where):
    _, r, cdim = parts.shape
    tr = _rtile(r)
    up = lambda a: a.astype(F32)
    return tile_call(name, lambda own, rv: (up(own) + up(rv[0])) + (up(rv[1]) + up(rv[2])), (r // tr,),
                     [(parts, (None, tr, cdim), lambda i, p: (p[0], i, 0)), (recv, (3, tr, cdim), lambda i, p: (0, i, 0))],
                     [((2, r, cdim), F32, (None, tr, cdim), lambda i, p: (p[1], i, 0), None)], prefetch=where)[0]


def adamw_fn(w, g, m, v):
    m = ADAM_B1 * m + (1.0 - ADAM_B1) * g
    v = ADAM_B2 * v + (1.0 - ADAM_B2) * jnp.square(g)
    m_hat = m / (1.0 - ADAM_B1 ** ADAM_STEP)
    v_hat = v / (1.0 - ADAM_B2 ** ADAM_STEP)
    delta = -ADAM_LR * (m_hat / (jnp.sqrt(v_hat) + ADAM_EPS) + ADAM_WD * w)
    return delta, m, v


def adamw(name, w, g, m, v):
    nl, r, cdim = w.shape
    tr = _rtile(r, 128)
    spec = ((None, tr, cdim), lambda l, i: (l, i, 0))
    return tile_call(name, adamw_fn, (nl, r // tr), [(a,) + spec for a in (w, g, m, v)],
                     [(w.shape, F32) + spec + (None,)] * 3)


def _as3(a):
    return a.reshape(a.shape[0], -1, a.shape[-1])


def _pack_rows(vals):
    rows = []
    for v in vals:
        v = v.reshape(-1).astype(F32)
        pad = (-v.shape[0]) % 128
        rows.append(jnp.pad(v, (0, pad)).reshape(-1, 128))
    out = jnp.concatenate(rows, axis=0)
    return jnp.pad(out, ((0, (-out.shape[0]) % 8), (0, 0)))


def _unpack_rows(buf, like):
    outs, r = [], 0
    for a in like:
        n = a.size
        nr = -(-n // 128)
        outs.append(buf[r:r + nr].reshape(-1)[:n].reshape(a.shape))
        r += nr
    return outs


def kernel(x, mem, ffn1_norm, ffn1_w_gate, ffn1_w_up, ffn1_w_down, mix_norm, w_in, conv_w, a_log, dt_bias, dn_out_norm, pool_w, pool_scale, w_out, xattn_norm, mem_norm, xattn_wq, xattn_wkv, xattn_wo, ffn2_norm, ffn2_w_gate, ffn2_w_up, ffn2_w_down, final_norm, loss_target, m_ffn1_norm, m_ffn1_w_gate, m_ffn1_w_up, m_ffn1_w_down, m_mix_norm, m_w_in, m_conv_w, m_a_log, m_dt_bias, m_dn_out_norm, m_pool_w, m_pool_scale, m_w_out, m_xattn_norm, m_mem_norm, m_xattn_wq, m_xattn_wkv, m_xattn_wo, m_ffn2_norm, m_ffn2_w_gate, m_ffn2_w_up, m_ffn2_w_down, m_final_norm, v_ffn1_norm, v_ffn1_w_gate, v_ffn1_w_up, v_ffn1_w_down, v_mix_norm, v_w_in, v_conv_w, v_a_log, v_dt_bias, v_dn_out_norm, v_pool_w, v_pool_scale, v_w_out, v_xattn_norm, v_mem_norm, v_xattn_wq, v_xattn_wkv, v_xattn_wo, v_ffn2_norm, v_ffn2_w_gate, v_ffn2_w_up, v_ffn2_w_down, v_final_norm):
    given = dict(locals())
    w = {n: given[n] for n in WEIGHTS}
    m = {n: given["m_" + n] for n in WEIGHTS}
    v = {n: given["v_" + n] for n in WEIGHTS}
    where = jnp.stack([2 * lax.axis_index("x") + lax.axis_index("y"), lax.axis_index("c")]).astype(jnp.int32)
    placed = [place_shard("place_" + n, _as3(w[n]), where, F32 if n == "conv_w" else BF16) for n in SHARDED]
    gath = dict(zip(SHARDED, gather_weights(placed)))
    rep = {n: w[n] for n in REPLICATED}
    loss_row, dx, big, small, d_final = model_grads(x[0], mem[0], loss_target[0], gath, rep)
    recv = exchange_layers([big[0][n] for n in SHARDED], [big[1][n] for n in SHARDED])
    parts = [chip_partial("partial_" + n, big[0][n], big[1][n], r) for n, r in zip(SHARDED, recv)]
    recv = exchange_shards(parts)
    sums = [sum_chips("sum_" + n, p, r, where) for n, p, r in zip(SHARDED, parts, recv)]
    grads = dict(zip(SHARDED, share_layers(sums)))
    out_g, out_d, out_m, out_v = {}, {}, {}, {}
    for n in SHARDED:
        d_, m_, v_ = adamw("adamw_" + n, _as3(w[n]), grads[n], _as3(m[n]), _as3(v[n]))
        out_g[n], out_d[n], out_m[n], out_v[n] = (a.reshape(w[n].shape) for a in (grads[n], d_, m_, v_))
    rep_names = [n for n in REPLICATED if n != "final_norm"]
    g_rep = {n: jnp.stack([small[l][n][0, :w[n].shape[1]] for l in range(len(small))]) for n in rep_names}
    g_rep["final_norm"] = d_final[0]
    like = [w[n] for n in REPLICATED] + [jnp.zeros((1,), F32)]
    summed = allreduce_small(_pack_rows([g_rep[n] for n in REPLICATED] + [loss_row[0, :1]]))
    pk = lambda tree: _pack_rows([tree[n] for n in REPLICATED] + [jnp.zeros((1,), F32)])
    wp, mp, vp = pk(w), pk(m), pk(v)
    dp, mp2, vp2 = adamw("adamw_small", wp[None], summed[None], mp[None], vp[None])
    for buf, dst in ((summed, out_g), (dp[0], out_d), (mp2[0], out_m), (vp2[0], out_v)):
        for n, a in zip(REPLICATED, _unpack_rows(buf, like)):
            dst[n] = a
    loss = _unpack_rows(summed, like)[-1][0]
    return (loss, dx[None], *[out_g[n] for n in WEIGHTS], *[out_d[n] for n in WEIGHTS],
            *[out_m[n] for n in WEIGHTS], *[out_v[n] for n in WEIGHTS])
```

```python
import functools

import jax
import jax.numpy as jnp
from jax import lax
from jax.experimental import pallas as pl
from jax.experimental.pallas import tpu as pltpu

F32, BF16 = jnp.float32, jnp.bfloat16
HI = lax.Precision.HIGHEST
MESH = pl.DeviceIdType.MESH

EPS = 1e-6
DN_HEADS = 8
HEAD_DIM = 128
X_HEADS = 4
POOL_WINDOWS = (2, 4, 8, 16)
CONV_WIDTH = 4
CHUNK = 64
N_SHARDS = 4
AB_PAD = 128
ADAM_LR, ADAM_B1, ADAM_B2, ADAM_EPS, ADAM_WD, ADAM_STEP = 0.001, 0.9, 0.999, 1e-08, 0.01, 10
VMEM_LIMIT = 56 << 20
ROW_TILE = 256


def _tile(n, pref, unit=128):
    best = None
    for t in range(unit, min(n, pref) + 1, unit):
        if n % t == 0:
            best = t
    return best if best is not None else n


def _params():
    return pltpu.CompilerParams(vmem_limit_bytes=VMEM_LIMIT)


def _split(a):
    a = a.astype(F32)
    head = a.astype(BF16)
    return head, (a - head.astype(F32)).astype(BF16)


def _dg(a, b, ca, cb, hi):
    dims = (((ca,), (cb,)), ((), ()))
    dot = lambda u, v: lax.dot_general(u, v, dims, preferred_element_type=F32)
    if hi:
        (a0, a1), (b0, b1) = _split(a), _split(b)
        return dot(a0, b0) + (dot(a0, b1) + dot(a1, b0))
    return dot(a.astype(BF16), b.astype(BF16))


@functools.partial(jax.custom_vjp, nondiff_argnums=(2, 3, 4))
def mmul(a, b, ca, cb, hi):
    return _dg(a, b, ca, cb, hi)


def _mmul_fwd(a, b, ca, cb, hi):
    return _dg(a, b, ca, cb, hi), (a, b)


def _mmul_bwd(ca, cb, hi, res, g):
    a, b = res
    da = _dg(g, b, 1, 1 - cb, hi) if ca == 1 else _dg(b, g, 1 - cb, 1, hi)
    db = _dg(a, g, 1 - ca, 0, hi) if cb == 0 else _dg(g, a, 0, 1 - ca, hi)
    return da.astype(a.dtype), db.astype(b.dtype)


mmul.defvjp(_mmul_fwd, _mmul_bwd)


@functools.partial(jax.custom_vjp, nondiff_argnums=(1,))
def shift_down(x, s):
    t = lax.broadcasted_iota(jnp.int32, x.shape, 0)
    return jnp.where(t >= s, pltpu.roll(x, s, 0), 0.0)


def _shift_up(x, s):
    n = x.shape[0]
    t = lax.broadcasted_iota(jnp.int32, x.shape, 0)
    return jnp.where(t < n - s, pltpu.roll(x, n - s, 0), 0.0)


shift_down.defvjp(lambda x, s: (shift_down(x, s), None), lambda s, _, g: (_shift_up(g, s),))


def sigmoid(x):
    return 0.5 * (jnp.tanh(0.5 * x) + 1.0)


def silu(x):
    return x * sigmoid(x)


@jax.custom_vjp
def softplus(x):
    u = jnp.exp(-jnp.abs(x))
    w = 1.0 + u
    log1p = jnp.where(w == 1.0, u, jnp.log(w) * u / jnp.where(w == 1.0, 1.0, w - 1.0))
    return jnp.maximum(x, 0.0) + log1p


softplus.defvjp(lambda x: (softplus(x), x), lambda x, g: (g * sigmoid(x),))


def rms(x, g):
    x = x.astype(F32)
    return x * lax.rsqrt(jnp.mean(x * x, axis=-1, keepdims=True) + EPS) * g


def swiglu(gate, up):
    return silu(gate) * up


def vjp_of(fn, n_in, diff, has_pids=False):
    def g(*args):
        pids = None
        if has_pids:
            pids, args = args[0], args[1:]
        ins, cots = list(args[:n_in]), args[n_in:]

        def f(*d):
            full = list(ins)
            for i, v in zip(diff, d):
                full[i] = v
            return fn(pids, *full) if has_pids else fn(*full)

        out, pull = jax.vjp(f, *[ins[i].astype(F32) for i in diff])
        if isinstance(out, (tuple, list)):
            return pull(tuple(c.astype(o.dtype) for c, o in zip(cots, out)))
        return pull(cots[0].astype(out.dtype))
    return g


def tile_call(name, fn, grid, ins, outs, with_pids=False, prefetch=None, into=None):
    n_in = len(ins)
    n_into = 0 if into is None else 1

    def body(*refs):
        if prefetch is not None:
            refs = refs[1:]
        pids = tuple(pl.program_id(a) for a in range(len(grid)))
        vals = [r[...] for r in refs[:n_in]]
        res = fn(pids, *vals) if with_pids else fn(*vals)
        if not isinstance(res, (tuple, list)):
            res = (res,)
        for r, o, spec in zip(res, refs[n_in + n_into:], outs):
            acc = spec[4]
            if acc is None:
                o[...] = r.astype(o.dtype)
            else:
                first = functools.reduce(jnp.logical_and, [pids[a] == 0 for a in acc])

                @pl.when(first)
                def _():
                    o[...] = r.astype(o.dtype)

                @pl.when(jnp.logical_not(first))
                def _():
                    o[...] += r.astype(o.dtype)

    in_specs = [pl.BlockSpec(b, im) for _, b, im in ins] + [pl.BlockSpec(memory_space=pl.ANY)] * n_into
    out_specs = [pl.BlockSpec(s[2], s[3]) for s in outs]
    out_shape = [jax.ShapeDtypeStruct(s[0], s[1]) for s in outs]
    args = [a for a, _, _ in ins] + ([] if into is None else [into])
    if prefetch is None:
        return pl.pallas_call(body, name=name, grid=grid, in_specs=in_specs, out_specs=out_specs, out_shape=out_shape,
                              input_output_aliases={n_in: 0} if n_into else {}, compiler_params=_params())(*args)
    spec = pltpu.PrefetchScalarGridSpec(num_scalar_prefetch=1, grid=grid, in_specs=in_specs, out_specs=out_specs)
    return pl.pallas_call(body, name=name, grid_spec=spec, out_shape=out_shape,
                          input_output_aliases={n_in + 1: 0} if n_into else {}, compiler_params=_params())(prefetch, *args)


def mm_call(name, grid, ins, pairs, n_acc, acc_shape, outs, epilogue, extras=(), after=None):
    n_in, n_ex, nk = len(ins), len(extras), grid[2]
    n_dep = 0 if after is None else 1

    def finish(accs, ex_refs, out_refs):
        res = epilogue(accs, *[r[...] for r in ex_refs])
        if not isinstance(res, (tuple, list)):
            res = (res,)
        for r, o in zip(res, out_refs):
            o[...] = r.astype(o.dtype)

    def body(*refs):
        in_refs, ex_refs = refs[:n_in], refs[n_in:n_in + n_ex]
        refs = refs[n_in + n_ex + n_dep:]
        out_refs, accs = refs[:len(outs)], refs[len(outs):]
        if nk == 1:
            vals = [None] * n_acc
            for ia, ib, ca, cb, ai in pairs:
                d = _dg(in_refs[ia][...], in_refs[ib][...], ca, cb, False)
                vals[ai] = d if vals[ai] is None else vals[ai] + d
            finish(vals, ex_refs, out_refs)
            return
        k = pl.program_id(2)

        @pl.when(k == 0)
        def _():
            for a in accs:
                a[...] = jnp.zeros_like(a)

        for ia, ib, ca, cb, ai in pairs:
            accs[ai][...] += _dg(in_refs[ia][...], in_refs[ib][...], ca, cb, False)

        @pl.when(k == nk - 1)
        def _():
            finish([a[...] for a in accs], ex_refs, out_refs)

    return pl.pallas_call(
        body, name=name, grid=grid,
        in_specs=[pl.BlockSpec(b, im) for _, b, im in list(ins) + list(extras)] + [pl.BlockSpec(memory_space=pl.ANY)] * n_dep,
        out_specs=[pl.BlockSpec(s[2], s[3]) for s in outs],
        out_shape=[jax.ShapeDtypeStruct(s[0], s[1]) for s in outs],
        scratch_shapes=[pltpu.VMEM(acc_shape, F32) for _ in range(n_acc if nk > 1 else 0)],
        compiler_params=_params(),
    )(*[a for a, _, _ in list(ins) + list(extras)], *([] if after is None else [after]))


MM_VMEM_BUDGET = 40 << 20


def _mm_tiles(m, n, kk, sa, sb, so, has_res):
    tk = _tile(kk, 2048)
    best = None
    for tm in (1024, 512, 256, 128):
        for tn in (1024, 512, 256, 128):
            tm_, tn_ = _tile(m, tm), _tile(n, tn)
            need = 2 * (tm_ * tk * sa + tk * tn_ * sb + tm_ * tn_ * so) + (tm_ * tn_ * 4 if tk < kk else 0)
            need += 2 * tm_ * tn_ * 4 if has_res else 0
            if need <= MM_VMEM_BUDGET and (best is None or tm_ * tn_ > best[0] * best[1]):
                best = (tm_, tn_)
    return best + (tk,)


def mm2(name, a, b, ca, cb, res=None, scale=None, out_dtype=F32):
    m, kk, n = a.shape[1 - ca], a.shape[ca], b.shape[1 - cb]
    tm, tn, tk = _mm_tiles(m, n, kk, a.dtype.itemsize, b.dtype.itemsize, jnp.dtype(out_dtype).itemsize, res is not None)
    a_spec = ((tm, tk), lambda i, j, k: (i, k)) if ca == 1 else ((tk, tm), lambda i, j, k: (k, i))
    b_spec = ((tk, tn), lambda i, j, k: (k, j)) if cb == 0 else ((tn, tk), lambda i, j, k: (j, k))
    extras = [] if res is None else [(res, (tm, tn), lambda i, j, k: (i, j))]

    def epi(accs, *ex):
        r = accs[0] if scale is None else accs[0] * scale
        return r + ex[0] if ex else r

    return mm_call(name, (m // tm, n // tn, kk // tk), [(a,) + a_spec, (b,) + b_spec], [(0, 1, ca, cb, 0)], 1, (tm, tn),
                   [((m, n), out_dtype, (tm, tn), lambda i, j, k: (i, j))], epi, extras)[0]


def rows_call(name, fn, rows, consts, outs, acc_outs=(), tr=ROW_TILE):
    s = rows[0].shape[0]
    tr = _tile(s, tr, 8)
    ins = [(r, (tr, r.shape[1]), lambda i: (i, 0)) for r in rows]
    ins += [(c, c.shape, (lambda nd: (lambda i: (0,) * nd))(c.ndim)) for c in consts]
    o = [((s, c), dt, (tr, c), lambda i: (i, 0), None) for c, dt in outs]
    o += [(shp, F32, shp, (lambda nd: (lambda i: (0,) * nd))(len(shp)), (0,)) for shp in acc_outs]
    return tile_call(name, fn, (s // tr,), ins, o)


def _lane_pick(x, h):
    lane = lax.broadcasted_iota(jnp.int32, x.shape, x.ndim - 1)
    return jnp.sum(jnp.where(lane == h, x, 0.0), axis=-1, keepdims=True)


def gateprep_fn(ab, alog, dtb):
    t = ab.shape[0]
    gs, bs = [], []
    for h in range(DN_HEADS):
        a_h = _lane_pick(ab, h)
        b_h = _lane_pick(ab, DN_HEADS + h)
        g_h = -jnp.exp(_lane_pick(alog, h)) * softplus(a_h + _lane_pick(dtb, h))
        gs.append(jnp.broadcast_to(g_h, (t, HEAD_DIM)))
        bs.append(jnp.broadcast_to(sigmoid(b_h), (t, HEAD_DIM)))
    return jnp.concatenate(gs, axis=1), jnp.concatenate(bs, axis=1)


def conv_fn(pids, x, w):
    kind = pids[0] // DN_HEADS
    y = x * w[CONV_WIDTH - 1:CONV_WIDTH]
    for i in range(CONV_WIDTH - 1):
        y = y + shift_down(x, CONV_WIDTH - 1 - i) * w[i:i + 1]
    y = silu(y)
    n = y * lax.rsqrt(jnp.sum(y * y, axis=-1, keepdims=True) + EPS)
    n = n * jnp.where(kind == 0, HEAD_DIM ** -0.5, 1.0)
    return jnp.where(kind == 2, y, n)


def unit_lower_inverse(a):
    c = a.shape[0]
    r = lax.broadcasted_iota(jnp.int32, (c, c), 0)
    cc = lax.broadcasted_iota(jnp.int32, (c, c), 1)
    x = -a
    t = jnp.where(r == cc, 1.0, 0.0) + x
    p = 2
    while p < c:
        x = _dg(x, x, 1, 0, True)
        t = t + _dg(t, x, 1, 0, True)
        p *= 2
    return t


@jax.custom_vjp
def known_inverse(a, t):
    return t


known_inverse.defvjp(lambda a, t: (t, t),
                     lambda t, g: (-_dg(_dg(t, g, 0, 0, True), t, 1, 1, True), jnp.zeros_like(t)))


def intra_head(q, k, v, g, b, t_known):
    c = q.shape[0]
    r = lax.broadcasted_iota(jnp.int32, (c, c), 0)
    cc = lax.broadcasted_iota(jnp.int32, (c, c), 1)
    tril = (r >= cc).astype(F32)
    gc = mmul(tril, g, 1, 0, True)
    m = gc[:, :c]
    decay = jnp.exp(jnp.where(r >= cc, m - m.T, -1e30))
    kb = k * b
    a = jnp.where(r > cc, mmul(kb, k, 1, 1, True) * decay, 0.0)
    t = unit_lower_inverse(a) if t_known is None else known_inverse(a, t_known)
    e = jnp.exp(gc)
    u = mmul(t, v * b, 1, 0, True)
    w = mmul(t, kb * e, 1, 0, True)
    qk = mmul(q, k, 1, 1, True) * decay
    gl = gc[c - 1:c, :]
    kd = k * jnp.exp(gl - gc)
    outs = (u, w, qk, q * e, kd, jnp.broadcast_to(jnp.exp(gl), (8, HEAD_DIM)))
    return outs + (t,) if t_known is None else outs


def _heads(x, h):
    return x[:, h * HEAD_DIM:(h + 1) * HEAD_DIM]


def intra_fn(q, k, v, g, b):
    outs = [intra_head(_heads(q, h), _heads(k, h), _heads(v, h), _heads(g, h), _heads(b, h), None) for h in range(DN_HEADS)]
    cat = lambda i: jnp.concatenate([o[i] for o in outs], axis=1)
    stack = lambda i: jnp.stack([o[i] for o in outs], axis=0)
    return cat(0), cat(1), stack(2), cat(3), cat(4), cat(5), stack(6)


def intra_bwd_fn(q, k, v, g, b, tinv, du, dw, dqk, dqd, dkd, dgl):
    res = []
    for h in range(DN_HEADS):
        hs = lambda x: _heads(x, h)
        res.append(vjp_of(intra_head, 6, (0, 1, 2, 3, 4))(hs(q), hs(k), hs(v), hs(g), hs(b), tinv[h],
                                                         hs(du), hs(dw), dqk[h], hs(dqd), hs(dkd), hs(dgl)))
    return tuple(jnp.concatenate([r[i] for r in res], axis=1) for i in range(5))


def scan_step(s, u, w, qk, qd, kd, gl):
    v_new = u - mmul(w, s, 1, 0, True)
    o = mmul(qd, s, 1, 0, True) + mmul(qk, v_new, 1, 0, True)
    return s * gl[0:1, :] + mmul(kd, v_new, 0, 0, True), o


def outgate_fn(o, z, g):
    return rms(o, g) * silu(z)


def pool_fn(pids, p):
    gid = pids[0]
    s = p.shape[0]
    t1 = (lax.broadcasted_iota(jnp.int32, p.shape, 0) + 1).astype(F32)
    acc, win, out = p, 1, None
    for gi, target in enumerate(POOL_WINDOWS):
        while win < target:
            acc = acc + shift_down(acc, win)
            win *= 2
        cand = acc / jnp.minimum(t1, float(target))
        out = cand if out is None else jnp.where(gid == gi, cand, out)
    return out - p


def poolmix_fn(pooled, pw, scale):
    return mmul(pooled, pw, 1, 0, False) * scale


def attn_fn(q, k, v):
    s = mmul(q, k, 1, 1, False) * (q.shape[1] ** -0.5)
    s = s - jnp.max(s, axis=-1, keepdims=True)
    e = jnp.exp(s)
    p = e / jnp.sum(e, axis=-1, keepdims=True)
    return mmul(p, v, 1, 0, False)


def rms_fwd(name, x, g):
    return rows_call(name, lambda a, b: rms(a, b), [x], [g], [(x.shape[1], BF16)])[0]


def rms_bwd(name, x, g, dy, dres):
    def fn(a, d, r, b):
        dx, dg = vjp_of(rms, 2, (0, 1))(a, b, d)
        return dx + r, dx + r, dg
    return rows_call(name, fn, [x, dy, dres], [g], [(x.shape[1], F32), (x.shape[1], BF16)], [g.shape])


def ffn_fwd(tag, x, g, wg, wu, wd):
    s, d = x.shape
    fj = wg.shape[-1]
    f = N_SHARDS * fj
    xn = rms_fwd(tag + "_norm", x, g)
    tm = _tile(s, 256)
    w_spec = ((None, d, fj), lambda j, i, k: (j, 0, 0))
    o_spec = ((tm, fj), lambda j, i, k: (i, j))
    gate, up, act = mm_call(
        tag + "_gu", (N_SHARDS, s // tm, 1),
        [(xn, (tm, d), lambda j, i, k: (i, 0)), (wg,) + w_spec, (wu,) + w_spec],
        [(0, 1, 1, 0, 0), (0, 2, 1, 0, 1)], 2, (tm, fj),
        [((s, f), F32) + o_spec, ((s, f), F32) + o_spec, ((s, f), BF16) + o_spec],
        lambda accs: (accs[0], accs[1], swiglu(accs[0], accs[1])))
    tm, tn = _tile(s, 1024), _tile(d, 1024)
    out = mm_call(
        tag + "_down", (s // tm, d // tn, N_SHARDS),
        [(act, (tm, fj), lambda i, j, k: (i, k)), (wd, (None, fj, tn), lambda i, j, k: (k, 0, j))],
        [(0, 1, 1, 0, 0)], 1, (tm, tn),
        [((s, d), F32, (tm, tn), lambda i, j, k: (i, j))],
        lambda accs, r: r + 0.5 * accs[0], [(x, (tm, tn), lambda i, j, k: (i, j))])[0]
    return out, (x, xn, gate, up, act)


def ffn_bwd(tag, saved, g, wg, wu, wd, dout, dout_b, after=None):
    x, xn, gate, up, act = saved
    s, d = x.shape
    fj = wg.shape[-1]
    f = N_SHARDS * fj
    tm = _tile(s, 512)
    o_spec = ((tm, fj), lambda j, i, k: (i, j))

    def epi(accs, ga, u):
        dgate, dup = vjp_of(swiglu, 2, (0, 1))(ga, u, 0.5 * accs[0])
        return dgate, dup

    dgate, dup = mm_call(
        tag + "_dact", (N_SHARDS, s // tm, 1),
        [(dout_b, (tm, d), lambda j, i, k: (i, 0)), (wd, (None, fj, d), lambda j, i, k: (j, 0, 0))],
        [(0, 1, 1, 1, 0)], 1, (tm, fj),
        [((s, f), BF16) + o_spec, ((s, f), BF16) + o_spec], epi,
        [(gate,) + o_spec, (up,) + o_spec], after=after)
    tn = _tile(d, 1024)
    dwd = mm_call(
        tag + "_dwd", (N_SHARDS, d // tn, 1),
        [(act, (s, fj), lambda j, i, k: (0, j)), (dout_b, (s, tn), lambda j, i, k: (0, i))],
        [(0, 1, 0, 0, 0)], 1, (fj, tn),
        [((N_SHARDS, fj, d), F32, (None, fj, tn), lambda j, i, k: (j, 0, i))],
        lambda accs: 0.5 * accs[0])[0]
    td = _tile(d, 512)
    g_spec = ((s, fj), lambda j, i, k: (0, j))
    w_out = ((N_SHARDS, d, fj), F32, (None, td, fj), lambda j, i, k: (j, i, 0))
    dwg, dwu = mm_call(
        tag + "_dwgu", (N_SHARDS, d // td, 1),
        [(xn, (s, td), lambda j, i, k: (0, i)), (dgate,) + g_spec, (dup,) + g_spec],
        [(0, 1, 0, 0, 0), (0, 2, 0, 0, 1)], 2, (td, fj), [w_out, w_out], lambda accs: (accs[0], accs[1]))
    tm, tn = _tile(s, 1024), _tile(d, 1024)
    a_spec = ((tm, fj), lambda i, j, k: (i, k))
    wt_spec = ((None, tn, fj), lambda i, j, k: (k, j, 0))
    dxn = mm_call(
        tag + "_dxn", (s // tm, d // tn, N_SHARDS),
        [(dgate,) + a_spec, (wg,) + wt_spec, (dup,) + a_spec, (wu,) + wt_spec],
        [(0, 1, 1, 1, 0), (2, 3, 1, 1, 0)], 1, (tm, tn),
        [((s, d), F32, (tm, tn), lambda i, j, k: (i, j))], lambda accs: accs[0])[0]
    dx, dx_b, dg = rms_bwd(tag + "_dnorm", x, g, dxn, dout)
    return dx, dx_b, dg, dwg, dwu, dwd


def _hspec(tc, width=HEAD_DIM):
    return (tc, width)


def mixer_fwd(tag, h, wts):
    s, d = h.shape
    dnw = DN_HEADS * HEAD_DIM
    pw_ = d - dnw
    gdim = pw_ // len(POOL_WINDOWS)
    nc = s // CHUNK
    hn = rms_fwd(tag + "_norm", h, wts["mix_norm"])
    qkv = mm2(tag + "_qkv", hn, wts["w_qkv"], 1, 0)
    z = mm2(tag + "_z", hn, wts["w_z"], 1, 0)
    ab = mm2(tag + "_ab", hn, wts["w_ab"], 1, 0)
    p = mm2(tag + "_p", hn, wts["w_p"], 1, 0)
    qkvn = tile_call(tag + "_conv", conv_fn, (3 * DN_HEADS,),
                     [(qkv, (s, HEAD_DIM), lambda i: (0, i)), (wts["conv_w"], (CONV_WIDTH, HEAD_DIM), lambda i: (0, i))],
                     [((s, 3 * dnw), F32, (s, HEAD_DIM), lambda i: (0, i), None)], with_pids=True)[0]
    g_bc, b_bc = rows_call(tag + "_gates", gateprep_fn, [ab], [wts["a_log"], wts["dt_bias"]], [(dnw, F32), (dnw, F32)])
    cw = (CHUNK, dnw)
    sq = ((DN_HEADS, s, CHUNK), F32, (DN_HEADS, CHUNK, CHUNK), lambda n: (0, n, 0), None)
    u, w, qk, qd, kd, gl, tinv = tile_call(
        tag + "_intra", intra_fn, (nc,),
        [(qkvn, cw, lambda n: (n, 0)), (qkvn, cw, lambda n: (n, 1)), (qkvn, cw, lambda n: (n, 2)),
         (g_bc, cw, lambda n: (n, 0)), (b_bc, cw, lambda n: (n, 0))],
        [((s, dnw), F32, cw, lambda n: (n, 0), None), ((s, dnw), F32, cw, lambda n: (n, 0), None), sq,
         ((s, dnw), F32, cw, lambda n: (n, 0), None), ((s, dnw), F32, cw, lambda n: (n, 0), None),
         ((nc * 8, dnw), F32, (8, dnw), lambda n: (n, 0), None), sq])
    o, states = scan_fwd(tag + "_scan", u, w, qk, qd, kd, gl)
    y_dn = tile_call(
        tag + "_outgate", outgate_fn, (DN_HEADS, s // ROW_TILE),
        [(o, (ROW_TILE, HEAD_DIM), lambda hh, i: (i, hh)), (z, (ROW_TILE, HEAD_DIM), lambda hh, i: (i, hh)),
         (wts["dn_out_norm"], (1, HEAD_DIM), lambda hh, i: (0, 0))],
        [((s, dnw), BF16, (ROW_TILE, HEAD_DIM), lambda hh, i: (i, hh), None)])[0]
    ng = len(POOL_WINDOWS)
    pooled = tile_call(tag + "_pool", pool_fn, (ng,), [(p, (s, gdim), lambda i: (0, i))],
                       [((s, pw_), BF16, (s, gdim), lambda i: (0, i), None)], with_pids=True)[0]
    tp = _tile(s, 512)
    y_pool = tile_call(
        tag + "_poolmix", poolmix_fn, (ng, s // tp),
        [(pooled, (tp, gdim), lambda gi, i: (i, gi)), (wts["pool_w"], (None, gdim, gdim), lambda gi, i: (gi, 0, 0)),
         (wts["pool_scale"], (1, gdim), lambda gi, i: (0, gi))],
        [((s, pw_), BF16, (tp, gdim), lambda gi, i: (i, gi), None)])[0]
    h1 = mm2(tag + "_out_a", y_dn, wts["w_out_a"], 1, 0, res=h)
    h2 = mm2(tag + "_out_b", y_pool, wts["w_out_b"], 1, 0, res=h1)
    saved = (h, hn, qkv, z, ab, p, qkvn, g_bc, b_bc, u, w, qk, qd, kd, gl, tinv, o, states, y_dn, pooled, y_pool)
    return h2, saved


def scan_fwd(name, u, w, qk, qd, kd, gl):
    s, dnw = u.shape
    nc = s // CHUNK
    cw = (CHUNK, dnw)

    def body(u_r, w_r, qk_r, qd_r, kd_r, gl_r, o_r, st_r, state):
        @pl.when(pl.program_id(0) == 0)
        def _():
            state[...] = jnp.zeros_like(state)

        st_r[...] = state[...]
        outs = []
        for h in range(DN_HEADS):
            hs = slice(h * HEAD_DIM, (h + 1) * HEAD_DIM)
            s_new, o_h = scan_step(state[hs, :], u_r[:, hs], w_r[:, hs], qk_r[h], qd_r[:, hs], kd_r[:, hs], gl_r[:, hs])
            state[hs, :] = s_new
            outs.append(o_h)
        o_r[...] = jnp.concatenate(outs, axis=1)

    row = lambda n: (n, 0)
    return pl.pallas_call(
        body, name=name, grid=(nc,),
        in_specs=[pl.BlockSpec(cw, row), pl.BlockSpec(cw, row), pl.BlockSpec((DN_HEADS, CHUNK, CHUNK), lambda n: (0, n, 0)),
                  pl.BlockSpec(cw, row), pl.BlockSpec(cw, row), pl.BlockSpec((8, dnw), row)],
        out_specs=[pl.BlockSpec(cw, row), pl.BlockSpec((None, dnw, HEAD_DIM), lambda n: (n, 0, 0))],
        out_shape=[jax.ShapeDtypeStruct((s, dnw), F32), jax.ShapeDtypeStruct((nc, dnw, HEAD_DIM), F32)],
        scratch_shapes=[pltpu.VMEM((dnw, HEAD_DIM), F32)],
        compiler_params=_params(),
    )(u, w, qk, qd, kd, gl)


def scan_bwd(name, states, u, w, qk, qd, kd, gl, do):
    s, dnw = u.shape
    nc = s // CHUNK
    cw = (CHUNK, dnw)

    def body(st_r, u_r, w_r, qk_r, qd_r, kd_r, gl_r, do_r, du_r, dw_r, dqk_r, dqd_r, dkd_r, dgl_r, dstate):
        @pl.when(pl.program_id(0) == 0)
        def _():
            dstate[...] = jnp.zeros_like(dstate)

        res = []
        for h in range(DN_HEADS):
            hs = slice(h * HEAD_DIM, (h + 1) * HEAD_DIM)
            r = vjp_of(scan_step, 7, tuple(range(7)))(
                st_r[hs, :], u_r[:, hs], w_r[:, hs], qk_r[h], qd_r[:, hs], kd_r[:, hs], gl_r[:, hs],
                dstate[hs, :], do_r[:, hs])
            dstate[hs, :] = r[0]
            res.append(r)
        cat = lambda i: jnp.concatenate([r[i] for r in res], axis=1)
        du_r[...] = cat(1)
        dw_r[...] = cat(2)
        dqk_r[...] = jnp.stack([r[3] for r in res], axis=0)
        dqd_r[...] = cat(4)
        dkd_r[...] = cat(5)
        dgl_r[...] = cat(6)

    row = lambda n: (nc - 1 - n, 0)
    qk_spec = pl.BlockSpec((DN_HEADS, CHUNK, CHUNK), lambda n: (0, nc - 1 - n, 0))
    return pl.pallas_call(
        body, name=name, grid=(nc,),
        in_specs=[pl.BlockSpec((None, dnw, HEAD_DIM), lambda n: (nc - 1 - n, 0, 0)), pl.BlockSpec(cw, row), pl.BlockSpec(cw, row),
                  qk_spec, pl.BlockSpec(cw, row), pl.BlockSpec(cw, row), pl.BlockSpec((8, dnw), row), pl.BlockSpec(cw, row)],
        out_specs=[pl.BlockSpec(cw, row), pl.BlockSpec(cw, row), qk_spec, pl.BlockSpec(cw, row), pl.BlockSpec(cw, row),
                   pl.BlockSpec((8, dnw), row)],
        out_shape=[jax.ShapeDtypeStruct((s, dnw), F32), jax.ShapeDtypeStruct((s, dnw), F32),
                   jax.ShapeDtypeStruct((DN_HEADS, s, CHUNK), F32), jax.ShapeDtypeStruct((s, dnw), F32),
                   jax.ShapeDtypeStruct((s, dnw), F32), jax.ShapeDtypeStruct((nc * 8, dnw), F32)],
        scratch_shapes=[pltpu.VMEM((dnw, HEAD_DIM), F32)],
        compiler_params=_params(),
    )(states, u, w, qk, qd, kd, gl, do)


def mixer_bwd(tag, saved, wts, dout, dout_b):
    h, hn, qkv, z, ab, p, qkvn, g_bc, b_bc, u, w, qk, qd, kd, gl, tinv, o, states, y_dn, pooled, y_pool = saved
    s, d = h.shape
    dnw = DN_HEADS * HEAD_DIM
    pw_ = d - dnw
    ng = len(POOL_WINDOWS)
    gdim = pw_ // ng
    nc = s // CHUNK
    gr = {}
    d_ydn = mm2(tag + "_dydn", dout_b, wts["w_out_a"], 1, 1)
    d_ypool = mm2(tag + "_dypool", dout_b, wts["w_out_b"], 1, 1)
    gr["w_out_a"] = mm2(tag + "_dwout_a", y_dn, dout_b, 0, 0)
    gr["w_out_b"] = mm2(tag + "_dwout_b", y_pool, dout_b, 0, 0)
    tp = _tile(s, 512)
    d_pooled, gr["pool_w"], gr["pool_scale"] = tile_call(
        tag + "_dpoolmix", vjp_of(poolmix_fn, 3, (0, 1, 2)), (ng, s // tp),
        [(pooled, (tp, gdim), lambda gi, i: (i, gi)), (wts["pool_w"], (None, gdim, gdim), lambda gi, i: (gi, 0, 0)),
         (wts["pool_scale"], (1, gdim), lambda gi, i: (0, gi)), (d_ypool, (tp, gdim), lambda gi, i: (i, gi))],
        [((s, pw_), F32, (tp, gdim), lambda gi, i: (i, gi), None),
         ((ng, gdim, gdim), F32, (None, gdim, gdim), lambda gi, i: (gi, 0, 0), (1,)),
         ((1, pw_), F32, (1, gdim), lambda gi, i: (0, gi), (1,))])
    d_p = tile_call(tag + "_dpool", vjp_of(pool_fn, 1, (0,), True), (ng,),
                    [(p, (s, gdim), lambda i: (0, i)), (d_pooled, (s, gdim), lambda i: (0, i))],
                    [((s, pw_), BF16, (s, gdim), lambda i: (0, i), None)], with_pids=True)[0]
    hb = (ROW_TILE, HEAD_DIM)
    d_o, d_z, gr["dn_out_norm"] = tile_call(
        tag + "_doutgate", vjp_of(outgate_fn, 3, (0, 1, 2)), (DN_HEADS, s // ROW_TILE),
        [(o, hb, lambda hh, i: (i, hh)), (z, hb, lambda hh, i: (i, hh)), (wts["dn_out_norm"], (1, HEAD_DIM), lambda hh, i: (0, 0)),
         (d_ydn, hb, lambda hh, i: (i, hh))],
        [((s, dnw), F32, hb, lambda hh, i: (i, hh), None), ((s, dnw), BF16, hb, lambda hh, i: (i, hh), None),
         ((1, HEAD_DIM), F32, (1, HEAD_DIM), lambda hh, i: (0, 0), (0, 1))])
    du, dw, dqk, dqd, dkd, dgl = scan_bwd(tag + "_dscan", states, u, w, qk, qd, kd, gl, d_o)
    cw = (CHUNK, dnw)
    row = lambda n: (n, 0)
    dq, dk, dv, dg_bc, db_bc = tile_call(
        tag + "_dintra", intra_bwd_fn, (nc,),
        [(qkvn, cw, lambda n: (n, 0)), (qkvn, cw, lambda n: (n, 1)), (qkvn, cw, lambda n: (n, 2)),
         (g_bc, cw, row), (b_bc, cw, row), (tinv, (DN_HEADS, CHUNK, CHUNK), lambda n: (0, n, 0)), (du, cw, row), (dw, cw, row),
         (dqk, (DN_HEADS, CHUNK, CHUNK), lambda n: (0, n, 0)), (dqd, cw, row), (dkd, cw, row), (dgl, (8, dnw), row)],
        [((s, dnw), F32, cw, row, None)] * 5)
    d_ab, gr["a_log"], gr["dt_bias"] = rows_call(
        tag + "_dgates", lambda a, dg, db, al, dt: vjp_of(gateprep_fn, 3, (0, 1, 2))(a, al, dt, dg, db),
        [ab, dg_bc, db_bc], [wts["a_log"], wts["dt_bias"]], [(AB_PAD, BF16)], [(1, AB_PAD), (1, AB_PAD)])
    d_qkvn = jnp.concatenate([dq, dk, dv], axis=1)
    d_qkv, gr["conv_w"] = tile_call(
        tag + "_dconv", vjp_of(conv_fn, 2, (0, 1), True), (3 * DN_HEADS,),
        [(qkv, (s, HEAD_DIM), lambda i: (0, i)), (wts["conv_w"], (CONV_WIDTH, HEAD_DIM), lambda i: (0, i)),
         (d_qkvn, (s, HEAD_DIM), lambda i: (0, i))],
        [((s, 3 * dnw), BF16, (s, HEAD_DIM), lambda i: (0, i), None),
         ((CONV_WIDTH, 3 * dnw), F32, (CONV_WIDTH, HEAD_DIM), lambda i: (0, i), None)], with_pids=True)
    gr["w_qkv"] = mm2(tag + "_dwqkv", hn, d_qkv, 0, 0)
    gr["w_z"] = mm2(tag + "_dwz", hn, d_z, 0, 0)
    gr["w_ab"] = mm2(tag + "_dwab", hn, d_ab, 0, 0)
    gr["w_p"] = mm2(tag + "_dwp", hn, d_p, 0, 0)
    d_hn = mm2(tag + "_dhn1", d_qkv, wts["w_qkv"], 1, 1)
    d_hn = mm2(tag + "_dhn2", d_z, wts["w_z"], 1, 1, res=d_hn)
    d_hn = mm2(tag + "_dhn3", d_ab, wts["w_ab"], 1, 1, res=d_hn)
    d_hn = mm2(tag + "_dhn4", d_p, wts["w_p"], 1, 1, res=d_hn)
    dh, dh_b, gr["mix_norm"] = rms_bwd(tag + "_dnorm", h, wts["mix_norm"], d_hn, dout)
    return dh, dh_b, gr


def xattn_fwd(tag, h, memn, wts):
    s, d = h.shape
    m = memn.shape[0]
    dh_ = d // X_HEADS
    hn = rms_fwd(tag + "_norm", h, wts["xattn_norm"])
    q = mm2(tag + "_q", hn, wts["wq"], 1, 0, out_dtype=BF16)
    kv = mm2(tag + "_kv", memn, wts["wkv"], 1, 0, out_dtype=BF16)
    tq = _tile(s, 512)
    o = tile_call(
        tag + "_attn", attn_fn, (X_HEADS, s // tq),
        [(q, (tq, dh_), lambda hh, i: (i, hh)), (kv, (m, dh_), lambda hh, i: (0, hh)), (kv, (m, dh_), lambda hh, i: (0, X_HEADS + hh))],
        [((s, d), BF16, (tq, dh_), lambda hh, i: (i, hh), None)])[0]
    out = mm2(tag + "_o", o, wts["wo"], 1, 0, res=h)
    return out, (h, hn, q, kv, o)


def xattn_bwd(tag, saved, memn, mem, wts, dout, dout_b):
    h, hn, q, kv, o = saved
    s, d = h.shape
    m = memn.shape[0]
    dh_ = d // X_HEADS
    gr = {}
    d_o = mm2(tag + "_do", dout_b, wts["wo"], 1, 1, out_dtype=BF16)
    gr["wo"] = mm2(tag + "_dwo", o, dout_b, 0, 0)
    tq = _tile(s, 512)
    dq, dk, dv = tile_call(
        tag + "_dattn", vjp_of(attn_fn, 3, (0, 1, 2)), (X_HEADS, s // tq),
        [(q, (tq, dh_), lambda hh, i: (i, hh)), (kv, (m, dh_), lambda hh, i: (0, hh)), (kv, (m, dh_), lambda hh, i: (0, X_HEADS + hh)),
         (d_o, (tq, dh_), lambda hh, i: (i, hh))],
        [((s, d), BF16, (tq, dh_), lambda hh, i: (i, hh), None),
         ((m, d), F32, (m, dh_), lambda hh, i: (0, hh), (1,)), ((m, d), F32, (m, dh_), lambda hh, i: (0, hh), (1,))])
    dkv = jnp.concatenate([dk, dv], axis=1).astype(BF16)
    gr["wq"] = mm2(tag + "_dwq", hn, dq, 0, 0)
    gr["wkv"] = mm2(tag + "_dwkv", memn, dkv, 0, 0)
    d_memn = mm2(tag + "_dmemn", dkv, wts["wkv"], 1, 1)
    gr["mem_norm"] = rows_call(
        tag + "_dmemnorm", lambda a, dy, b: vjp_of(rms, 2, (1,))(a, b, dy)[0], [mem, d_memn], [wts["mem_norm"]], [],
        [wts["mem_norm"].shape], tr=128)[0]
    d_hn = mm2(tag + "_dhn", dq, wts["wq"], 1, 1)
    dh, dh_b, gr["xattn_norm"] = rms_bwd(tag + "_dnorm", h, wts["xattn_norm"], d_hn, dout)
    return dh, dh_b, gr


def final_loss(x, g, tgt):
    d = x.shape[1]

    def fn(a, t, b):
        def f(aa, bb):
            return 0.5 * jnp.sum(jnp.square(rms(aa, bb) - t)) / d
        loss, (dx, dg) = jax.value_and_grad(f, (0, 1))(a, b)
        lane = lax.broadcasted_iota(jnp.int32, (1, 128), 1)
        return dx, dx, dg, jnp.where(lane == 0, loss, 0.0)
    return rows_call("final_loss", fn, [x, tgt], [g], [(d, F32), (d, BF16)], [g.shape, (1, 128)])


SHARDED = ("ffn1_w_gate", "ffn1_w_up", "ffn1_w_down", "w_in", "conv_w", "pool_w", "w_out", "xattn_wq", "xattn_wkv",
           "xattn_wo", "ffn2_w_gate", "ffn2_w_up", "ffn2_w_down")
REPLICATED = ("ffn1_norm", "mix_norm", "a_log", "dt_bias", "dn_out_norm", "pool_scale", "xattn_norm", "mem_norm",
              "ffn2_norm", "final_norm")
WEIGHTS = ("ffn1_norm", "ffn1_w_gate", "ffn1_w_up", "ffn1_w_down", "mix_norm", "w_in", "conv_w", "a_log", "dt_bias",
           "dn_out_norm", "pool_w", "pool_scale", "w_out", "xattn_norm", "mem_norm", "xattn_wq", "xattn_wkv", "xattn_wo",
           "ffn2_norm", "ffn2_w_gate", "ffn2_w_up", "ffn2_w_down", "final_norm")


def _lane_pad(v, width=128):
    return jnp.pad(v, (0, width - v.shape[0]))[None, :]


GROUPS = {"ffn1": ("ffn1_w_gate", "ffn1_w_up", "ffn1_w_down"), "mixer": ("w_in", "conv_w", "pool_w", "w_out"),
          "xattn": ("xattn_wq", "xattn_wkv", "xattn_wo"), "ffn2": ("ffn2_w_gate", "ffn2_w_up", "ffn2_w_down")}


def _cols(g):
    return jnp.transpose(g, (1, 0, 2)).reshape(g.shape[1], -1)


def _rows(g):
    return g.reshape(-1, g.shape[2])


def _rep_row(rep, name, l):
    return rep[name][l][None, :].astype(F32)


def mixer_weights(g, rep, l):
    dnw = DN_HEADS * HEAD_DIM
    w_in = _cols(g["w_in"])
    o_ab = 4 * dnw
    w_out = _rows(g["w_out"])
    gdim = g["pool_w"].shape[-1]
    pw = jnp.transpose(g["pool_w"].reshape(N_SHARDS, len(POOL_WINDOWS), gdim // N_SHARDS, gdim), (1, 0, 2, 3))
    return dict(
        mix_norm=_rep_row(rep, "mix_norm", l), w_qkv=w_in[:, :3 * dnw], w_z=w_in[:, 3 * dnw:o_ab],
        w_ab=jnp.pad(w_in[:, o_ab:o_ab + 2 * DN_HEADS], ((0, 0), (0, AB_PAD - 2 * DN_HEADS))),
        w_p=w_in[:, o_ab + 2 * DN_HEADS:], conv_w=_cols(g["conv_w"].reshape(N_SHARDS, CONV_WIDTH, -1)).astype(F32),
        a_log=_lane_pad(rep["a_log"][l].astype(F32)), dt_bias=_lane_pad(rep["dt_bias"][l].astype(F32)),
        dn_out_norm=_rep_row(rep, "dn_out_norm", l), pool_w=pw.reshape(len(POOL_WINDOWS), gdim, gdim),
        pool_scale=_rep_row(rep, "pool_scale", l), w_out_a=w_out[:dnw], w_out_b=w_out[dnw:])


def xattn_weights(g, rep, l):
    return dict(xattn_norm=_rep_row(rep, "xattn_norm", l), mem_norm=_rep_row(rep, "mem_norm", l), wq=_rows(g["xattn_wq"]),
                wkv=_cols(g["xattn_wkv"]), wo=_rows(g["xattn_wo"]))


def _col_shards(g):
    k, n = g.shape
    return jnp.transpose(g.reshape(k, N_SHARDS, n // N_SHARDS), (1, 0, 2))


MIN_COMM_ROWS = 32


def _comm_rows(a):
    if a.shape[-2] >= MIN_COMM_ROWS:
        return a
    return a.reshape(a.shape[:-2] + (MIN_COMM_ROWS, -1))


def model_grads(x, mem, tgt, fetch, rep, layer_done=None):
    s, d = x.shape
    depth = rep["ffn1_norm"].shape[0]
    saved, wl = [], []
    h = x
    for l in range(depth):
        t = "l%d" % l
        f1 = fetch(l, "ffn1", h)
        h, s1 = ffn_fwd(t + "_ffn1", h, _rep_row(rep, "ffn1_norm", l), f1["ffn1_w_gate"], f1["ffn1_w_up"], f1["ffn1_w_down"])
        wm = mixer_weights(fetch(l, "mixer", h), rep, l)
        h, s2 = mixer_fwd(t + "_mix", h, wm)
        wx = xattn_weights(fetch(l, "xattn", h), rep, l)
        memn = rms_fwd(t + "_memnorm", mem, wx["mem_norm"])
        h, s3 = xattn_fwd(t + "_xattn", h, memn, wx)
        f2 = fetch(l, "ffn2", h)
        h, s4 = ffn_fwd(t + "_ffn2", h, _rep_row(rep, "ffn2_norm", l), f2["ffn2_w_gate"], f2["ffn2_w_up"], f2["ffn2_w_down"])
        saved.append((memn, s1, s2, s3, s4))
        wl.append((f1, wm, wx, f2))
    dh, dh_b, d_final, loss_row = final_loss(h, rep["final_norm"][None, :].astype(F32), tgt)
    big, small = [None] * depth, [None] * depth
    after = None
    for l in reversed(range(depth)):
        f1, wm, wx, f2 = wl[l]
        t = "l%d" % l
        memn, s1, s2, s3, s4 = saved[l]
        gb, gs = {}, {}
        dh, dh_b, gs["ffn2_norm"], gb["ffn2_w_gate"], gb["ffn2_w_up"], gb["ffn2_w_down"] = ffn_bwd(
            t + "_ffn2", s4, _rep_row(rep, "ffn2_norm", l), f2["ffn2_w_gate"], f2["ffn2_w_up"], f2["ffn2_w_down"], dh, dh_b, after)
        dh, dh_b, gx = xattn_bwd(t + "_xattn", s3, memn, mem, wx, dh, dh_b)
        dh, dh_b, gm = mixer_bwd(t + "_mix", s2, wm, dh, dh_b)
        dh, dh_b, gs["ffn1_norm"], gb["ffn1_w_gate"], gb["ffn1_w_up"], gb["ffn1_w_down"] = ffn_bwd(
            t + "_ffn1", s1, _rep_row(rep, "ffn1_norm", l), f1["ffn1_w_gate"], f1["ffn1_w_up"], f1["ffn1_w_down"], dh, dh_b)
        for n in ("xattn_norm", "mem_norm"):
            gs[n] = gx[n]
        for n in ("mix_norm", "a_log", "dt_bias", "dn_out_norm", "pool_scale"):
            gs[n] = gm[n]
        gb["w_in"] = _col_shards(jnp.concatenate([gm["w_qkv"], gm["w_z"], gm["w_ab"][:, :2 * DN_HEADS], gm["w_p"]], axis=1))
        gb["conv_w"] = _comm_rows(_col_shards(gm["conv_w"]))
        gdim = gm["pool_w"].shape[-1]
        gb["pool_w"] = jnp.transpose(gm["pool_w"].reshape(len(POOL_WINDOWS), N_SHARDS, gdim // N_SHARDS, gdim),
                                     (1, 0, 2, 3)).reshape(N_SHARDS, gdim, gdim)
        gb["w_out"] = jnp.concatenate([gm["w_out_a"], gm["w_out_b"]], axis=0).reshape(N_SHARDS, d // N_SHARDS, d)
        gb["xattn_wq"] = gx["wq"].reshape(N_SHARDS, d // N_SHARDS, d)
        gb["xattn_wo"] = gx["wo"].reshape(N_SHARDS, d // N_SHARDS, d)
        gb["xattn_wkv"] = _col_shards(gx["wkv"])
        big[l], small[l] = gb, gs
        if layer_done is not None:
            after = layer_done(l, gb)
    return loss_row, dh, big, small, d_final


HBM = pl.BlockSpec(memory_space=pltpu.HBM)


def _place():
    x, y, c = lax.axis_index("x"), lax.axis_index("y"), lax.axis_index("c")
    chips = [(1 - x, y), (x, 1 - y), (1 - x, 1 - y)]
    return x, y, c, 2 * x + y, chips, [2 * px + py for px, py in chips]


SEM = pl.BlockSpec(memory_space=pltpu.SEMAPHORE)
SPLIT_COPY = pltpu.CompilerParams(has_side_effects=pltpu.SideEffectType.DATAFLOW_SIDE_EFFECTING)


def _half(ref, lead, c):
    r = ref.shape[1] // 2
    return ref.at[lead, pl.ds(c * r, r)]


def place_shards(name, w, where, dtype):
    nl, r, cdim = w.shape
    tr = _rtile(r)
    out = ((N_SHARDS, r, cdim), dtype, (None, tr, cdim), lambda i, p: (p[0], i, 0), None)
    return tile_call(name, lambda a: tuple(a[l] for l in range(nl)), (r // tr,),
                     [(w, (nl, tr, cdim), lambda i, p: (0, i, 0))], [out] * nl, prefetch=where)


def _gather_copies(bufs, ssem, rsem):
    x, y, c, j_own, chips, js = _place()
    res = []
    for i, b in enumerate(bufs):
        for k, (px, py) in enumerate(chips):
            mk = lambda slot: pltpu.make_async_remote_copy(
                src_ref=_half(b, slot, c), dst_ref=_half(b, slot, c), send_sem=ssem.at[3 * i + k],
                recv_sem=rsem.at[3 * i + k], device_id=(px, py, c), device_id_type=MESH)
            res.append((mk(j_own), mk(js[k])))
    return res


def gather_start(bufs, groups):
    n, ng = len(bufs), len(groups)

    def body(*refs):
        sems, outs, token = refs[n:n + 2 * ng], refs[n + 2 * ng:2 * n + 2 * ng], refs[2 * n + 2 * ng]
        for gi, group in enumerate(groups):
            for mine, _ in _gather_copies([outs[t] for t in group], sems[2 * gi], sems[2 * gi + 1]):
                mine.start()
        token[...] = jnp.zeros_like(token)

    sem_shapes = []
    for group in groups:
        sem_shapes += [pltpu.SemaphoreType.DMA((3 * len(group),))] * 2
    res = pl.pallas_call(
        body, name="gather_start", in_specs=[HBM] * n,
        out_specs=[SEM] * (2 * ng) + [HBM] * n + [pl.BlockSpec(memory_space=pltpu.VMEM)],
        out_shape=sem_shapes + [pltpu.HBM(a.shape, a.dtype) for a in bufs] + [jax.ShapeDtypeStruct((8, 128), F32)],
        input_output_aliases={t: 2 * ng + t for t in range(n)}, compiler_params=SPLIT_COPY,
    )(*[pltpu.with_memory_space_constraint(a, pltpu.HBM) for a in bufs])
    sems = [(res[2 * gi], res[2 * gi + 1]) for gi in range(ng)]
    return sems, res[2 * ng:2 * ng + n], res[-1]


def gather_wait(name, bufs, sems, after):
    n = len(bufs)

    def body(*refs):
        ins, ssem, rsem = refs[:n], refs[n], refs[n + 1]
        for mine, theirs in _gather_copies(ins, ssem, rsem):
            mine.wait_send()
            theirs.wait_recv()

    return pl.pallas_call(
        body, name=name, in_specs=[HBM] * n + [SEM, SEM, pl.BlockSpec(memory_space=pl.ANY)], out_specs=[HBM] * n,
        out_shape=[pltpu.HBM(a.shape, a.dtype) for a in bufs],
        input_output_aliases={t: t for t in range(n)}, compiler_params=SPLIT_COPY,
    )(*bufs, sems[0], sems[1], after)


def gather_swap(name, bufs):
    n = len(bufs)

    def body(*refs):
        outs = refs[n:2 * n]
        ssem, rsem = refs[2 * n:]
        x, y, c, j_own, chips, js = _place()

        def copy(i, k, half):
            blk = _half(outs[i], js[k], half)
            return pltpu.make_async_remote_copy(src_ref=blk, dst_ref=blk, send_sem=ssem.at[3 * i + k],
                                                recv_sem=rsem.at[3 * i + k], device_id=(x, y, 1 - c), device_id_type=MESH)

        for i in range(n):
            for k in range(3):
                copy(i, k, c).start()
        for i in range(n):
            for k in range(3):
                copy(i, k, 1 - c).wait_recv()
                copy(i, k, c).wait_send()

    return pl.pallas_call(
        body, name=name, in_specs=[HBM] * n, out_specs=[HBM] * n, input_output_aliases={t: t for t in range(n)},
        out_shape=[jax.ShapeDtypeStruct(a.shape, a.dtype) for a in bufs],
        scratch_shapes=[pltpu.SemaphoreType.DMA((3 * n,)), pltpu.SemaphoreType.DMA((3 * n,))],
    )(*bufs)


def pair_swap(name, grads):
    n = len(grads)

    def body(*refs):
        ins, outs = refs[:n], refs[n:2 * n]
        ssem, rsem = refs[2 * n:]
        x, y, c, _, _, _ = _place()

        def copy(t):
            r = ins[t].shape[1] // 2
            return pltpu.make_async_remote_copy(src_ref=ins[t].at[:, pl.ds((1 - c) * r, r)], dst_ref=outs[t], send_sem=ssem.at[t],
                                                recv_sem=rsem.at[t], device_id=(x, y, 1 - c), device_id_type=MESH)

        for t in range(n):
            copy(t).start()
        for t in range(n):
            copy(t).wait()

    return pl.pallas_call(
        body, name=name, in_specs=[HBM] * n, out_specs=[HBM] * n,
        out_shape=[jax.ShapeDtypeStruct((a.shape[0], a.shape[1] // 2, a.shape[2]), a.dtype) for a in grads],
        scratch_shapes=[pltpu.SemaphoreType.DMA((n,)), pltpu.SemaphoreType.DMA((n,))],
    )(*grads)


def _reduce_copies(parts, lands, ssem, rsem):
    x, y, c, j_own, chips, js = _place()
    return [pltpu.make_async_remote_copy(src_ref=parts[t].at[js[k]], dst_ref=lands[t].at[k], send_sem=ssem.at[3 * t + k],
                                         recv_sem=rsem.at[3 * t + k], device_id=(px, py, c), device_id_type=MESH)
            for t in range(len(parts)) for k, (px, py) in enumerate(chips)]


def reduce_start(name, parts):
    n = len(parts)
    lands = [lax.empty((3,) + a.shape[1:], a.dtype) for a in parts]

    def body(*refs):
        ssem, rsem = refs[2 * n], refs[2 * n + 1]
        p_out, l_out, token = refs[2 * n + 2:3 * n + 2], refs[3 * n + 2:4 * n + 2], refs[4 * n + 2]
        for cp in _reduce_copies(p_out, l_out, ssem, rsem):
            cp.start()
        token[...] = jnp.zeros_like(token)

    res = pl.pallas_call(
        body, name=name, in_specs=[HBM] * (2 * n),
        out_specs=[SEM, SEM] + [HBM] * (2 * n) + [pl.BlockSpec(memory_space=pltpu.VMEM)],
        out_shape=[pltpu.SemaphoreType.DMA((3 * n,))] * 2 + [pltpu.HBM(a.shape, a.dtype) for a in parts + lands]
        + [jax.ShapeDtypeStruct((8, 128), F32)],
        input_output_aliases={t: 2 + t for t in range(2 * n)}, compiler_params=SPLIT_COPY,
    )(*[pltpu.with_memory_space_constraint(a, pltpu.HBM) for a in parts + lands])
    return res[0], res[1], res[2:2 + n], res[2 + n:2 + 2 * n], res[-1]


def reduce_wait(name, ssem, rsem, parts, lands, after):
    n = len(parts)

    def body(*refs):
        for cp in _reduce_copies(refs[:n], refs[n:2 * n], refs[2 * n], refs[2 * n + 1]):
            cp.wait_send()
            cp.wait_recv()

    res = pl.pallas_call(
        body, name=name, in_specs=[HBM] * (2 * n) + [SEM, SEM, pl.BlockSpec(memory_space=pl.ANY)], out_specs=[HBM] * (2 * n),
        out_shape=[pltpu.HBM(a.shape, a.dtype) for a in list(parts) + list(lands)],
        input_output_aliases={t: t for t in range(2 * n)}, compiler_params=SPLIT_COPY,
    )(*parts, *lands, ssem, rsem, after)
    return res[:n], res[n:]


def share_halves(bufs):
    n = len(bufs)

    def body(*refs):
        outs = refs[n:2 * n]
        ssem, rsem = refs[2 * n:]
        x, y, c, _, _, _ = _place()

        def copy(t, l, half):
            blk = _half(outs[t], l, half)
            return pltpu.make_async_remote_copy(src_ref=blk, dst_ref=blk, send_sem=ssem.at[2 * t + l],
                                                recv_sem=rsem.at[2 * t + l], device_id=(x, y, 1 - c), device_id_type=MESH)

        for t in range(n):
            for l in range(2):
                copy(t, l, c).start()
        for t in range(n):
            for l in range(2):
                copy(t, l, 1 - c).wait_recv()
                copy(t, l, c).wait_send()

    return pl.pallas_call(
        body, name="share_halves", in_specs=[HBM] * n, out_specs=[HBM] * n, input_output_aliases={t: t for t in range(n)},
        out_shape=[jax.ShapeDtypeStruct(a.shape, a.dtype) for a in bufs],
        scratch_shapes=[pltpu.SemaphoreType.DMA((2 * n,)), pltpu.SemaphoreType.DMA((2 * n,))],
    )(*bufs)


def allreduce_small(buf):
    r = buf.shape[0]
    n_dev = 8

    def body(in_ref, out_ref, gath, ssem, rsem):
        x, y, c = lax.axis_index("x"), lax.axis_index("y"), lax.axis_index("c")
        flip = lambda v, bit: 1 - v if bit else v
        me = 4 * x + 2 * y + c
        gath[me] = in_ref[...]
        peers = [(flip(x, k >> 2 & 1), flip(y, k >> 1 & 1), flip(c, k & 1)) for k in range(1, n_dev)]
        sends = []
        for k, peer in enumerate(peers):
            cp = pltpu.make_async_remote_copy(src_ref=in_ref, dst_ref=gath.at[me], send_sem=ssem.at[k], recv_sem=rsem.at[k],
                                              device_id=peer, device_id_type=MESH)
            cp.start()
            sends.append(cp)
        for k, (px, py, pc) in enumerate(peers):
            pltpu.make_async_remote_copy(src_ref=in_ref, dst_ref=gath.at[4 * px + 2 * py + pc], send_sem=ssem.at[k],
                                         recv_sem=rsem.at[k], device_id=(px, py, pc), device_id_type=MESH).wait_recv()
        for cp in sends:
            cp.wait_send()
        acc = gath[0]
        for i in range(1, n_dev):
            acc = acc + gath[i]
        out_ref[...] = acc

    return pl.pallas_call(
        body, name="allreduce_small",
        in_specs=[pl.BlockSpec(memory_space=pltpu.VMEM)], out_specs=pl.BlockSpec(memory_space=pltpu.VMEM),
        out_shape=jax.ShapeDtypeStruct(buf.shape, F32),
        scratch_shapes=[pltpu.VMEM((n_dev, r, 128), F32), pltpu.SemaphoreType.DMA((n_dev - 1,)), pltpu.SemaphoreType.DMA((n_dev - 1,))],
    )(buf)


def _rtile(r, pref=256):
    return _tile(r, pref, 16)


def chip_partial(name, grad, recv, where):
    _, rh, cdim = recv.shape
    tr = _rtile(rh)
    nt = rh // tr
    return tile_call(name, lambda a, b: a + b, (N_SHARDS, nt),
                     [(grad, (None, tr, cdim), lambda j, i, p: (j, p[1] * nt + i, 0)), (recv, (None, tr, cdim), lambda j, i, p: (j, i, 0))],
                     [(recv.shape, BF16, (None, tr, cdim), lambda j, i, p: (j, i, 0), None)], prefetch=where)[0]


def sum_chips(name, parts, lands, where, layer, n_layers, into):
    _, rh, cdim = parts.shape
    tr = _rtile(rh)
    nt = rh // tr
    up = lambda a: a.astype(F32)
    return tile_call(name, lambda own, rv: (up(own) + up(rv[0])) + (up(rv[1]) + up(rv[2])), (nt,),
                     [(parts, (None, tr, cdim), lambda i, p: (p[0], i, 0)), (lands, (3, tr, cdim), lambda i, p: (0, i, 0))],
                     [((n_layers, 2 * rh, cdim), F32, (None, tr, cdim), lambda i, p: (layer, p[1] * nt + i, 0), None)],
                     prefetch=where, into=into)[0]


def adamw_fn(w, g, m, v):
    m = ADAM_B1 * m + (1.0 - ADAM_B1) * g
    v = ADAM_B2 * v + (1.0 - ADAM_B2) * jnp.square(g)
    m_hat = m / (1.0 - ADAM_B1 ** ADAM_STEP)
    v_hat = v / (1.0 - ADAM_B2 ** ADAM_STEP)
    delta = -ADAM_LR * (m_hat / (jnp.sqrt(v_hat) + ADAM_EPS) + ADAM_WD * w)
    return delta, m, v


def adamw(name, w, g, m, v):
    nl, r, cdim = w.shape
    tr = _rtile(r, 128)
    spec = ((None, tr, cdim), lambda l, i: (l, i, 0))
    return tile_call(name, adamw_fn, (nl, r // tr), [(a,) + spec for a in (w, g, m, v)],
                     [(w.shape, F32) + spec + (None,)] * 3)


def _as3(a):
    return _comm_rows(a.reshape(a.shape[0], -1, a.shape[-1]))


COMM_GROUPS = ((0, ("ffn1",)), (0, ("mixer",)), (0, ("xattn", "ffn2")), (1, ("ffn1", "mixer")), (1, ("xattn", "ffn2")))


def _pack_rows(vals):
    rows = []
    for v in vals:
        v = v.reshape(-1).astype(F32)
        pad = (-v.shape[0]) % 128
        rows.append(jnp.pad(v, (0, pad)).reshape(-1, 128))
    out = jnp.concatenate(rows, axis=0)
    return jnp.pad(out, ((0, (-out.shape[0]) % 8), (0, 0)))


def _unpack_rows(buf, like):
    outs, r = [], 0
    for a in like:
        n = a.size
        nr = -(-n // 128)
        outs.append(buf[r:r + nr].reshape(-1)[:n].reshape(a.shape))
        r += nr
    return outs


def kernel(x, mem, ffn1_norm, ffn1_w_gate, ffn1_w_up, ffn1_w_down, mix_norm, w_in, conv_w, a_log, dt_bias, dn_out_norm, pool_w, pool_scale, w_out, xattn_norm, mem_norm, xattn_wq, xattn_wkv, xattn_wo, ffn2_norm, ffn2_w_gate, ffn2_w_up, ffn2_w_down, final_norm, loss_target, m_ffn1_norm, m_ffn1_w_gate, m_ffn1_w_up, m_ffn1_w_down, m_mix_norm, m_w_in, m_conv_w, m_a_log, m_dt_bias, m_dn_out_norm, m_pool_w, m_pool_scale, m_w_out, m_xattn_norm, m_mem_norm, m_xattn_wq, m_xattn_wkv, m_xattn_wo, m_ffn2_norm, m_ffn2_w_gate, m_ffn2_w_up, m_ffn2_w_down, m_final_norm, v_ffn1_norm, v_ffn1_w_gate, v_ffn1_w_up, v_ffn1_w_down, v_mix_norm, v_w_in, v_conv_w, v_a_log, v_dt_bias, v_dn_out_norm, v_pool_w, v_pool_scale, v_w_out, v_xattn_norm, v_mem_norm, v_xattn_wq, v_xattn_wkv, v_xattn_wo, v_ffn2_norm, v_ffn2_w_gate, v_ffn2_w_up, v_ffn2_w_down, v_final_norm):
    given = dict(locals())
    w = {n: given[n] for n in WEIGHTS}
    m = {n: given["m_" + n] for n in WEIGHTS}
    v = {n: given["v_" + n] for n in WEIGHTS}
    where = jnp.stack([2 * lax.axis_index("x") + lax.axis_index("y"), lax.axis_index("c")]).astype(jnp.int32)
    placed = {}
    for n in SHARDED:
        for l, buf in enumerate(place_shards("place_" + n, _as3(w[n]), where, F32 if n == "conv_w" else BF16)):
            placed[(l, n)] = buf
    keys, groups = [], []
    for l, entries in COMM_GROUPS:
        groups.append([])
        for e in entries:
            for n in GROUPS[e]:
                groups[-1].append(len(keys))
                keys.append((l, n))
    sems, bufs, _ = gather_start([placed[k] for k in keys], groups)
    fetched = {}

    def fetch(l, entry, after):
        gi = [i for i, (gl, entries) in enumerate(COMM_GROUPS) if gl == l and entry in entries][0]
        if gi not in fetched:
            landed = gather_wait("gather_wait%d" % gi, [bufs[i] for i in groups[gi]], sems[gi], after)
            fetched[gi] = {keys[i][1]: a for i, a in zip(groups[gi], gather_swap("gather_swap%d" % gi, landed))}
        return fetched[gi]

    pending = {}

    def layer_done(l, gb):
        mine = [gb[n] for n in SHARDED]
        theirs = pair_swap("pair_swap%d" % l, mine)
        parts = [chip_partial("partial%d_%s" % (l, n), g, r, where) for n, g, r in zip(SHARDED, mine, theirs)]
        ssem, rsem, parts, lands, token = reduce_start("reduce_start%d" % l, parts)
        pending[l] = (ssem, rsem, parts, lands)
        return token

    rep = {n: w[n] for n in REPLICATED}
    loss_row, dx, big, small, d_final = model_grads(x[0], mem[0], loss_target[0], fetch, rep, layer_done)
    sums = [None] * len(SHARDED)
    for l in reversed(range(len(big))):
        ssem, rsem, parts, lands = pending[l]
        parts, lands = reduce_wait("reduce_wait%d" % l, ssem, rsem, parts, lands, dx)
        sums = [sum_chips("sum%d_%s" % (l, n), p, r, where, l, len(big), s_) for n, p, r, s_ in zip(SHARDED, parts, lands, sums)]
    grads = dict(zip(SHARDED, share_halves(sums)))
    out_g, out_d, out_m, out_v = {}, {}, {}, {}
    for n in SHARDED:
        d_, m_, v_ = adamw("adamw_" + n, _as3(w[n]), grads[n], _as3(m[n]), _as3(v[n]))
        out_g[n], out_d[n], out_m[n], out_v[n] = (a.reshape(w[n].shape) for a in (grads[n], d_, m_, v_))
    rep_names = [n for n in REPLICATED if n != "final_norm"]
    g_rep = {n: jnp.stack([small[l][n][0, :w[n].shape[1]] for l in range(len(small))]) for n in rep_names}
    g_rep["final_norm"] = d_final[0]
    like = [w[n] for n in REPLICATED] + [jnp.zeros((1,), F32)]
    summed = allreduce_small(_pack_rows([g_rep[n] for n in REPLICATED] + [loss_row[0, :1]]))
    pk = lambda tree: _pack_rows([tree[n] for n in REPLICATED] + [jnp.zeros((1,), F32)])
    wp, mp, vp = pk(w), pk(m), pk(v)
    dp, mp2, vp2 = adamw("adamw_small", wp[None], summed[None], mp[None], vp[None])
    for buf, dst in ((summed, out_g), (dp[0], out_d), (mp2[0], out_m), (vp2[0], out_v)):
        for n, a in zip(REPLICATED, _unpack_rows(buf, like)):
            dst[n] = a
    loss = _unpack_rows(summed, like)[-1][0]
    return (loss, dx[None], *[out_g[n] for n in WEIGHTS], *[out_d[n] for n in WEIGHTS],
            *[out_m[n] for n in WEIGHTS], *[out_v[n] for n in WEIGHTS])
```

```python
import functools

import jax
import jax.numpy as jnp
from jax import lax
from jax.experimental import pallas as pl
from jax.experimental.pallas import tpu as pltpu

F32, BF16 = jnp.float32, jnp.bfloat16
HI = lax.Precision.HIGHEST
MESH = pl.DeviceIdType.MESH

EPS = 1e-6
DN_HEADS = 8
HEAD_DIM = 128
X_HEADS = 4
POOL_WINDOWS = (2, 4, 8, 16)
CONV_WIDTH = 4
CHUNK = 64
N_SHARDS = 4
AB_PAD = 128
ADAM_LR, ADAM_B1, ADAM_B2, ADAM_EPS, ADAM_WD, ADAM_STEP = 0.001, 0.9, 0.999, 1e-08, 0.01, 10
VMEM_LIMIT = 56 << 20
ROW_TILE = 256


def _tile(n, pref, unit=128):
    best = None
    for t in range(unit, min(n, pref) + 1, unit):
        if n % t == 0:
            best = t
    return best if best is not None else n


def _params():
    return pltpu.CompilerParams(vmem_limit_bytes=VMEM_LIMIT)


def _split(a):
    a = a.astype(F32)
    head = a.astype(BF16)
    return head, (a - head.astype(F32)).astype(BF16)


def _dg(a, b, ca, cb, hi):
    dims = (((ca,), (cb,)), ((), ()))
    dot = lambda u, v: lax.dot_general(u, v, dims, preferred_element_type=F32)
    if hi:
        (a0, a1), (b0, b1) = _split(a), _split(b)
        return dot(a0, b0) + (dot(a0, b1) + dot(a1, b0))
    return dot(a.astype(BF16), b.astype(BF16))


@functools.partial(jax.custom_vjp, nondiff_argnums=(2, 3, 4))
def mmul(a, b, ca, cb, hi):
    return _dg(a, b, ca, cb, hi)


def _mmul_fwd(a, b, ca, cb, hi):
    return _dg(a, b, ca, cb, hi), (a, b)


def _mmul_bwd(ca, cb, hi, res, g):
    a, b = res
    da = _dg(g, b, 1, 1 - cb, hi) if ca == 1 else _dg(b, g, 1 - cb, 1, hi)
    db = _dg(a, g, 1 - ca, 0, hi) if cb == 0 else _dg(g, a, 0, 1 - ca, hi)
    return da.astype(a.dtype), db.astype(b.dtype)


mmul.defvjp(_mmul_fwd, _mmul_bwd)


@functools.partial(jax.custom_vjp, nondiff_argnums=(1,))
def shift_down(x, s):
    t = lax.broadcasted_iota(jnp.int32, x.shape, 0)
    return jnp.where(t >= s, pltpu.roll(x, s, 0), 0.0)


def _shift_up(x, s):
    n = x.shape[0]
    t = lax.broadcasted_iota(jnp.int32, x.shape, 0)
    return jnp.where(t < n - s, pltpu.roll(x, n - s, 0), 0.0)


shift_down.defvjp(lambda x, s: (shift_down(x, s), None), lambda s, _, g: (_shift_up(g, s),))


def sigmoid(x):
    return 0.5 * (jnp.tanh(0.5 * x) + 1.0)


def silu(x):
    return x * sigmoid(x)


@jax.custom_vjp
def softplus(x):
    u = jnp.exp(-jnp.abs(x))
    w = 1.0 + u
    log1p = jnp.where(w == 1.0, u, jnp.log(w) * u / jnp.where(w == 1.0, 1.0, w - 1.0))
    return jnp.maximum(x, 0.0) + log1p


softplus.defvjp(lambda x: (softplus(x), x), lambda x, g: (g * sigmoid(x),))


def rms(x, g):
    x = x.astype(F32)
    return x * lax.rsqrt(jnp.mean(x * x, axis=-1, keepdims=True) + EPS) * g


def swiglu(gate, up):
    return silu(gate) * up


def vjp_of(fn, n_in, diff, has_pids=False):
    def g(*args):
        pids = None
        if has_pids:
            pids, args = args[0], args[1:]
        ins, cots = list(args[:n_in]), args[n_in:]

        def f(*d):
            full = list(ins)
            for i, v in zip(diff, d):
                full[i] = v
            return fn(pids, *full) if has_pids else fn(*full)

        out, pull = jax.vjp(f, *[ins[i].astype(F32) for i in diff])
        if isinstance(out, (tuple, list)):
            return pull(tuple(c.astype(o.dtype) for c, o in zip(cots, out)))
        return pull(cots[0].astype(out.dtype))
    return g


def tile_call(name, fn, grid, ins, outs, with_pids=False, prefetch=None, into=None):
    n_in = len(ins)
    n_into = 0 if into is None else 1

    def body(*refs):
        if prefetch is not None:
            refs = refs[1:]
        pids = tuple(pl.program_id(a) for a in range(len(grid)))
        vals = [r[...] for r in refs[:n_in]]
        res = fn(pids, *vals) if with_pids else fn(*vals)
        if not isinstance(res, (tuple, list)):
            res = (res,)
        for r, o, spec in zip(res, refs[n_in + n_into:], outs):
            acc = spec[4]
            if acc is None:
                o[...] = r.astype(o.dtype)
            else:
                first = functools.reduce(jnp.logical_and, [pids[a] == 0 for a in acc])

                @pl.when(first)
                def _():
                    o[...] = r.astype(o.dtype)

                @pl.when(jnp.logical_not(first))
                def _():
                    o[...] += r.astype(o.dtype)

    in_specs = [pl.BlockSpec(b, im) for _, b, im in ins] + [pl.BlockSpec(memory_space=pl.ANY)] * n_into
    out_specs = [pl.BlockSpec(s[2], s[3]) for s in outs]
    out_shape = [jax.ShapeDtypeStruct(s[0], s[1]) for s in outs]
    args = [a for a, _, _ in ins] + ([] if into is None else [into])
    if prefetch is None:
        return pl.pallas_call(body, name=name, grid=grid, in_specs=in_specs, out_specs=out_specs, out_shape=out_shape,
                              input_output_aliases={n_in: 0} if n_into else {}, compiler_params=_params())(*args)
    spec = pltpu.PrefetchScalarGridSpec(num_scalar_prefetch=1, grid=grid, in_specs=in_specs, out_specs=out_specs)
    return pl.pallas_call(body, name=name, grid_spec=spec, out_shape=out_shape,
                          input_output_aliases={n_in + 1: 0} if n_into else {}, compiler_params=_params())(prefetch, *args)


def mm_call(name, grid, ins, pairs, n_acc, acc_shape, outs, epilogue, extras=(), after=None):
    n_in, n_ex, nk = len(ins), len(extras), grid[2]
    n_dep = 0 if after is None else 1

    def finish(accs, ex_refs, out_refs):
        res = epilogue(accs, *[r[...] for r in ex_refs])
        if not isinstance(res, (tuple, list)):
            res = (res,)
        for r, o in zip(res, out_refs):
            o[...] = r.astype(o.dtype)

    def body(*refs):
        in_refs, ex_refs = refs[:n_in], refs[n_in:n_in + n_ex]
        refs = refs[n_in + n_ex + n_dep:]
        out_refs, accs = refs[:len(outs)], refs[len(outs):]
        if nk == 1:
            vals = [None] * n_acc
            for ia, ib, ca, cb, ai in pairs:
                d = _dg(in_refs[ia][...], in_refs[ib][...], ca, cb, False)
                vals[ai] = d if vals[ai] is None else vals[ai] + d
            finish(vals, ex_refs, out_refs)
            return
        k = pl.program_id(2)

        @pl.when(k == 0)
        def _():
            for a in accs:
                a[...] = jnp.zeros_like(a)

        for ia, ib, ca, cb, ai in pairs:
            accs[ai][...] += _dg(in_refs[ia][...], in_refs[ib][...], ca, cb, False)

        @pl.when(k == nk - 1)
        def _():
            finish([a[...] for a in accs], ex_refs, out_refs)

    return pl.pallas_call(
        body, name=name, grid=grid,
        in_specs=[pl.BlockSpec(b, im) for _, b, im in list(ins) + list(extras)] + [pl.BlockSpec(memory_space=pl.ANY)] * n_dep,
        out_specs=[pl.BlockSpec(s[2], s[3]) for s in outs],
        out_shape=[jax.ShapeDtypeStruct(s[0], s[1]) for s in outs],
        scratch_shapes=[pltpu.VMEM(acc_shape, F32) for _ in range(n_acc if nk > 1 else 0)],
        compiler_params=_params(),
    )(*[a for a, _, _ in list(ins) + list(extras)], *([] if after is None else [after]))


MM_VMEM_BUDGET = 40 << 20


def _mm_tiles(m, n, kk, sa, sb, so, has_res):
    tk = _tile(kk, 2048)
    best = None
    for tm in (1024, 512, 256, 128):
        for tn in (1024, 512, 256, 128):
            tm_, tn_ = _tile(m, tm), _tile(n, tn)
            need = 2 * (tm_ * tk * sa + tk * tn_ * sb + tm_ * tn_ * so) + (tm_ * tn_ * 4 if tk < kk else 0)
            need += 2 * tm_ * tn_ * 4 if has_res else 0
            if need <= MM_VMEM_BUDGET and (best is None or tm_ * tn_ > best[0] * best[1]):
                best = (tm_, tn_)
    return best + (tk,)


def mm2(name, a, b, ca, cb, res=None, scale=None, out_dtype=F32, after=None):
    m, kk, n = a.shape[1 - ca], a.shape[ca], b.shape[1 - cb]
    tm, tn, tk = _mm_tiles(m, n, kk, a.dtype.itemsize, b.dtype.itemsize, jnp.dtype(out_dtype).itemsize, res is not None)
    a_spec = ((tm, tk), lambda i, j, k: (i, k)) if ca == 1 else ((tk, tm), lambda i, j, k: (k, i))
    b_spec = ((tk, tn), lambda i, j, k: (k, j)) if cb == 0 else ((tn, tk), lambda i, j, k: (j, k))
    extras = [] if res is None else [(res, (tm, tn), lambda i, j, k: (i, j))]

    def epi(accs, *ex):
        r = accs[0] if scale is None else accs[0] * scale
        return r + ex[0] if ex else r

    return mm_call(name, (m // tm, n // tn, kk // tk), [(a,) + a_spec, (b,) + b_spec], [(0, 1, ca, cb, 0)], 1, (tm, tn),
                   [((m, n), out_dtype, (tm, tn), lambda i, j, k: (i, j))], epi, extras, after)[0]


def rows_call(name, fn, rows, consts, outs, acc_outs=(), tr=ROW_TILE):
    s = rows[0].shape[0]
    tr = _tile(s, tr, 8)
    ins = [(r, (tr, r.shape[1]), lambda i: (i, 0)) for r in rows]
    ins += [(c, c.shape, (lambda nd: (lambda i: (0,) * nd))(c.ndim)) for c in consts]
    o = [((s, c), dt, (tr, c), lambda i: (i, 0), None) for c, dt in outs]
    o += [(shp, F32, shp, (lambda nd: (lambda i: (0,) * nd))(len(shp)), (0,)) for shp in acc_outs]
    return tile_call(name, fn, (s // tr,), ins, o)


def _lane_pick(x, h):
    lane = lax.broadcasted_iota(jnp.int32, x.shape, x.ndim - 1)
    return jnp.sum(jnp.where(lane == h, x, 0.0), axis=-1, keepdims=True)


def gateprep_fn(ab, alog, dtb):
    t = ab.shape[0]
    gs, bs = [], []
    for h in range(DN_HEADS):
        a_h = _lane_pick(ab, h)
        b_h = _lane_pick(ab, DN_HEADS + h)
        g_h = -jnp.exp(_lane_pick(alog, h)) * softplus(a_h + _lane_pick(dtb, h))
        gs.append(jnp.broadcast_to(g_h, (t, HEAD_DIM)))
        bs.append(jnp.broadcast_to(sigmoid(b_h), (t, HEAD_DIM)))
    return jnp.concatenate(gs, axis=1), jnp.concatenate(bs, axis=1)


def conv_fn(pids, x, w):
    kind = pids[0] // DN_HEADS
    y = x * w[CONV_WIDTH - 1:CONV_WIDTH]
    for i in range(CONV_WIDTH - 1):
        y = y + shift_down(x, CONV_WIDTH - 1 - i) * w[i:i + 1]
    y = silu(y)
    n = y * lax.rsqrt(jnp.sum(y * y, axis=-1, keepdims=True) + EPS)
    n = n * jnp.where(kind == 0, HEAD_DIM ** -0.5, 1.0)
    return jnp.where(kind == 2, y, n)


def unit_lower_inverse(a):
    c = a.shape[0]
    r = lax.broadcasted_iota(jnp.int32, (c, c), 0)
    cc = lax.broadcasted_iota(jnp.int32, (c, c), 1)
    x = -a
    t = jnp.where(r == cc, 1.0, 0.0) + x
    p = 2
    while p < c:
        x = _dg(x, x, 1, 0, True)
        t = t + _dg(t, x, 1, 0, True)
        p *= 2
    return t


@jax.custom_vjp
def known_inverse(a, t):
    return t


known_inverse.defvjp(lambda a, t: (t, t),
                     lambda t, g: (-_dg(_dg(t, g, 0, 0, True), t, 1, 1, True), jnp.zeros_like(t)))


def intra_head(q, k, v, g, b, t_known):
    c = q.shape[0]
    r = lax.broadcasted_iota(jnp.int32, (c, c), 0)
    cc = lax.broadcasted_iota(jnp.int32, (c, c), 1)
    tril = (r >= cc).astype(F32)
    gc = mmul(tril, g, 1, 0, True)
    m = gc[:, :c]
    decay = jnp.exp(jnp.where(r >= cc, m - m.T, -1e30))
    kb = k * b
    a = jnp.where(r > cc, mmul(kb, k, 1, 1, True) * decay, 0.0)
    t = unit_lower_inverse(a) if t_known is None else known_inverse(a, t_known)
    e = jnp.exp(gc)
    u = mmul(t, v * b, 1, 0, True)
    w = mmul(t, kb * e, 1, 0, True)
    qk = mmul(q, k, 1, 1, True) * decay
    gl = gc[c - 1:c, :]
    kd = k * jnp.exp(gl - gc)
    outs = (u, w, qk, q * e, kd, jnp.broadcast_to(jnp.exp(gl), (8, HEAD_DIM)))
    return outs + (t,) if t_known is None else outs


def _heads(x, h):
    return x[:, h * HEAD_DIM:(h + 1) * HEAD_DIM]


def intra_fn(q, k, v, g, b):
    outs = [intra_head(_heads(q, h), _heads(k, h), _heads(v, h), _heads(g, h), _heads(b, h), None) for h in range(DN_HEADS)]
    cat = lambda i: jnp.concatenate([o[i] for o in outs], axis=1)
    stack = lambda i: jnp.stack([o[i] for o in outs], axis=0)
    return cat(0), cat(1), stack(2), cat(3), cat(4), cat(5), stack(6)


def intra_bwd_fn(q, k, v, g, b, tinv, du, dw, dqk, dqd, dkd, dgl):
    res = []
    for h in range(DN_HEADS):
        hs = lambda x: _heads(x, h)
        res.append(vjp_of(intra_head, 6, (0, 1, 2, 3, 4))(hs(q), hs(k), hs(v), hs(g), hs(b), tinv[h],
                                                         hs(du), hs(dw), dqk[h], hs(dqd), hs(dkd), hs(dgl)))
    return tuple(jnp.concatenate([r[i] for r in res], axis=1) for i in range(5))


def scan_step(s, u, w, qk, qd, kd, gl):
    v_new = u - mmul(w, s, 1, 0, True)
    o = mmul(qd, s, 1, 0, True) + mmul(qk, v_new, 1, 0, True)
    return s * gl[0:1, :] + mmul(kd, v_new, 0, 0, True), o


def outgate_fn(o, z, g):
    return rms(o, g) * silu(z)


def pool_fn(pids, p):
    gid = pids[0]
    s = p.shape[0]
    t1 = (lax.broadcasted_iota(jnp.int32, p.shape, 0) + 1).astype(F32)
    acc, win, out = p, 1, None
    for gi, target in enumerate(POOL_WINDOWS):
        while win < target:
            acc = acc + shift_down(acc, win)
            win *= 2
        cand = acc / jnp.minimum(t1, float(target))
        out = cand if out is None else jnp.where(gid == gi, cand, out)
    return out - p


def poolmix_fn(pooled, pw, scale):
    return mmul(pooled, pw, 1, 0, False) * scale


def attn_fn(q, k, v):
    s = mmul(q, k, 1, 1, False) * (q.shape[1] ** -0.5)
    s = s - jnp.max(s, axis=-1, keepdims=True)
    e = jnp.exp(s)
    p = e / jnp.sum(e, axis=-1, keepdims=True)
    return mmul(p, v, 1, 0, False)


def rms_fwd(name, x, g):
    return rows_call(name, lambda a, b: rms(a, b), [x], [g], [(x.shape[1], BF16)])[0]


def rms_bwd(name, x, g, dy, dres):
    def fn(a, d, r, b):
        dx, dg = vjp_of(rms, 2, (0, 1))(a, b, d)
        return dx + r, dx + r, dg
    return rows_call(name, fn, [x, dy, dres], [g], [(x.shape[1], F32), (x.shape[1], BF16)], [g.shape])


def ffn_fwd(tag, x, g, wg, wu, wd):
    s, d = x.shape
    fj = wg.shape[-1]
    f = N_SHARDS * fj
    xn = rms_fwd(tag + "_norm", x, g)
    tm = _tile(s, 256)
    w_spec = ((None, d, fj), lambda j, i, k: (j, 0, 0))
    o_spec = ((tm, fj), lambda j, i, k: (i, j))
    gate, up, act = mm_call(
        tag + "_gu", (N_SHARDS, s // tm, 1),
        [(xn, (tm, d), lambda j, i, k: (i, 0)), (wg,) + w_spec, (wu,) + w_spec],
        [(0, 1, 1, 0, 0), (0, 2, 1, 0, 1)], 2, (tm, fj),
        [((s, f), F32) + o_spec, ((s, f), F32) + o_spec, ((s, f), BF16) + o_spec],
        lambda accs: (accs[0], accs[1], swiglu(accs[0], accs[1])))
    tm, tn = _tile(s, 1024), _tile(d, 1024)
    out = mm_call(
        tag + "_down", (s // tm, d // tn, N_SHARDS),
        [(act, (tm, fj), lambda i, j, k: (i, k)), (wd, (None, fj, tn), lambda i, j, k: (k, 0, j))],
        [(0, 1, 1, 0, 0)], 1, (tm, tn),
        [((s, d), F32, (tm, tn), lambda i, j, k: (i, j))],
        lambda accs, r: r + 0.5 * accs[0], [(x, (tm, tn), lambda i, j, k: (i, j))])[0]
    return out, (x, xn, gate, up, act)


def ffn_bwd(tag, saved, g, wg, wu, wd, dout, dout_b, after=None):
    x, xn, gate, up, act = saved
    s, d = x.shape
    fj = wg.shape[-1]
    f = N_SHARDS * fj
    tm = _tile(s, 512)
    o_spec = ((tm, fj), lambda j, i, k: (i, j))

    def epi(accs, ga, u):
        dgate, dup = vjp_of(swiglu, 2, (0, 1))(ga, u, 0.5 * accs[0])
        return dgate, dup

    dgate, dup = mm_call(
        tag + "_dact", (N_SHARDS, s // tm, 1),
        [(dout_b, (tm, d), lambda j, i, k: (i, 0)), (wd, (None, fj, d), lambda j, i, k: (j, 0, 0))],
        [(0, 1, 1, 1, 0)], 1, (tm, fj),
        [((s, f), BF16) + o_spec, ((s, f), BF16) + o_spec], epi,
        [(gate,) + o_spec, (up,) + o_spec], after=after)
    tn = _tile(d, 1024)
    dwd = mm_call(
        tag + "_dwd", (N_SHARDS, d // tn, 1),
        [(act, (s, fj), lambda j, i, k: (0, j)), (dout_b, (s, tn), lambda j, i, k: (0, i))],
        [(0, 1, 0, 0, 0)], 1, (fj, tn),
        [((N_SHARDS, fj, d), F32, (None, fj, tn), lambda j, i, k: (j, 0, i))],
        lambda accs: 0.5 * accs[0])[0]
    td = _tile(d, 512)
    g_spec = ((s, fj), lambda j, i, k: (0, j))
    w_out = ((N_SHARDS, d, fj), F32, (None, td, fj), lambda j, i, k: (j, i, 0))
    dwg, dwu = mm_call(
        tag + "_dwgu", (N_SHARDS, d // td, 1),
        [(xn, (s, td), lambda j, i, k: (0, i)), (dgate,) + g_spec, (dup,) + g_spec],
        [(0, 1, 0, 0, 0), (0, 2, 0, 0, 1)], 2, (td, fj), [w_out, w_out], lambda accs: (accs[0], accs[1]))
    tm, tn = _tile(s, 1024), _tile(d, 1024)
    a_spec = ((tm, fj), lambda i, j, k: (i, k))
    wt_spec = ((None, tn, fj), lambda i, j, k: (k, j, 0))
    dxn = mm_call(
        tag + "_dxn", (s // tm, d // tn, N_SHARDS),
        [(dgate,) + a_spec, (wg,) + wt_spec, (dup,) + a_spec, (wu,) + wt_spec],
        [(0, 1, 1, 1, 0), (2, 3, 1, 1, 0)], 1, (tm, tn),
        [((s, d), F32, (tm, tn), lambda i, j, k: (i, j))], lambda accs: accs[0])[0]
    dx, dx_b, dg = rms_bwd(tag + "_dnorm", x, g, dxn, dout)
    return dx, dx_b, dg, dwg, dwu, dwd


def _hspec(tc, width=HEAD_DIM):
    return (tc, width)


def mixer_fwd(tag, h, wts):
    s, d = h.shape
    dnw = DN_HEADS * HEAD_DIM
    pw_ = d - dnw
    gdim = pw_ // len(POOL_WINDOWS)
    nc = s // CHUNK
    hn = rms_fwd(tag + "_norm", h, wts["mix_norm"])
    qkv = mm2(tag + "_qkv", hn, wts["w_qkv"], 1, 0)
    z = mm2(tag + "_z", hn, wts["w_z"], 1, 0)
    ab = mm2(tag + "_ab", hn, wts["w_ab"], 1, 0)
    p = mm2(tag + "_p", hn, wts["w_p"], 1, 0)
    qkvn = tile_call(tag + "_conv", conv_fn, (3 * DN_HEADS,),
                     [(qkv, (s, HEAD_DIM), lambda i: (0, i)), (wts["conv_w"], (CONV_WIDTH, HEAD_DIM), lambda i: (0, i))],
                     [((s, 3 * dnw), F32, (s, HEAD_DIM), lambda i: (0, i), None)], with_pids=True)[0]
    g_bc, b_bc = rows_call(tag + "_gates", gateprep_fn, [ab], [wts["a_log"], wts["dt_bias"]], [(dnw, F32), (dnw, F32)])
    cw = (CHUNK, dnw)
    sq = ((DN_HEADS, s, CHUNK), F32, (DN_HEADS, CHUNK, CHUNK), lambda n: (0, n, 0), None)
    u, w, qk, qd, kd, gl, tinv = tile_call(
        tag + "_intra", intra_fn, (nc,),
        [(qkvn, cw, lambda n: (n, 0)), (qkvn, cw, lambda n: (n, 1)), (qkvn, cw, lambda n: (n, 2)),
         (g_bc, cw, lambda n: (n, 0)), (b_bc, cw, lambda n: (n, 0))],
        [((s, dnw), F32, cw, lambda n: (n, 0), None), ((s, dnw), F32, cw, lambda n: (n, 0), None), sq,
         ((s, dnw), F32, cw, lambda n: (n, 0), None), ((s, dnw), F32, cw, lambda n: (n, 0), None),
         ((nc * 8, dnw), F32, (8, dnw), lambda n: (n, 0), None), sq])
    o, states = scan_fwd(tag + "_scan", u, w, qk, qd, kd, gl)
    y_dn = tile_call(
        tag + "_outgate", outgate_fn, (DN_HEADS, s // ROW_TILE),
        [(o, (ROW_TILE, HEAD_DIM), lambda hh, i: (i, hh)), (z, (ROW_TILE, HEAD_DIM), lambda hh, i: (i, hh)),
         (wts["dn_out_norm"], (1, HEAD_DIM), lambda hh, i: (0, 0))],
        [((s, dnw), BF16, (ROW_TILE, HEAD_DIM), lambda hh, i: (i, hh), None)])[0]
    ng = len(POOL_WINDOWS)
    pooled = tile_call(tag + "_pool", pool_fn, (ng,), [(p, (s, gdim), lambda i: (0, i))],
                       [((s, pw_), BF16, (s, gdim), lambda i: (0, i), None)], with_pids=True)[0]
    tp = _tile(s, 512)
    y_pool = tile_call(
        tag + "_poolmix", poolmix_fn, (ng, s // tp),
        [(pooled, (tp, gdim), lambda gi, i: (i, gi)), (wts["pool_w"], (None, gdim, gdim), lambda gi, i: (gi, 0, 0)),
         (wts["pool_scale"], (1, gdim), lambda gi, i: (0, gi))],
        [((s, pw_), BF16, (tp, gdim), lambda gi, i: (i, gi), None)])[0]
    h1 = mm2(tag + "_out_a", y_dn, wts["w_out_a"], 1, 0, res=h)
    h2 = mm2(tag + "_out_b", y_pool, wts["w_out_b"], 1, 0, res=h1)
    saved = (h, hn, qkv, z, ab, p, qkvn, g_bc, b_bc, u, w, qk, qd, kd, gl, tinv, o, states, y_dn, pooled, y_pool)
    return h2, saved


def scan_fwd(name, u, w, qk, qd, kd, gl):
    s, dnw = u.shape
    nc = s // CHUNK
    cw = (CHUNK, dnw)

    def body(u_r, w_r, qk_r, qd_r, kd_r, gl_r, o_r, st_r, state):
        @pl.when(pl.program_id(0) == 0)
        def _():
            state[...] = jnp.zeros_like(state)

        st_r[...] = state[...]
        outs = []
        for h in range(DN_HEADS):
            hs = slice(h * HEAD_DIM, (h + 1) * HEAD_DIM)
            s_new, o_h = scan_step(state[hs, :], u_r[:, hs], w_r[:, hs], qk_r[h], qd_r[:, hs], kd_r[:, hs], gl_r[:, hs])
            state[hs, :] = s_new
            outs.append(o_h)
        o_r[...] = jnp.concatenate(outs, axis=1)

    row = lambda n: (n, 0)
    return pl.pallas_call(
        body, name=name, grid=(nc,),
        in_specs=[pl.BlockSpec(cw, row), pl.BlockSpec(cw, row), pl.BlockSpec((DN_HEADS, CHUNK, CHUNK), lambda n: (0, n, 0)),
                  pl.BlockSpec(cw, row), pl.BlockSpec(cw, row), pl.BlockSpec((8, dnw), row)],
        out_specs=[pl.BlockSpec(cw, row), pl.BlockSpec((None, dnw, HEAD_DIM), lambda n: (n, 0, 0))],
        out_shape=[jax.ShapeDtypeStruct((s, dnw), F32), jax.ShapeDtypeStruct((nc, dnw, HEAD_DIM), F32)],
        scratch_shapes=[pltpu.VMEM((dnw, HEAD_DIM), F32)],
        compiler_params=_params(),
    )(u, w, qk, qd, kd, gl)


def scan_bwd(name, states, u, w, qk, qd, kd, gl, do):
    s, dnw = u.shape
    nc = s // CHUNK
    cw = (CHUNK, dnw)

    def body(st_r, u_r, w_r, qk_r, qd_r, kd_r, gl_r, do_r, du_r, dw_r, dqk_r, dqd_r, dkd_r, dgl_r, dstate):
        @pl.when(pl.program_id(0) == 0)
        def _():
            dstate[...] = jnp.zeros_like(dstate)

        res = []
        for h in range(DN_HEADS):
            hs = slice(h * HEAD_DIM, (h + 1) * HEAD_DIM)
            r = vjp_of(scan_step, 7, tuple(range(7)))(
                st_r[hs, :], u_r[:, hs], w_r[:, hs], qk_r[h], qd_r[:, hs], kd_r[:, hs], gl_r[:, hs],
                dstate[hs, :], do_r[:, hs])
            dstate[hs, :] = r[0]
            res.append(r)
        cat = lambda i: jnp.concatenate([r[i] for r in res], axis=1)
        du_r[...] = cat(1)
        dw_r[...] = cat(2)
        dqk_r[...] = jnp.stack([r[3] for r in res], axis=0)
        dqd_r[...] = cat(4)
        dkd_r[...] = cat(5)
        dgl_r[...] = cat(6)

    row = lambda n: (nc - 1 - n, 0)
    qk_spec = pl.BlockSpec((DN_HEADS, CHUNK, CHUNK), lambda n: (0, nc - 1 - n, 0))
    return pl.pallas_call(
        body, name=name, grid=(nc,),
        in_specs=[pl.BlockSpec((None, dnw, HEAD_DIM), lambda n: (nc - 1 - n, 0, 0)), pl.BlockSpec(cw, row), pl.BlockSpec(cw, row),
                  qk_spec, pl.BlockSpec(cw, row), pl.BlockSpec(cw, row), pl.BlockSpec((8, dnw), row), pl.BlockSpec(cw, row)],
        out_specs=[pl.BlockSpec(cw, row), pl.BlockSpec(cw, row), qk_spec, pl.BlockSpec(cw, row), pl.BlockSpec(cw, row),
                   pl.BlockSpec((8, dnw), row)],
        out_shape=[jax.ShapeDtypeStruct((s, dnw), F32), jax.ShapeDtypeStruct((s, dnw), F32),
                   jax.ShapeDtypeStruct((DN_HEADS, s, CHUNK), F32), jax.ShapeDtypeStruct((s, dnw), F32),
                   jax.ShapeDtypeStruct((s, dnw), F32), jax.ShapeDtypeStruct((nc * 8, dnw), F32)],
        scratch_shapes=[pltpu.VMEM((dnw, HEAD_DIM), F32)],
        compiler_params=_params(),
    )(states, u, w, qk, qd, kd, gl, do)


def mixer_bwd(tag, saved, wts, dout, dout_b, after=None):
    h, hn, qkv, z, ab, p, qkvn, g_bc, b_bc, u, w, qk, qd, kd, gl, tinv, o, states, y_dn, pooled, y_pool = saved
    s, d = h.shape
    dnw = DN_HEADS * HEAD_DIM
    pw_ = d - dnw
    ng = len(POOL_WINDOWS)
    gdim = pw_ // ng
    nc = s // CHUNK
    gr = {}
    d_ydn = mm2(tag + "_dydn", dout_b, wts["w_out_a"], 1, 1, after=after)
    d_ypool = mm2(tag + "_dypool", dout_b, wts["w_out_b"], 1, 1)
    gr["w_out_a"] = mm2(tag + "_dwout_a", y_dn, dout_b, 0, 0)
    gr["w_out_b"] = mm2(tag + "_dwout_b", y_pool, dout_b, 0, 0)
    tp = _tile(s, 512)
    d_pooled, gr["pool_w"], gr["pool_scale"] = tile_call(
        tag + "_dpoolmix", vjp_of(poolmix_fn, 3, (0, 1, 2)), (ng, s // tp),
        [(pooled, (tp, gdim), lambda gi, i: (i, gi)), (wts["pool_w"], (None, gdim, gdim), lambda gi, i: (gi, 0, 0)),
         (wts["pool_scale"], (1, gdim), lambda gi, i: (0, gi)), (d_ypool, (tp, gdim), lambda gi, i: (i, gi))],
        [((s, pw_), F32, (tp, gdim), lambda gi, i: (i, gi), None),
         ((ng, gdim, gdim), F32, (None, gdim, gdim), lambda gi, i: (gi, 0, 0), (1,)),
         ((1, pw_), F32, (1, gdim), lambda gi, i: (0, gi), (1,))])
    d_p = tile_call(tag + "_dpool", vjp_of(pool_fn, 1, (0,), True), (ng,),
                    [(p, (s, gdim), lambda i: (0, i)), (d_pooled, (s, gdim), lambda i: (0, i))],
                    [((s, pw_), BF16, (s, gdim), lambda i: (0, i), None)], with_pids=True)[0]
    hb = (ROW_TILE, HEAD_DIM)
    d_o, d_z, gr["dn_out_norm"] = tile_call(
        tag + "_doutgate", vjp_of(outgate_fn, 3, (0, 1, 2)), (DN_HEADS, s // ROW_TILE),
        [(o, hb, lambda hh, i: (i, hh)), (z, hb, lambda hh, i: (i, hh)), (wts["dn_out_norm"], (1, HEAD_DIM), lambda hh, i: (0, 0)),
         (d_ydn, hb, lambda hh, i: (i, hh))],
        [((s, dnw), F32, hb, lambda hh, i: (i, hh), None), ((s, dnw), BF16, hb, lambda hh, i: (i, hh), None),
         ((1, HEAD_DIM), F32, (1, HEAD_DIM), lambda hh, i: (0, 0), (0, 1))])
    du, dw, dqk, dqd, dkd, dgl = scan_bwd(tag + "_dscan", states, u, w, qk, qd, kd, gl, d_o)
    cw = (CHUNK, dnw)
    row = lambda n: (n, 0)
    dq, dk, dv, dg_bc, db_bc = tile_call(
        tag + "_dintra", intra_bwd_fn, (nc,),
        [(qkvn, cw, lambda n: (n, 0)), (qkvn, cw, lambda n: (n, 1)), (qkvn, cw, lambda n: (n, 2)),
         (g_bc, cw, row), (b_bc, cw, row), (tinv, (DN_HEADS, CHUNK, CHUNK), lambda n: (0, n, 0)), (du, cw, row), (dw, cw, row),
         (dqk, (DN_HEADS, CHUNK, CHUNK), lambda n: (0, n, 0)), (dqd, cw, row), (dkd, cw, row), (dgl, (8, dnw), row)],
        [((s, dnw), F32, cw, row, None)] * 5)
    d_ab, gr["a_log"], gr["dt_bias"] = rows_call(
        tag + "_dgates", lambda a, dg, db, al, dt: vjp_of(gateprep_fn, 3, (0, 1, 2))(a, al, dt, dg, db),
        [ab, dg_bc, db_bc], [wts["a_log"], wts["dt_bias"]], [(AB_PAD, BF16)], [(1, AB_PAD), (1, AB_PAD)])
    d_qkvn = jnp.concatenate([dq, dk, dv], axis=1)
    d_qkv, gr["conv_w"] = tile_call(
        tag + "_dconv", vjp_of(conv_fn, 2, (0, 1), True), (3 * DN_HEADS,),
        [(qkv, (s, HEAD_DIM), lambda i: (0, i)), (wts["conv_w"], (CONV_WIDTH, HEAD_DIM), lambda i: (0, i)),
         (d_qkvn, (s, HEAD_DIM), lambda i: (0, i))],
        [((s, 3 * dnw), BF16, (s, HEAD_DIM), lambda i: (0, i), None),
         ((CONV_WIDTH, 3 * dnw), F32, (CONV_WIDTH, HEAD_DIM), lambda i: (0, i), None)], with_pids=True)
    gr["w_qkv"] = mm2(tag + "_dwqkv", hn, d_qkv, 0, 0)
    gr["w_z"] = mm2(tag + "_dwz", hn, d_z, 0, 0)
    gr["w_ab"] = mm2(tag + "_dwab", hn, d_ab, 0, 0)
    gr["w_p"] = mm2(tag + "_dwp", hn, d_p, 0, 0)
    d_hn = mm2(tag + "_dhn1", d_qkv, wts["w_qkv"], 1, 1)
    d_hn = mm2(tag + "_dhn2", d_z, wts["w_z"], 1, 1, res=d_hn)
    d_hn = mm2(tag + "_dhn3", d_ab, wts["w_ab"], 1, 1, res=d_hn)
    d_hn = mm2(tag + "_dhn4", d_p, wts["w_p"], 1, 1, res=d_hn)
    dh, dh_b, gr["mix_norm"] = rms_bwd(tag + "_dnorm", h, wts["mix_norm"], d_hn, dout)
    return dh, dh_b, gr


def xattn_fwd(tag, h, memn, wts):
    s, d = h.shape
    m = memn.shape[0]
    dh_ = d // X_HEADS
    hn = rms_fwd(tag + "_norm", h, wts["xattn_norm"])
    q = mm2(tag + "_q", hn, wts["wq"], 1, 0, out_dtype=BF16)
    kv = mm2(tag + "_kv", memn, wts["wkv"], 1, 0, out_dtype=BF16)
    tq = _tile(s, 512)
    o = tile_call(
        tag + "_attn", attn_fn, (X_HEADS, s // tq),
        [(q, (tq, dh_), lambda hh, i: (i, hh)), (kv, (m, dh_), lambda hh, i: (0, hh)), (kv, (m, dh_), lambda hh, i: (0, X_HEADS + hh))],
        [((s, d), BF16, (tq, dh_), lambda hh, i: (i, hh), None)])[0]
    out = mm2(tag + "_o", o, wts["wo"], 1, 0, res=h)
    return out, (h, hn, q, kv, o)


def xattn_bwd(tag, saved, memn, mem, wts, dout, dout_b, after=None):
    h, hn, q, kv, o = saved
    s, d = h.shape
    m = memn.shape[0]
    dh_ = d // X_HEADS
    gr = {}
    d_o = mm2(tag + "_do", dout_b, wts["wo"], 1, 1, out_dtype=BF16, after=after)
    gr["wo"] = mm2(tag + "_dwo", o, dout_b, 0, 0)
    tq = _tile(s, 512)
    dq, dk, dv = tile_call(
        tag + "_dattn", vjp_of(attn_fn, 3, (0, 1, 2)), (X_HEADS, s // tq),
        [(q, (tq, dh_), lambda hh, i: (i, hh)), (kv, (m, dh_), lambda hh, i: (0, hh)), (kv, (m, dh_), lambda hh, i: (0, X_HEADS + hh)),
         (d_o, (tq, dh_), lambda hh, i: (i, hh))],
        [((s, d), BF16, (tq, dh_), lambda hh, i: (i, hh), None),
         ((m, d), F32, (m, dh_), lambda hh, i: (0, hh), (1,)), ((m, d), F32, (m, dh_), lambda hh, i: (0, hh), (1,))])
    dkv = jnp.concatenate([dk, dv], axis=1).astype(BF16)
    gr["wq"] = mm2(tag + "_dwq", hn, dq, 0, 0)
    gr["wkv"] = mm2(tag + "_dwkv", memn, dkv, 0, 0)
    d_memn = mm2(tag + "_dmemn", dkv, wts["wkv"], 1, 1)
    gr["mem_norm"] = rows_call(
        tag + "_dmemnorm", lambda a, dy, b: vjp_of(rms, 2, (1,))(a, b, dy)[0], [mem, d_memn], [wts["mem_norm"]], [],
        [wts["mem_norm"].shape], tr=128)[0]
    d_hn = mm2(tag + "_dhn", dq, wts["wq"], 1, 1)
    dh, dh_b, gr["xattn_norm"] = rms_bwd(tag + "_dnorm", h, wts["xattn_norm"], d_hn, dout)
    return dh, dh_b, gr


def final_loss(x, g, tgt):
    d = x.shape[1]

    def fn(a, t, b):
        def f(aa, bb):
            return 0.5 * jnp.sum(jnp.square(rms(aa, bb) - t)) / d
        loss, (dx, dg) = jax.value_and_grad(f, (0, 1))(a, b)
        lane = lax.broadcasted_iota(jnp.int32, (1, 128), 1)
        return dx, dx, dg, jnp.where(lane == 0, loss, 0.0)
    return rows_call("final_loss", fn, [x, tgt], [g], [(d, F32), (d, BF16)], [g.shape, (1, 128)])


SHARDED = ("ffn1_w_gate", "ffn1_w_up", "ffn1_w_down", "w_in", "conv_w", "pool_w", "w_out", "xattn_wq", "xattn_wkv",
           "xattn_wo", "ffn2_w_gate", "ffn2_w_up", "ffn2_w_down")
REPLICATED = ("ffn1_norm", "mix_norm", "a_log", "dt_bias", "dn_out_norm", "pool_scale", "xattn_norm", "mem_norm",
              "ffn2_norm", "final_norm")
WEIGHTS = ("ffn1_norm", "ffn1_w_gate", "ffn1_w_up", "ffn1_w_down", "mix_norm", "w_in", "conv_w", "a_log", "dt_bias",
           "dn_out_norm", "pool_w", "pool_scale", "w_out", "xattn_norm", "mem_norm", "xattn_wq", "xattn_wkv", "xattn_wo",
           "ffn2_norm", "ffn2_w_gate", "ffn2_w_up", "ffn2_w_down", "final_norm")


def _lane_pad(v, width=128):
    return jnp.pad(v, (0, width - v.shape[0]))[None, :]


GROUPS = {"ffn1": ("ffn1_w_gate", "ffn1_w_up", "ffn1_w_down"), "mixer": ("w_in", "conv_w", "pool_w", "w_out"),
          "xattn": ("xattn_wq", "xattn_wkv", "xattn_wo"), "ffn2": ("ffn2_w_gate", "ffn2_w_up", "ffn2_w_down")}


def _cols(g):
    return jnp.transpose(g, (1, 0, 2)).reshape(g.shape[1], -1)


def _rows(g):
    return g.reshape(-1, g.shape[2])


def _rep_row(rep, name, l):
    return rep[name][l][None, :].astype(F32)


def mixer_weights(g, rep, l):
    dnw = DN_HEADS * HEAD_DIM
    w_in = _cols(g["w_in"])
    o_ab = 4 * dnw
    w_out = _rows(g["w_out"])
    gdim = g["pool_w"].shape[-1]
    pw = jnp.transpose(g["pool_w"].reshape(N_SHARDS, len(POOL_WINDOWS), gdim // N_SHARDS, gdim), (1, 0, 2, 3))
    return dict(
        mix_norm=_rep_row(rep, "mix_norm", l), w_qkv=w_in[:, :3 * dnw], w_z=w_in[:, 3 * dnw:o_ab],
        w_ab=jnp.pad(w_in[:, o_ab:o_ab + 2 * DN_HEADS], ((0, 0), (0, AB_PAD - 2 * DN_HEADS))),
        w_p=w_in[:, o_ab + 2 * DN_HEADS:], conv_w=_cols(g["conv_w"].reshape(N_SHARDS, CONV_WIDTH, -1)).astype(F32),
        a_log=_lane_pad(rep["a_log"][l].astype(F32)), dt_bias=_lane_pad(rep["dt_bias"][l].astype(F32)),
        dn_out_norm=_rep_row(rep, "dn_out_norm", l), pool_w=pw.reshape(len(POOL_WINDOWS), gdim, gdim),
        pool_scale=_rep_row(rep, "pool_scale", l), w_out_a=w_out[:dnw], w_out_b=w_out[dnw:])


def xattn_weights(g, rep, l):
    return dict(xattn_norm=_rep_row(rep, "xattn_norm", l), mem_norm=_rep_row(rep, "mem_norm", l), wq=_rows(g["xattn_wq"]),
                wkv=_cols(g["xattn_wkv"]), wo=_rows(g["xattn_wo"]))


def _col_shards(g):
    k, n = g.shape
    return jnp.transpose(g.reshape(k, N_SHARDS, n // N_SHARDS), (1, 0, 2))


MIN_COMM_ROWS = 32


def _comm_rows(a):
    if a.shape[-2] >= MIN_COMM_ROWS:
        return a
    return a.reshape(a.shape[:-2] + (MIN_COMM_ROWS, -1))


GRAD_PARTS = (("ffn2_w_gate", "ffn2_w_up", "ffn2_w_down", "xattn_wq", "xattn_wkv", "xattn_wo"),
              ("w_in", "conv_w", "pool_w", "w_out"), ("ffn1_w_gate", "ffn1_w_up", "ffn1_w_down"))


def model_grads(x, mem, tgt, fetch, rep, grads_done=None):
    s, d = x.shape
    depth = rep["ffn1_norm"].shape[0]
    saved, wl = [], []
    h = x
    for l in range(depth):
        t = "l%d" % l
        f1 = fetch(l, "ffn1", h)
        h, s1 = ffn_fwd(t + "_ffn1", h, _rep_row(rep, "ffn1_norm", l), f1["ffn1_w_gate"], f1["ffn1_w_up"], f1["ffn1_w_down"])
        wm = mixer_weights(fetch(l, "mixer", h), rep, l)
        h, s2 = mixer_fwd(t + "_mix", h, wm)
        wx = xattn_weights(fetch(l, "xattn", h), rep, l)
        memn = rms_fwd(t + "_memnorm", mem, wx["mem_norm"])
        h, s3 = xattn_fwd(t + "_xattn", h, memn, wx)
        f2 = fetch(l, "ffn2", h)
        h, s4 = ffn_fwd(t + "_ffn2", h, _rep_row(rep, "ffn2_norm", l), f2["ffn2_w_gate"], f2["ffn2_w_up"], f2["ffn2_w_down"])
        saved.append((memn, s1, s2, s3, s4))
        wl.append((f1, wm, wx, f2))
    dh, dh_b, d_final, loss_row = final_loss(h, rep["final_norm"][None, :].astype(F32), tgt)
    big, small = [None] * depth, [None] * depth
    after = None
    done = (lambda l, part, gb: None) if grads_done is None else grads_done
    for l in reversed(range(depth)):
        f1, wm, wx, f2 = wl[l]
        t = "l%d" % l
        memn, s1, s2, s3, s4 = saved[l]
        gb, gs = {}, {}
        dh, dh_b, gs["ffn2_norm"], gb["ffn2_w_gate"], gb["ffn2_w_up"], gb["ffn2_w_down"] = ffn_bwd(
            t + "_ffn2", s4, _rep_row(rep, "ffn2_norm", l), f2["ffn2_w_gate"], f2["ffn2_w_up"], f2["ffn2_w_down"], dh, dh_b, after)
        dh, dh_b, gx = xattn_bwd(t + "_xattn", s3, memn, mem, wx, dh, dh_b)
        gb["xattn_wq"] = gx["wq"].reshape(N_SHARDS, d // N_SHARDS, d)
        gb["xattn_wo"] = gx["wo"].reshape(N_SHARDS, d // N_SHARDS, d)
        gb["xattn_wkv"] = _col_shards(gx["wkv"])
        after = done(l, 0, gb)
        dh, dh_b, gm = mixer_bwd(t + "_mix", s2, wm, dh, dh_b, after)
        gb["w_in"] = _col_shards(jnp.concatenate([gm["w_qkv"], gm["w_z"], gm["w_ab"][:, :2 * DN_HEADS], gm["w_p"]], axis=1))
        gb["conv_w"] = _comm_rows(_col_shards(gm["conv_w"]))
        gdim = gm["pool_w"].shape[-1]
        gb["pool_w"] = jnp.transpose(gm["pool_w"].reshape(len(POOL_WINDOWS), N_SHARDS, gdim // N_SHARDS, gdim),
                                     (1, 0, 2, 3)).reshape(N_SHARDS, gdim, gdim)
        gb["w_out"] = jnp.concatenate([gm["w_out_a"], gm["w_out_b"]], axis=0).reshape(N_SHARDS, d // N_SHARDS, d)
        after = done(l, 1, gb)
        dh, dh_b, gs["ffn1_norm"], gb["ffn1_w_gate"], gb["ffn1_w_up"], gb["ffn1_w_down"] = ffn_bwd(
            t + "_ffn1", s1, _rep_row(rep, "ffn1_norm", l), f1["ffn1_w_gate"], f1["ffn1_w_up"], f1["ffn1_w_down"], dh, dh_b, after)
        after = done(l, 2, gb)
        for n in ("xattn_norm", "mem_norm"):
            gs[n] = gx[n]
        for n in ("mix_norm", "a_log", "dt_bias", "dn_out_norm", "pool_scale"):
            gs[n] = gm[n]
        big[l], small[l] = gb, gs
    return loss_row, dh, big, small, d_final


HBM = pl.BlockSpec(memory_space=pltpu.HBM)


def _place():
    x, y, c = lax.axis_index("x"), lax.axis_index("y"), lax.axis_index("c")
    chips = [(1 - x, y), (x, 1 - y), (1 - x, 1 - y)]
    return x, y, c, 2 * x + y, chips, [2 * px + py for px, py in chips]


SEM = pl.BlockSpec(memory_space=pltpu.SEMAPHORE)
SPLIT_COPY = pltpu.CompilerParams(has_side_effects=pltpu.SideEffectType.DATAFLOW_SIDE_EFFECTING)


def _half(ref, lead, c):
    r = ref.shape[1] // 2
    return ref.at[lead, pl.ds(c * r, r)]


def place_shards(name, w, where, dtype):
    nl, r, cdim = w.shape
    tr = _rtile(r)
    out = ((N_SHARDS, r, cdim), dtype, (None, tr, cdim), lambda i, p: (p[0], i, 0), None)
    return tile_call(name, lambda a: tuple(a[l] for l in range(nl)), (r // tr,),
                     [(w, (nl, tr, cdim), lambda i, p: (0, i, 0))], [out] * nl, prefetch=where)


def _gather_copies(bufs, ssem, rsem):
    x, y, c, j_own, chips, js = _place()
    res = []
    for i, b in enumerate(bufs):
        for k, (px, py) in enumerate(chips):
            mk = lambda slot: pltpu.make_async_remote_copy(
                src_ref=_half(b, slot, c), dst_ref=_half(b, slot, c), send_sem=ssem.at[3 * i + k],
                recv_sem=rsem.at[3 * i + k], device_id=(px, py, c), device_id_type=MESH)
            res.append((mk(j_own), mk(js[k])))
    return res


def gather_start(bufs, groups):
    n, ng = len(bufs), len(groups)

    def body(*refs):
        sems, outs, token = refs[n:n + 2 * ng], refs[n + 2 * ng:2 * n + 2 * ng], refs[2 * n + 2 * ng]
        for gi, group in enumerate(groups):
            for mine, _ in _gather_copies([outs[t] for t in group], sems[2 * gi], sems[2 * gi + 1]):
                mine.start()
        token[...] = jnp.zeros_like(token)

    sem_shapes = []
    for group in groups:
        sem_shapes += [pltpu.SemaphoreType.DMA((3 * len(group),))] * 2
    res = pl.pallas_call(
        body, name="gather_start", in_specs=[HBM] * n,
        out_specs=[SEM] * (2 * ng) + [HBM] * n + [pl.BlockSpec(memory_space=pltpu.VMEM)],
        out_shape=sem_shapes + [pltpu.HBM(a.shape, a.dtype) for a in bufs] + [jax.ShapeDtypeStruct((8, 128), F32)],
        input_output_aliases={t: 2 * ng + t for t in range(n)}, compiler_params=SPLIT_COPY,
    )(*[pltpu.with_memory_space_constraint(a, pltpu.HBM) for a in bufs])
    sems = [(res[2 * gi], res[2 * gi + 1]) for gi in range(ng)]
    return sems, res[2 * ng:2 * ng + n], res[-1]


def gather_wait(name, bufs, sems, after):
    n = len(bufs)

    def body(*refs):
        ins, ssem, rsem = refs[:n], refs[n], refs[n + 1]
        for mine, theirs in _gather_copies(ins, ssem, rsem):
            mine.wait_send()
            theirs.wait_recv()

    return pl.pallas_call(
        body, name=name, in_specs=[HBM] * n + [SEM, SEM, pl.BlockSpec(memory_space=pl.ANY)], out_specs=[HBM] * n,
        out_shape=[pltpu.HBM(a.shape, a.dtype) for a in bufs],
        input_output_aliases={t: t for t in range(n)}, compiler_params=SPLIT_COPY,
    )(*bufs, sems[0], sems[1], after)


def gather_swap(name, bufs):
    n = len(bufs)

    def body(*refs):
        outs = refs[n:2 * n]
        ssem, rsem = refs[2 * n:]
        x, y, c, j_own, chips, js = _place()

        def copy(i, k, half):
            blk = _half(outs[i], js[k], half)
            return pltpu.make_async_remote_copy(src_ref=blk, dst_ref=blk, send_sem=ssem.at[3 * i + k],
                                                recv_sem=rsem.at[3 * i + k], device_id=(x, y, 1 - c), device_id_type=MESH)

        for i in range(n):
            for k in range(3):
                copy(i, k, c).start()
        for i in range(n):
            for k in range(3):
                copy(i, k, 1 - c).wait_recv()
                copy(i, k, c).wait_send()

    return pl.pallas_call(
        body, name=name, in_specs=[HBM] * n, out_specs=[HBM] * n, input_output_aliases={t: t for t in range(n)},
        out_shape=[jax.ShapeDtypeStruct(a.shape, a.dtype) for a in bufs],
        scratch_shapes=[pltpu.SemaphoreType.DMA((3 * n,)), pltpu.SemaphoreType.DMA((3 * n,))],
    )(*bufs)


def _pair_copies(grads, lands, ssem, rsem):
    x, y, c, _, _, _ = _place()
    res = []
    for t in range(len(grads)):
        r = grads[t].shape[1] // 2
        res.append(pltpu.make_async_remote_copy(src_ref=grads[t].at[:, pl.ds((1 - c) * r, r)], dst_ref=lands[t], send_sem=ssem.at[t],
                                                recv_sem=rsem.at[t], device_id=(x, y, 1 - c), device_id_type=MESH))
    return res


def _split_start(name, copies, arrays, n_sems, after=None):
    n = len(arrays)
    n_dep = 0 if after is None else 1

    def body(*refs):
        refs = refs[n + n_dep:]
        for cp in copies(refs[2:n + 2], refs[0], refs[1]):
            cp.start()
        refs[n + 2][...] = jnp.zeros((8, 128), F32)

    res = pl.pallas_call(
        body, name=name, in_specs=[HBM] * n + [pl.BlockSpec(memory_space=pl.ANY)] * n_dep,
        out_specs=[SEM, SEM] + [HBM] * n + [pl.BlockSpec(memory_space=pltpu.VMEM)],
        out_shape=[pltpu.SemaphoreType.DMA((n_sems,))] * 2 + [pltpu.HBM(a.shape, a.dtype) for a in arrays]
        + [jax.ShapeDtypeStruct((8, 128), F32)],
        input_output_aliases={t: 2 + t for t in range(n)}, compiler_params=SPLIT_COPY,
    )(*[pltpu.with_memory_space_constraint(a, pltpu.HBM) for a in arrays], *([] if after is None else [after]))
    return res[0], res[1], res[2:2 + n], res[-1]


def _split_wait(name, copies, ssem, rsem, arrays, after):
    n = len(arrays)

    def body(*refs):
        for cp in copies(refs[:n], refs[n], refs[n + 1]):
            cp.wait_send()
            cp.wait_recv()

    return pl.pallas_call(
        body, name=name, in_specs=[HBM] * n + [SEM, SEM, pl.BlockSpec(memory_space=pl.ANY)], out_specs=[HBM] * n,
        out_shape=[pltpu.HBM(a.shape, a.dtype) for a in arrays],
        input_output_aliases={t: t for t in range(n)}, compiler_params=SPLIT_COPY,
    )(*arrays, ssem, rsem, after)


def pair_start(name, grads):
    n = len(grads)
    lands = [lax.empty((a.shape[0], a.shape[1] // 2, a.shape[2]), a.dtype) for a in grads]
    ssem, rsem, arrays, token = _split_start(name, lambda a, s, r: _pair_copies(a[:n], a[n:], s, r), list(grads) + lands, n)
    return ssem, rsem, arrays[:n], arrays[n:], token


def pair_wait(name, ssem, rsem, grads, lands, after):
    n = len(grads)
    arrays = _split_wait(name, lambda a, s, r: _pair_copies(a[:n], a[n:], s, r), ssem, rsem, list(grads) + list(lands), after)
    return arrays[:n], arrays[n:]


def _reduce_copies(parts, lands, ssem, rsem):
    x, y, c, j_own, chips, js = _place()
    return [pltpu.make_async_remote_copy(src_ref=parts[t].at[js[k]], dst_ref=lands[t].at[k], send_sem=ssem.at[3 * t + k],
                                         recv_sem=rsem.at[3 * t + k], device_id=(px, py, c), device_id_type=MESH)
            for t in range(len(parts)) for k, (px, py) in enumerate(chips)]


def reduce_start(name, parts, after):
    n = len(parts)
    lands = [lax.empty((3,) + a.shape[1:], a.dtype) for a in parts]
    ssem, rsem, arrays, token = _split_start(name, lambda a, s, r: _reduce_copies(a[:n], a[n:], s, r), list(parts) + lands,
                                             3 * n, after)
    return ssem, rsem, arrays[:n], arrays[n:], token


def reduce_wait(name, ssem, rsem, parts, lands, after):
    n = len(parts)
    arrays = _split_wait(name, lambda a, s, r: _reduce_copies(a[:n], a[n:], s, r), ssem, rsem, list(parts) + list(lands), after)
    return arrays[:n], arrays[n:]


def share_halves(name, bufs):
    n = len(bufs)

    def body(*refs):
        outs = refs[n:2 * n]
        ssem, rsem = refs[2 * n:]
        x, y, c, _, _, _ = _place()

        def copy(t, l, half):
            blk = _half(outs[t], l, half)
            return pltpu.make_async_remote_copy(src_ref=blk, dst_ref=blk, send_sem=ssem.at[2 * t + l],
                                                recv_sem=rsem.at[2 * t + l], device_id=(x, y, 1 - c), device_id_type=MESH)

        for t in range(n):
            for l in range(2):
                copy(t, l, c).start()
        for t in range(n):
            for l in range(2):
                copy(t, l, 1 - c).wait_recv()
                copy(t, l, c).wait_send()

    return pl.pallas_call(
        body, name=name, in_specs=[HBM] * n, out_specs=[HBM] * n, input_output_aliases={t: t for t in range(n)},
        out_shape=[jax.ShapeDtypeStruct(a.shape, a.dtype) for a in bufs],
        scratch_shapes=[pltpu.SemaphoreType.DMA((2 * n,)), pltpu.SemaphoreType.DMA((2 * n,))],
    )(*bufs)


def allreduce_small(buf):
    r = buf.shape[0]
    n_dev = 8

    def body(in_ref, out_ref, gath, ssem, rsem):
        x, y, c = lax.axis_index("x"), lax.axis_index("y"), lax.axis_index("c")
        flip = lambda v, bit: 1 - v if bit else v
        me = 4 * x + 2 * y + c
        gath[me] = in_ref[...]
        peers = [(flip(x, k >> 2 & 1), flip(y, k >> 1 & 1), flip(c, k & 1)) for k in range(1, n_dev)]
        sends = []
        for k, peer in enumerate(peers):
            cp = pltpu.make_async_remote_copy(src_ref=in_ref, dst_ref=gath.at[me], send_sem=ssem.at[k], recv_sem=rsem.at[k],
                                              device_id=peer, device_id_type=MESH)
            cp.start()
            sends.append(cp)
        for k, (px, py, pc) in enumerate(peers):
            pltpu.make_async_remote_copy(src_ref=in_ref, dst_ref=gath.at[4 * px + 2 * py + pc], send_sem=ssem.at[k],
                                         recv_sem=rsem.at[k], device_id=(px, py, pc), device_id_type=MESH).wait_recv()
        for cp in sends:
            cp.wait_send()
        acc = gath[0]
        for i in range(1, n_dev):
            acc = acc + gath[i]
        out_ref[...] = acc

    return pl.pallas_call(
        body, name="allreduce_small",
        in_specs=[pl.BlockSpec(memory_space=pltpu.VMEM)], out_specs=pl.BlockSpec(memory_space=pltpu.VMEM),
        out_shape=jax.ShapeDtypeStruct(buf.shape, F32),
        scratch_shapes=[pltpu.VMEM((n_dev, r, 128), F32), pltpu.SemaphoreType.DMA((n_dev - 1,)), pltpu.SemaphoreType.DMA((n_dev - 1,))],
    )(buf)


def _rtile(r, pref=256):
    return _tile(r, pref, 16)


def chip_partial(name, grad, recv, where):
    _, rh, cdim = recv.shape
    tr = _rtile(rh)
    nt = rh // tr
    return tile_call(name, lambda a, b: a + b, (N_SHARDS, nt),
                     [(grad, (None, tr, cdim), lambda j, i, p: (j, p[1] * nt + i, 0)), (recv, (None, tr, cdim), lambda j, i, p: (j, i, 0))],
                     [(recv.shape, BF16, (None, tr, cdim), lambda j, i, p: (j, i, 0), None)], prefetch=where)[0]


def sum_chips(name, parts, lands, where, layer, n_layers, into):
    _, rh, cdim = parts.shape
    tr = _rtile(rh)
    nt = rh // tr
    up = lambda a: a.astype(F32)
    return tile_call(name, lambda own, rv: (up(own) + up(rv[0])) + (up(rv[1]) + up(rv[2])), (nt,),
                     [(parts, (None, tr, cdim), lambda i, p: (p[0], i, 0)), (lands, (3, tr, cdim), lambda i, p: (0, i, 0))],
                     [((n_layers, 2 * rh, cdim), F32, (None, tr, cdim), lambda i, p: (layer, p[1] * nt + i, 0), None)],
                     prefetch=where, into=into)[0]


def adamw_fn(w, g, m, v):
    m = ADAM_B1 * m + (1.0 - ADAM_B1) * g
    v = ADAM_B2 * v + (1.0 - ADAM_B2) * jnp.square(g)
    m_hat = m / (1.0 - ADAM_B1 ** ADAM_STEP)
    v_hat = v / (1.0 - ADAM_B2 ** ADAM_STEP)
    delta = -ADAM_LR * (m_hat / (jnp.sqrt(v_hat) + ADAM_EPS) + ADAM_WD * w)
    return delta, m, v


def adamw(name, w, g, m, v):
    nl, r, cdim = w.shape
    tr = _rtile(r, 128)
    spec = ((None, tr, cdim), lambda l, i: (l, i, 0))
    return tile_call(name, adamw_fn, (nl, r // tr), [(a,) + spec for a in (w, g, m, v)],
                     [(w.shape, F32) + spec + (None,)] * 3)


def _as3(a):
    return _comm_rows(a.reshape(a.shape[0], -1, a.shape[-1]))


COMM_GROUPS = ((0, ("ffn1",)), (0, ("mixer",)), (0, ("xattn", "ffn2")), (1, ("ffn1", "mixer")), (1, ("xattn", "ffn2")))


def _pack_rows(vals):
    rows = []
    for v in vals:
        v = v.reshape(-1).astype(F32)
        pad = (-v.shape[0]) % 128
        rows.append(jnp.pad(v, (0, pad)).reshape(-1, 128))
    out = jnp.concatenate(rows, axis=0)
    return jnp.pad(out, ((0, (-out.shape[0]) % 8), (0, 0)))


def _unpack_rows(buf, like):
    outs, r = [], 0
    for a in like:
        n = a.size
        nr = -(-n // 128)
        outs.append(buf[r:r + nr].reshape(-1)[:n].reshape(a.shape))
        r += nr
    return outs


def kernel(x, mem, ffn1_norm, ffn1_w_gate, ffn1_w_up, ffn1_w_down, mix_norm, w_in, conv_w, a_log, dt_bias, dn_out_norm, pool_w, pool_scale, w_out, xattn_norm, mem_norm, xattn_wq, xattn_wkv, xattn_wo, ffn2_norm, ffn2_w_gate, ffn2_w_up, ffn2_w_down, final_norm, loss_target, m_ffn1_norm, m_ffn1_w_gate, m_ffn1_w_up, m_ffn1_w_down, m_mix_norm, m_w_in, m_conv_w, m_a_log, m_dt_bias, m_dn_out_norm, m_pool_w, m_pool_scale, m_w_out, m_xattn_norm, m_mem_norm, m_xattn_wq, m_xattn_wkv, m_xattn_wo, m_ffn2_norm, m_ffn2_w_gate, m_ffn2_w_up, m_ffn2_w_down, m_final_norm, v_ffn1_norm, v_ffn1_w_gate, v_ffn1_w_up, v_ffn1_w_down, v_mix_norm, v_w_in, v_conv_w, v_a_log, v_dt_bias, v_dn_out_norm, v_pool_w, v_pool_scale, v_w_out, v_xattn_norm, v_mem_norm, v_xattn_wq, v_xattn_wkv, v_xattn_wo, v_ffn2_norm, v_ffn2_w_gate, v_ffn2_w_up, v_ffn2_w_down, v_final_norm):
    given = dict(locals())
    w = {n: given[n] for n in WEIGHTS}
    m = {n: given["m_" + n] for n in WEIGHTS}
    v = {n: given["v_" + n] for n in WEIGHTS}
    where = jnp.stack([2 * lax.axis_index("x") + lax.axis_index("y"), lax.axis_index("c")]).astype(jnp.int32)
    placed = {}
    for n in SHARDED:
        for l, buf in enumerate(place_shards("place_" + n, _as3(w[n]), where, F32 if n == "conv_w" else BF16)):
            placed[(l, n)] = buf
    keys, groups = [], []
    for l, entries in COMM_GROUPS:
        groups.append([])
        for e in entries:
            for n in GROUPS[e]:
                groups[-1].append(len(keys))
                keys.append((l, n))
    sems, bufs, _ = gather_start([placed[k] for k in keys], groups)
    fetched = {}

    def fetch(l, entry, after):
        gi = [i for i, (gl, entries) in enumerate(COMM_GROUPS) if gl == l and entry in entries][0]
        if gi not in fetched:
            landed = gather_wait("gather_wait%d" % gi, [bufs[i] for i in groups[gi]], sems[gi], after)
            fetched[gi] = {keys[i][1]: a for i, a in zip(groups[gi], gather_swap("gather_swap%d" % gi, landed))}
        return fetched[gi]

    swapping, travelling = [], []

    def to_ici(after):
        tag, names, (ssem, rsem, mine, theirs) = swapping.pop(0)
        mine, theirs = pair_wait("pair_wait" + tag, ssem, rsem, mine, theirs, after)
        parts = [chip_partial("partial%s_%s" % (tag, n), g, r, where) for n, g, r in zip(names, mine, theirs)]
        ssem, rsem, parts, lands, token = reduce_start("reduce_start" + tag, parts, after)
        travelling.append((tag, names, (ssem, rsem, parts, lands)))
        return token

    def grads_done(l, part, gb):
        tag, names = "%d%d" % (l, part), GRAD_PARTS[part]
        ssem, rsem, mine, theirs, token = pair_start("pair_start" + tag, [gb[n] for n in names])
        swapping.append((tag, names, (ssem, rsem, mine, theirs)))
        return to_ici(token) if len(swapping) > 1 else token

    rep = {n: w[n] for n in REPLICATED}
    loss_row, dx, big, small, d_final = model_grads(x[0], mem[0], loss_target[0], fetch, rep, grads_done)
    to_ici(dx)
    n_layers = len(big)
    sums, out_g, out_d, out_m, out_v = {}, {}, {}, {}, {}

    def land(after):
        tag, names, (ssem, rsem, parts, lands) = travelling.pop(0)
        parts, lands = reduce_wait("reduce_wait" + tag, ssem, rsem, parts, lands, after)
        for n, p, r in zip(names, parts, lands):
            sums[n] = sum_chips("sum%s_%s" % (tag, n), p, r, where, int(tag[0]), n_layers, sums.get(n))

    def finish(part):
        names = GRAD_PARTS[part]
        for n, g in zip(names, share_halves("share_halves%d" % part, [sums[n] for n in names])):
            d_, m_, v_ = adamw("adamw_" + n, _as3(w[n]), g, _as3(m[n]), _as3(v[n]))
            out_g[n], out_d[n], out_m[n], out_v[n] = (a.reshape(w[n].shape) for a in (g, d_, m_, v_))
        return out_d[names[-1]]

    for _ in range(len(travelling) - 1):
        land(dx)
    finish(0)
    land(finish(1))
    finish(2)
    rep_names = [n for n in REPLICATED if n != "final_norm"]
    g_rep = {n: jnp.stack([small[l][n][0, :w[n].shape[1]] for l in range(len(small))]) for n in rep_names}
    g_rep["final_norm"] = d_final[0]
    like = [w[n] for n in REPLICATED] + [jnp.zeros((1,), F32)]
    summed = allreduce_small(_pack_rows([g_rep[n] for n in REPLICATED] + [loss_row[0, :1]]))
    pk = lambda tree: _pack_rows([tree[n] for n in REPLICATED] + [jnp.zeros((1,), F32)])
    wp, mp, vp = pk(w), pk(m), pk(v)
    dp, mp2, vp2 = adamw("adamw_small", wp[None], summed[None], mp[None], vp[None])
    for buf, dst in ((summed, out_g), (dp[0], out_d), (mp2[0], out_m), (vp2[0], out_v)):
        for n, a in zip(REPLICATED, _unpack_rows(buf, like)):
            dst[n] = a
    loss = _unpack_rows(summed, like)[-1][0]
    return (loss, dx[None], *[out_g[n] for n in WEIGHTS], *[out_d[n] for n in WEIGHTS],
            *[out_m[n] for n in WEIGHTS], *[out_v[n] for n in WEIGHTS])
```

```python
import functools

import jax
import jax.numpy as jnp
from jax import lax
from jax.experimental import pallas as pl
from jax.experimental.pallas import tpu as pltpu

F32, BF16 = jnp.float32, jnp.bfloat16
HI = lax.Precision.HIGHEST
MESH = pl.DeviceIdType.MESH

EPS = 1e-6
DN_HEADS = 8
HEAD_DIM = 128
X_HEADS = 4
POOL_WINDOWS = (2, 4, 8, 16)
CONV_WIDTH = 4
CHUNK = 64
N_SHARDS = 4
AB_PAD = 128
ADAM_LR, ADAM_B1, ADAM_B2, ADAM_EPS, ADAM_WD, ADAM_STEP = 0.001, 0.9, 0.999, 1e-08, 0.01, 10
VMEM_LIMIT = 56 << 20
ROW_TILE = 256


def _tile(n, pref, unit=128):
    best = None
    for t in range(unit, min(n, pref) + 1, unit):
        if n % t == 0:
            best = t
    return best if best is not None else n


def _params():
    return pltpu.CompilerParams(vmem_limit_bytes=VMEM_LIMIT)


def _split(a):
    a = a.astype(F32)
    head = a.astype(BF16)
    return head, (a - head.astype(F32)).astype(BF16)


def _dg(a, b, ca, cb, hi):
    dims = (((ca,), (cb,)), ((), ()))
    dot = lambda u, v: lax.dot_general(u, v, dims, preferred_element_type=F32)
    if hi:
        (a0, a1), (b0, b1) = _split(a), _split(b)
        return dot(a0, b0) + (dot(a0, b1) + dot(a1, b0))
    return dot(a.astype(BF16), b.astype(BF16))


@functools.partial(jax.custom_vjp, nondiff_argnums=(2, 3, 4))
def mmul(a, b, ca, cb, hi):
    return _dg(a, b, ca, cb, hi)


def _mmul_fwd(a, b, ca, cb, hi):
    return _dg(a, b, ca, cb, hi), (a, b)


def _mmul_bwd(ca, cb, hi, res, g):
    a, b = res
    da = _dg(g, b, 1, 1 - cb, hi) if ca == 1 else _dg(b, g, 1 - cb, 1, hi)
    db = _dg(a, g, 1 - ca, 0, hi) if cb == 0 else _dg(g, a, 0, 1 - ca, hi)
    return da.astype(a.dtype), db.astype(b.dtype)


mmul.defvjp(_mmul_fwd, _mmul_bwd)


@functools.partial(jax.custom_vjp, nondiff_argnums=(1,))
def shift_down(x, s):
    t = lax.broadcasted_iota(jnp.int32, x.shape, 0)
    return jnp.where(t >= s, pltpu.roll(x, s, 0), 0.0)


def _shift_up(x, s):
    n = x.shape[0]
    t = lax.broadcasted_iota(jnp.int32, x.shape, 0)
    return jnp.where(t < n - s, pltpu.roll(x, n - s, 0), 0.0)


shift_down.defvjp(lambda x, s: (shift_down(x, s), None), lambda s, _, g: (_shift_up(g, s),))


def sigmoid(x):
    return 0.5 * (jnp.tanh(0.5 * x) + 1.0)


def silu(x):
    return x * sigmoid(x)


@jax.custom_vjp
def softplus(x):
    u = jnp.exp(-jnp.abs(x))
    w = 1.0 + u
    log1p = jnp.where(w == 1.0, u, jnp.log(w) * u / jnp.where(w == 1.0, 1.0, w - 1.0))
    return jnp.maximum(x, 0.0) + log1p


softplus.defvjp(lambda x: (softplus(x), x), lambda x, g: (g * sigmoid(x),))


def rms(x, g):
    x = x.astype(F32)
    return x * lax.rsqrt(jnp.mean(x * x, axis=-1, keepdims=True) + EPS) * g


def swiglu(gate, up):
    return silu(gate) * up


def vjp_of(fn, n_in, diff, has_pids=False):
    def g(*args):
        pids = None
        if has_pids:
            pids, args = args[0], args[1:]
        ins, cots = list(args[:n_in]), args[n_in:]

        def f(*d):
            full = list(ins)
            for i, v in zip(diff, d):
                full[i] = v
            return fn(pids, *full) if has_pids else fn(*full)

        out, pull = jax.vjp(f, *[ins[i].astype(F32) for i in diff])
        if isinstance(out, (tuple, list)):
            return pull(tuple(c.astype(o.dtype) for c, o in zip(cots, out)))
        return pull(cots[0].astype(out.dtype))
    return g


def tile_call(name, fn, grid, ins, outs, with_pids=False, prefetch=None, into=None):
    n_in = len(ins)
    n_into = 0 if into is None else 1

    def body(*refs):
        if prefetch is not None:
            refs = refs[1:]
        pids = tuple(pl.program_id(a) for a in range(len(grid)))
        vals = [r[...] for r in refs[:n_in]]
        res = fn(pids, *vals) if with_pids else fn(*vals)
        if not isinstance(res, (tuple, list)):
            res = (res,)
        for r, o, spec in zip(res, refs[n_in + n_into:], outs):
            acc = spec[4]
            if acc is None:
                o[...] = r.astype(o.dtype)
            else:
                first = functools.reduce(jnp.logical_and, [pids[a] == 0 for a in acc])

                @pl.when(first)
                def _():
                    o[...] = r.astype(o.dtype)

                @pl.when(jnp.logical_not(first))
                def _():
                    o[...] += r.astype(o.dtype)

    in_specs = [pl.BlockSpec(b, im) for _, b, im in ins] + [pl.BlockSpec(memory_space=pl.ANY)] * n_into
    out_specs = [pl.BlockSpec(s[2], s[3]) for s in outs]
    out_shape = [jax.ShapeDtypeStruct(s[0], s[1]) for s in outs]
    args = [a for a, _, _ in ins] + ([] if into is None else [into])
    if prefetch is None:
        return pl.pallas_call(body, name=name, grid=grid, in_specs=in_specs, out_specs=out_specs, out_shape=out_shape,
                              input_output_aliases={n_in: 0} if n_into else {}, compiler_params=_params())(*args)
    spec = pltpu.PrefetchScalarGridSpec(num_scalar_prefetch=1, grid=grid, in_specs=in_specs, out_specs=out_specs)
    return pl.pallas_call(body, name=name, grid_spec=spec, out_shape=out_shape,
                          input_output_aliases={n_in + 1: 0} if n_into else {}, compiler_params=_params())(prefetch, *args)


def mm_call(name, grid, ins, pairs, n_acc, acc_shape, outs, epilogue, extras=(), after=None):
    n_in, n_ex, nk = len(ins), len(extras), grid[2]
    n_dep = 0 if after is None else 1

    def finish(accs, ex_refs, out_refs):
        res = epilogue(accs, *[r[...] for r in ex_refs])
        if not isinstance(res, (tuple, list)):
            res = (res,)
        for r, o in zip(res, out_refs):
            o[...] = r.astype(o.dtype)

    def body(*refs):
        in_refs, ex_refs = refs[:n_in], refs[n_in:n_in + n_ex]
        refs = refs[n_in + n_ex + n_dep:]
        out_refs, accs = refs[:len(outs)], refs[len(outs):]
        if nk == 1:
            vals = [None] * n_acc
            for ia, ib, ca, cb, ai in pairs:
                d = _dg(in_refs[ia][...], in_refs[ib][...], ca, cb, False)
                vals[ai] = d if vals[ai] is None else vals[ai] + d
            finish(vals, ex_refs, out_refs)
            return
        k = pl.program_id(2)

        @pl.when(k == 0)
        def _():
            for a in accs:
                a[...] = jnp.zeros_like(a)

        for ia, ib, ca, cb, ai in pairs:
            accs[ai][...] += _dg(in_refs[ia][...], in_refs[ib][...], ca, cb, False)

        @pl.when(k == nk - 1)
        def _():
            finish([a[...] for a in accs], ex_refs, out_refs)

    return pl.pallas_call(
        body, name=name, grid=grid,
        in_specs=[pl.BlockSpec(b, im) for _, b, im in list(ins) + list(extras)] + [pl.BlockSpec(memory_space=pl.ANY)] * n_dep,
        out_specs=[pl.BlockSpec(s[2], s[3]) for s in outs],
        out_shape=[jax.ShapeDtypeStruct(s[0], s[1]) for s in outs],
        scratch_shapes=[pltpu.VMEM(acc_shape, F32) for _ in range(n_acc if nk > 1 else 0)],
        compiler_params=_params(),
    )(*[a for a, _, _ in list(ins) + list(extras)], *([] if after is None else [after]))


MM_VMEM_BUDGET = 40 << 20


def _mm_tiles(m, n, kk, sa, sb, so, has_res):
    tk = _tile(kk, 2048)
    best = None
    for tm in (1024, 512, 256, 128):
        for tn in (1024, 512, 256, 128):
            tm_, tn_ = _tile(m, tm), _tile(n, tn)
            need = 2 * (tm_ * tk * sa + tk * tn_ * sb + tm_ * tn_ * so) + (tm_ * tn_ * 4 if tk < kk else 0)
            need += 2 * tm_ * tn_ * 4 if has_res else 0
            if need <= MM_VMEM_BUDGET and (best is None or tm_ * tn_ > best[0] * best[1]):
                best = (tm_, tn_)
    return best + (tk,)


def mm2(name, a, b, ca, cb, res=None, scale=None, out_dtype=F32, after=None):
    m, kk, n = a.shape[1 - ca], a.shape[ca], b.shape[1 - cb]
    tm, tn, tk = _mm_tiles(m, n, kk, a.dtype.itemsize, b.dtype.itemsize, jnp.dtype(out_dtype).itemsize, res is not None)
    a_spec = ((tm, tk), lambda i, j, k: (i, k)) if ca == 1 else ((tk, tm), lambda i, j, k: (k, i))
    b_spec = ((tk, tn), lambda i, j, k: (k, j)) if cb == 0 else ((tn, tk), lambda i, j, k: (j, k))
    extras = [] if res is None else [(res, (tm, tn), lambda i, j, k: (i, j))]

    def epi(accs, *ex):
        r = accs[0] if scale is None else accs[0] * scale
        return r + ex[0] if ex else r

    return mm_call(name, (m // tm, n // tn, kk // tk), [(a,) + a_spec, (b,) + b_spec], [(0, 1, ca, cb, 0)], 1, (tm, tn),
                   [((m, n), out_dtype, (tm, tn), lambda i, j, k: (i, j))], epi, extras, after)[0]


def rows_call(name, fn, rows, consts, outs, acc_outs=(), tr=ROW_TILE):
    s = rows[0].shape[0]
    tr = _tile(s, tr, 8)
    ins = [(r, (tr, r.shape[1]), lambda i: (i, 0)) for r in rows]
    ins += [(c, c.shape, (lambda nd: (lambda i: (0,) * nd))(c.ndim)) for c in consts]
    o = [((s, c), dt, (tr, c), lambda i: (i, 0), None) for c, dt in outs]
    o += [(shp, F32, shp, (lambda nd: (lambda i: (0,) * nd))(len(shp)), (0,)) for shp in acc_outs]
    return tile_call(name, fn, (s // tr,), ins, o)


def _lane_pick(x, h):
    lane = lax.broadcasted_iota(jnp.int32, x.shape, x.ndim - 1)
    return jnp.sum(jnp.where(lane == h, x, 0.0), axis=-1, keepdims=True)


def gateprep_fn(ab, alog, dtb):
    t = ab.shape[0]
    gs, bs = [], []
    for h in range(DN_HEADS):
        a_h = _lane_pick(ab, h)
        b_h = _lane_pick(ab, DN_HEADS + h)
        g_h = -jnp.exp(_lane_pick(alog, h)) * softplus(a_h + _lane_pick(dtb, h))
        gs.append(jnp.broadcast_to(g_h, (t, HEAD_DIM)))
        bs.append(jnp.broadcast_to(sigmoid(b_h), (t, HEAD_DIM)))
    return jnp.concatenate(gs, axis=1), jnp.concatenate(bs, axis=1)


def conv_fn(pids, x, w):
    kind = pids[0] // DN_HEADS
    y = x * w[CONV_WIDTH - 1:CONV_WIDTH]
    for i in range(CONV_WIDTH - 1):
        y = y + shift_down(x, CONV_WIDTH - 1 - i) * w[i:i + 1]
    y = silu(y)
    n = y * lax.rsqrt(jnp.sum(y * y, axis=-1, keepdims=True) + EPS)
    n = n * jnp.where(kind == 0, HEAD_DIM ** -0.5, 1.0)
    return jnp.where(kind == 2, y, n)


def unit_lower_inverse(a):
    c = a.shape[0]
    r = lax.broadcasted_iota(jnp.int32, (c, c), 0)
    cc = lax.broadcasted_iota(jnp.int32, (c, c), 1)
    x = -a
    t = jnp.where(r == cc, 1.0, 0.0) + x
    p = 2
    while p < c:
        x = _dg(x, x, 1, 0, True)
        t = t + _dg(t, x, 1, 0, True)
        p *= 2
    return t


@jax.custom_vjp
def known_inverse(a, t):
    return t


known_inverse.defvjp(lambda a, t: (t, t),
                     lambda t, g: (-_dg(_dg(t, g, 0, 0, True), t, 1, 1, True), jnp.zeros_like(t)))


def intra_head(q, k, v, g, b, t_known):
    c = q.shape[0]
    r = lax.broadcasted_iota(jnp.int32, (c, c), 0)
    cc = lax.broadcasted_iota(jnp.int32, (c, c), 1)
    tril = (r >= cc).astype(F32)
    gc = mmul(tril, g, 1, 0, True)
    m = gc[:, :c]
    decay = jnp.exp(jnp.where(r >= cc, m - m.T, -1e30))
    kb = k * b
    a = jnp.where(r > cc, mmul(kb, k, 1, 1, False) * decay, 0.0)
    t = unit_lower_inverse(a) if t_known is None else known_inverse(a, t_known)
    e = jnp.exp(gc)
    u = mmul(t, v * b, 1, 0, True)
    w = mmul(t, kb * e, 1, 0, True)
    qk = mmul(q, k, 1, 1, False) * decay
    gl = gc[c - 1:c, :]
    kd = k * jnp.exp(gl - gc)
    outs = (u, w, qk, q * e, kd, jnp.broadcast_to(jnp.exp(gl), (8, HEAD_DIM)))
    return outs + (t,) if t_known is None else outs


def _heads(x, h):
    return x[:, h * HEAD_DIM:(h + 1) * HEAD_DIM]


def intra_fn(q, k, v, g, b):
    outs = [intra_head(_heads(q, h), _heads(k, h), _heads(v, h), _heads(g, h), _heads(b, h), None) for h in range(DN_HEADS)]
    cat = lambda i: jnp.concatenate([o[i] for o in outs], axis=1)
    stack = lambda i: jnp.stack([o[i] for o in outs], axis=0)
    return cat(0), cat(1), stack(2), cat(3), cat(4), cat(5), stack(6)


def intra_bwd_fn(q, k, v, g, b, tinv, du, dw, dqk, dqd, dkd, dgl):
    res = []
    for h in range(DN_HEADS):
        hs = lambda x: _heads(x, h)
        res.append(vjp_of(intra_head, 6, (0, 1, 2, 3, 4))(hs(q), hs(k), hs(v), hs(g), hs(b), tinv[h],
                                                         hs(du), hs(dw), dqk[h], hs(dqd), hs(dkd), hs(dgl)))
    cat = lambda i: jnp.concatenate([r[i] for r in res], axis=1)
    return jnp.concatenate([cat(0), cat(1), cat(2)], axis=1), cat(3), cat(4)


def scan_step(s, u, w, qk, qd, kd, gl):
    v_new = u - mmul(w, s, 1, 0, False)
    o = mmul(qd, s, 1, 0, False) + mmul(qk, v_new, 1, 0, False)
    return s * gl[0:1, :] + mmul(kd, v_new, 0, 0, False), o


def outgate_fn(o, z, g):
    return rms(o, g) * silu(z)


def pool_fn(pids, p):
    gid = pids[0]
    s = p.shape[0]
    t1 = (lax.broadcasted_iota(jnp.int32, p.shape, 0) + 1).astype(F32)
    acc, win, out = p, 1, None
    for gi, target in enumerate(POOL_WINDOWS):
        while win < target:
            acc = acc + shift_down(acc, win)
            win *= 2
        cand = acc / jnp.minimum(t1, float(target))
        out = cand if out is None else jnp.where(gid == gi, cand, out)
    return out - p


def poolmix_fn(pooled, pw, scale):
    return mmul(pooled, pw, 1, 0, False) * scale


def attn_fn(q, k, v):
    s = mmul(q, k, 1, 1, False) * (q.shape[1] ** -0.5)
    s = s - jnp.max(s, axis=-1, keepdims=True)
    e = jnp.exp(s)
    p = e / jnp.sum(e, axis=-1, keepdims=True)
    return mmul(p, v, 1, 0, False)


def rms_fwd(name, x, g):
    return rows_call(name, lambda a, b: rms(a, b), [x], [g], [(x.shape[1], BF16)])[0]


def rms_bwd(name, x, g, dy, dres):
    def fn(a, d, r, b):
        dx, dg = vjp_of(rms, 2, (0, 1))(a, b, d)
        return dx + r, dx + r, dg
    return rows_call(name, fn, [x, dy, dres], [g], [(x.shape[1], F32), (x.shape[1], BF16)], [g.shape])


def ffn_fwd(tag, x, g, wg, wu, wd):
    s, d = x.shape
    fj = wg.shape[-1]
    f = N_SHARDS * fj
    xn = rms_fwd(tag + "_norm", x, g)
    tm = _tile(s, 256)
    w_spec = ((None, d, fj), lambda j, i, k: (j, 0, 0))
    o_spec = ((tm, fj), lambda j, i, k: (i, j))
    gate, up, act = mm_call(
        tag + "_gu", (N_SHARDS, s // tm, 1),
        [(xn, (tm, d), lambda j, i, k: (i, 0)), (wg,) + w_spec, (wu,) + w_spec],
        [(0, 1, 1, 0, 0), (0, 2, 1, 0, 1)], 2, (tm, fj),
        [((s, f), F32) + o_spec, ((s, f), F32) + o_spec, ((s, f), BF16) + o_spec],
        lambda accs: (accs[0], accs[1], swiglu(accs[0], accs[1])))
    if callable(wd):
        wd = wd(act)
    tm, tn = _tile(s, 1024), _tile(d, 1024)
    out = mm_call(
        tag + "_down", (s // tm, d // tn, N_SHARDS),
        [(act, (tm, fj), lambda i, j, k: (i, k)), (wd, (None, fj, tn), lambda i, j, k: (k, 0, j))],
        [(0, 1, 1, 0, 0)], 1, (tm, tn),
        [((s, d), F32, (tm, tn), lambda i, j, k: (i, j))],
        lambda accs, r: r + 0.5 * accs[0], [(x, (tm, tn), lambda i, j, k: (i, j))])[0]
    return out, (x, xn, gate, up, act), wd


def ffn_bwd(tag, saved, g, wg, wu, wd, dout, dout_b, after=None):
    x, xn, gate, up, act = saved
    s, d = x.shape
    fj = wg.shape[-1]
    f = N_SHARDS * fj
    tm = _tile(s, 512)
    o_spec = ((tm, fj), lambda j, i, k: (i, j))

    def epi(accs, ga, u):
        dgate, dup = vjp_of(swiglu, 2, (0, 1))(ga, u, 0.5 * accs[0])
        return dgate, dup

    dgate, dup = mm_call(
        tag + "_dact", (N_SHARDS, s // tm, 1),
        [(dout_b, (tm, d), lambda j, i, k: (i, 0)), (wd, (None, fj, d), lambda j, i, k: (j, 0, 0))],
        [(0, 1, 1, 1, 0)], 1, (tm, fj),
        [((s, f), BF16) + o_spec, ((s, f), BF16) + o_spec], epi,
        [(gate,) + o_spec, (up,) + o_spec], after=after)
    tn = _tile(d, 1024)
    dwd = mm_call(
        tag + "_dwd", (N_SHARDS, d // tn, 1),
        [(act, (s, fj), lambda j, i, k: (0, j)), (dout_b, (s, tn), lambda j, i, k: (0, i))],
        [(0, 1, 0, 0, 0)], 1, (fj, tn),
        [((N_SHARDS, fj, d), F32, (None, fj, tn), lambda j, i, k: (j, 0, i))],
        lambda accs: 0.5 * accs[0])[0]
    td = _tile(d, 512)
    g_spec = ((s, fj), lambda j, i, k: (0, j))
    w_out = ((N_SHARDS, d, fj), F32, (None, td, fj), lambda j, i, k: (j, i, 0))
    dwg, dwu = mm_call(
        tag + "_dwgu", (N_SHARDS, d // td, 1),
        [(xn, (s, td), lambda j, i, k: (0, i)), (dgate,) + g_spec, (dup,) + g_spec],
        [(0, 1, 0, 0, 0), (0, 2, 0, 0, 1)], 2, (td, fj), [w_out, w_out], lambda accs: (accs[0], accs[1]))
    tm, tn = _tile(s, 1024), _tile(d, 1024)
    a_spec = ((tm, fj), lambda i, j, k: (i, k))
    wt_spec = ((None, tn, fj), lambda i, j, k: (k, j, 0))
    dxn = mm_call(
        tag + "_dxn", (s // tm, d // tn, N_SHARDS),
        [(dgate,) + a_spec, (wg,) + wt_spec, (dup,) + a_spec, (wu,) + wt_spec],
        [(0, 1, 1, 1, 0), (2, 3, 1, 1, 0)], 1, (tm, tn),
        [((s, d), F32, (tm, tn), lambda i, j, k: (i, j))], lambda accs: accs[0])[0]
    dx, dx_b, dg = rms_bwd(tag + "_dnorm", x, g, dxn, dout)
    return dx, dx_b, dg, dwg, dwu, dwd


def _hspec(tc, width=HEAD_DIM):
    return (tc, width)


def mixer_fwd(tag, h, wts):
    s, d = h.shape
    dnw = DN_HEADS * HEAD_DIM
    pw_ = d - dnw
    gdim = pw_ // len(POOL_WINDOWS)
    nc = s // CHUNK
    hn = rms_fwd(tag + "_norm", h, wts["mix_norm"])
    qkv = mm2(tag + "_qkv", hn, wts["w_qkv"], 1, 0)
    z = mm2(tag + "_z", hn, wts["w_z"], 1, 0)
    ab = mm2(tag + "_ab", hn, wts["w_ab"], 1, 0)
    p = mm2(tag + "_p", hn, wts["w_p"], 1, 0)
    qkvn = tile_call(tag + "_conv", conv_fn, (3 * DN_HEADS,),
                     [(qkv, (s, HEAD_DIM), lambda i: (0, i)), (wts["conv_w"], (CONV_WIDTH, HEAD_DIM), lambda i: (0, i))],
                     [((s, 3 * dnw), F32, (s, HEAD_DIM), lambda i: (0, i), None)], with_pids=True)[0]
    g_bc, b_bc = rows_call(tag + "_gates", gateprep_fn, [ab], [wts["a_log"], wts["dt_bias"]], [(dnw, F32), (dnw, F32)])
    cw = (CHUNK, dnw)
    sq = ((DN_HEADS, s, CHUNK), F32, (DN_HEADS, CHUNK, CHUNK), lambda n: (0, n, 0), None)
    u, w, qk, qd, kd, gl, tinv = tile_call(
        tag + "_intra", intra_fn, (nc,),
        [(qkvn, cw, lambda n: (n, 0)), (qkvn, cw, lambda n: (n, 1)), (qkvn, cw, lambda n: (n, 2)),
         (g_bc, cw, lambda n: (n, 0)), (b_bc, cw, lambda n: (n, 0))],
        [((s, dnw), F32, cw, lambda n: (n, 0), None), ((s, dnw), F32, cw, lambda n: (n, 0), None), sq,
         ((s, dnw), F32, cw, lambda n: (n, 0), None), ((s, dnw), F32, cw, lambda n: (n, 0), None),
         ((nc * 8, dnw), F32, (8, dnw), lambda n: (n, 0), None), sq])
    o, states = scan_fwd(tag + "_scan", u, w, qk, qd, kd, gl)
    y_dn = tile_call(
        tag + "_outgate", outgate_fn, (DN_HEADS, s // ROW_TILE),
        [(o, (ROW_TILE, HEAD_DIM), lambda hh, i: (i, hh)), (z, (ROW_TILE, HEAD_DIM), lambda hh, i: (i, hh)),
         (wts["dn_out_norm"], (1, HEAD_DIM), lambda hh, i: (0, 0))],
        [((s, dnw), BF16, (ROW_TILE, HEAD_DIM), lambda hh, i: (i, hh), None)])[0]
    ng = len(POOL_WINDOWS)
    pooled = tile_call(tag + "_pool", pool_fn, (ng,), [(p, (s, gdim), lambda i: (0, i))],
                       [((s, pw_), BF16, (s, gdim), lambda i: (0, i), None)], with_pids=True)[0]
    tp = _tile(s, 512)
    y_pool = tile_call(
        tag + "_poolmix", poolmix_fn, (ng, s // tp),
        [(pooled, (tp, gdim), lambda gi, i: (i, gi)), (wts["pool_w"], (None, gdim, gdim), lambda gi, i: (gi, 0, 0)),
         (wts["pool_scale"], (1, gdim), lambda gi, i: (0, gi))],
        [((s, pw_), BF16, (tp, gdim), lambda gi, i: (i, gi), None)])[0]
    h1 = mm2(tag + "_out_a", y_dn, wts["w_out_a"], 1, 0, res=h)
    h2 = mm2(tag + "_out_b", y_pool, wts["w_out_b"], 1, 0, res=h1)
    saved = (h, hn, qkv, z, ab, p, qkvn, g_bc, b_bc, u, w, qk, qd, kd, gl, tinv, o, states, y_dn, pooled, y_pool)
    return h2, saved


def scan_fwd(name, u, w, qk, qd, kd, gl):
    s, dnw = u.shape
    nc = s // CHUNK
    cw = (CHUNK, dnw)

    def body(u_r, w_r, qk_r, qd_r, kd_r, gl_r, o_r, st_r, state):
        @pl.when(pl.program_id(0) == 0)
        def _():
            state[...] = jnp.zeros_like(state)

        st_r[...] = state[...]
        outs = []
        for h in range(DN_HEADS):
            hs = slice(h * HEAD_DIM, (h + 1) * HEAD_DIM)
            s_new, o_h = scan_step(state[hs, :], u_r[:, hs], w_r[:, hs], qk_r[h], qd_r[:, hs], kd_r[:, hs], gl_r[:, hs])
            state[hs, :] = s_new
            outs.append(o_h)
        o_r[...] = jnp.concatenate(outs, axis=1)

    row = lambda n: (n, 0)
    return pl.pallas_call(
        body, name=name, grid=(nc,),
        in_specs=[pl.BlockSpec(cw, row), pl.BlockSpec(cw, row), pl.BlockSpec((DN_HEADS, CHUNK, CHUNK), lambda n: (0, n, 0)),
                  pl.BlockSpec(cw, row), pl.BlockSpec(cw, row), pl.BlockSpec((8, dnw), row)],
        out_specs=[pl.BlockSpec(cw, row), pl.BlockSpec((None, dnw, HEAD_DIM), lambda n: (n, 0, 0))],
        out_shape=[jax.ShapeDtypeStruct((s, dnw), F32), jax.ShapeDtypeStruct((nc, dnw, HEAD_DIM), F32)],
        scratch_shapes=[pltpu.VMEM((dnw, HEAD_DIM), F32)],
        compiler_params=_params(),
    )(u, w, qk, qd, kd, gl)


def scan_bwd(name, states, u, w, qk, qd, kd, gl, do):
    s, dnw = u.shape
    nc = s // CHUNK
    cw = (CHUNK, dnw)

    def body(st_r, u_r, w_r, qk_r, qd_r, kd_r, gl_r, do_r, du_r, dw_r, dqk_r, dqd_r, dkd_r, dgl_r, dstate):
        @pl.when(pl.program_id(0) == 0)
        def _():
            dstate[...] = jnp.zeros_like(dstate)

        res = []
        for h in range(DN_HEADS):
            hs = slice(h * HEAD_DIM, (h + 1) * HEAD_DIM)
            r = vjp_of(scan_step, 7, tuple(range(7)))(
                st_r[hs, :], u_r[:, hs], w_r[:, hs], qk_r[h], qd_r[:, hs], kd_r[:, hs], gl_r[:, hs],
                dstate[hs, :], do_r[:, hs])
            dstate[hs, :] = r[0]
            res.append(r)
        cat = lambda i: jnp.concatenate([r[i] for r in res], axis=1)
        du_r[...] = cat(1)
        dw_r[...] = cat(2)
        dqk_r[...] = jnp.stack([r[3] for r in res], axis=0)
        dqd_r[...] = cat(4)
        dkd_r[...] = cat(5)
        dgl_r[...] = cat(6)

    row = lambda n: (nc - 1 - n, 0)
    qk_spec = pl.BlockSpec((DN_HEADS, CHUNK, CHUNK), lambda n: (0, nc - 1 - n, 0))
    return pl.pallas_call(
        body, name=name, grid=(nc,),
        in_specs=[pl.BlockSpec((None, dnw, HEAD_DIM), lambda n: (nc - 1 - n, 0, 0)), pl.BlockSpec(cw, row), pl.BlockSpec(cw, row),
                  qk_spec, pl.BlockSpec(cw, row), pl.BlockSpec(cw, row), pl.BlockSpec((8, dnw), row), pl.BlockSpec(cw, row)],
        out_specs=[pl.BlockSpec(cw, row), pl.BlockSpec(cw, row), qk_spec, pl.BlockSpec(cw, row), pl.BlockSpec(cw, row),
                   pl.BlockSpec((8, dnw), row)],
        out_shape=[jax.ShapeDtypeStruct((s, dnw), F32), jax.ShapeDtypeStruct((s, dnw), F32),
                   jax.ShapeDtypeStruct((DN_HEADS, s, CHUNK), F32), jax.ShapeDtypeStruct((s, dnw), F32),
                   jax.ShapeDtypeStruct((s, dnw), F32), jax.ShapeDtypeStruct((nc * 8, dnw), F32)],
        scratch_shapes=[pltpu.VMEM((dnw, HEAD_DIM), F32)],
        compiler_params=_params(),
    )(states, u, w, qk, qd, kd, gl, do)


def mixer_bwd(tag, saved, wts, dout, dout_b, after=None):
    h, hn, qkv, z, ab, p, qkvn, g_bc, b_bc, u, w, qk, qd, kd, gl, tinv, o, states, y_dn, pooled, y_pool = saved
    s, d = h.shape
    dnw = DN_HEADS * HEAD_DIM
    pw_ = d - dnw
    ng = len(POOL_WINDOWS)
    gdim = pw_ // ng
    nc = s // CHUNK
    gr = {}
    d_ydn = mm2(tag + "_dydn", dout_b, wts["w_out_a"], 1, 1, after=after)
    d_ypool = mm2(tag + "_dypool", dout_b, wts["w_out_b"], 1, 1)
    gr["w_out_a"] = mm2(tag + "_dwout_a", y_dn, dout_b, 0, 0)
    gr["w_out_b"] = mm2(tag + "_dwout_b", y_pool, dout_b, 0, 0)
    tp = _tile(s, 512)
    d_pooled, gr["pool_w"], gr["pool_scale"] = tile_call(
        tag + "_dpoolmix", vjp_of(poolmix_fn, 3, (0, 1, 2)), (ng, s // tp),
        [(pooled, (tp, gdim), lambda gi, i: (i, gi)), (wts["pool_w"], (None, gdim, gdim), lambda gi, i: (gi, 0, 0)),
         (wts["pool_scale"], (1, gdim), lambda gi, i: (0, gi)), (d_ypool, (tp, gdim), lambda gi, i: (i, gi))],
        [((s, pw_), F32, (tp, gdim), lambda gi, i: (i, gi), None),
         ((ng, gdim, gdim), F32, (None, gdim, gdim), lambda gi, i: (gi, 0, 0), (1,)),
         ((1, pw_), F32, (1, gdim), lambda gi, i: (0, gi), (1,))])
    d_p = tile_call(tag + "_dpool", vjp_of(pool_fn, 1, (0,), True), (ng,),
                    [(p, (s, gdim), lambda i: (0, i)), (d_pooled, (s, gdim), lambda i: (0, i))],
                    [((s, pw_), BF16, (s, gdim), lambda i: (0, i), None)], with_pids=True)[0]
    hb = (ROW_TILE, HEAD_DIM)
    d_o, d_z, gr["dn_out_norm"] = tile_call(
        tag + "_doutgate", vjp_of(outgate_fn, 3, (0, 1, 2)), (DN_HEADS, s // ROW_TILE),
        [(o, hb, lambda hh, i: (i, hh)), (z, hb, lambda hh, i: (i, hh)), (wts["dn_out_norm"], (1, HEAD_DIM), lambda hh, i: (0, 0)),
         (d_ydn, hb, lambda hh, i: (i, hh))],
        [((s, dnw), F32, hb, lambda hh, i: (i, hh), None), ((s, dnw), BF16, hb, lambda hh, i: (i, hh), None),
         ((1, HEAD_DIM), F32, (1, HEAD_DIM), lambda hh, i: (0, 0), (0, 1))])
    du, dw, dqk, dqd, dkd, dgl = scan_bwd(tag + "_dscan", states, u, w, qk, qd, kd, gl, d_o)
    cw = (CHUNK, dnw)
    row = lambda n: (n, 0)
    d_qkvn, dg_bc, db_bc = tile_call(
        tag + "_dintra", intra_bwd_fn, (nc,),
        [(qkvn, cw, lambda n: (n, 0)), (qkvn, cw, lambda n: (n, 1)), (qkvn, cw, lambda n: (n, 2)),
         (g_bc, cw, row), (b_bc, cw, row), (tinv, (DN_HEADS, CHUNK, CHUNK), lambda n: (0, n, 0)), (du, cw, row), (dw, cw, row),
         (dqk, (DN_HEADS, CHUNK, CHUNK), lambda n: (0, n, 0)), (dqd, cw, row), (dkd, cw, row), (dgl, (8, dnw), row)],
        [((s, 3 * dnw), F32, (CHUNK, 3 * dnw), row, None)] + [((s, dnw), F32, cw, row, None)] * 2)
    d_ab, gr["a_log"], gr["dt_bias"] = rows_call(
        tag + "_dgates", lambda a, dg, db, al, dt: vjp_of(gateprep_fn, 3, (0, 1, 2))(a, al, dt, dg, db),
        [ab, dg_bc, db_bc], [wts["a_log"], wts["dt_bias"]], [(AB_PAD, BF16)], [(1, AB_PAD), (1, AB_PAD)])
    d_qkv, gr["conv_w"] = tile_call(
        tag + "_dconv", vjp_of(conv_fn, 2, (0, 1), True), (3 * DN_HEADS,),
        [(qkv, (s, HEAD_DIM), lambda i: (0, i)), (wts["conv_w"], (CONV_WIDTH, HEAD_DIM), lambda i: (0, i)),
         (d_qkvn, (s, HEAD_DIM), lambda i: (0, i))],
        [((s, 3 * dnw), BF16, (s, HEAD_DIM), lambda i: (0, i), None),
         ((CONV_WIDTH, 3 * dnw), F32, (CONV_WIDTH, HEAD_DIM), lambda i: (0, i), None)], with_pids=True)
    gr["w_qkv"] = mm2(tag + "_dwqkv", hn, d_qkv, 0, 0)
    gr["w_z"] = mm2(tag + "_dwz", hn, d_z, 0, 0)
    gr["w_ab"] = mm2(tag + "_dwab", hn, d_ab, 0, 0)
    gr["w_p"] = mm2(tag + "_dwp", hn, d_p, 0, 0)
    d_hn = mm2(tag + "_dhn1", d_qkv, wts["w_qkv"], 1, 1)
    d_hn = mm2(tag + "_dhn2", d_z, wts["w_z"], 1, 1, res=d_hn)
    d_hn = mm2(tag + "_dhn3", d_ab, wts["w_ab"], 1, 1, res=d_hn)
    d_hn = mm2(tag + "_dhn4", d_p, wts["w_p"], 1, 1, res=d_hn)
    dh, dh_b, gr["mix_norm"] = rms_bwd(tag + "_dnorm", h, wts["mix_norm"], d_hn, dout)
    return dh, dh_b, gr


def xattn_fwd(tag, h, memn, wts):
    s, d = h.shape
    m = memn.shape[0]
    dh_ = d // X_HEADS
    hn = rms_fwd(tag + "_norm", h, wts["xattn_norm"])
    q = mm2(tag + "_q", hn, wts["wq"], 1, 0, out_dtype=BF16)
    wkv = wts["wkv"]
    nj = wkv.shape[2]
    kv = mm_call(tag + "_kv", (1, N_SHARDS, 1), [(memn, (m, d), lambda i, j, k: (0, 0)), (wkv, (None, d, nj), lambda i, j, k: (j, 0, 0))],
                 [(0, 1, 1, 0, 0)], 1, (m, nj), [((m, 2 * d), BF16, (m, nj), lambda i, j, k: (0, j))], lambda accs: accs[0])[0]
    tq = _tile(s, 512)
    o = tile_call(
        tag + "_attn", attn_fn, (X_HEADS, s // tq),
        [(q, (tq, dh_), lambda hh, i: (i, hh)), (kv, (m, dh_), lambda hh, i: (0, hh)), (kv, (m, dh_), lambda hh, i: (0, X_HEADS + hh))],
        [((s, d), BF16, (tq, dh_), lambda hh, i: (i, hh), None)])[0]
    out = mm2(tag + "_o", o, wts["wo"], 1, 0, res=h)
    return out, (h, hn, q, kv, o)


def xattn_bwd(tag, saved, memn, mem, wts, dout, dout_b, after=None):
    h, hn, q, kv, o = saved
    s, d = h.shape
    m = memn.shape[0]
    dh_ = d // X_HEADS
    gr = {}
    d_o = mm2(tag + "_do", dout_b, wts["wo"], 1, 1, out_dtype=BF16, after=after)
    gr["wo"] = mm2(tag + "_dwo", o, dout_b, 0, 0)
    tq = _tile(s, 512)
    dq, dk, dv = tile_call(
        tag + "_dattn", vjp_of(attn_fn, 3, (0, 1, 2)), (X_HEADS, s // tq),
        [(q, (tq, dh_), lambda hh, i: (i, hh)), (kv, (m, dh_), lambda hh, i: (0, hh)), (kv, (m, dh_), lambda hh, i: (0, X_HEADS + hh)),
         (d_o, (tq, dh_), lambda hh, i: (i, hh))],
        [((s, d), BF16, (tq, dh_), lambda hh, i: (i, hh), None),
         ((m, d), F32, (m, dh_), lambda hh, i: (0, hh), (1,)), ((m, d), F32, (m, dh_), lambda hh, i: (0, hh), (1,))])
    dkv = jnp.concatenate([dk, dv], axis=1).astype(BF16)
    gr["wq"] = mm2(tag + "_dwq", hn, dq, 0, 0)
    wkv = wts["wkv"]
    nj = wkv.shape[2]
    td, tn = _tile(d, 1024), _tile(d, 1024)
    gr["wkv"] = mm_call(tag + "_dwkv", (d // td, N_SHARDS, 1),
                        [(memn, (m, td), lambda i, j, k: (0, i)), (dkv, (m, nj), lambda i, j, k: (0, j))], [(0, 1, 0, 0, 0)], 1, (td, nj),
                        [((N_SHARDS, d, nj), F32, (None, td, nj), lambda i, j, k: (j, i, 0))], lambda accs: accs[0])[0]
    d_memn = mm_call(tag + "_dmemn", (1, d // tn, N_SHARDS),
                     [(dkv, (m, nj), lambda i, j, k: (0, k)), (wkv, (None, tn, nj), lambda i, j, k: (k, j, 0))], [(0, 1, 1, 1, 0)], 1, (m, tn),
                     [((m, d), F32, (m, tn), lambda i, j, k: (0, j))], lambda accs: accs[0])[0]
    gr["mem_norm"] = rows_call(
        tag + "_dmemnorm", lambda a, dy, b: vjp_of(rms, 2, (1,))(a, b, dy)[0], [mem, d_memn], [wts["mem_norm"]], [],
        [wts["mem_norm"].shape], tr=128)[0]
    d_hn = mm2(tag + "_dhn", dq, wts["wq"], 1, 1)
    dh, dh_b, gr["xattn_norm"] = rms_bwd(tag + "_dnorm", h, wts["xattn_norm"], d_hn, dout)
    return dh, dh_b, gr


def final_loss(x, g, tgt):
    d = x.shape[1]

    def fn(a, t, b):
        def f(aa, bb):
            return 0.5 * jnp.sum(jnp.square(rms(aa, bb) - t)) / d
        loss, (dx, dg) = jax.value_and_grad(f, (0, 1))(a, b)
        lane = lax.broadcasted_iota(jnp.int32, (1, 128), 1)
        return dx, dx, dg, jnp.where(lane == 0, loss, 0.0)
    return rows_call("final_loss", fn, [x, tgt], [g], [(d, F32), (d, BF16)], [g.shape, (1, 128)])


SHARDED = ("ffn1_w_gate", "ffn1_w_up", "ffn1_w_down", "w_in", "conv_w", "pool_w", "w_out", "xattn_wq", "xattn_wkv",
           "xattn_wo", "ffn2_w_gate", "ffn2_w_up", "ffn2_w_down")
REPLICATED = ("ffn1_norm", "mix_norm", "a_log", "dt_bias", "dn_out_norm", "pool_scale", "xattn_norm", "mem_norm",
              "ffn2_norm", "final_norm")
WEIGHTS = ("ffn1_norm", "ffn1_w_gate", "ffn1_w_up", "ffn1_w_down", "mix_norm", "w_in", "conv_w", "a_log", "dt_bias",
           "dn_out_norm", "pool_w", "pool_scale", "w_out", "xattn_norm", "mem_norm", "xattn_wq", "xattn_wkv", "xattn_wo",
           "ffn2_norm", "ffn2_w_gate", "ffn2_w_up", "ffn2_w_down", "final_norm")


def _lane_pad(v, width=128):
    return jnp.pad(v, (0, width - v.shape[0]))[None, :]


GROUPS = {"ffn1": ("ffn1_w_gate", "ffn1_w_up"), "ffn1d": ("ffn1_w_down",), "mixer": ("w_in", "conv_w", "pool_w", "w_out"),
          "xattn": ("xattn_wq", "xattn_wkv", "xattn_wo"), "ffn2": ("ffn2_w_gate", "ffn2_w_up", "ffn2_w_down")}


def _cols(g):
    return jnp.transpose(g, (1, 0, 2)).reshape(g.shape[1], -1)


def _rows(g):
    return g.reshape(-1, g.shape[2])


def _rep_row(rep, name, l):
    return rep[name][l][None, :].astype(F32)


def mixer_weights(g, rep, l):
    dnw = DN_HEADS * HEAD_DIM
    w_in = _cols(g["w_in"])
    o_ab = 4 * dnw
    w_out = _rows(g["w_out"])
    gdim = g["pool_w"].shape[-1]
    pw = jnp.transpose(g["pool_w"].reshape(N_SHARDS, len(POOL_WINDOWS), gdim // N_SHARDS, gdim), (1, 0, 2, 3))
    return dict(
        mix_norm=_rep_row(rep, "mix_norm", l), w_qkv=w_in[:, :3 * dnw], w_z=w_in[:, 3 * dnw:o_ab],
        w_ab=jnp.pad(w_in[:, o_ab:o_ab + 2 * DN_HEADS], ((0, 0), (0, AB_PAD - 2 * DN_HEADS))),
        w_p=w_in[:, o_ab + 2 * DN_HEADS:], conv_w=_cols(g["conv_w"].reshape(N_SHARDS, CONV_WIDTH, -1)).astype(F32),
        a_log=_lane_pad(rep["a_log"][l].astype(F32)), dt_bias=_lane_pad(rep["dt_bias"][l].astype(F32)),
        dn_out_norm=_rep_row(rep, "dn_out_norm", l), pool_w=pw.reshape(len(POOL_WINDOWS), gdim, gdim),
        pool_scale=_rep_row(rep, "pool_scale", l), w_out_a=w_out[:dnw], w_out_b=w_out[dnw:])


def xattn_weights(g, rep, l):
    return dict(xattn_norm=_rep_row(rep, "xattn_norm", l), mem_norm=_rep_row(rep, "mem_norm", l), wq=_rows(g["xattn_wq"]),
                wkv=g["xattn_wkv"], wo=_rows(g["xattn_wo"]))


def _col_shards(g):
    k, n = g.shape
    return jnp.transpose(g.reshape(k, N_SHARDS, n // N_SHARDS), (1, 0, 2))


MIN_COMM_ROWS = 32


def _comm_rows(a):
    if a.shape[-2] >= MIN_COMM_ROWS:
        return a
    return a.reshape(a.shape[:-2] + (MIN_COMM_ROWS, -1))


GRAD_PARTS = (("ffn2_w_gate", "ffn2_w_up", "ffn2_w_down", "xattn_wq", "xattn_wkv", "xattn_wo"),
              ("w_in", "conv_w", "pool_w", "w_out"), ("ffn1_w_gate", "ffn1_w_up", "ffn1_w_down"))


def model_grads(x, mem, tgt, fetch, rep, grads_done=None):
    s, d = x.shape
    depth = rep["ffn1_norm"].shape[0]
    saved, wl = [], []
    h = x
    for l in range(depth):
        t = "l%d" % l
        f1 = dict(fetch(l, "ffn1", h))
        h, s1, f1["ffn1_w_down"] = ffn_fwd(t + "_ffn1", h, _rep_row(rep, "ffn1_norm", l), f1["ffn1_w_gate"], f1["ffn1_w_up"],
                                           lambda act: fetch(l, "ffn1d", act)["ffn1_w_down"])
        wm = mixer_weights(fetch(l, "mixer", h), rep, l)
        h, s2 = mixer_fwd(t + "_mix", h, wm)
        wx = xattn_weights(fetch(l, "xattn", h), rep, l)
        memn = rms_fwd(t + "_memnorm", mem, wx["mem_norm"])
        h, s3 = xattn_fwd(t + "_xattn", h, memn, wx)
        f2 = fetch(l, "ffn2", h)
        h, s4, _ = ffn_fwd(t + "_ffn2", h, _rep_row(rep, "ffn2_norm", l), f2["ffn2_w_gate"], f2["ffn2_w_up"], f2["ffn2_w_down"])
        saved.append((memn, s1, s2, s3, s4))
        wl.append((f1, wm, wx, f2))
    dh, dh_b, d_final, loss_row = final_loss(h, rep["final_norm"][None, :].astype(F32), tgt)
    big, small = [None] * depth, [None] * depth
    after = None
    done = (lambda l, part, gb: None) if grads_done is None else grads_done
    for l in reversed(range(depth)):
        f1, wm, wx, f2 = wl[l]
        t = "l%d" % l
        memn, s1, s2, s3, s4 = saved[l]
        gb, gs = {}, {}
        dh, dh_b, gs["ffn2_norm"], gb["ffn2_w_gate"], gb["ffn2_w_up"], gb["ffn2_w_down"] = ffn_bwd(
            t + "_ffn2", s4, _rep_row(rep, "ffn2_norm", l), f2["ffn2_w_gate"], f2["ffn2_w_up"], f2["ffn2_w_down"], dh, dh_b, after)
        dh, dh_b, gx = xattn_bwd(t + "_xattn", s3, memn, mem, wx, dh, dh_b)
        gb["xattn_wq"] = gx["wq"].reshape(N_SHARDS, d // N_SHARDS, d)
        gb["xattn_wo"] = gx["wo"].reshape(N_SHARDS, d // N_SHARDS, d)
        gb["xattn_wkv"] = gx["wkv"]
        after = done(l, 0, gb)
        dh, dh_b, gm = mixer_bwd(t + "_mix", s2, wm, dh, dh_b, after)
        gb["w_in"] = _col_shards(jnp.concatenate([gm["w_qkv"], gm["w_z"], gm["w_ab"][:, :2 * DN_HEADS], gm["w_p"]], axis=1))
        gb["conv_w"] = _comm_rows(_col_shards(gm["conv_w"]))
        gdim = gm["pool_w"].shape[-1]
        gb["pool_w"] = jnp.transpose(gm["pool_w"].reshape(len(POOL_WINDOWS), N_SHARDS, gdim // N_SHARDS, gdim),
                                     (1, 0, 2, 3)).reshape(N_SHARDS, gdim, gdim)
        gb["w_out"] = jnp.concatenate([gm["w_out_a"], gm["w_out_b"]], axis=0).reshape(N_SHARDS, d // N_SHARDS, d)
        after = done(l, 1, gb)
        dh, dh_b, gs["ffn1_norm"], gb["ffn1_w_gate"], gb["ffn1_w_up"], gb["ffn1_w_down"] = ffn_bwd(
            t + "_ffn1", s1, _rep_row(rep, "ffn1_norm", l), f1["ffn1_w_gate"], f1["ffn1_w_up"], f1["ffn1_w_down"], dh, dh_b, after)
        after = done(l, 2, gb)
        for n in ("xattn_norm", "mem_norm"):
            gs[n] = gx[n]
        for n in ("mix_norm", "a_log", "dt_bias", "dn_out_norm", "pool_scale"):
            gs[n] = gm[n]
        big[l], small[l] = gb, gs
    return loss_row, dh, big, small, d_final


HBM = pl.BlockSpec(memory_space=pltpu.HBM)


def _place():
    x, y, c = lax.axis_index("x"), lax.axis_index("y"), lax.axis_index("c")
    chips = [(1 - x, y), (x, 1 - y), (1 - x, 1 - y)]
    return x, y, c, 2 * x + y, chips, [2 * px + py for px, py in chips]


SEM = pl.BlockSpec(memory_space=pltpu.SEMAPHORE)
SPLIT_COPY = pltpu.CompilerParams(has_side_effects=pltpu.SideEffectType.DATAFLOW_SIDE_EFFECTING)


def _half(ref, lead, c):
    r = ref.shape[1] // 2
    return ref.at[lead, pl.ds(c * r, r)]


def place_shards(name, w, where, dtype):
    nl, r, cdim = w.shape
    tr = _rtile(r)
    out = ((N_SHARDS, r, cdim), dtype, (None, tr, cdim), lambda i, p: (p[0], i, 0), None)
    return tile_call(name, lambda a: tuple(a[l] for l in range(nl)), (r // tr,),
                     [(w, (nl, tr, cdim), lambda i, p: (0, i, 0))], [out] * nl, prefetch=where)


def _gather_copies(bufs, ssem, rsem):
    x, y, c, j_own, chips, js = _place()
    res = []
    for i, b in enumerate(bufs):
        for k, (px, py) in enumerate(chips):
            mk = lambda slot: pltpu.make_async_remote_copy(
                src_ref=_half(b, slot, c), dst_ref=_half(b, slot, c), send_sem=ssem.at[3 * i + k],
                recv_sem=rsem.at[3 * i + k], device_id=(px, py, c), device_id_type=MESH)
            res.append((mk(j_own), mk(js[k])))
    return res


def gather_start(bufs, groups):
    n, ng = len(bufs), len(groups)

    def body(*refs):
        sems, outs, token = refs[n:n + 2 * ng], refs[n + 2 * ng:2 * n + 2 * ng], refs[2 * n + 2 * ng]
        for gi, group in enumerate(groups):
            for mine, _ in _gather_copies([outs[t] for t in group], sems[2 * gi], sems[2 * gi + 1]):
                mine.start()
        token[...] = jnp.zeros_like(token)

    sem_shapes = []
    for group in groups:
        sem_shapes += [pltpu.SemaphoreType.DMA((3 * len(group),))] * 2
    res = pl.pallas_call(
        body, name="gather_start", in_specs=[HBM] * n,
        out_specs=[SEM] * (2 * ng) + [HBM] * n + [pl.BlockSpec(memory_space=pltpu.VMEM)],
        out_shape=sem_shapes + [pltpu.HBM(a.shape, a.dtype) for a in bufs] + [jax.ShapeDtypeStruct((8, 128), F32)],
        input_output_aliases={t: 2 * ng + t for t in range(n)}, compiler_params=SPLIT_COPY,
    )(*[pltpu.with_memory_space_constraint(a, pltpu.HBM) for a in bufs])
    sems = [(res[2 * gi], res[2 * gi + 1]) for gi in range(ng)]
    return sems, res[2 * ng:2 * ng + n], res[-1]


def gather_wait(name, bufs, sems, after):
    n = len(bufs)

    def body(*refs):
        ins, ssem, rsem = refs[:n], refs[n], refs[n + 1]
        for mine, theirs in _gather_copies(ins, ssem, rsem):
            mine.wait_send()
            theirs.wait_recv()

    return pl.pallas_call(
        body, name=name, in_specs=[HBM] * n + [SEM, SEM, pl.BlockSpec(memory_space=pl.ANY)], out_specs=[HBM] * n,
        out_shape=[pltpu.HBM(a.shape, a.dtype) for a in bufs],
        input_output_aliases={t: t for t in range(n)}, compiler_params=SPLIT_COPY,
    )(*bufs, sems[0], sems[1], after)


def gather_swap(name, bufs):
    n = len(bufs)

    def body(*refs):
        outs = refs[n:2 * n]
        ssem, rsem = refs[2 * n:]
        x, y, c, j_own, chips, js = _place()

        def copy(i, k, half):
            blk = _half(outs[i], js[k], half)
            return pltpu.make_async_remote_copy(src_ref=blk, dst_ref=blk, send_sem=ssem.at[3 * i + k],
                                                recv_sem=rsem.at[3 * i + k], device_id=(x, y, 1 - c), device_id_type=MESH)

        for i in range(n):
            for k in range(3):
                copy(i, k, c).start()
        for i in range(n):
            for k in range(3):
                copy(i, k, 1 - c).wait_recv()
                copy(i, k, c).wait_send()

    return pl.pallas_call(
        body, name=name, in_specs=[HBM] * n, out_specs=[HBM] * n, input_output_aliases={t: t for t in range(n)},
        out_shape=[jax.ShapeDtypeStruct(a.shape, a.dtype) for a in bufs],
        scratch_shapes=[pltpu.SemaphoreType.DMA((3 * n,)), pltpu.SemaphoreType.DMA((3 * n,))],
    )(*bufs)


def _pair_copies(grads, lands, ssem, rsem):
    x, y, c, _, _, _ = _place()
    res = []
    for t in range(len(grads)):
        r = grads[t].shape[1] // 2
        res.append(pltpu.make_async_remote_copy(src_ref=grads[t].at[:, pl.ds((1 - c) * r, r)], dst_ref=lands[t], send_sem=ssem.at[t],
                                                recv_sem=rsem.at[t], device_id=(x, y, 1 - c), device_id_type=MESH))
    return res


def _split_start(name, copies, arrays, n_sems, after=None):
    n = len(arrays)
    n_dep = 0 if after is None else 1

    def body(*refs):
        refs = refs[n + n_dep:]
        for cp in copies(refs[2:n + 2], refs[0], refs[1]):
            cp.start()
        refs[n + 2][...] = jnp.zeros((8, 128), F32)

    res = pl.pallas_call(
        body, name=name, in_specs=[HBM] * n + [pl.BlockSpec(memory_space=pl.ANY)] * n_dep,
        out_specs=[SEM, SEM] + [HBM] * n + [pl.BlockSpec(memory_space=pltpu.VMEM)],
        out_shape=[pltpu.SemaphoreType.DMA((n_sems,))] * 2 + [pltpu.HBM(a.shape, a.dtype) for a in arrays]
        + [jax.ShapeDtypeStruct((8, 128), F32)],
        input_output_aliases={t: 2 + t for t in range(n)}, compiler_params=SPLIT_COPY,
    )(*[pltpu.with_memory_space_constraint(a, pltpu.HBM) for a in arrays], *([] if after is None else [after]))
    return res[0], res[1], res[2:2 + n], res[-1]


def _split_wait(name, copies, ssem, rsem, arrays, after):
    n = len(arrays)

    def body(*refs):
        for cp in copies(refs[:n], refs[n], refs[n + 1]):
            cp.wait_send()
            cp.wait_recv()

    return pl.pallas_call(
        body, name=name, in_specs=[HBM] * n + [SEM, SEM, pl.BlockSpec(memory_space=pl.ANY)], out_specs=[HBM] * n,
        out_shape=[pltpu.HBM(a.shape, a.dtype) for a in arrays],
        input_output_aliases={t: t for t in range(n)}, compiler_params=SPLIT_COPY,
    )(*arrays, ssem, rsem, after)


def pair_start(name, grads):
    n = len(grads)
    lands = [lax.empty((a.shape[0], a.shape[1] // 2, a.shape[2]), a.dtype) for a in grads]
    ssem, rsem, arrays, token = _split_start(name, lambda a, s, r: _pair_copies(a[:n], a[n:], s, r), list(grads) + lands, n)
    return ssem, rsem, arrays[:n], arrays[n:], token


def pair_wait(name, ssem, rsem, grads, lands, after):
    n = len(grads)
    arrays = _split_wait(name, lambda a, s, r: _pair_copies(a[:n], a[n:], s, r), ssem, rsem, list(grads) + list(lands), after)
    return arrays[:n], arrays[n:]


def _reduce_copies(parts, lands, ssem, rsem):
    x, y, c, j_own, chips, js = _place()
    return [pltpu.make_async_remote_copy(src_ref=parts[t].at[js[k]], dst_ref=lands[t].at[k], send_sem=ssem.at[3 * t + k],
                                         recv_sem=rsem.at[3 * t + k], device_id=(px, py, c), device_id_type=MESH)
            for t in range(len(parts)) for k, (px, py) in enumerate(chips)]


def reduce_start(name, parts, after):
    n = len(parts)
    lands = [lax.empty((3,) + a.shape[1:], a.dtype) for a in parts]
    ssem, rsem, arrays, token = _split_start(name, lambda a, s, r: _reduce_copies(a[:n], a[n:], s, r), list(parts) + lands,
                                             3 * n, after)
    return ssem, rsem, arrays[:n], arrays[n:], token


def reduce_wait(name, ssem, rsem, parts, lands, after):
    n = len(parts)
    arrays = _split_wait(name, lambda a, s, r: _reduce_copies(a[:n], a[n:], s, r), ssem, rsem, list(parts) + list(lands), after)
    return arrays[:n], arrays[n:]


def share_halves(name, bufs):
    n = len(bufs)

    def body(*refs):
        outs = refs[n:2 * n]
        ssem, rsem = refs[2 * n:]
        x, y, c, _, _, _ = _place()

        def copy(t, l, half):
            blk = _half(outs[t], l, half)
            return pltpu.make_async_remote_copy(src_ref=blk, dst_ref=blk, send_sem=ssem.at[2 * t + l],
                                                recv_sem=rsem.at[2 * t + l], device_id=(x, y, 1 - c), device_id_type=MESH)

        for t in range(n):
            for l in range(2):
                copy(t, l, c).start()
        for t in range(n):
            for l in range(2):
                copy(t, l, 1 - c).wait_recv()
                copy(t, l, c).wait_send()

    return pl.pallas_call(
        body, name=name, in_specs=[HBM] * n, out_specs=[HBM] * n, input_output_aliases={t: t for t in range(n)},
        out_shape=[jax.ShapeDtypeStruct(a.shape, a.dtype) for a in bufs],
        scratch_shapes=[pltpu.SemaphoreType.DMA((2 * n,)), pltpu.SemaphoreType.DMA((2 * n,))],
    )(*bufs)


def allreduce_small(buf):
    r = buf.shape[0]
    n_dev = 8

    def body(in_ref, out_ref, gath, ssem, rsem):
        x, y, c = lax.axis_index("x"), lax.axis_index("y"), lax.axis_index("c")
        flip = lambda v, bit: 1 - v if bit else v
        me = 4 * x + 2 * y + c
        gath[me] = in_ref[...]
        peers = [(flip(x, k >> 2 & 1), flip(y, k >> 1 & 1), flip(c, k & 1)) for k in range(1, n_dev)]
        sends = []
        for k, peer in enumerate(peers):
            cp = pltpu.make_async_remote_copy(src_ref=in_ref, dst_ref=gath.at[me], send_sem=ssem.at[k], recv_sem=rsem.at[k],
                                              device_id=peer, device_id_type=MESH)
            cp.start()
            sends.append(cp)
        for k, (px, py, pc) in enumerate(peers):
            pltpu.make_async_remote_copy(src_ref=in_ref, dst_ref=gath.at[4 * px + 2 * py + pc], send_sem=ssem.at[k],
                                         recv_sem=rsem.at[k], device_id=(px, py, pc), device_id_type=MESH).wait_recv()
        for cp in sends:
            cp.wait_send()
        acc = gath[0]
        for i in range(1, n_dev):
            acc = acc + gath[i]
        out_ref[...] = acc

    return pl.pallas_call(
        body, name="allreduce_small",
        in_specs=[pl.BlockSpec(memory_space=pltpu.VMEM)], out_specs=pl.BlockSpec(memory_space=pltpu.VMEM),
        out_shape=jax.ShapeDtypeStruct(buf.shape, F32),
        scratch_shapes=[pltpu.VMEM((n_dev, r, 128), F32), pltpu.SemaphoreType.DMA((n_dev - 1,)), pltpu.SemaphoreType.DMA((n_dev - 1,))],
    )(buf)


def _rtile(r, pref=256):
    return _tile(r, pref, 16)


def chip_partial(name, grad, recv, where):
    _, rh, cdim = recv.shape
    tr = _rtile(rh, 512)
    nt = rh // tr
    return tile_call(name, lambda a, b: a + b, (N_SHARDS, nt),
                     [(grad, (None, tr, cdim), lambda j, i, p: (j, p[1] * nt + i, 0)), (recv, (None, tr, cdim), lambda j, i, p: (j, i, 0))],
                     [(recv.shape, BF16, (None, tr, cdim), lambda j, i, p: (j, i, 0), None)], prefetch=where)[0]


def sum_chips(name, parts, lands, where, layer, n_layers, into):
    _, rh, cdim = parts.shape
    tr = _rtile(rh, 512)
    nt = rh // tr
    up = lambda a: a.astype(F32)
    return tile_call(name, lambda own, rv: (up(own) + up(rv[0])) + (up(rv[1]) + up(rv[2])), (nt,),
                     [(parts, (None, tr, cdim), lambda i, p: (p[0], i, 0)), (lands, (3, tr, cdim), lambda i, p: (0, i, 0))],
                     [((n_layers, 2 * rh, cdim), F32, (None, tr, cdim), lambda i, p: (layer, p[1] * nt + i, 0), None)],
                     prefetch=where, into=into)[0]


def adamw_fn(w, g, m, v):
    m = ADAM_B1 * m + (1.0 - ADAM_B1) * g
    v = ADAM_B2 * v + (1.0 - ADAM_B2) * jnp.square(g)
    m_hat = m / (1.0 - ADAM_B1 ** ADAM_STEP)
    v_hat = v / (1.0 - ADAM_B2 ** ADAM_STEP)
    delta = -ADAM_LR * (m_hat / (jnp.sqrt(v_hat) + ADAM_EPS) + ADAM_WD * w)
    return delta, m, v, g


def adamw(name, w, g, m, v):
    nl, r, cdim = w.shape
    tr = _rtile(r, 256)
    spec = ((None, tr, cdim), lambda l, i: (l, i, 0))
    return tile_call(name, adamw_fn, (nl, r // tr), [(a,) + spec for a in (w, g, m, v)],
                     [(w.shape, F32) + spec + (None,)] * 4)


def _as3(a):
    return _comm_rows(a.reshape(a.shape[0], -1, a.shape[-1]))


COMM_GROUPS = ((0, ("ffn1",)), (0, ("ffn1d",)), (0, ("mixer",)), (0, ("xattn", "ffn2")), (1, ("ffn1", "ffn1d", "mixer")),
               (1, ("xattn", "ffn2")))


def _pack_rows(vals):
    rows = []
    for v in vals:
        v = v.reshape(-1).astype(F32)
        pad = (-v.shape[0]) % 128
        rows.append(jnp.pad(v, (0, pad)).reshape(-1, 128))
    out = jnp.concatenate(rows, axis=0)
    return jnp.pad(out, ((0, (-out.shape[0]) % 8), (0, 0)))


def _unpack_rows(buf, like):
    outs, r = [], 0
    for a in like:
        n = a.size
        nr = -(-n // 128)
        outs.append(buf[r:r + nr].reshape(-1)[:n].reshape(a.shape))
        r += nr
    return outs


def kernel(x, mem, ffn1_norm, ffn1_w_gate, ffn1_w_up, ffn1_w_down, mix_norm, w_in, conv_w, a_log, dt_bias, dn_out_norm, pool_w, pool_scale, w_out, xattn_norm, mem_norm, xattn_wq, xattn_wkv, xattn_wo, ffn2_norm, ffn2_w_gate, ffn2_w_up, ffn2_w_down, final_norm, loss_target, m_ffn1_norm, m_ffn1_w_gate, m_ffn1_w_up, m_ffn1_w_down, m_mix_norm, m_w_in, m_conv_w, m_a_log, m_dt_bias, m_dn_out_norm, m_pool_w, m_pool_scale, m_w_out, m_xattn_norm, m_mem_norm, m_xattn_wq, m_xattn_wkv, m_xattn_wo, m_ffn2_norm, m_ffn2_w_gate, m_ffn2_w_up, m_ffn2_w_down, m_final_norm, v_ffn1_norm, v_ffn1_w_gate, v_ffn1_w_up, v_ffn1_w_down, v_mix_norm, v_w_in, v_conv_w, v_a_log, v_dt_bias, v_dn_out_norm, v_pool_w, v_pool_scale, v_w_out, v_xattn_norm, v_mem_norm, v_xattn_wq, v_xattn_wkv, v_xattn_wo, v_ffn2_norm, v_ffn2_w_gate, v_ffn2_w_up, v_ffn2_w_down, v_final_norm):
    given = dict(locals())
    w = {n: given[n] for n in WEIGHTS}
    m = {n: given["m_" + n] for n in WEIGHTS}
    v = {n: given["v_" + n] for n in WEIGHTS}
    where = jnp.stack([2 * lax.axis_index("x") + lax.axis_index("y"), lax.axis_index("c")]).astype(jnp.int32)
    placed = {}
    for n in SHARDED:
        for l, buf in enumerate(place_shards("place_" + n, _as3(w[n]), where, F32 if n == "conv_w" else BF16)):
            placed[(l, n)] = buf
    keys, groups = [], []
    for l, entries in COMM_GROUPS:
        groups.append([])
        for e in entries:
            for n in GROUPS[e]:
                groups[-1].append(len(keys))
                keys.append((l, n))
    sems, bufs, _ = gather_start([placed[k] for k in keys], groups)
    fetched = {}

    def fetch(l, entry, after):
        gi = [i for i, (gl, entries) in enumerate(COMM_GROUPS) if gl == l and entry in entries][0]
        if gi not in fetched:
            landed = gather_wait("gather_wait%d" % gi, [bufs[i] for i in groups[gi]], sems[gi], after)
            fetched[gi] = {keys[i][1]: a for i, a in zip(groups[gi], gather_swap("gather_swap%d" % gi, landed))}
        return fetched[gi]

    swapping, travelling = [], []

    def to_ici(after):
        tag, names, (ssem, rsem, mine, theirs) = swapping.pop(0)
        mine, theirs = pair_wait("pair_wait" + tag, ssem, rsem, mine, theirs, after)
        parts = [chip_partial("partial%s_%s" % (tag, n), g, r, where) for n, g, r in zip(names, mine, theirs)]
        ssem, rsem, parts, lands, token = reduce_start("reduce_start" + tag, parts, after)
        travelling.append((tag, names, (ssem, rsem, parts, lands)))
        return token

    def grads_done(l, part, gb):
        tag, names = "%d%d" % (l, part), GRAD_PARTS[part]
        ssem, rsem, mine, theirs, token = pair_start("pair_start" + tag, [gb[n] for n in names])
        swapping.append((tag, names, (ssem, rsem, mine, theirs)))
        return to_ici(token) if len(swapping) > 1 else token

    rep = {n: w[n] for n in REPLICATED}
    loss_row, dx, big, small, d_final = model_grads(x[0], mem[0], loss_target[0], fetch, rep, grads_done)
    to_ici(dx)
    n_layers = len(big)
    sums, out_g, out_d, out_m, out_v = {}, {}, {}, {}, {}

    def land(after):
        tag, names, (ssem, rsem, parts, lands) = travelling.pop(0)
        parts, lands = reduce_wait("reduce_wait" + tag, ssem, rsem, parts, lands, after)
        for n, p, r in zip(names, parts, lands):
            sums[n] = sum_chips("sum%s_%s" % (tag, n), p, r, where, int(tag[0]), n_layers, sums.get(n))

    def finish(part):
        names = GRAD_PARTS[part]
        for n, g in zip(names, share_halves("share_halves%d" % part, [sums[n] for n in names])):
            d_, m_, v_, g_ = adamw("adamw_" + n, _as3(w[n]), g, _as3(m[n]), _as3(v[n]))
            out_g[n], out_d[n], out_m[n], out_v[n] = (a.reshape(w[n].shape) for a in (g_, d_, m_, v_))
        return out_d[names[-1]]

    for _ in range(len(travelling) - 1):
        land(dx)
    finish(0)
    land(finish(1))
    finish(2)
    rep_names = [n for n in REPLICATED if n != "final_norm"]
    g_rep = {n: jnp.stack([small[l][n][0, :w[n].shape[1]] for l in range(len(small))]) for n in rep_names}
    g_rep["final_norm"] = d_final[0]
    like = [w[n] for n in REPLICATED] + [jnp.zeros((1,), F32)]
    summed = allreduce_small(_pack_rows([g_rep[n] for n in REPLICATED] + [loss_row[0, :1]]))
    pk = lambda tree: _pack_rows([tree[n] for n in REPLICATED] + [jnp.zeros((1,), F32)])
    wp, mp, vp = pk(w), pk(m), pk(v)
    dp, mp2, vp2, _ = adamw("adamw_small", wp[None], summed[None], mp[None], vp[None])
    for buf, dst in ((summed, out_g), (dp[0], out_d), (mp2[0], out_m), (vp2[0], out_v)):
        for n, a in zip(REPLICATED, _unpack_rows(buf, like)):
            dst[n] = a
    loss = _unpack_rows(summed, like)[-1][0]
    return (loss, dx[None], *[out_g[n] for n in WEIGHTS], *[out_d[n] for n in WEIGHTS],
            *[out_m[n] for n in WEIGHTS], *[out_v[n] for n in WEIGHTS])
```

```python
import functools

import jax
import jax.numpy as jnp
from jax import lax
from jax.experimental import pallas as pl
from jax.experimental.pallas import tpu as pltpu

F32, BF16 = jnp.float32, jnp.bfloat16
HI = lax.Precision.HIGHEST
MESH = pl.DeviceIdType.MESH

EPS = 1e-6
DN_HEADS = 8
HEAD_DIM = 128
X_HEADS = 4
POOL_WINDOWS = (2, 4, 8, 16)
CONV_WIDTH = 4
CHUNK = 64
N_SHARDS = 4
AB_PAD = 128
ADAM_LR, ADAM_B1, ADAM_B2, ADAM_EPS, ADAM_WD, ADAM_STEP = 0.001, 0.9, 0.999, 1e-08, 0.01, 10
VMEM_LIMIT = 56 << 20
ROW_TILE = 256


def _tile(n, pref, unit=128):
    best = None
    for t in range(unit, min(n, pref) + 1, unit):
        if n % t == 0:
            best = t
    return best if best is not None else n


def _params():
    return pltpu.CompilerParams(vmem_limit_bytes=VMEM_LIMIT)


def _split(a):
    a = a.astype(F32)
    head = a.astype(BF16)
    return head, (a - head.astype(F32)).astype(BF16)


def _dg(a, b, ca, cb, hi):
    dims = (((ca,), (cb,)), ((), ()))
    dot = lambda u, v: lax.dot_general(u, v, dims, preferred_element_type=F32)
    if hi:
        (a0, a1), (b0, b1) = _split(a), _split(b)
        return dot(a0, b0) + (dot(a0, b1) + dot(a1, b0))
    return dot(a.astype(BF16), b.astype(BF16))


@functools.partial(jax.custom_vjp, nondiff_argnums=(2, 3, 4))
def mmul(a, b, ca, cb, hi):
    return _dg(a, b, ca, cb, hi)


def _mmul_fwd(a, b, ca, cb, hi):
    return _dg(a, b, ca, cb, hi), (a, b)


def _mmul_bwd(ca, cb, hi, res, g):
    a, b = res
    da = _dg(g, b, 1, 1 - cb, hi) if ca == 1 else _dg(b, g, 1 - cb, 1, hi)
    db = _dg(a, g, 1 - ca, 0, hi) if cb == 0 else _dg(g, a, 0, 1 - ca, hi)
    return da.astype(a.dtype), db.astype(b.dtype)


mmul.defvjp(_mmul_fwd, _mmul_bwd)


@functools.partial(jax.custom_vjp, nondiff_argnums=(1,))
def shift_down(x, s):
    t = lax.broadcasted_iota(jnp.int32, x.shape, 0)
    return jnp.where(t >= s, pltpu.roll(x, s, 0), 0.0)


def _shift_up(x, s):
    n = x.shape[0]
    t = lax.broadcasted_iota(jnp.int32, x.shape, 0)
    return jnp.where(t < n - s, pltpu.roll(x, n - s, 0), 0.0)


shift_down.defvjp(lambda x, s: (shift_down(x, s), None), lambda s, _, g: (_shift_up(g, s),))


def sigmoid(x):
    return 0.5 * (jnp.tanh(0.5 * x) + 1.0)


def silu(x):
    return x * sigmoid(x)


@jax.custom_vjp
def softplus(x):
    u = jnp.exp(-jnp.abs(x))
    w = 1.0 + u
    log1p = jnp.where(w == 1.0, u, jnp.log(w) * u / jnp.where(w == 1.0, 1.0, w - 1.0))
    return jnp.maximum(x, 0.0) + log1p


softplus.defvjp(lambda x: (softplus(x), x), lambda x, g: (g * sigmoid(x),))


def rms(x, g):
    x = x.astype(F32)
    return x * lax.rsqrt(jnp.mean(x * x, axis=-1, keepdims=True) + EPS) * g


def swiglu(gate, up):
    return silu(gate) * up


def vjp_of(fn, n_in, diff, has_pids=False):
    def g(*args):
        pids = None
        if has_pids:
            pids, args = args[0], args[1:]
        ins, cots = list(args[:n_in]), args[n_in:]

        def f(*d):
            full = list(ins)
            for i, v in zip(diff, d):
                full[i] = v
            return fn(pids, *full) if has_pids else fn(*full)

        out, pull = jax.vjp(f, *[ins[i].astype(F32) for i in diff])
        if isinstance(out, (tuple, list)):
            return pull(tuple(c.astype(o.dtype) for c, o in zip(cots, out)))
        return pull(cots[0].astype(out.dtype))
    return g


def tile_call(name, fn, grid, ins, outs, with_pids=False, prefetch=None, into=None):
    n_in = len(ins)
    n_into = 0 if into is None else 1

    def body(*refs):
        if prefetch is not None:
            refs = refs[1:]
        pids = tuple(pl.program_id(a) for a in range(len(grid)))
        vals = [r[...] for r in refs[:n_in]]
        res = fn(pids, *vals) if with_pids else fn(*vals)
        if not isinstance(res, (tuple, list)):
            res = (res,)
        for r, o, spec in zip(res, refs[n_in + n_into:], outs):
            acc = spec[4]
            if acc is None:
                o[...] = r.astype(o.dtype)
            else:
                first = functools.reduce(jnp.logical_and, [pids[a] == 0 for a in acc])

                @pl.when(first)
                def _():
                    o[...] = r.astype(o.dtype)

                @pl.when(jnp.logical_not(first))
                def _():
                    o[...] += r.astype(o.dtype)

    in_specs = [pl.BlockSpec(b, im) for _, b, im in ins] + [pl.BlockSpec(memory_space=pl.ANY)] * n_into
    out_specs = [pl.BlockSpec(s[2], s[3]) for s in outs]
    out_shape = [jax.ShapeDtypeStruct(s[0], s[1]) for s in outs]
    args = [a for a, _, _ in ins] + ([] if into is None else [into])
    if prefetch is None:
        return pl.pallas_call(body, name=name, grid=grid, in_specs=in_specs, out_specs=out_specs, out_shape=out_shape,
                              input_output_aliases={n_in: 0} if n_into else {}, compiler_params=_params())(*args)
    spec = pltpu.PrefetchScalarGridSpec(num_scalar_prefetch=1, grid=grid, in_specs=in_specs, out_specs=out_specs)
    return pl.pallas_call(body, name=name, grid_spec=spec, out_shape=out_shape,
                          input_output_aliases={n_in + 1: 0} if n_into else {}, compiler_params=_params())(prefetch, *args)


def mm_call(name, grid, ins, pairs, n_acc, acc_shape, outs, epilogue, extras=(), after=None):
    n_in, n_ex, nk = len(ins), len(extras), grid[2]
    n_dep = 0 if after is None else 1

    def finish(accs, ex_refs, out_refs):
        res = epilogue(accs, *[r[...] for r in ex_refs])
        if not isinstance(res, (tuple, list)):
            res = (res,)
        for r, o in zip(res, out_refs):
            o[...] = r.astype(o.dtype)

    def body(*refs):
        in_refs, ex_refs = refs[:n_in], refs[n_in:n_in + n_ex]
        refs = refs[n_in + n_ex + n_dep:]
        out_refs, accs = refs[:len(outs)], refs[len(outs):]
        if nk == 1:
            vals = [None] * n_acc
            for ia, ib, ca, cb, ai in pairs:
                d = _dg(in_refs[ia][...], in_refs[ib][...], ca, cb, False)
                vals[ai] = d if vals[ai] is None else vals[ai] + d
            finish(vals, ex_refs, out_refs)
            return
        k = pl.program_id(2)

        @pl.when(k == 0)
        def _():
            for a in accs:
                a[...] = jnp.zeros_like(a)

        for ia, ib, ca, cb, ai in pairs:
            accs[ai][...] += _dg(in_refs[ia][...], in_refs[ib][...], ca, cb, False)

        @pl.when(k == nk - 1)
        def _():
            finish([a[...] for a in accs], ex_refs, out_refs)

    return pl.pallas_call(
        body, name=name, grid=grid,
        in_specs=[pl.BlockSpec(b, im) for _, b, im in list(ins) + list(extras)] + [pl.BlockSpec(memory_space=pl.ANY)] * n_dep,
        out_specs=[pl.BlockSpec(s[2], s[3]) for s in outs],
        out_shape=[jax.ShapeDtypeStruct(s[0], s[1]) for s in outs],
        scratch_shapes=[pltpu.VMEM(acc_shape, F32) for _ in range(n_acc if nk > 1 else 0)],
        compiler_params=_params(),
    )(*[a for a, _, _ in list(ins) + list(extras)], *([] if after is None else [after]))


MM_VMEM_BUDGET = 40 << 20


def _mm_tiles(m, n, kk, sa, sb, so, has_res):
    tk = _tile(kk, 2048)
    best = None
    for tm in (1024, 512, 256, 128):
        for tn in (1024, 512, 256, 128):
            tm_, tn_ = _tile(m, tm), _tile(n, tn)
            need = 2 * (tm_ * tk * sa + tk * tn_ * sb + tm_ * tn_ * so) + (tm_ * tn_ * 4 if tk < kk else 0)
            need += 2 * tm_ * tn_ * 4 if has_res else 0
            if need <= MM_VMEM_BUDGET and (best is None or tm_ * tn_ > best[0] * best[1]):
                best = (tm_, tn_)
    return best + (tk,)


def mm2(name, a, b, ca, cb, res=None, scale=None, out_dtype=F32, after=None):
    m, kk, n = a.shape[1 - ca], a.shape[ca], b.shape[1 - cb]
    tm, tn, tk = _mm_tiles(m, n, kk, a.dtype.itemsize, b.dtype.itemsize, jnp.dtype(out_dtype).itemsize, res is not None)
    a_spec = ((tm, tk), lambda i, j, k: (i, k)) if ca == 1 else ((tk, tm), lambda i, j, k: (k, i))
    b_spec = ((tk, tn), lambda i, j, k: (k, j)) if cb == 0 else ((tn, tk), lambda i, j, k: (j, k))
    extras = [] if res is None else [(res, (tm, tn), lambda i, j, k: (i, j))]

    def epi(accs, *ex):
        r = accs[0] if scale is None else accs[0] * scale
        return r + ex[0] if ex else r

    return mm_call(name, (m // tm, n // tn, kk // tk), [(a,) + a_spec, (b,) + b_spec], [(0, 1, ca, cb, 0)], 1, (tm, tn),
                   [((m, n), out_dtype, (tm, tn), lambda i, j, k: (i, j))], epi, extras, after)[0]


def rows_call(name, fn, rows, consts, outs, acc_outs=(), tr=ROW_TILE):
    s = rows[0].shape[0]
    tr = _tile(s, tr, 8)
    ins = [(r, (tr, r.shape[1]), lambda i: (i, 0)) for r in rows]
    ins += [(c, c.shape, (lambda nd: (lambda i: (0,) * nd))(c.ndim)) for c in consts]
    o = [((s, c), dt, (tr, c), lambda i: (i, 0), None) for c, dt in outs]
    o += [(shp, F32, shp, (lambda nd: (lambda i: (0,) * nd))(len(shp)), (0,)) for shp in acc_outs]
    return tile_call(name, fn, (s // tr,), ins, o)


def _lane_pick(x, h):
    lane = lax.broadcasted_iota(jnp.int32, x.shape, x.ndim - 1)
    return jnp.sum(jnp.where(lane == h, x, 0.0), axis=-1, keepdims=True)


def gateprep_fn(ab, alog, dtb):
    t = ab.shape[0]
    gs, bs = [], []
    for h in range(DN_HEADS):
        a_h = _lane_pick(ab, h)
        b_h = _lane_pick(ab, DN_HEADS + h)
        g_h = -jnp.exp(_lane_pick(alog, h)) * softplus(a_h + _lane_pick(dtb, h))
        gs.append(jnp.broadcast_to(g_h, (t, HEAD_DIM)))
        bs.append(jnp.broadcast_to(sigmoid(b_h), (t, HEAD_DIM)))
    return jnp.concatenate(gs, axis=1), jnp.concatenate(bs, axis=1)


def conv_fn(pids, x, w):
    kind = pids[0] // DN_HEADS
    y = x * w[CONV_WIDTH - 1:CONV_WIDTH]
    for i in range(CONV_WIDTH - 1):
        y = y + shift_down(x, CONV_WIDTH - 1 - i) * w[i:i + 1]
    y = silu(y)
    n = y * lax.rsqrt(jnp.sum(y * y, axis=-1, keepdims=True) + EPS)
    n = n * jnp.where(kind == 0, HEAD_DIM ** -0.5, 1.0)
    return jnp.where(kind == 2, y, n)


def unit_lower_inverse(a):
    c = a.shape[0]
    r = lax.broadcasted_iota(jnp.int32, (c, c), 0)
    cc = lax.broadcasted_iota(jnp.int32, (c, c), 1)
    x = -a
    t = jnp.where(r == cc, 1.0, 0.0) + x
    p = 2
    while p < c:
        x = _dg(x, x, 1, 0, True)
        t = t + _dg(t, x, 1, 0, True)
        p *= 2
    return t


@jax.custom_vjp
def known_inverse(a, t):
    return t


known_inverse.defvjp(lambda a, t: (t, t),
                     lambda t, g: (-_dg(_dg(t, g, 0, 0, True), t, 1, 1, True), jnp.zeros_like(t)))


def intra_head(q, k, v, g, b, t_known):
    c = q.shape[0]
    r = lax.broadcasted_iota(jnp.int32, (c, c), 0)
    cc = lax.broadcasted_iota(jnp.int32, (c, c), 1)
    tril = (r >= cc).astype(F32)
    gc = mmul(tril, g, 1, 0, True)
    m = gc[:, :c]
    decay = jnp.exp(jnp.where(r >= cc, m - m.T, -1e30))
    kb = k * b
    a = jnp.where(r > cc, mmul(kb, k, 1, 1, False) * decay, 0.0)
    t = unit_lower_inverse(a) if t_known is None else known_inverse(a, t_known)
    e = jnp.exp(gc)
    u = mmul(t, v * b, 1, 0, True)
    w = mmul(t, kb * e, 1, 0, True)
    qk = mmul(q, k, 1, 1, False) * decay
    gl = gc[c - 1:c, :]
    kd = k * jnp.exp(gl - gc)
    outs = (u, w, qk, q * e, kd, jnp.broadcast_to(jnp.exp(gl), (8, HEAD_DIM)))
    return outs + (t,) if t_known is None else outs


def _heads(x, h):
    return x[:, h * HEAD_DIM:(h + 1) * HEAD_DIM]


def intra_fn(q, k, v, g, b):
    outs = [intra_head(_heads(q, h), _heads(k, h), _heads(v, h), _heads(g, h), _heads(b, h), None) for h in range(DN_HEADS)]
    cat = lambda i: jnp.concatenate([o[i] for o in outs], axis=1)
    stack = lambda i: jnp.stack([o[i] for o in outs], axis=0)
    return cat(0), cat(1), stack(2), cat(3), cat(4), cat(5), stack(6)


def intra_bwd_fn(q, k, v, g, b, tinv, du, dw, dqk, dqd, dkd, dgl):
    res = []
    for h in range(DN_HEADS):
        hs = lambda x: _heads(x, h)
        res.append(vjp_of(intra_head, 6, (0, 1, 2, 3, 4))(hs(q), hs(k), hs(v), hs(g), hs(b), tinv[h],
                                                         hs(du), hs(dw), dqk[h], hs(dqd), hs(dkd), hs(dgl)))
    cat = lambda i: jnp.concatenate([r[i] for r in res], axis=1)
    return jnp.concatenate([cat(0), cat(1), cat(2)], axis=1), cat(3), cat(4)


def scan_step(s, u, w, qk, qd, kd, gl):
    v_new = u - mmul(w, s, 1, 0, False)
    o = mmul(qd, s, 1, 0, False) + mmul(qk, v_new, 1, 0, False)
    return s * gl[0:1, :] + mmul(kd, v_new, 0, 0, False), o


def outgate_fn(o, z, g):
    return rms(o, g) * silu(z)


def pool_fn(pids, p):
    gid = pids[0]
    s = p.shape[0]
    t1 = (lax.broadcasted_iota(jnp.int32, p.shape, 0) + 1).astype(F32)
    acc, win, out = p, 1, None
    for gi, target in enumerate(POOL_WINDOWS):
        while win < target:
            acc = acc + shift_down(acc, win)
            win *= 2
        cand = acc / jnp.minimum(t1, float(target))
        out = cand if out is None else jnp.where(gid == gi, cand, out)
    return out - p


def poolmix_fn(pooled, pw, scale):
    return mmul(pooled, pw, 1, 0, False) * scale


def attn_fn(q, k, v):
    s = mmul(q, k, 1, 1, False) * (q.shape[1] ** -0.5)
    s = s - jnp.max(s, axis=-1, keepdims=True)
    e = jnp.exp(s)
    p = e / jnp.sum(e, axis=-1, keepdims=True)
    return mmul(p, v, 1, 0, False)


def rms_fwd(name, x, g):
    return rows_call(name, lambda a, b: rms(a, b), [x], [g], [(x.shape[1], BF16)])[0]


def rms_bwd(name, x, g, dy, dres):
    def fn(a, d, r, b):
        dx, dg = vjp_of(rms, 2, (0, 1))(a, b, d)
        return dx + r, dx + r, dg
    return rows_call(name, fn, [x, dy, dres], [g], [(x.shape[1], F32), (x.shape[1], BF16)], [g.shape])


def ffn_fwd(tag, x, g, wg, wu, wd):
    s, d = x.shape
    fj = wg.shape[-1]
    f = N_SHARDS * fj
    xn = rms_fwd(tag + "_norm", x, g)
    tm = _tile(s, 256)
    w_spec = ((None, d, fj), lambda j, i, k: (j, 0, 0))
    o_spec = ((tm, fj), lambda j, i, k: (i, j))
    gate, up, act = mm_call(
        tag + "_gu", (N_SHARDS, s // tm, 1),
        [(xn, (tm, d), lambda j, i, k: (i, 0)), (wg,) + w_spec, (wu,) + w_spec],
        [(0, 1, 1, 0, 0), (0, 2, 1, 0, 1)], 2, (tm, fj),
        [((s, f), F32) + o_spec, ((s, f), F32) + o_spec, ((s, f), BF16) + o_spec],
        lambda accs: (accs[0], accs[1], swiglu(accs[0], accs[1])))
    if callable(wd):
        wd = wd(act)
    tm, tn = _tile(s, 1024), _tile(d, 1024)
    out = mm_call(
        tag + "_down", (s // tm, d // tn, N_SHARDS),
        [(act, (tm, fj), lambda i, j, k: (i, k)), (wd, (None, fj, tn), lambda i, j, k: (k, 0, j))],
        [(0, 1, 1, 0, 0)], 1, (tm, tn),
        [((s, d), F32, (tm, tn), lambda i, j, k: (i, j))],
        lambda accs, r: r + 0.5 * accs[0], [(x, (tm, tn), lambda i, j, k: (i, j))])[0]
    return out, (x, xn, gate, up, act), wd


def ffn_bwd(tag, saved, g, wg, wu, wd, dout, dout_b, after=None):
    x, xn, gate, up, act = saved
    s, d = x.shape
    fj = wg.shape[-1]
    f = N_SHARDS * fj
    tm = _tile(s, 512)
    o_spec = ((tm, fj), lambda j, i, k: (i, j))

    def epi(accs, ga, u):
        dgate, dup = vjp_of(swiglu, 2, (0, 1))(ga, u, 0.5 * accs[0])
        return dgate, dup

    dgate, dup = mm_call(
        tag + "_dact", (N_SHARDS, s // tm, 1),
        [(dout_b, (tm, d), lambda j, i, k: (i, 0)), (wd, (None, fj, d), lambda j, i, k: (j, 0, 0))],
        [(0, 1, 1, 1, 0)], 1, (tm, fj),
        [((s, f), BF16) + o_spec, ((s, f), BF16) + o_spec], epi,
        [(gate,) + o_spec, (up,) + o_spec], after=after)
    tn = _tile(d, 1024)
    dwd = mm_call(
        tag + "_dwd", (N_SHARDS, d // tn, 1),
        [(act, (s, fj), lambda j, i, k: (0, j)), (dout_b, (s, tn), lambda j, i, k: (0, i))],
        [(0, 1, 0, 0, 0)], 1, (fj, tn),
        [((N_SHARDS, fj, d), F32, (None, fj, tn), lambda j, i, k: (j, 0, i))],
        lambda accs: 0.5 * accs[0])[0]
    td = _tile(d, 512)
    g_spec = ((s, fj), lambda j, i, k: (0, j))
    w_out = ((N_SHARDS, d, fj), F32, (None, td, fj), lambda j, i, k: (j, i, 0))
    dwg, dwu = mm_call(
        tag + "_dwgu", (N_SHARDS, d // td, 1),
        [(xn, (s, td), lambda j, i, k: (0, i)), (dgate,) + g_spec, (dup,) + g_spec],
        [(0, 1, 0, 0, 0), (0, 2, 0, 0, 1)], 2, (td, fj), [w_out, w_out], lambda accs: (accs[0], accs[1]))
    tm, tn = _tile(s, 1024), _tile(d, 1024)
    a_spec = ((tm, fj), lambda i, j, k: (i, k))
    wt_spec = ((None, tn, fj), lambda i, j, k: (k, j, 0))
    dxn = mm_call(
        tag + "_dxn", (s // tm, d // tn, N_SHARDS),
        [(dgate,) + a_spec, (wg,) + wt_spec, (dup,) + a_spec, (wu,) + wt_spec],
        [(0, 1, 1, 1, 0), (2, 3, 1, 1, 0)], 1, (tm, tn),
        [((s, d), F32, (tm, tn), lambda i, j, k: (i, j))], lambda accs: accs[0])[0]
    dx, dx_b, dg = rms_bwd(tag + "_dnorm", x, g, dxn, dout)
    return dx, dx_b, dg, dwg, dwu, dwd


def mixer_fwd(tag, h, wts):
    s, d = h.shape
    dnw = DN_HEADS * HEAD_DIM
    pw_ = d - dnw
    gdim = pw_ // len(POOL_WINDOWS)
    nc = s // CHUNK
    hn = rms_fwd(tag + "_norm", h, wts["mix_norm"])
    qkv = mm2(tag + "_qkv", hn, wts["w_qkv"], 1, 0)
    z = mm2(tag + "_z", hn, wts["w_z"], 1, 0)
    ab = mm2(tag + "_ab", hn, wts["w_ab"], 1, 0)
    p = mm2(tag + "_p", hn, wts["w_p"], 1, 0)
    qkvn = tile_call(tag + "_conv", conv_fn, (3 * DN_HEADS,),
                     [(qkv, (s, HEAD_DIM), lambda i: (0, i)), (wts["conv_w"], (CONV_WIDTH, HEAD_DIM), lambda i: (0, i))],
                     [((s, 3 * dnw), F32, (s, HEAD_DIM), lambda i: (0, i), None)], with_pids=True)[0]
    g_bc, b_bc = rows_call(tag + "_gates", gateprep_fn, [ab], [wts["a_log"], wts["dt_bias"]], [(dnw, F32), (dnw, F32)])
    cw = (CHUNK, dnw)
    sq = ((DN_HEADS, s, CHUNK), F32, (DN_HEADS, CHUNK, CHUNK), lambda n: (0, n, 0), None)
    u, w, qk, qd, kd, gl, tinv = tile_call(
        tag + "_intra", intra_fn, (nc,),
        [(qkvn, cw, lambda n: (n, 0)), (qkvn, cw, lambda n: (n, 1)), (qkvn, cw, lambda n: (n, 2)),
         (g_bc, cw, lambda n: (n, 0)), (b_bc, cw, lambda n: (n, 0))],
        [((s, dnw), F32, cw, lambda n: (n, 0), None), ((s, dnw), F32, cw, lambda n: (n, 0), None), sq,
         ((s, dnw), F32, cw, lambda n: (n, 0), None), ((s, dnw), F32, cw, lambda n: (n, 0), None),
         ((nc * 8, dnw), F32, (8, dnw), lambda n: (n, 0), None), sq])
    o, states = scan_fwd(tag + "_scan", u, w, qk, qd, kd, gl)
    y_dn = tile_call(
        tag + "_outgate", outgate_fn, (DN_HEADS, s // ROW_TILE),
        [(o, (ROW_TILE, HEAD_DIM), lambda hh, i: (i, hh)), (z, (ROW_TILE, HEAD_DIM), lambda hh, i: (i, hh)),
         (wts["dn_out_norm"], (1, HEAD_DIM), lambda hh, i: (0, 0))],
        [((s, dnw), BF16, (ROW_TILE, HEAD_DIM), lambda hh, i: (i, hh), None)])[0]
    ng = len(POOL_WINDOWS)
    pooled = tile_call(tag + "_pool", pool_fn, (ng,), [(p, (s, gdim), lambda i: (0, i))],
                       [((s, pw_), BF16, (s, gdim), lambda i: (0, i), None)], with_pids=True)[0]
    tp = _tile(s, 512)
    y_pool = tile_call(
        tag + "_poolmix", poolmix_fn, (ng, s // tp),
        [(pooled, (tp, gdim), lambda gi, i: (i, gi)), (wts["pool_w"], (None, gdim, gdim), lambda gi, i: (gi, 0, 0)),
         (wts["pool_scale"], (1, gdim), lambda gi, i: (0, gi))],
        [((s, pw_), BF16, (tp, gdim), lambda gi, i: (i, gi), None)])[0]
    h1 = mm2(tag + "_out_a", y_dn, wts["w_out_a"], 1, 0, res=h)
    h2 = mm2(tag + "_out_b", y_pool, wts["w_out_b"], 1, 0, res=h1)
    saved = (h, hn, qkv, z, ab, p, qkvn, g_bc, b_bc, u, w, qk, qd, kd, gl, tinv, o, states, y_dn, pooled, y_pool)
    return h2, saved


def scan_fwd(name, u, w, qk, qd, kd, gl):
    s, dnw = u.shape
    nc = s // CHUNK
    cw = (CHUNK, dnw)

    def body(u_r, w_r, qk_r, qd_r, kd_r, gl_r, o_r, st_r, state):
        @pl.when(pl.program_id(0) == 0)
        def _():
            state[...] = jnp.zeros_like(state)

        st_r[...] = state[...]
        outs = []
        for h in range(DN_HEADS):
            hs = slice(h * HEAD_DIM, (h + 1) * HEAD_DIM)
            s_new, o_h = scan_step(state[hs, :], u_r[:, hs], w_r[:, hs], qk_r[h], qd_r[:, hs], kd_r[:, hs], gl_r[:, hs])
            state[hs, :] = s_new
            outs.append(o_h)
        o_r[...] = jnp.concatenate(outs, axis=1)

    row = lambda n: (n, 0)
    return pl.pallas_call(
        body, name=name, grid=(nc,),
        in_specs=[pl.BlockSpec(cw, row), pl.BlockSpec(cw, row), pl.BlockSpec((DN_HEADS, CHUNK, CHUNK), lambda n: (0, n, 0)),
                  pl.BlockSpec(cw, row), pl.BlockSpec(cw, row), pl.BlockSpec((8, dnw), row)],
        out_specs=[pl.BlockSpec(cw, row), pl.BlockSpec((None, dnw, HEAD_DIM), lambda n: (n, 0, 0))],
        out_shape=[jax.ShapeDtypeStruct((s, dnw), F32), jax.ShapeDtypeStruct((nc, dnw, HEAD_DIM), F32)],
        scratch_shapes=[pltpu.VMEM((dnw, HEAD_DIM), F32)],
        compiler_params=_params(),
    )(u, w, qk, qd, kd, gl)


def scan_bwd(name, states, u, w, qk, qd, kd, gl, do):
    s, dnw = u.shape
    nc = s // CHUNK
    cw = (CHUNK, dnw)

    def body(st_r, u_r, w_r, qk_r, qd_r, kd_r, gl_r, do_r, du_r, dw_r, dqk_r, dqd_r, dkd_r, dgl_r, dstate):
        @pl.when(pl.program_id(0) == 0)
        def _():
            dstate[...] = jnp.zeros_like(dstate)

        res = []
        for h in range(DN_HEADS):
            hs = slice(h * HEAD_DIM, (h + 1) * HEAD_DIM)
            r = vjp_of(scan_step, 7, tuple(range(7)))(
                st_r[hs, :], u_r[:, hs], w_r[:, hs], qk_r[h], qd_r[:, hs], kd_r[:, hs], gl_r[:, hs],
                dstate[hs, :], do_r[:, hs])
            dstate[hs, :] = r[0]
            res.append(r)
        cat = lambda i: jnp.concatenate([r[i] for r in res], axis=1)
        du_r[...] = cat(1)
        dw_r[...] = cat(2)
        dqk_r[...] = jnp.stack([r[3] for r in res], axis=0)
        dqd_r[...] = cat(4)
        dkd_r[...] = cat(5)
        dgl_r[...] = cat(6)

    row = lambda n: (nc - 1 - n, 0)
    qk_spec = pl.BlockSpec((DN_HEADS, CHUNK, CHUNK), lambda n: (0, nc - 1 - n, 0))
    return pl.pallas_call(
        body, name=name, grid=(nc,),
        in_specs=[pl.BlockSpec((None, dnw, HEAD_DIM), lambda n: (nc - 1 - n, 0, 0)), pl.BlockSpec(cw, row), pl.BlockSpec(cw, row),
                  qk_spec, pl.BlockSpec(cw, row), pl.BlockSpec(cw, row), pl.BlockSpec((8, dnw), row), pl.BlockSpec(cw, row)],
        out_specs=[pl.BlockSpec(cw, row), pl.BlockSpec(cw, row), qk_spec, pl.BlockSpec(cw, row), pl.BlockSpec(cw, row),
                   pl.BlockSpec((8, dnw), row)],
        out_shape=[jax.ShapeDtypeStruct((s, dnw), F32), jax.ShapeDtypeStruct((s, dnw), F32),
                   jax.ShapeDtypeStruct((DN_HEADS, s, CHUNK), F32), jax.ShapeDtypeStruct((s, dnw), F32),
                   jax.ShapeDtypeStruct((s, dnw), F32), jax.ShapeDtypeStruct((nc * 8, dnw), F32)],
        scratch_shapes=[pltpu.VMEM((dnw, HEAD_DIM), F32)],
        compiler_params=_params(),
    )(states, u, w, qk, qd, kd, gl, do)


def mixer_bwd(tag, saved, wts, dout, dout_b, after=None):
    h, hn, qkv, z, ab, p, qkvn, g_bc, b_bc, u, w, qk, qd, kd, gl, tinv, o, states, y_dn, pooled, y_pool = saved
    s, d = h.shape
    dnw = DN_HEADS * HEAD_DIM
    pw_ = d - dnw
    ng = len(POOL_WINDOWS)
    gdim = pw_ // ng
    nc = s // CHUNK
    gr = {}
    d_ydn = mm2(tag + "_dydn", dout_b, wts["w_out_a"], 1, 1, after=after)
    d_ypool = mm2(tag + "_dypool", dout_b, wts["w_out_b"], 1, 1)
    gr["w_out_a"] = mm2(tag + "_dwout_a", y_dn, dout_b, 0, 0)
    gr["w_out_b"] = mm2(tag + "_dwout_b", y_pool, dout_b, 0, 0)
    tp = _tile(s, 512)
    d_pooled, gr["pool_w"], gr["pool_scale"] = tile_call(
        tag + "_dpoolmix", vjp_of(poolmix_fn, 3, (0, 1, 2)), (ng, s // tp),
        [(pooled, (tp, gdim), lambda gi, i: (i, gi)), (wts["pool_w"], (None, gdim, gdim), lambda gi, i: (gi, 0, 0)),
         (wts["pool_scale"], (1, gdim), lambda gi, i: (0, gi)), (d_ypool, (tp, gdim), lambda gi, i: (i, gi))],
        [((s, pw_), F32, (tp, gdim), lambda gi, i: (i, gi), None),
         ((ng, gdim, gdim), F32, (None, gdim, gdim), lambda gi, i: (gi, 0, 0), (1,)),
         ((1, pw_), F32, (1, gdim), lambda gi, i: (0, gi), (1,))])
    d_p = tile_call(tag + "_dpool", vjp_of(pool_fn, 1, (0,), True), (ng,),
                    [(p, (s, gdim), lambda i: (0, i)), (d_pooled, (s, gdim), lambda i: (0, i))],
                    [((s, pw_), BF16, (s, gdim), lambda i: (0, i), None)], with_pids=True)[0]
    hb = (ROW_TILE, HEAD_DIM)
    d_o, d_z, gr["dn_out_norm"] = tile_call(
        tag + "_doutgate", vjp_of(outgate_fn, 3, (0, 1, 2)), (DN_HEADS, s // ROW_TILE),
        [(o, hb, lambda hh, i: (i, hh)), (z, hb, lambda hh, i: (i, hh)), (wts["dn_out_norm"], (1, HEAD_DIM), lambda hh, i: (0, 0)),
         (d_ydn, hb, lambda hh, i: (i, hh))],
        [((s, dnw), F32, hb, lambda hh, i: (i, hh), None), ((s, dnw), BF16, hb, lambda hh, i: (i, hh), None),
         ((1, HEAD_DIM), F32, (1, HEAD_DIM), lambda hh, i: (0, 0), (0, 1))])
    du, dw, dqk, dqd, dkd, dgl = scan_bwd(tag + "_dscan", states, u, w, qk, qd, kd, gl, d_o)
    cw = (CHUNK, dnw)
    row = lambda n: (n, 0)
    d_qkvn, dg_bc, db_bc = tile_call(
        tag + "_dintra", intra_bwd_fn, (nc,),
        [(qkvn, cw, lambda n: (n, 0)), (qkvn, cw, lambda n: (n, 1)), (qkvn, cw, lambda n: (n, 2)),
         (g_bc, cw, row), (b_bc, cw, row), (tinv, (DN_HEADS, CHUNK, CHUNK), lambda n: (0, n, 0)), (du, cw, row), (dw, cw, row),
         (dqk, (DN_HEADS, CHUNK, CHUNK), lambda n: (0, n, 0)), (dqd, cw, row), (dkd, cw, row), (dgl, (8, dnw), row)],
        [((s, 3 * dnw), F32, (CHUNK, 3 * dnw), row, None)] + [((s, dnw), F32, cw, row, None)] * 2)
    d_ab, gr["a_log"], gr["dt_bias"] = rows_call(
        tag + "_dgates", lambda a, dg, db, al, dt: vjp_of(gateprep_fn, 3, (0, 1, 2))(a, al, dt, dg, db),
        [ab, dg_bc, db_bc], [wts["a_log"], wts["dt_bias"]], [(AB_PAD, BF16)], [(1, AB_PAD), (1, AB_PAD)])
    d_qkv, gr["conv_w"] = tile_call(
        tag + "_dconv", vjp_of(conv_fn, 2, (0, 1), True), (3 * DN_HEADS,),
        [(qkv, (s, HEAD_DIM), lambda i: (0, i)), (wts["conv_w"], (CONV_WIDTH, HEAD_DIM), lambda i: (0, i)),
         (d_qkvn, (s, HEAD_DIM), lambda i: (0, i))],
        [((s, 3 * dnw), BF16, (s, HEAD_DIM), lambda i: (0, i), None),
         ((CONV_WIDTH, 3 * dnw), F32, (CONV_WIDTH, HEAD_DIM), lambda i: (0, i), None)], with_pids=True)
    gr["w_qkv"] = mm2(tag + "_dwqkv", hn, d_qkv, 0, 0)
    gr["w_z"] = mm2(tag + "_dwz", hn, d_z, 0, 0)
    gr["w_ab"] = mm2(tag + "_dwab", hn, d_ab, 0, 0)
    gr["w_p"] = mm2(tag + "_dwp", hn, d_p, 0, 0)
    d_hn = mm2(tag + "_dhn1", d_qkv, wts["w_qkv"], 1, 1)
    d_hn = mm2(tag + "_dhn2", d_z, wts["w_z"], 1, 1, res=d_hn)
    d_hn = mm2(tag + "_dhn3", d_ab, wts["w_ab"], 1, 1, res=d_hn)
    d_hn = mm2(tag + "_dhn4", d_p, wts["w_p"], 1, 1, res=d_hn)
    dh, dh_b, gr["mix_norm"] = rms_bwd(tag + "_dnorm", h, wts["mix_norm"], d_hn, dout)
    return dh, dh_b, gr


def xattn_fwd(tag, h, memn, wts):
    s, d = h.shape
    m = memn.shape[0]
    dh_ = d // X_HEADS
    hn = rms_fwd(tag + "_norm", h, wts["xattn_norm"])
    q = mm2(tag + "_q", hn, wts["wq"], 1, 0, out_dtype=BF16)
    wkv = wts["wkv"]
    nj = wkv.shape[2]
    kv = mm_call(tag + "_kv", (1, N_SHARDS, 1), [(memn, (m, d), lambda i, j, k: (0, 0)), (wkv, (None, d, nj), lambda i, j, k: (j, 0, 0))],
                 [(0, 1, 1, 0, 0)], 1, (m, nj), [((m, 2 * d), BF16, (m, nj), lambda i, j, k: (0, j))], lambda accs: accs[0])[0]
    tq = _tile(s, 512)
    o = tile_call(
        tag + "_attn", attn_fn, (X_HEADS, s // tq),
        [(q, (tq, dh_), lambda hh, i: (i, hh)), (kv, (m, dh_), lambda hh, i: (0, hh)), (kv, (m, dh_), lambda hh, i: (0, X_HEADS + hh))],
        [((s, d), BF16, (tq, dh_), lambda hh, i: (i, hh), None)])[0]
    out = mm2(tag + "_o", o, wts["wo"], 1, 0, res=h)
    return out, (h, hn, q, kv, o)


def xattn_bwd(tag, saved, memn, mem, wts, dout, dout_b, after=None):
    h, hn, q, kv, o = saved
    s, d = h.shape
    m = memn.shape[0]
    dh_ = d // X_HEADS
    gr = {}
    d_o = mm2(tag + "_do", dout_b, wts["wo"], 1, 1, out_dtype=BF16, after=after)
    gr["wo"] = mm2(tag + "_dwo", o, dout_b, 0, 0)
    tq = _tile(s, 512)
    dq, dk, dv = tile_call(
        tag + "_dattn", vjp_of(attn_fn, 3, (0, 1, 2)), (X_HEADS, s // tq),
        [(q, (tq, dh_), lambda hh, i: (i, hh)), (kv, (m, dh_), lambda hh, i: (0, hh)), (kv, (m, dh_), lambda hh, i: (0, X_HEADS + hh)),
         (d_o, (tq, dh_), lambda hh, i: (i, hh))],
        [((s, d), BF16, (tq, dh_), lambda hh, i: (i, hh), None),
         ((m, d), F32, (m, dh_), lambda hh, i: (0, hh), (1,)), ((m, d), F32, (m, dh_), lambda hh, i: (0, hh), (1,))])
    dkv = jnp.concatenate([dk, dv], axis=1).astype(BF16)
    gr["wq"] = mm2(tag + "_dwq", hn, dq, 0, 0)
    wkv = wts["wkv"]
    nj = wkv.shape[2]
    td, tn = _tile(d, 1024), _tile(d, 1024)
    gr["wkv"] = mm_call(tag + "_dwkv", (d // td, N_SHARDS, 1),
                        [(memn, (m, td), lambda i, j, k: (0, i)), (dkv, (m, nj), lambda i, j, k: (0, j))], [(0, 1, 0, 0, 0)], 1, (td, nj),
                        [((N_SHARDS, d, nj), F32, (None, td, nj), lambda i, j, k: (j, i, 0))], lambda accs: accs[0])[0]
    d_memn = mm_call(tag + "_dmemn", (1, d // tn, N_SHARDS),
                     [(dkv, (m, nj), lambda i, j, k: (0, k)), (wkv, (None, tn, nj), lambda i, j, k: (k, j, 0))], [(0, 1, 1, 1, 0)], 1, (m, tn),
                     [((m, d), F32, (m, tn), lambda i, j, k: (0, j))], lambda accs: accs[0])[0]
    gr["mem_norm"] = rows_call(
        tag + "_dmemnorm", lambda a, dy, b: vjp_of(rms, 2, (1,))(a, b, dy)[0], [mem, d_memn], [wts["mem_norm"]], [],
        [wts["mem_norm"].shape], tr=128)[0]
    d_hn = mm2(tag + "_dhn", dq, wts["wq"], 1, 1)
    dh, dh_b, gr["xattn_norm"] = rms_bwd(tag + "_dnorm", h, wts["xattn_norm"], d_hn, dout)
    return dh, dh_b, gr


def final_loss(x, g, tgt):
    d = x.shape[1]

    def fn(a, t, b):
        def f(aa, bb):
            return 0.5 * jnp.sum(jnp.square(rms(aa, bb) - t)) / d
        loss, (dx, dg) = jax.value_and_grad(f, (0, 1))(a, b)
        lane = lax.broadcasted_iota(jnp.int32, (1, 128), 1)
        return dx, dx, dg, jnp.where(lane == 0, loss, 0.0)
    return rows_call("final_loss", fn, [x, tgt], [g], [(d, F32), (d, BF16)], [g.shape, (1, 128)])


SHARDED = ("ffn1_w_gate", "ffn1_w_up", "ffn1_w_down", "w_in", "conv_w", "pool_w", "w_out", "xattn_wq", "xattn_wkv",
           "xattn_wo", "ffn2_w_gate", "ffn2_w_up", "ffn2_w_down")
REPLICATED = ("ffn1_norm", "mix_norm", "a_log", "dt_bias", "dn_out_norm", "pool_scale", "xattn_norm", "mem_norm",
              "ffn2_norm", "final_norm")
WEIGHTS = ("ffn1_norm", "ffn1_w_gate", "ffn1_w_up", "ffn1_w_down", "mix_norm", "w_in", "conv_w", "a_log", "dt_bias",
           "dn_out_norm", "pool_w", "pool_scale", "w_out", "xattn_norm", "mem_norm", "xattn_wq", "xattn_wkv", "xattn_wo",
           "ffn2_norm", "ffn2_w_gate", "ffn2_w_up", "ffn2_w_down", "final_norm")


def _lane_pad(v, width=128):
    return jnp.pad(v, (0, width - v.shape[0]))[None, :]


GROUPS = {"ffn1": ("ffn1_w_gate", "ffn1_w_up"), "ffn1d": ("ffn1_w_down",), "mixer": ("w_in", "conv_w", "pool_w", "w_out"),
          "xattn": ("xattn_wq", "xattn_wkv", "xattn_wo"), "ffn2": ("ffn2_w_gate", "ffn2_w_up", "ffn2_w_down")}


def _cols(g):
    return jnp.transpose(g, (1, 0, 2)).reshape(g.shape[1], -1)


def _rows(g):
    return g.reshape(-1, g.shape[2])


def _rep_row(rep, name, l):
    return rep[name][l][None, :].astype(F32)


def mixer_weights(g, rep, l):
    dnw = DN_HEADS * HEAD_DIM
    w_in = _cols(g["w_in"])
    o_ab = 4 * dnw
    w_out = _rows(g["w_out"])
    gdim = g["pool_w"].shape[-1]
    pw = jnp.transpose(g["pool_w"].reshape(N_SHARDS, len(POOL_WINDOWS), gdim // N_SHARDS, gdim), (1, 0, 2, 3))
    return dict(
        mix_norm=_rep_row(rep, "mix_norm", l), w_qkv=w_in[:, :3 * dnw], w_z=w_in[:, 3 * dnw:o_ab],
        w_ab=jnp.pad(w_in[:, o_ab:o_ab + 2 * DN_HEADS], ((0, 0), (0, AB_PAD - 2 * DN_HEADS))),
        w_p=w_in[:, o_ab + 2 * DN_HEADS:], conv_w=_cols(g["conv_w"].reshape(N_SHARDS, CONV_WIDTH, -1)).astype(F32),
        a_log=_lane_pad(rep["a_log"][l].astype(F32)), dt_bias=_lane_pad(rep["dt_bias"][l].astype(F32)),
        dn_out_norm=_rep_row(rep, "dn_out_norm", l), pool_w=pw.reshape(len(POOL_WINDOWS), gdim, gdim),
        pool_scale=_rep_row(rep, "pool_scale", l), w_out_a=w_out[:dnw], w_out_b=w_out[dnw:])


def xattn_weights(g, rep, l):
    return dict(xattn_norm=_rep_row(rep, "xattn_norm", l), mem_norm=_rep_row(rep, "mem_norm", l), wq=_rows(g["xattn_wq"]),
                wkv=g["xattn_wkv"], wo=_rows(g["xattn_wo"]))


def _col_shards(g):
    k, n = g.shape
    return jnp.transpose(g.reshape(k, N_SHARDS, n // N_SHARDS), (1, 0, 2))


MIN_COMM_ROWS = 32


def _comm_rows(a):
    if a.shape[-2] >= MIN_COMM_ROWS:
        return a
    return a.reshape(a.shape[:-2] + (MIN_COMM_ROWS, -1))


GRAD_PARTS = (("ffn2_w_gate", "ffn2_w_up", "ffn2_w_down", "xattn_wq", "xattn_wkv", "xattn_wo"),
              ("w_in", "conv_w", "pool_w", "w_out"), ("ffn1_w_gate", "ffn1_w_up", "ffn1_w_down"))


def model_grads(x, mem, tgt, fetch, rep, grads_done=None):
    s, d = x.shape
    depth = rep["ffn1_norm"].shape[0]
    saved, wl = [], []
    h = x
    for l in range(depth):
        t = "l%d" % l
        f1 = dict(fetch(l, "ffn1", h))
        h, s1, f1["ffn1_w_down"] = ffn_fwd(t + "_ffn1", h, _rep_row(rep, "ffn1_norm", l), f1["ffn1_w_gate"], f1["ffn1_w_up"],
                                           lambda act: fetch(l, "ffn1d", act)["ffn1_w_down"])
        wm = mixer_weights(fetch(l, "mixer", h), rep, l)
        h, s2 = mixer_fwd(t + "_mix", h, wm)
        wx = xattn_weights(fetch(l, "xattn", h), rep, l)
        memn = rms_fwd(t + "_memnorm", mem, wx["mem_norm"])
        h, s3 = xattn_fwd(t + "_xattn", h, memn, wx)
        f2 = fetch(l, "ffn2", h)
        h, s4, _ = ffn_fwd(t + "_ffn2", h, _rep_row(rep, "ffn2_norm", l), f2["ffn2_w_gate"], f2["ffn2_w_up"], f2["ffn2_w_down"])
        saved.append((memn, s1, s2, s3, s4))
        wl.append((f1, wm, wx, f2))
    dh, dh_b, d_final, loss_row = final_loss(h, rep["final_norm"][None, :].astype(F32), tgt)
    big, small = [None] * depth, [None] * depth
    after = None
    done = (lambda l, part, gb: None) if grads_done is None else grads_done
    for l in reversed(range(depth)):
        f1, wm, wx, f2 = wl[l]
        t = "l%d" % l
        memn, s1, s2, s3, s4 = saved[l]
        gb, gs = {}, {}
        dh, dh_b, gs["ffn2_norm"], gb["ffn2_w_gate"], gb["ffn2_w_up"], gb["ffn2_w_down"] = ffn_bwd(
            t + "_ffn2", s4, _rep_row(rep, "ffn2_norm", l), f2["ffn2_w_gate"], f2["ffn2_w_up"], f2["ffn2_w_down"], dh, dh_b, after)
        dh, dh_b, gx = xattn_bwd(t + "_xattn", s3, memn, mem, wx, dh, dh_b)
        gb["xattn_wq"] = gx["wq"].reshape(N_SHARDS, d // N_SHARDS, d)
        gb["xattn_wo"] = gx["wo"].reshape(N_SHARDS, d // N_SHARDS, d)
        gb["xattn_wkv"] = gx["wkv"]
        after = done(l, 0, gb)
        dh, dh_b, gm = mixer_bwd(t + "_mix", s2, wm, dh, dh_b, after)
        gb["w_in"] = _col_shards(jnp.concatenate([gm["w_qkv"], gm["w_z"], gm["w_ab"][:, :2 * DN_HEADS], gm["w_p"]], axis=1))
        gb["conv_w"] = _comm_rows(_col_shards(gm["conv_w"]))
        gdim = gm["pool_w"].shape[-1]
        gb["pool_w"] = jnp.transpose(gm["pool_w"].reshape(len(POOL_WINDOWS), N_SHARDS, gdim // N_SHARDS, gdim),
                                     (1, 0, 2, 3)).reshape(N_SHARDS, gdim, gdim)
        gb["w_out"] = jnp.concatenate([gm["w_out_a"], gm["w_out_b"]], axis=0).reshape(N_SHARDS, d // N_SHARDS, d)
        after = done(l, 1, gb)
        dh, dh_b, gs["ffn1_norm"], gb["ffn1_w_gate"], gb["ffn1_w_up"], gb["ffn1_w_down"] = ffn_bwd(
            t + "_ffn1", s1, _rep_row(rep, "ffn1_norm", l), f1["ffn1_w_gate"], f1["ffn1_w_up"], f1["ffn1_w_down"], dh, dh_b, after)
        after = done(l, 2, gb)
        for n in ("xattn_norm", "mem_norm"):
            gs[n] = gx[n]
        for n in ("mix_norm", "a_log", "dt_bias", "dn_out_norm", "pool_scale"):
            gs[n] = gm[n]
        big[l], small[l] = gb, gs
    return loss_row, dh, big, small, d_final


HBM = pl.BlockSpec(memory_space=pltpu.HBM)


def _place():
    x, y, c = lax.axis_index("x"), lax.axis_index("y"), lax.axis_index("c")
    chips = [(1 - x, y), (x, 1 - y), (1 - x, 1 - y)]
    return x, y, c, 2 * x + y, chips, [2 * px + py for px, py in chips]


SEM = pl.BlockSpec(memory_space=pltpu.SEMAPHORE)
SPLIT_COPY = pltpu.CompilerParams(has_side_effects=pltpu.SideEffectType.DATAFLOW_SIDE_EFFECTING)


def _half(ref, lead, c):
    r = ref.shape[1] // 2
    return ref.at[lead, pl.ds(c * r, r)]


def place_shards(name, w, where, dtype):
    nl, r, cdim = w.shape
    tr = _rtile(r)
    out = ((N_SHARDS, r, cdim), dtype, (None, tr, cdim), lambda i, p: (p[0], i, 0), None)
    return tile_call(name, lambda a: tuple(a[l] for l in range(nl)), (r // tr,),
                     [(w, (nl, tr, cdim), lambda i, p: (0, i, 0))], [out] * nl, prefetch=where)


def _gather_copies(bufs, ssem, rsem):
    x, y, c, j_own, chips, js = _place()
    res = []
    for i, b in enumerate(bufs):
        for k, (px, py) in enumerate(chips):
            mk = lambda slot: pltpu.make_async_remote_copy(
                src_ref=_half(b, slot, c), dst_ref=_half(b, slot, c), send_sem=ssem.at[3 * i + k],
                recv_sem=rsem.at[3 * i + k], device_id=(px, py, c), device_id_type=MESH)
            res.append((mk(j_own), mk(js[k])))
    return res


def gather_start(bufs, groups):
    n, ng = len(bufs), len(groups)

    def body(*refs):
        sems, outs, token = refs[n:n + 2 * ng], refs[n + 2 * ng:2 * n + 2 * ng], refs[2 * n + 2 * ng]
        for gi, group in enumerate(groups):
            for mine, _ in _gather_copies([outs[t] for t in group], sems[2 * gi], sems[2 * gi + 1]):
                mine.start()
        token[...] = jnp.zeros_like(token)

    sem_shapes = []
    for group in groups:
        sem_shapes += [pltpu.SemaphoreType.DMA((3 * len(group),))] * 2
    res = pl.pallas_call(
        body, name="gather_start", in_specs=[HBM] * n,
        out_specs=[SEM] * (2 * ng) + [HBM] * n + [pl.BlockSpec(memory_space=pltpu.VMEM)],
        out_shape=sem_shapes + [pltpu.HBM(a.shape, a.dtype) for a in bufs] + [jax.ShapeDtypeStruct((8, 128), F32)],
        input_output_aliases={t: 2 * ng + t for t in range(n)}, compiler_params=SPLIT_COPY,
    )(*[pltpu.with_memory_space_constraint(a, pltpu.HBM) for a in bufs])
    sems = [(res[2 * gi], res[2 * gi + 1]) for gi in range(ng)]
    return sems, res[2 * ng:2 * ng + n], res[-1]


def gather_wait(name, bufs, sems, after):
    n = len(bufs)

    def body(*refs):
        ins, ssem, rsem = refs[:n], refs[n], refs[n + 1]
        for mine, theirs in _gather_copies(ins, ssem, rsem):
            mine.wait_send()
            theirs.wait_recv()

    return pl.pallas_call(
        body, name=name, in_specs=[HBM] * n + [SEM, SEM, pl.BlockSpec(memory_space=pl.ANY)], out_specs=[HBM] * n,
        out_shape=[pltpu.HBM(a.shape, a.dtype) for a in bufs],
        input_output_aliases={t: t for t in range(n)}, compiler_params=SPLIT_COPY,
    )(*bufs, sems[0], sems[1], after)


def gather_swap(name, bufs):
    n = len(bufs)

    def body(*refs):
        outs = refs[n:2 * n]
        ssem, rsem = refs[2 * n:]
        x, y, c, j_own, chips, js = _place()

        def copy(i, k, half):
            blk = _half(outs[i], js[k], half)
            return pltpu.make_async_remote_copy(src_ref=blk, dst_ref=blk, send_sem=ssem.at[3 * i + k],
                                                recv_sem=rsem.at[3 * i + k], device_id=(x, y, 1 - c), device_id_type=MESH)

        for i in range(n):
            for k in range(3):
                copy(i, k, c).start()
        for i in range(n):
            for k in range(3):
                copy(i, k, 1 - c).wait_recv()
                copy(i, k, c).wait_send()

    return pl.pallas_call(
        body, name=name, in_specs=[HBM] * n, out_specs=[HBM] * n, input_output_aliases={t: t for t in range(n)},
        out_shape=[jax.ShapeDtypeStruct(a.shape, a.dtype) for a in bufs],
        scratch_shapes=[pltpu.SemaphoreType.DMA((3 * n,)), pltpu.SemaphoreType.DMA((3 * n,))],
    )(*bufs)


def _pair_copies(grads, lands, ssem, rsem):
    x, y, c, _, _, _ = _place()
    res = []
    for t in range(len(grads)):
        r = grads[t].shape[1] // 2
        res.append(pltpu.make_async_remote_copy(src_ref=grads[t].at[:, pl.ds((1 - c) * r, r)], dst_ref=lands[t], send_sem=ssem.at[t],
                                                recv_sem=rsem.at[t], device_id=(x, y, 1 - c), device_id_type=MESH))
    return res


def _split_start(name, copies, arrays, n_sems, after=None):
    n = len(arrays)
    n_dep = 0 if after is None else 1

    def body(*refs):
        refs = refs[n + n_dep:]
        for cp in copies(refs[2:n + 2], refs[0], refs[1]):
            cp.start()
        refs[n + 2][...] = jnp.zeros((8, 128), F32)

    res = pl.pallas_call(
        body, name=name, in_specs=[HBM] * n + [pl.BlockSpec(memory_space=pl.ANY)] * n_dep,
        out_specs=[SEM, SEM] + [HBM] * n + [pl.BlockSpec(memory_space=pltpu.VMEM)],
        out_shape=[pltpu.SemaphoreType.DMA((n_sems,))] * 2 + [pltpu.HBM(a.shape, a.dtype) for a in arrays]
        + [jax.ShapeDtypeStruct((8, 128), F32)],
        input_output_aliases={t: 2 + t for t in range(n)}, compiler_params=SPLIT_COPY,
    )(*[pltpu.with_memory_space_constraint(a, pltpu.HBM) for a in arrays], *([] if after is None else [after]))
    return res[0], res[1], res[2:2 + n], res[-1]


def _split_wait(name, copies, ssem, rsem, arrays, after):
    n = len(arrays)

    def body(*refs):
        for cp in copies(refs[:n], refs[n], refs[n + 1]):
            cp.wait_send()
            cp.wait_recv()

    return pl.pallas_call(
        body, name=name, in_specs=[HBM] * n + [SEM, SEM, pl.BlockSpec(memory_space=pl.ANY)], out_specs=[HBM] * n,
        out_shape=[pltpu.HBM(a.shape, a.dtype) for a in arrays],
        input_output_aliases={t: t for t in range(n)}, compiler_params=SPLIT_COPY,
    )(*arrays, ssem, rsem, after)


def pair_start(name, grads):
    n = len(grads)
    lands = [lax.empty((a.shape[0], a.shape[1] // 2, a.shape[2]), a.dtype) for a in grads]
    ssem, rsem, arrays, token = _split_start(name, lambda a, s, r: _pair_copies(a[:n], a[n:], s, r), list(grads) + lands, n)
    return ssem, rsem, arrays[:n], arrays[n:], token


def pair_wait(name, ssem, rsem, grads, lands, after):
    n = len(grads)
    arrays = _split_wait(name, lambda a, s, r: _pair_copies(a[:n], a[n:], s, r), ssem, rsem, list(grads) + list(lands), after)
    return arrays[:n], arrays[n:]


def _reduce_copies(parts, lands, ssem, rsem):
    x, y, c, j_own, chips, js = _place()
    return [pltpu.make_async_remote_copy(src_ref=parts[t].at[js[k]], dst_ref=lands[t].at[k], send_sem=ssem.at[3 * t + k],
                                         recv_sem=rsem.at[3 * t + k], device_id=(px, py, c), device_id_type=MESH)
            for t in range(len(parts)) for k, (px, py) in enumerate(chips)]


def reduce_start(name, parts, after):
    n = len(parts)
    lands = [lax.empty((3,) + a.shape[1:], a.dtype) for a in parts]
    ssem, rsem, arrays, token = _split_start(name, lambda a, s, r: _reduce_copies(a[:n], a[n:], s, r), list(parts) + lands,
                                             3 * n, after)
    return ssem, rsem, arrays[:n], arrays[n:], token


def reduce_wait(name, ssem, rsem, parts, lands, after):
    n = len(parts)
    arrays = _split_wait(name, lambda a, s, r: _reduce_copies(a[:n], a[n:], s, r), ssem, rsem, list(parts) + list(lands), after)
    return arrays[:n], arrays[n:]


def share_halves(name, bufs):
    n = len(bufs)

    def body(*refs):
        outs = refs[n:2 * n]
        ssem, rsem = refs[2 * n:]
        x, y, c, _, _, _ = _place()

        def copy(t, l, half):
            blk = _half(outs[t], l, half)
            return pltpu.make_async_remote_copy(src_ref=blk, dst_ref=blk, send_sem=ssem.at[2 * t + l],
                                                recv_sem=rsem.at[2 * t + l], device_id=(x, y, 1 - c), device_id_type=MESH)

        for t in range(n):
            for l in range(2):
                copy(t, l, c).start()
        for t in range(n):
            for l in range(2):
                copy(t, l, 1 - c).wait_recv()
                copy(t, l, c).wait_send()

    return pl.pallas_call(
        body, name=name, in_specs=[HBM] * n, out_specs=[HBM] * n, input_output_aliases={t: t for t in range(n)},
        out_shape=[jax.ShapeDtypeStruct(a.shape, a.dtype) for a in bufs],
        scratch_shapes=[pltpu.SemaphoreType.DMA((2 * n,)), pltpu.SemaphoreType.DMA((2 * n,))],
    )(*bufs)


def allreduce_small(buf, after):
    r = buf.shape[0]
    n_dev = 8

    def body(in_ref, _, out_ref, gath, ssem, rsem):
        x, y, c = lax.axis_index("x"), lax.axis_index("y"), lax.axis_index("c")
        flip = lambda v, bit: 1 - v if bit else v
        me = 4 * x + 2 * y + c
        gath[me] = in_ref[...]
        peers = [(flip(x, k >> 2 & 1), flip(y, k >> 1 & 1), flip(c, k & 1)) for k in range(1, n_dev)]
        sends = []
        for k, peer in enumerate(peers):
            cp = pltpu.make_async_remote_copy(src_ref=in_ref, dst_ref=gath.at[me], send_sem=ssem.at[k], recv_sem=rsem.at[k],
                                              device_id=peer, device_id_type=MESH)
            cp.start()
            sends.append(cp)
        for k, (px, py, pc) in enumerate(peers):
            pltpu.make_async_remote_copy(src_ref=in_ref, dst_ref=gath.at[4 * px + 2 * py + pc], send_sem=ssem.at[k],
                                         recv_sem=rsem.at[k], device_id=(px, py, pc), device_id_type=MESH).wait_recv()
        for cp in sends:
            cp.wait_send()
        acc = gath[0]
        for i in range(1, n_dev):
            acc = acc + gath[i]
        out_ref[...] = acc

    return pl.pallas_call(
        body, name="allreduce_small",
        in_specs=[pl.BlockSpec(memory_space=pltpu.VMEM), pl.BlockSpec(memory_space=pl.ANY)],
        out_specs=pl.BlockSpec(memory_space=pltpu.VMEM), out_shape=jax.ShapeDtypeStruct(buf.shape, F32),
        scratch_shapes=[pltpu.VMEM((n_dev, r, 128), F32), pltpu.SemaphoreType.DMA((n_dev - 1,)), pltpu.SemaphoreType.DMA((n_dev - 1,))],
    )(buf, after)


def _rtile(r, pref=256):
    return _tile(r, pref, 16)


def chip_partial(name, grad, recv, where):
    _, rh, cdim = recv.shape
    tr = _rtile(rh, 512)
    nt = rh // tr
    return tile_call(name, lambda a, b: a + b, (N_SHARDS, nt),
                     [(grad, (None, tr, cdim), lambda j, i, p: (j, p[1] * nt + i, 0)), (recv, (None, tr, cdim), lambda j, i, p: (j, i, 0))],
                     [(recv.shape, BF16, (None, tr, cdim), lambda j, i, p: (j, i, 0), None)], prefetch=where)[0]


def sum_chips(name, parts, lands, where, layer, n_layers, into):
    _, rh, cdim = parts.shape
    tr = _rtile(rh, 512)
    nt = rh // tr
    up = lambda a: a.astype(F32)
    return tile_call(name, lambda own, rv: (up(own) + up(rv[0])) + (up(rv[1]) + up(rv[2])), (nt,),
                     [(parts, (None, tr, cdim), lambda i, p: (p[0], i, 0)), (lands, (3, tr, cdim), lambda i, p: (0, i, 0))],
                     [((n_layers, 2 * rh, cdim), F32, (None, tr, cdim), lambda i, p: (layer, p[1] * nt + i, 0), None)],
                     prefetch=where, into=into)[0]


def adamw_fn(w, g, m, v):
    m = ADAM_B1 * m + (1.0 - ADAM_B1) * g
    v = ADAM_B2 * v + (1.0 - ADAM_B2) * jnp.square(g)
    m_hat = m / (1.0 - ADAM_B1 ** ADAM_STEP)
    v_hat = v / (1.0 - ADAM_B2 ** ADAM_STEP)
    delta = -ADAM_LR * (m_hat / (jnp.sqrt(v_hat) + ADAM_EPS) + ADAM_WD * w)
    return delta, m, v, g


def adamw(name, w, g, m, v):
    nl, r, cdim = w.shape
    tr = _rtile(r, 256)
    spec = ((None, tr, cdim), lambda l, i: (l, i, 0))
    return tile_call(name, adamw_fn, (nl, r // tr), [(a,) + spec for a in (w, g, m, v)],
                     [(w.shape, F32) + spec + (None,)] * 4)


def _as3(a):
    return _comm_rows(a.reshape(a.shape[0], -1, a.shape[-1]))


COMM_GROUPS = ((0, ("ffn1",)), (0, ("ffn1d",)), (0, ("mixer",)), (0, ("xattn", "ffn2")), (1, ("ffn1", "ffn1d", "mixer")),
               (1, ("xattn", "ffn2")))


def _pack_rows(vals):
    rows = []
    for v in vals:
        v = v.reshape(-1).astype(F32)
        pad = (-v.shape[0]) % 128
        rows.append(jnp.pad(v, (0, pad)).reshape(-1, 128))
    out = jnp.concatenate(rows, axis=0)
    return jnp.pad(out, ((0, (-out.shape[0]) % 8), (0, 0)))


def _unpack_rows(buf, like):
    outs, r = [], 0
    for a in like:
        n = a.size
        nr = -(-n // 128)
        outs.append(buf[r:r + nr].reshape(-1)[:n].reshape(a.shape))
        r += nr
    return outs


def kernel(x, mem, ffn1_norm, ffn1_w_gate, ffn1_w_up, ffn1_w_down, mix_norm, w_in, conv_w, a_log, dt_bias, dn_out_norm, pool_w, pool_scale, w_out, xattn_norm, mem_norm, xattn_wq, xattn_wkv, xattn_wo, ffn2_norm, ffn2_w_gate, ffn2_w_up, ffn2_w_down, final_norm, loss_target, m_ffn1_norm, m_ffn1_w_gate, m_ffn1_w_up, m_ffn1_w_down, m_mix_norm, m_w_in, m_conv_w, m_a_log, m_dt_bias, m_dn_out_norm, m_pool_w, m_pool_scale, m_w_out, m_xattn_norm, m_mem_norm, m_xattn_wq, m_xattn_wkv, m_xattn_wo, m_ffn2_norm, m_ffn2_w_gate, m_ffn2_w_up, m_ffn2_w_down, m_final_norm, v_ffn1_norm, v_ffn1_w_gate, v_ffn1_w_up, v_ffn1_w_down, v_mix_norm, v_w_in, v_conv_w, v_a_log, v_dt_bias, v_dn_out_norm, v_pool_w, v_pool_scale, v_w_out, v_xattn_norm, v_mem_norm, v_xattn_wq, v_xattn_wkv, v_xattn_wo, v_ffn2_norm, v_ffn2_w_gate, v_ffn2_w_up, v_ffn2_w_down, v_final_norm):
    given = dict(locals())
    w = {n: given[n] for n in WEIGHTS}
    m = {n: given["m_" + n] for n in WEIGHTS}
    v = {n: given["v_" + n] for n in WEIGHTS}
    where = jnp.stack([2 * lax.axis_index("x") + lax.axis_index("y"), lax.axis_index("c")]).astype(jnp.int32)
    placed = {}
    for n in SHARDED:
        for l, buf in enumerate(place_shards("place_" + n, _as3(w[n]), where, F32 if n == "conv_w" else BF16)):
            placed[(l, n)] = buf
    keys, groups = [], []
    for l, entries in COMM_GROUPS:
        groups.append([])
        for e in entries:
            for n in GROUPS[e]:
                groups[-1].append(len(keys))
                keys.append((l, n))
    sems, bufs, _ = gather_start([placed[k] for k in keys], groups)
    fetched = {}

    def fetch(l, entry, after):
        gi = [i for i, (gl, entries) in enumerate(COMM_GROUPS) if gl == l and entry in entries][0]
        if gi not in fetched:
            landed = gather_wait("gather_wait%d" % gi, [bufs[i] for i in groups[gi]], sems[gi], after)
            fetched[gi] = {keys[i][1]: a for i, a in zip(groups[gi], gather_swap("gather_swap%d" % gi, landed))}
        return fetched[gi]

    swapping, travelling = [], []

    def to_ici(after):
        tag, names, (ssem, rsem, mine, theirs) = swapping.pop(0)
        mine, theirs = pair_wait("pair_wait" + tag, ssem, rsem, mine, theirs, after)
        parts = [chip_partial("partial%s_%s" % (tag, n), g, r, where) for n, g, r in zip(names, mine, theirs)]
        ssem, rsem, parts, lands, token = reduce_start("reduce_start" + tag, parts, after)
        travelling.append((tag, names, (ssem, rsem, parts, lands)))
        return token

    def grads_done(l, part, gb):
        tag, names = "%d%d" % (l, part), GRAD_PARTS[part]
        ssem, rsem, mine, theirs, token = pair_start("pair_start" + tag, [gb[n] for n in names])
        swapping.append((tag, names, (ssem, rsem, mine, theirs)))
        return to_ici(token) if len(swapping) > 1 else token

    rep = {n: w[n] for n in REPLICATED}
    loss_row, dx, big, small, d_final = model_grads(x[0], mem[0], loss_target[0], fetch, rep, grads_done)
    last_start = to_ici(dx)
    n_layers = len(big)
    sums, out_g, out_d, out_m, out_v = {}, {}, {}, {}, {}

    def land(after):
        tag, names, (ssem, rsem, parts, lands) = travelling.pop(0)
        parts, lands = reduce_wait("reduce_wait" + tag, ssem, rsem, parts, lands, after)
        for n, p, r in zip(names, parts, lands):
            sums[n] = sum_chips("sum%s_%s" % (tag, n), p, r, where, int(tag[0]), n_layers, sums.get(n))

    def finish(part):
        names = GRAD_PARTS[part]
        for n, g in zip(names, share_halves("share_halves%d" % part, [sums[n] for n in names])):
            d_, m_, v_, g_ = adamw("adamw_" + n, _as3(w[n]), g, _as3(m[n]), _as3(v[n]))
            out_g[n], out_d[n], out_m[n], out_v[n] = (a.reshape(w[n].shape) for a in (g_, d_, m_, v_))
        return out_d[names[-1]]

    for _ in range(len(travelling) - 2):
        land(last_start)
    land(finish(0))
    land(finish(1))
    done = finish(2)
    rep_names = [n for n in REPLICATED if n != "final_norm"]
    g_rep = {n: jnp.stack([small[l][n][0, :w[n].shape[1]] for l in range(len(small))]) for n in rep_names}
    g_rep["final_norm"] = d_final[0]
    like = [w[n] for n in REPLICATED] + [jnp.zeros((1,), F32)]
    summed = allreduce_small(_pack_rows([g_rep[n] for n in REPLICATED] + [loss_row[0, :1]]), done)
    pk = lambda tree: _pack_rows([tree[n] for n in REPLICATED] + [jnp.zeros((1,), F32)])
    wp, mp, vp = pk(w), pk(m), pk(v)
    dp, mp2, vp2, _ = adamw("adamw_small", wp[None], summed[None], mp[None], vp[None])
    for buf, dst in ((summed, out_g), (dp[0], out_d), (mp2[0], out_m), (vp2[0], out_v)):
        for n, a in zip(REPLICATED, _unpack_rows(buf, like)):
            dst[n] = a
    loss = _unpack_rows(summed, like)[-1][0]
    return (loss, dx[None], *[out_g[n] for n in WEIGHTS], *[out_d[n] for n in WEIGHTS],
            *[out_m[n] for n in WEIGHTS], *[out_v[n] for n in WEIGHTS])
```

```python
import functools

import jax
import jax.numpy as jnp
from jax import lax
from jax.experimental import pallas as pl
from jax.experimental.pallas import tpu as pltpu

F32, BF16 = jnp.float32, jnp.bfloat16
HI = lax.Precision.HIGHEST
MESH = pl.DeviceIdType.MESH

EPS = 1e-6
DN_HEADS = 8
HEAD_DIM = 128
X_HEADS = 4
POOL_WINDOWS = (2, 4, 8, 16)
CONV_WIDTH = 4
CHUNK = 64
N_SHARDS = 4
AB_PAD = 128
ADAM_LR, ADAM_B1, ADAM_B2, ADAM_EPS, ADAM_WD, ADAM_STEP = 0.001, 0.9, 0.999, 1e-08, 0.01, 10
VMEM_LIMIT = 56 << 20
ROW_TILE = 512


def _tile(n, pref, unit=128):
    best = None
    for t in range(unit, min(n, pref) + 1, unit):
        if n % t == 0:
            best = t
    return best if best is not None else n


def _params():
    return pltpu.CompilerParams(vmem_limit_bytes=VMEM_LIMIT)


def _split(a):
    a = a.astype(F32)
    head = a.astype(BF16)
    return head, (a - head.astype(F32)).astype(BF16)


def _dg(a, b, ca, cb, hi):
    dims = (((ca,), (cb,)), ((), ()))
    dot = lambda u, v: lax.dot_general(u, v, dims, preferred_element_type=F32)
    if hi:
        (a0, a1), (b0, b1) = _split(a), _split(b)
        return dot(a0, b0) + (dot(a0, b1) + dot(a1, b0))
    return dot(a.astype(BF16), b.astype(BF16))


@functools.partial(jax.custom_vjp, nondiff_argnums=(2, 3, 4))
def mmul(a, b, ca, cb, hi):
    return _dg(a, b, ca, cb, hi)


def _mmul_fwd(a, b, ca, cb, hi):
    return _dg(a, b, ca, cb, hi), (a, b)


def _mmul_bwd(ca, cb, hi, res, g):
    a, b = res
    da = _dg(g, b, 1, 1 - cb, hi) if ca == 1 else _dg(b, g, 1 - cb, 1, hi)
    db = _dg(a, g, 1 - ca, 0, hi) if cb == 0 else _dg(g, a, 0, 1 - ca, hi)
    return da.astype(a.dtype), db.astype(b.dtype)


mmul.defvjp(_mmul_fwd, _mmul_bwd)


@functools.partial(jax.custom_vjp, nondiff_argnums=(1,))
def shift_down(x, s):
    t = lax.broadcasted_iota(jnp.int32, x.shape, 0)
    return jnp.where(t >= s, pltpu.roll(x, s, 0), 0.0)


def _shift_up(x, s):
    n = x.shape[0]
    t = lax.broadcasted_iota(jnp.int32, x.shape, 0)
    return jnp.where(t < n - s, pltpu.roll(x, n - s, 0), 0.0)


shift_down.defvjp(lambda x, s: (shift_down(x, s), None), lambda s, _, g: (_shift_up(g, s),))


def sigmoid(x):
    return 0.5 * (jnp.tanh(0.5 * x) + 1.0)


def silu(x):
    return x * sigmoid(x)


@jax.custom_vjp
def softplus(x):
    u = jnp.exp(-jnp.abs(x))
    w = 1.0 + u
    log1p = jnp.where(w == 1.0, u, jnp.log(w) * u / jnp.where(w == 1.0, 1.0, w - 1.0))
    return jnp.maximum(x, 0.0) + log1p


softplus.defvjp(lambda x: (softplus(x), x), lambda x, g: (g * sigmoid(x),))


def rms(x, g):
    x = x.astype(F32)
    return x * lax.rsqrt(jnp.mean(x * x, axis=-1, keepdims=True) + EPS) * g


def swiglu(gate, up):
    return silu(gate) * up


def vjp_of(fn, n_in, diff, has_pids=False):
    def g(*args):
        pids = None
        if has_pids:
            pids, args = args[0], args[1:]
        ins, cots = list(args[:n_in]), args[n_in:]

        def f(*d):
            full = list(ins)
            for i, v in zip(diff, d):
                full[i] = v
            return fn(pids, *full) if has_pids else fn(*full)

        out, pull = jax.vjp(f, *[ins[i].astype(F32) for i in diff])
        if isinstance(out, (tuple, list)):
            return pull(tuple(c.astype(o.dtype) for c, o in zip(cots, out)))
        return pull(cots[0].astype(out.dtype))
    return g


def tile_call(name, fn, grid, ins, outs, with_pids=False, prefetch=None, into=None):
    n_in = len(ins)
    n_into = 0 if into is None else 1

    def body(*refs):
        if prefetch is not None:
            refs = refs[1:]
        pids = tuple(pl.program_id(a) for a in range(len(grid)))
        vals = [r[...] for r in refs[:n_in]]
        res = fn(pids, *vals) if with_pids else fn(*vals)
        if not isinstance(res, (tuple, list)):
            res = (res,)
        for r, o, spec in zip(res, refs[n_in + n_into:], outs):
            acc = spec[4]
            if acc is None:
                o[...] = r.astype(o.dtype)
            else:
                first = functools.reduce(jnp.logical_and, [pids[a] == 0 for a in acc])

                @pl.when(first)
                def _():
                    o[...] = r.astype(o.dtype)

                @pl.when(jnp.logical_not(first))
                def _():
                    o[...] += r.astype(o.dtype)

    in_specs = [pl.BlockSpec(b, im) for _, b, im in ins] + [pl.BlockSpec(memory_space=pl.ANY)] * n_into
    out_specs = [pl.BlockSpec(s[2], s[3]) for s in outs]
    out_shape = [jax.ShapeDtypeStruct(s[0], s[1]) for s in outs]
    args = [a for a, _, _ in ins] + ([] if into is None else [into])
    if prefetch is None:
        return pl.pallas_call(body, name=name, grid=grid, in_specs=in_specs, out_specs=out_specs, out_shape=out_shape,
                              input_output_aliases={n_in: 0} if n_into else {}, compiler_params=_params())(*args)
    spec = pltpu.PrefetchScalarGridSpec(num_scalar_prefetch=1, grid=grid, in_specs=in_specs, out_specs=out_specs)
    return pl.pallas_call(body, name=name, grid_spec=spec, out_shape=out_shape,
                          input_output_aliases={n_in + 1: 0} if n_into else {}, compiler_params=_params())(prefetch, *args)


def mm_call(name, grid, ins, pairs, n_acc, acc_shape, outs, epilogue, extras=(), after=None):
    n_in, n_ex, nk = len(ins), len(extras), grid[2]
    n_dep = 0 if after is None else 1

    def finish(accs, ex_refs, out_refs):
        res = epilogue(accs, *[r[...] for r in ex_refs])
        if not isinstance(res, (tuple, list)):
            res = (res,)
        for r, o in zip(res, out_refs):
            o[...] = r.astype(o.dtype)

    def body(*refs):
        in_refs, ex_refs = refs[:n_in], refs[n_in:n_in + n_ex]
        refs = refs[n_in + n_ex + n_dep:]
        out_refs, accs = refs[:len(outs)], refs[len(outs):]
        if nk == 1:
            vals = [None] * n_acc
            for ia, ib, ca, cb, ai in pairs:
                d = _dg(in_refs[ia][...], in_refs[ib][...], ca, cb, False)
                vals[ai] = d if vals[ai] is None else vals[ai] + d
            finish(vals, ex_refs, out_refs)
            return
        k = pl.program_id(2)

        @pl.when(k == 0)
        def _():
            for a in accs:
                a[...] = jnp.zeros_like(a)

        for ia, ib, ca, cb, ai in pairs:
            accs[ai][...] += _dg(in_refs[ia][...], in_refs[ib][...], ca, cb, False)

        @pl.when(k == nk - 1)
        def _():
            finish([a[...] for a in accs], ex_refs, out_refs)

    return pl.pallas_call(
        body, name=name, grid=grid,
        in_specs=[pl.BlockSpec(b, im) for _, b, im in list(ins) + list(extras)] + [pl.BlockSpec(memory_space=pl.ANY)] * n_dep,
        out_specs=[pl.BlockSpec(s[2], s[3]) for s in outs],
        out_shape=[jax.ShapeDtypeStruct(s[0], s[1]) for s in outs],
        scratch_shapes=[pltpu.VMEM(acc_shape, F32) for _ in range(n_acc if nk > 1 else 0)],
        compiler_params=_params(),
    )(*[a for a, _, _ in list(ins) + list(extras)], *([] if after is None else [after]))


MM_VMEM_BUDGET = 40 << 20


def _mm_tiles(m, n, kk, sa, sb, so, has_res):
    tk = _tile(kk, 2048)
    best = None
    for tm in (1024, 512, 256, 128):
        for tn in (1024, 512, 256, 128):
            tm_, tn_ = _tile(m, tm), _tile(n, tn)
            need = 2 * (tm_ * tk * sa + tk * tn_ * sb + tm_ * tn_ * so) + (tm_ * tn_ * 4 if tk < kk else 0)
            need += 2 * tm_ * tn_ * 4 if has_res else 0
            if need <= MM_VMEM_BUDGET and (best is None or tm_ * tn_ > best[0] * best[1]):
                best = (tm_, tn_)
    return best + (tk,)


def mm2(name, a, b, ca, cb, res=None, scale=None, out_dtype=F32, after=None):
    m, kk, n = a.shape[1 - ca], a.shape[ca], b.shape[1 - cb]
    tm, tn, tk = _mm_tiles(m, n, kk, a.dtype.itemsize, b.dtype.itemsize, jnp.dtype(out_dtype).itemsize, res is not None)
    a_spec = ((tm, tk), lambda i, j, k: (i, k)) if ca == 1 else ((tk, tm), lambda i, j, k: (k, i))
    b_spec = ((tk, tn), lambda i, j, k: (k, j)) if cb == 0 else ((tn, tk), lambda i, j, k: (j, k))
    extras = [] if res is None else [(res, (tm, tn), lambda i, j, k: (i, j))]

    def epi(accs, *ex):
        r = accs[0] if scale is None else accs[0] * scale
        return r + ex[0] if ex else r

    return mm_call(name, (m // tm, n // tn, kk // tk), [(a,) + a_spec, (b,) + b_spec], [(0, 1, ca, cb, 0)], 1, (tm, tn),
                   [((m, n), out_dtype, (tm, tn), lambda i, j, k: (i, j))], epi, extras, after)[0]


def rows_call(name, fn, rows, consts, outs, acc_outs=(), tr=ROW_TILE):
    s = rows[0].shape[0]
    tr = _tile(s, tr, 8)
    ins = [(r, (tr, r.shape[1]), lambda i: (i, 0)) for r in rows]
    ins += [(c, c.shape, (lambda nd: (lambda i: (0,) * nd))(c.ndim)) for c in consts]
    o = [((s, c), dt, (tr, c), lambda i: (i, 0), None) for c, dt in outs]
    o += [(shp, F32, shp, (lambda nd: (lambda i: (0,) * nd))(len(shp)), (0,)) for shp in acc_outs]
    return tile_call(name, fn, (s // tr,), ins, o)


def _lane_pick(x, h):
    lane = lax.broadcasted_iota(jnp.int32, x.shape, x.ndim - 1)
    return jnp.sum(jnp.where(lane == h, x, 0.0), axis=-1, keepdims=True)


def gateprep_fn(ab, alog, dtb):
    t = ab.shape[0]
    gs, bs = [], []
    for h in range(DN_HEADS):
        a_h = _lane_pick(ab, h)
        b_h = _lane_pick(ab, DN_HEADS + h)
        g_h = -jnp.exp(_lane_pick(alog, h)) * softplus(a_h + _lane_pick(dtb, h))
        gs.append(jnp.broadcast_to(g_h, (t, HEAD_DIM)))
        bs.append(jnp.broadcast_to(sigmoid(b_h), (t, HEAD_DIM)))
    return jnp.concatenate(gs, axis=1), jnp.concatenate(bs, axis=1)


def conv_fn(pids, x, w):
    kind = pids[0] // DN_HEADS
    y = x * w[CONV_WIDTH - 1:CONV_WIDTH]
    for i in range(CONV_WIDTH - 1):
        y = y + shift_down(x, CONV_WIDTH - 1 - i) * w[i:i + 1]
    y = silu(y)
    n = y * lax.rsqrt(jnp.sum(y * y, axis=-1, keepdims=True) + EPS)
    n = n * jnp.where(kind == 0, HEAD_DIM ** -0.5, 1.0)
    return jnp.where(kind == 2, y, n)


def unit_lower_inverse(a):
    c = a.shape[0]
    r = lax.broadcasted_iota(jnp.int32, (c, c), 0)
    cc = lax.broadcasted_iota(jnp.int32, (c, c), 1)
    x = -a
    t = jnp.where(r == cc, 1.0, 0.0) + x
    p = 2
    while p < c:
        x = _dg(x, x, 1, 0, True)
        t = t + _dg(t, x, 1, 0, True)
        p *= 2
    return t


@jax.custom_vjp
def known_inverse(a, t):
    return t


known_inverse.defvjp(lambda a, t: (t, t),
                     lambda t, g: (-_dg(_dg(t, g, 0, 0, True), t, 1, 1, True), jnp.zeros_like(t)))


def intra_head(q, k, v, g, b, t_known):
    c = q.shape[0]
    r = lax.broadcasted_iota(jnp.int32, (c, c), 0)
    cc = lax.broadcasted_iota(jnp.int32, (c, c), 1)
    tril = (r >= cc).astype(F32)
    gc = mmul(tril, g, 1, 0, True)
    m = gc[:, :c]
    decay = jnp.exp(jnp.where(r >= cc, m - m.T, -1e30))
    kb = k * b
    a = jnp.where(r > cc, mmul(kb, k, 1, 1, False) * decay, 0.0)
    t = unit_lower_inverse(a) if t_known is None else known_inverse(a, t_known)
    e = jnp.exp(gc)
    u = mmul(t, v * b, 1, 0, True)
    w = mmul(t, kb * e, 1, 0, True)
    qk = mmul(q, k, 1, 1, False) * decay
    gl = gc[c - 1:c, :]
    kd = k * jnp.exp(gl - gc)
    outs = (u, w, qk, q * e, kd, jnp.broadcast_to(jnp.exp(gl), (8, HEAD_DIM)))
    return outs + (t,) if t_known is None else outs


def _heads(x, h):
    return x[:, h * HEAD_DIM:(h + 1) * HEAD_DIM]


def intra_fn(q, k, v, g, b):
    outs = [intra_head(_heads(q, h), _heads(k, h), _heads(v, h), _heads(g, h), _heads(b, h), None) for h in range(DN_HEADS)]
    cat = lambda i: jnp.concatenate([o[i] for o in outs], axis=1)
    stack = lambda i: jnp.stack([o[i] for o in outs], axis=0)
    return cat(0), cat(1), stack(2), cat(3), cat(4), cat(5), stack(6)


def intra_bwd_fn(q, k, v, g, b, tinv, du, dw, dqk, dqd, dkd, dgl):
    res = []
    for h in range(DN_HEADS):
        hs = lambda x: _heads(x, h)
        res.append(vjp_of(intra_head, 6, (0, 1, 2, 3, 4))(hs(q), hs(k), hs(v), hs(g), hs(b), tinv[h],
                                                         hs(du), hs(dw), dqk[h], hs(dqd), hs(dkd), hs(dgl)))
    cat = lambda i: jnp.concatenate([r[i] for r in res], axis=1)
    return jnp.concatenate([cat(0), cat(1), cat(2)], axis=1), cat(3), cat(4)


def scan_step(s, u, w, qk, qd, kd, gl):
    v_new = u - mmul(w, s, 1, 0, False)
    o = mmul(qd, s, 1, 0, False) + mmul(qk, v_new, 1, 0, False)
    return s * gl[0:1, :] + mmul(kd, v_new, 0, 0, False), o


def outgate_fn(o, z, g):
    return rms(o, g) * silu(z)


def pool_fn(pids, p):
    gid = pids[0]
    s = p.shape[0]
    t1 = (lax.broadcasted_iota(jnp.int32, p.shape, 0) + 1).astype(F32)
    acc, win, out = p, 1, None
    for gi, target in enumerate(POOL_WINDOWS):
        while win < target:
            acc = acc + shift_down(acc, win)
            win *= 2
        cand = acc / jnp.minimum(t1, float(target))
        out = cand if out is None else jnp.where(gid == gi, cand, out)
    return out - p


def poolmix_fn(pooled, pw, scale):
    return mmul(pooled, pw, 1, 0, False) * scale


def attn_fn(q, k, v):
    s = mmul(q, k, 1, 1, False) * (q.shape[1] ** -0.5)
    s = s - jnp.max(s, axis=-1, keepdims=True)
    e = jnp.exp(s)
    p = e / jnp.sum(e, axis=-1, keepdims=True)
    return mmul(p, v, 1, 0, False)


def rms_fwd(name, x, g):
    return rows_call(name, lambda a, b: rms(a, b), [x], [g], [(x.shape[1], BF16)])[0]


def rms_bwd(name, x, g, dy, dres):
    def fn(a, d, r, b):
        dx, dg = vjp_of(rms, 2, (0, 1))(a, b, d)
        return dx + r, dx + r, dg
    return rows_call(name, fn, [x, dy, dres], [g], [(x.shape[1], F32), (x.shape[1], BF16)], [g.shape])


def ffn_fwd(tag, x, g, wg, wu, wd):
    s, d = x.shape
    fj = wg.shape[-1]
    f = N_SHARDS * fj
    xn = rms_fwd(tag + "_norm", x, g)
    tm = _tile(s, 256)
    w_spec = ((None, d, fj), lambda j, i, k: (j, 0, 0))
    o_spec = ((tm, fj), lambda j, i, k: (i, j))
    gate, up, act = mm_call(
        tag + "_gu", (N_SHARDS, s // tm, 1),
        [(xn, (tm, d), lambda j, i, k: (i, 0)), (wg,) + w_spec, (wu,) + w_spec],
        [(0, 1, 1, 0, 0), (0, 2, 1, 0, 1)], 2, (tm, fj),
        [((s, f), F32) + o_spec, ((s, f), F32) + o_spec, ((s, f), BF16) + o_spec],
        lambda accs: (accs[0], accs[1], swiglu(accs[0], accs[1])))
    if callable(wd):
        wd = wd(act)
    tm, tn = _tile(s, 1024), _tile(d, 1024)
    out = mm_call(
        tag + "_down", (s // tm, d // tn, N_SHARDS),
        [(act, (tm, fj), lambda i, j, k: (i, k)), (wd, (None, fj, tn), lambda i, j, k: (k, 0, j))],
        [(0, 1, 1, 0, 0)], 1, (tm, tn),
        [((s, d), F32, (tm, tn), lambda i, j, k: (i, j))],
        lambda accs, r: r + 0.5 * accs[0], [(x, (tm, tn), lambda i, j, k: (i, j))])[0]
    return out, (x, xn, gate, up, act), wd


def ffn_bwd(tag, saved, g, wg, wu, wd, dout, dout_b, after=None):
    x, xn, gate, up, act = saved
    s, d = x.shape
    fj = wg.shape[-1]
    f = N_SHARDS * fj
    tm = _tile(s, 512)
    o_spec = ((tm, fj), lambda j, i, k: (i, j))

    def epi(accs, ga, u):
        dgate, dup = vjp_of(swiglu, 2, (0, 1))(ga, u, 0.5 * accs[0])
        return dgate, dup

    dgate, dup = mm_call(
        tag + "_dact", (N_SHARDS, s // tm, 1),
        [(dout_b, (tm, d), lambda j, i, k: (i, 0)), (wd, (None, fj, d), lambda j, i, k: (j, 0, 0))],
        [(0, 1, 1, 1, 0)], 1, (tm, fj),
        [((s, f), BF16) + o_spec, ((s, f), BF16) + o_spec], epi,
        [(gate,) + o_spec, (up,) + o_spec], after=after)
    tn = _tile(d, 1024)
    dwd = mm_call(
        tag + "_dwd", (N_SHARDS, d // tn, 1),
        [(act, (s, fj), lambda j, i, k: (0, j)), (dout_b, (s, tn), lambda j, i, k: (0, i))],
        [(0, 1, 0, 0, 0)], 1, (fj, tn),
        [((N_SHARDS, fj, d), F32, (None, fj, tn), lambda j, i, k: (j, 0, i))],
        lambda accs: 0.5 * accs[0])[0]
    td = _tile(d, 512)
    g_spec = ((s, fj), lambda j, i, k: (0, j))
    w_out = ((N_SHARDS, d, fj), F32, (None, td, fj), lambda j, i, k: (j, i, 0))
    dwg, dwu = mm_call(
        tag + "_dwgu", (N_SHARDS, d // td, 1),
        [(xn, (s, td), lambda j, i, k: (0, i)), (dgate,) + g_spec, (dup,) + g_spec],
        [(0, 1, 0, 0, 0), (0, 2, 0, 0, 1)], 2, (td, fj), [w_out, w_out], lambda accs: (accs[0], accs[1]))
    tm, tn = _tile(s, 1024), _tile(d, 1024)
    a_spec = ((tm, fj), lambda i, j, k: (i, k))
    wt_spec = ((None, tn, fj), lambda i, j, k: (k, j, 0))
    dxn = mm_call(
        tag + "_dxn", (s // tm, d // tn, N_SHARDS),
        [(dgate,) + a_spec, (wg,) + wt_spec, (dup,) + a_spec, (wu,) + wt_spec],
        [(0, 1, 1, 1, 0), (2, 3, 1, 1, 0)], 1, (tm, tn),
        [((s, d), F32, (tm, tn), lambda i, j, k: (i, j))], lambda accs: accs[0])[0]
    dx, dx_b, dg = rms_bwd(tag + "_dnorm", x, g, dxn, dout)
    return dx, dx_b, dg, dwg, dwu, dwd


def mixer_fwd(tag, h, wts):
    s, d = h.shape
    dnw = DN_HEADS * HEAD_DIM
    pw_ = d - dnw
    gdim = pw_ // len(POOL_WINDOWS)
    nc = s // CHUNK
    hn = rms_fwd(tag + "_norm", h, wts["mix_norm"])
    qkv = mm2(tag + "_qkv", hn, wts["w_qkv"], 1, 0)
    z = mm2(tag + "_z", hn, wts["w_z"], 1, 0)
    ab = mm2(tag + "_ab", hn, wts["w_ab"], 1, 0)
    p = mm2(tag + "_p", hn, wts["w_p"], 1, 0)
    qkvn = tile_call(tag + "_conv", conv_fn, (3 * DN_HEADS,),
                     [(qkv, (s, HEAD_DIM), lambda i: (0, i)), (wts["conv_w"], (CONV_WIDTH, HEAD_DIM), lambda i: (0, i))],
                     [((s, 3 * dnw), F32, (s, HEAD_DIM), lambda i: (0, i), None)], with_pids=True)[0]
    g_bc, b_bc = rows_call(tag + "_gates", gateprep_fn, [ab], [wts["a_log"], wts["dt_bias"]], [(dnw, F32), (dnw, F32)])
    cw = (CHUNK, dnw)
    sq = ((DN_HEADS, s, CHUNK), F32, (DN_HEADS, CHUNK, CHUNK), lambda n: (0, n, 0), None)
    u, w, qk, qd, kd, gl, tinv = tile_call(
        tag + "_intra", intra_fn, (nc,),
        [(qkvn, cw, lambda n: (n, 0)), (qkvn, cw, lambda n: (n, 1)), (qkvn, cw, lambda n: (n, 2)),
         (g_bc, cw, lambda n: (n, 0)), (b_bc, cw, lambda n: (n, 0))],
        [((s, dnw), F32, cw, lambda n: (n, 0), None), ((s, dnw), F32, cw, lambda n: (n, 0), None), sq,
         ((s, dnw), F32, cw, lambda n: (n, 0), None), ((s, dnw), F32, cw, lambda n: (n, 0), None),
         ((nc * 8, dnw), F32, (8, dnw), lambda n: (n, 0), None), sq])
    o, states = scan_fwd(tag + "_scan", u, w, qk, qd, kd, gl)
    th = _tile(s, ROW_TILE, 8)
    y_dn = tile_call(
        tag + "_outgate", outgate_fn, (DN_HEADS, s // th),
        [(o, (th, HEAD_DIM), lambda hh, i: (i, hh)), (z, (th, HEAD_DIM), lambda hh, i: (i, hh)),
         (wts["dn_out_norm"], (1, HEAD_DIM), lambda hh, i: (0, 0))],
        [((s, dnw), BF16, (th, HEAD_DIM), lambda hh, i: (i, hh), None)])[0]
    ng = len(POOL_WINDOWS)
    pooled = tile_call(tag + "_pool", pool_fn, (ng,), [(p, (s, gdim), lambda i: (0, i))],
                       [((s, pw_), BF16, (s, gdim), lambda i: (0, i), None)], with_pids=True)[0]
    tp = _tile(s, 512)
    y_pool = tile_call(
        tag + "_poolmix", poolmix_fn, (ng, s // tp),
        [(pooled, (tp, gdim), lambda gi, i: (i, gi)), (wts["pool_w"], (None, gdim, gdim), lambda gi, i: (gi, 0, 0)),
         (wts["pool_scale"], (1, gdim), lambda gi, i: (0, gi))],
        [((s, pw_), BF16, (tp, gdim), lambda gi, i: (i, gi), None)])[0]
    h1 = mm2(tag + "_out_a", y_dn, wts["w_out_a"], 1, 0, res=h)
    h2 = mm2(tag + "_out_b", y_pool, wts["w_out_b"], 1, 0, res=h1)
    saved = (h, hn, qkv, z, ab, p, qkvn, g_bc, b_bc, u, w, qk, qd, kd, gl, tinv, o, states, y_dn, pooled, y_pool)
    return h2, saved


def scan_fwd(name, u, w, qk, qd, kd, gl):
    s, dnw = u.shape
    nc = s // CHUNK
    cw = (CHUNK, dnw)

    def body(u_r, w_r, qk_r, qd_r, kd_r, gl_r, o_r, st_r, state):
        @pl.when(pl.program_id(0) == 0)
        def _():
            state[...] = jnp.zeros_like(state)

        st_r[...] = state[...]
        outs = []
        for h in range(DN_HEADS):
            hs = slice(h * HEAD_DIM, (h + 1) * HEAD_DIM)
            s_new, o_h = scan_step(state[hs, :], u_r[:, hs], w_r[:, hs], qk_r[h], qd_r[:, hs], kd_r[:, hs], gl_r[:, hs])
            state[hs, :] = s_new
            outs.append(o_h)
        o_r[...] = jnp.concatenate(outs, axis=1)

    row = lambda n: (n, 0)
    return pl.pallas_call(
        body, name=name, grid=(nc,),
        in_specs=[pl.BlockSpec(cw, row), pl.BlockSpec(cw, row), pl.BlockSpec((DN_HEADS, CHUNK, CHUNK), lambda n: (0, n, 0)),
                  pl.BlockSpec(cw, row), pl.BlockSpec(cw, row), pl.BlockSpec((8, dnw), row)],
        out_specs=[pl.BlockSpec(cw, row), pl.BlockSpec((None, dnw, HEAD_DIM), lambda n: (n, 0, 0))],
        out_shape=[jax.ShapeDtypeStruct((s, dnw), F32), jax.ShapeDtypeStruct((nc, dnw, HEAD_DIM), F32)],
        scratch_shapes=[pltpu.VMEM((dnw, HEAD_DIM), F32)],
        compiler_params=_params(),
    )(u, w, qk, qd, kd, gl)


def scan_bwd(name, states, u, w, qk, qd, kd, gl, do):
    s, dnw = u.shape
    nc = s // CHUNK
    cw = (CHUNK, dnw)

    def body(st_r, u_r, w_r, qk_r, qd_r, kd_r, gl_r, do_r, du_r, dw_r, dqk_r, dqd_r, dkd_r, dgl_r, dstate):
        @pl.when(pl.program_id(0) == 0)
        def _():
            dstate[...] = jnp.zeros_like(dstate)

        res = []
        for h in range(DN_HEADS):
            hs = slice(h * HEAD_DIM, (h + 1) * HEAD_DIM)
            r = vjp_of(scan_step, 7, tuple(range(7)))(
                st_r[hs, :], u_r[:, hs], w_r[:, hs], qk_r[h], qd_r[:, hs], kd_r[:, hs], gl_r[:, hs],
                dstate[hs, :], do_r[:, hs])
            dstate[hs, :] = r[0]
            res.append(r)
        cat = lambda i: jnp.concatenate([r[i] for r in res], axis=1)
        du_r[...] = cat(1)
        dw_r[...] = cat(2)
        dqk_r[...] = jnp.stack([r[3] for r in res], axis=0)
        dqd_r[...] = cat(4)
        dkd_r[...] = cat(5)
        dgl_r[...] = cat(6)

    row = lambda n: (nc - 1 - n, 0)
    qk_spec = pl.BlockSpec((DN_HEADS, CHUNK, CHUNK), lambda n: (0, nc - 1 - n, 0))
    return pl.pallas_call(
        body, name=name, grid=(nc,),
        in_specs=[pl.BlockSpec((None, dnw, HEAD_DIM), lambda n: (nc - 1 - n, 0, 0)), pl.BlockSpec(cw, row), pl.BlockSpec(cw, row),
                  qk_spec, pl.BlockSpec(cw, row), pl.BlockSpec(cw, row), pl.BlockSpec((8, dnw), row), pl.BlockSpec(cw, row)],
        out_specs=[pl.BlockSpec(cw, row), pl.BlockSpec(cw, row), qk_spec, pl.BlockSpec(cw, row), pl.BlockSpec(cw, row),
                   pl.BlockSpec((8, dnw), row)],
        out_shape=[jax.ShapeDtypeStruct((s, dnw), F32), jax.ShapeDtypeStruct((s, dnw), F32),
                   jax.ShapeDtypeStruct((DN_HEADS, s, CHUNK), F32), jax.ShapeDtypeStruct((s, dnw), F32),
                   jax.ShapeDtypeStruct((s, dnw), F32), jax.ShapeDtypeStruct((nc * 8, dnw), F32)],
        scratch_shapes=[pltpu.VMEM((dnw, HEAD_DIM), F32)],
        compiler_params=_params(),
    )(states, u, w, qk, qd, kd, gl, do)


def mixer_bwd(tag, saved, wts, dout, dout_b, after=None):
    h, hn, qkv, z, ab, p, qkvn, g_bc, b_bc, u, w, qk, qd, kd, gl, tinv, o, states, y_dn, pooled, y_pool = saved
    s, d = h.shape
    dnw = DN_HEADS * HEAD_DIM
    pw_ = d - dnw
    ng = len(POOL_WINDOWS)
    gdim = pw_ // ng
    nc = s // CHUNK
    gr = {}
    d_ydn = mm2(tag + "_dydn", dout_b, wts["w_out_a"], 1, 1, after=after)
    d_ypool = mm2(tag + "_dypool", dout_b, wts["w_out_b"], 1, 1)
    gr["w_out_a"] = mm2(tag + "_dwout_a", y_dn, dout_b, 0, 0)
    gr["w_out_b"] = mm2(tag + "_dwout_b", y_pool, dout_b, 0, 0)
    tp = _tile(s, 512)
    d_pooled, gr["pool_w"], gr["pool_scale"] = tile_call(
        tag + "_dpoolmix", vjp_of(poolmix_fn, 3, (0, 1, 2)), (ng, s // tp),
        [(pooled, (tp, gdim), lambda gi, i: (i, gi)), (wts["pool_w"], (None, gdim, gdim), lambda gi, i: (gi, 0, 0)),
         (wts["pool_scale"], (1, gdim), lambda gi, i: (0, gi)), (d_ypool, (tp, gdim), lambda gi, i: (i, gi))],
        [((s, pw_), F32, (tp, gdim), lambda gi, i: (i, gi), None),
         ((ng, gdim, gdim), F32, (None, gdim, gdim), lambda gi, i: (gi, 0, 0), (1,)),
         ((1, pw_), F32, (1, gdim), lambda gi, i: (0, gi), (1,))])
    d_p = tile_call(tag + "_dpool", vjp_of(pool_fn, 1, (0,), True), (ng,),
                    [(p, (s, gdim), lambda i: (0, i)), (d_pooled, (s, gdim), lambda i: (0, i))],
                    [((s, pw_), BF16, (s, gdim), lambda i: (0, i), None)], with_pids=True)[0]
    hb = (_tile(s, ROW_TILE, 8), HEAD_DIM)
    d_o, d_z, gr["dn_out_norm"] = tile_call(
        tag + "_doutgate", vjp_of(outgate_fn, 3, (0, 1, 2)), (DN_HEADS, s // hb[0]),
        [(o, hb, lambda hh, i: (i, hh)), (z, hb, lambda hh, i: (i, hh)), (wts["dn_out_norm"], (1, HEAD_DIM), lambda hh, i: (0, 0)),
         (d_ydn, hb, lambda hh, i: (i, hh))],
        [((s, dnw), F32, hb, lambda hh, i: (i, hh), None), ((s, dnw), BF16, hb, lambda hh, i: (i, hh), None),
         ((1, HEAD_DIM), F32, (1, HEAD_DIM), lambda hh, i: (0, 0), (0, 1))])
    du, dw, dqk, dqd, dkd, dgl = scan_bwd(tag + "_dscan", states, u, w, qk, qd, kd, gl, d_o)
    cw = (CHUNK, dnw)
    row = lambda n: (n, 0)
    d_qkvn, dg_bc, db_bc = tile_call(
        tag + "_dintra", intra_bwd_fn, (nc,),
        [(qkvn, cw, lambda n: (n, 0)), (qkvn, cw, lambda n: (n, 1)), (qkvn, cw, lambda n: (n, 2)),
         (g_bc, cw, row), (b_bc, cw, row), (tinv, (DN_HEADS, CHUNK, CHUNK), lambda n: (0, n, 0)), (du, cw, row), (dw, cw, row),
         (dqk, (DN_HEADS, CHUNK, CHUNK), lambda n: (0, n, 0)), (dqd, cw, row), (dkd, cw, row), (dgl, (8, dnw), row)],
        [((s, 3 * dnw), F32, (CHUNK, 3 * dnw), row, None)] + [((s, dnw), F32, cw, row, None)] * 2)
    d_ab, gr["a_log"], gr["dt_bias"] = rows_call(
        tag + "_dgates", lambda a, dg, db, al, dt: vjp_of(gateprep_fn, 3, (0, 1, 2))(a, al, dt, dg, db),
        [ab, dg_bc, db_bc], [wts["a_log"], wts["dt_bias"]], [(AB_PAD, BF16)], [(1, AB_PAD), (1, AB_PAD)])
    d_qkv, gr["conv_w"] = tile_call(
        tag + "_dconv", vjp_of(conv_fn, 2, (0, 1), True), (3 * DN_HEADS,),
        [(qkv, (s, HEAD_DIM), lambda i: (0, i)), (wts["conv_w"], (CONV_WIDTH, HEAD_DIM), lambda i: (0, i)),
         (d_qkvn, (s, HEAD_DIM), lambda i: (0, i))],
        [((s, 3 * dnw), BF16, (s, HEAD_DIM), lambda i: (0, i), None),
         ((CONV_WIDTH, 3 * dnw), F32, (CONV_WIDTH, HEAD_DIM), lambda i: (0, i), None)], with_pids=True)
    gr["w_qkv"] = mm2(tag + "_dwqkv", hn, d_qkv, 0, 0)
    gr["w_z"] = mm2(tag + "_dwz", hn, d_z, 0, 0)
    gr["w_ab"] = mm2(tag + "_dwab", hn, d_ab, 0, 0)
    gr["w_p"] = mm2(tag + "_dwp", hn, d_p, 0, 0)
    d_hn = mm2(tag + "_dhn1", d_qkv, wts["w_qkv"], 1, 1)
    d_hn = mm2(tag + "_dhn2", d_z, wts["w_z"], 1, 1, res=d_hn)
    d_hn = mm2(tag + "_dhn3", d_ab, wts["w_ab"], 1, 1, res=d_hn)
    d_hn = mm2(tag + "_dhn4", d_p, wts["w_p"], 1, 1, res=d_hn)
    dh, dh_b, gr["mix_norm"] = rms_bwd(tag + "_dnorm", h, wts["mix_norm"], d_hn, dout)
    return dh, dh_b, gr


def xattn_fwd(tag, h, memn, wts):
    s, d = h.shape
    m = memn.shape[0]
    dh_ = d // X_HEADS
    hn = rms_fwd(tag + "_norm", h, wts["xattn_norm"])
    q = mm2(tag + "_q", hn, wts["wq"], 1, 0, out_dtype=BF16)
    wkv = wts["wkv"]
    nj = wkv.shape[2]
    kv = mm_call(tag + "_kv", (1, N_SHARDS, 1), [(memn, (m, d), lambda i, j, k: (0, 0)), (wkv, (None, d, nj), lambda i, j, k: (j, 0, 0))],
                 [(0, 1, 1, 0, 0)], 1, (m, nj), [((m, 2 * d), BF16, (m, nj), lambda i, j, k: (0, j))], lambda accs: accs[0])[0]
    tq = _tile(s, 512)
    o = tile_call(
        tag + "_attn", attn_fn, (X_HEADS, s // tq),
        [(q, (tq, dh_), lambda hh, i: (i, hh)), (kv, (m, dh_), lambda hh, i: (0, hh)), (kv, (m, dh_), lambda hh, i: (0, X_HEADS + hh))],
        [((s, d), BF16, (tq, dh_), lambda hh, i: (i, hh), None)])[0]
    out = mm2(tag + "_o", o, wts["wo"], 1, 0, res=h)
    return out, (h, hn, q, kv, o)


def xattn_bwd(tag, saved, memn, mem, wts, dout, dout_b, after=None):
    h, hn, q, kv, o = saved
    s, d = h.shape
    m = memn.shape[0]
    dh_ = d // X_HEADS
    gr = {}
    d_o = mm2(tag + "_do", dout_b, wts["wo"], 1, 1, out_dtype=BF16, after=after)
    gr["wo"] = mm2(tag + "_dwo", o, dout_b, 0, 0)
    tq = _tile(s, 512)
    dq, dk, dv = tile_call(
        tag + "_dattn", vjp_of(attn_fn, 3, (0, 1, 2)), (X_HEADS, s // tq),
        [(q, (tq, dh_), lambda hh, i: (i, hh)), (kv, (m, dh_), lambda hh, i: (0, hh)), (kv, (m, dh_), lambda hh, i: (0, X_HEADS + hh)),
         (d_o, (tq, dh_), lambda hh, i: (i, hh))],
        [((s, d), BF16, (tq, dh_), lambda hh, i: (i, hh), None),
         ((m, d), F32, (m, dh_), lambda hh, i: (0, hh), (1,)), ((m, d), F32, (m, dh_), lambda hh, i: (0, hh), (1,))])
    dkv = jnp.concatenate([dk, dv], axis=1).astype(BF16)
    gr["wq"] = mm2(tag + "_dwq", hn, dq, 0, 0)
    wkv = wts["wkv"]
    nj = wkv.shape[2]
    td, tn = _tile(d, 1024), _tile(d, 1024)
    gr["wkv"] = mm_call(tag + "_dwkv", (d // td, N_SHARDS, 1),
                        [(memn, (m, td), lambda i, j, k: (0, i)), (dkv, (m, nj), lambda i, j, k: (0, j))], [(0, 1, 0, 0, 0)], 1, (td, nj),
                        [((N_SHARDS, d, nj), F32, (None, td, nj), lambda i, j, k: (j, i, 0))], lambda accs: accs[0])[0]
    d_memn = mm_call(tag + "_dmemn", (1, d // tn, N_SHARDS),
                     [(dkv, (m, nj), lambda i, j, k: (0, k)), (wkv, (None, tn, nj), lambda i, j, k: (k, j, 0))], [(0, 1, 1, 1, 0)], 1, (m, tn),
                     [((m, d), F32, (m, tn), lambda i, j, k: (0, j))], lambda accs: accs[0])[0]
    gr["mem_norm"] = rows_call(
        tag + "_dmemnorm", lambda a, dy, b: vjp_of(rms, 2, (1,))(a, b, dy)[0], [mem, d_memn], [wts["mem_norm"]], [],
        [wts["mem_norm"].shape], tr=128)[0]
    d_hn = mm2(tag + "_dhn", dq, wts["wq"], 1, 1)
    dh, dh_b, gr["xattn_norm"] = rms_bwd(tag + "_dnorm", h, wts["xattn_norm"], d_hn, dout)
    return dh, dh_b, gr


def final_loss(x, g, tgt):
    d = x.shape[1]

    def fn(a, t, b):
        def f(aa, bb):
            return 0.5 * jnp.sum(jnp.square(rms(aa, bb) - t)) / d
        loss, (dx, dg) = jax.value_and_grad(f, (0, 1))(a, b)
        lane = lax.broadcasted_iota(jnp.int32, (1, 128), 1)
        return dx, dx, dg, jnp.where(lane == 0, loss, 0.0)
    return rows_call("final_loss", fn, [x, tgt], [g], [(d, F32), (d, BF16)], [g.shape, (1, 128)])


SHARDED = ("ffn1_w_gate", "ffn1_w_up", "ffn1_w_down", "w_in", "conv_w", "pool_w", "w_out", "xattn_wq", "xattn_wkv",
           "xattn_wo", "ffn2_w_gate", "ffn2_w_up", "ffn2_w_down")
REPLICATED = ("ffn1_norm", "mix_norm", "a_log", "dt_bias", "dn_out_norm", "pool_scale", "xattn_norm", "mem_norm",
              "ffn2_norm", "final_norm")
WEIGHTS = ("ffn1_norm", "ffn1_w_gate", "ffn1_w_up", "ffn1_w_down", "mix_norm", "w_in", "conv_w", "a_log", "dt_bias",
           "dn_out_norm", "pool_w", "pool_scale", "w_out", "xattn_norm", "mem_norm", "xattn_wq", "xattn_wkv", "xattn_wo",
           "ffn2_norm", "ffn2_w_gate", "ffn2_w_up", "ffn2_w_down", "final_norm")


def _lane_pad(v, width=128):
    return jnp.pad(v, (0, width - v.shape[0]))[None, :]


GROUPS = {"ffn1": ("ffn1_w_gate", "ffn1_w_up"), "ffn1d": ("ffn1_w_down",), "mixer": ("w_in", "conv_w", "pool_w", "w_out"),
          "xattn": ("xattn_wq", "xattn_wkv", "xattn_wo"), "ffn2": ("ffn2_w_gate", "ffn2_w_up", "ffn2_w_down")}


def _cols(g):
    return jnp.transpose(g, (1, 0, 2)).reshape(g.shape[1], -1)


def _rows(g):
    return g.reshape(-1, g.shape[2])


def _rep_row(rep, name, l):
    return rep[name][l][None, :].astype(F32)


def mixer_weights(g, rep, l):
    dnw = DN_HEADS * HEAD_DIM
    w_in = _cols(g["w_in"])
    o_ab = 4 * dnw
    w_out = _rows(g["w_out"])
    gdim = g["pool_w"].shape[-1]
    pw = jnp.transpose(g["pool_w"].reshape(N_SHARDS, len(POOL_WINDOWS), gdim // N_SHARDS, gdim), (1, 0, 2, 3))
    return dict(
        mix_norm=_rep_row(rep, "mix_norm", l), w_qkv=w_in[:, :3 * dnw], w_z=w_in[:, 3 * dnw:o_ab],
        w_ab=jnp.pad(w_in[:, o_ab:o_ab + 2 * DN_HEADS], ((0, 0), (0, AB_PAD - 2 * DN_HEADS))),
        w_p=w_in[:, o_ab + 2 * DN_HEADS:], conv_w=_cols(g["conv_w"].reshape(N_SHARDS, CONV_WIDTH, -1)).astype(F32),
        a_log=_lane_pad(rep["a_log"][l].astype(F32)), dt_bias=_lane_pad(rep["dt_bias"][l].astype(F32)),
        dn_out_norm=_rep_row(rep, "dn_out_norm", l), pool_w=pw.reshape(len(POOL_WINDOWS), gdim, gdim),
        pool_scale=_rep_row(rep, "pool_scale", l), w_out_a=w_out[:dnw], w_out_b=w_out[dnw:])


def xattn_weights(g, rep, l):
    return dict(xattn_norm=_rep_row(rep, "xattn_norm", l), mem_norm=_rep_row(rep, "mem_norm", l), wq=_rows(g["xattn_wq"]),
                wkv=g["xattn_wkv"], wo=_rows(g["xattn_wo"]))


def _col_shards(g):
    k, n = g.shape
    return jnp.transpose(g.reshape(k, N_SHARDS, n // N_SHARDS), (1, 0, 2))


MIN_COMM_ROWS = 32


def _comm_rows(a):
    if a.shape[-2] >= MIN_COMM_ROWS:
        return a
    return a.reshape(a.shape[:-2] + (MIN_COMM_ROWS, -1))


GRAD_PARTS = (("ffn2_w_gate", "ffn2_w_up", "ffn2_w_down"), ("xattn_wq", "xattn_wkv", "xattn_wo"),
              ("w_in", "conv_w", "pool_w", "w_out"), ("ffn1_w_gate", "ffn1_w_up", "ffn1_w_down"))


def model_grads(x, mem, tgt, fetch, rep, grads_done=None):
    s, d = x.shape
    depth = rep["ffn1_norm"].shape[0]
    saved, wl = [], []
    h = x
    for l in range(depth):
        t = "l%d" % l
        f1 = dict(fetch(l, "ffn1", h))
        h, s1, f1["ffn1_w_down"] = ffn_fwd(t + "_ffn1", h, _rep_row(rep, "ffn1_norm", l), f1["ffn1_w_gate"], f1["ffn1_w_up"],
                                           lambda act: fetch(l, "ffn1d", act)["ffn1_w_down"])
        wm = mixer_weights(fetch(l, "mixer", h), rep, l)
        h, s2 = mixer_fwd(t + "_mix", h, wm)
        wx = xattn_weights(fetch(l, "xattn", h), rep, l)
        memn = rms_fwd(t + "_memnorm", mem, wx["mem_norm"])
        h, s3 = xattn_fwd(t + "_xattn", h, memn, wx)
        f2 = fetch(l, "ffn2", h)
        h, s4, _ = ffn_fwd(t + "_ffn2", h, _rep_row(rep, "ffn2_norm", l), f2["ffn2_w_gate"], f2["ffn2_w_up"], f2["ffn2_w_down"])
        saved.append((memn, s1, s2, s3, s4))
        wl.append((f1, wm, wx, f2))
    dh, dh_b, d_final, loss_row = final_loss(h, rep["final_norm"][None, :].astype(F32), tgt)
    big, small = [None] * depth, [None] * depth
    after = None
    done = (lambda l, part, gb: None) if grads_done is None else grads_done
    for l in reversed(range(depth)):
        f1, wm, wx, f2 = wl[l]
        t = "l%d" % l
        memn, s1, s2, s3, s4 = saved[l]
        gb, gs = {}, {}
        dh, dh_b, gs["ffn2_norm"], gb["ffn2_w_gate"], gb["ffn2_w_up"], gb["ffn2_w_down"] = ffn_bwd(
            t + "_ffn2", s4, _rep_row(rep, "ffn2_norm", l), f2["ffn2_w_gate"], f2["ffn2_w_up"], f2["ffn2_w_down"], dh, dh_b, after)
        after = done(l, 0, gb)
        dh, dh_b, gx = xattn_bwd(t + "_xattn", s3, memn, mem, wx, dh, dh_b, after)
        gb["xattn_wq"] = gx["wq"].reshape(N_SHARDS, d // N_SHARDS, d)
        gb["xattn_wo"] = gx["wo"].reshape(N_SHARDS, d // N_SHARDS, d)
        gb["xattn_wkv"] = gx["wkv"]
        after = done(l, 1, gb)
        dh, dh_b, gm = mixer_bwd(t + "_mix", s2, wm, dh, dh_b, after)
        gb["w_in"] = _col_shards(jnp.concatenate([gm["w_qkv"], gm["w_z"], gm["w_ab"][:, :2 * DN_HEADS], gm["w_p"]], axis=1))
        gb["conv_w"] = _comm_rows(_col_shards(gm["conv_w"]))
        gdim = gm["pool_w"].shape[-1]
        gb["pool_w"] = jnp.transpose(gm["pool_w"].reshape(len(POOL_WINDOWS), N_SHARDS, gdim // N_SHARDS, gdim),
                                     (1, 0, 2, 3)).reshape(N_SHARDS, gdim, gdim)
        gb["w_out"] = jnp.concatenate([gm["w_out_a"], gm["w_out_b"]], axis=0).reshape(N_SHARDS, d // N_SHARDS, d)
        after = done(l, 2, gb)
        dh, dh_b, gs["ffn1_norm"], gb["ffn1_w_gate"], gb["ffn1_w_up"], gb["ffn1_w_down"] = ffn_bwd(
            t + "_ffn1", s1, _rep_row(rep, "ffn1_norm", l), f1["ffn1_w_gate"], f1["ffn1_w_up"], f1["ffn1_w_down"], dh, dh_b, after)
        after = done(l, 3, gb)
        for n in ("xattn_norm", "mem_norm"):
            gs[n] = gx[n]
        for n in ("mix_norm", "a_log", "dt_bias", "dn_out_norm", "pool_scale"):
            gs[n] = gm[n]
        big[l], small[l] = gb, gs
    return loss_row, dh, big, small, d_final


HBM = pl.BlockSpec(memory_space=pltpu.HBM)


def _place():
    x, y, c = lax.axis_index("x"), lax.axis_index("y"), lax.axis_index("c")
    chips = [(1 - x, y), (x, 1 - y), (1 - x, 1 - y)]
    return x, y, c, 2 * x + y, chips, [2 * px + py for px, py in chips]


SEM = pl.BlockSpec(memory_space=pltpu.SEMAPHORE)
SPLIT_COPY = pltpu.CompilerParams(has_side_effects=pltpu.SideEffectType.DATAFLOW_SIDE_EFFECTING)


def _half(ref, lead, c):
    r = ref.shape[1] // 2
    return ref.at[lead, pl.ds(c * r, r)]


def place_shards(name, w, where, dtype):
    nl, r, cdim = w.shape
    tr = _rtile(r)
    out = ((N_SHARDS, r, cdim), dtype, (None, tr, cdim), lambda i, p: (p[0], i, 0), None)
    return tile_call(name, lambda a: tuple(a[l] for l in range(nl)), (r // tr,),
                     [(w, (nl, tr, cdim), lambda i, p: (0, i, 0))], [out] * nl, prefetch=where)


def _gather_copies(bufs, ssem, rsem):
    x, y, c, j_own, chips, js = _place()
    res = []
    for i, b in enumerate(bufs):
        for k, (px, py) in enumerate(chips):
            mk = lambda slot: pltpu.make_async_remote_copy(
                src_ref=_half(b, slot, c), dst_ref=_half(b, slot, c), send_sem=ssem.at[3 * i + k],
                recv_sem=rsem.at[3 * i + k], device_id=(px, py, c), device_id_type=MESH)
            res.append((mk(j_own), mk(js[k])))
    return res


def gather_start(bufs, groups):
    n, ng = len(bufs), len(groups)

    def body(*refs):
        sems, outs, token = refs[n:n + 2 * ng], refs[n + 2 * ng:2 * n + 2 * ng], refs[2 * n + 2 * ng]
        for gi, group in enumerate(groups):
            for mine, _ in _gather_copies([outs[t] for t in group], sems[2 * gi], sems[2 * gi + 1]):
                mine.start()
        token[...] = jnp.zeros_like(token)

    sem_shapes = []
    for group in groups:
        sem_shapes += [pltpu.SemaphoreType.DMA((3 * len(group),))] * 2
    res = pl.pallas_call(
        body, name="gather_start", in_specs=[HBM] * n,
        out_specs=[SEM] * (2 * ng) + [HBM] * n + [pl.BlockSpec(memory_space=pltpu.VMEM)],
        out_shape=sem_shapes + [pltpu.HBM(a.shape, a.dtype) for a in bufs] + [jax.ShapeDtypeStruct((8, 128), F32)],
        input_output_aliases={t: 2 * ng + t for t in range(n)}, compiler_params=SPLIT_COPY,
    )(*[pltpu.with_memory_space_constraint(a, pltpu.HBM) for a in bufs])
    sems = [(res[2 * gi], res[2 * gi + 1]) for gi in range(ng)]
    return sems, res[2 * ng:2 * ng + n], res[-1]


def gather_wait(name, bufs, sems, after):
    n = len(bufs)

    def body(*refs):
        ins, ssem, rsem = refs[:n], refs[n], refs[n + 1]
        for mine, theirs in _gather_copies(ins, ssem, rsem):
            mine.wait_send()
            theirs.wait_recv()

    return pl.pallas_call(
        body, name=name, in_specs=[HBM] * n + [SEM, SEM, pl.BlockSpec(memory_space=pl.ANY)], out_specs=[HBM] * n,
        out_shape=[pltpu.HBM(a.shape, a.dtype) for a in bufs],
        input_output_aliases={t: t for t in range(n)}, compiler_params=SPLIT_COPY,
    )(*bufs, sems[0], sems[1], after)


def gather_swap(name, bufs):
    n = len(bufs)

    def body(*refs):
        outs = refs[n:2 * n]
        ssem, rsem = refs[2 * n:]
        x, y, c, j_own, chips, js = _place()

        def copy(i, k, half):
            blk = _half(outs[i], js[k], half)
            return pltpu.make_async_remote_copy(src_ref=blk, dst_ref=blk, send_sem=ssem.at[3 * i + k],
                                                recv_sem=rsem.at[3 * i + k], device_id=(x, y, 1 - c), device_id_type=MESH)

        for i in range(n):
            for k in range(3):
                copy(i, k, c).start()
        for i in range(n):
            for k in range(3):
                copy(i, k, 1 - c).wait_recv()
                copy(i, k, c).wait_send()

    return pl.pallas_call(
        body, name=name, in_specs=[HBM] * n, out_specs=[HBM] * n, input_output_aliases={t: t for t in range(n)},
        out_shape=[jax.ShapeDtypeStruct(a.shape, a.dtype) for a in bufs],
        scratch_shapes=[pltpu.SemaphoreType.DMA((3 * n,)), pltpu.SemaphoreType.DMA((3 * n,))],
    )(*bufs)


def _pair_copies(grads, lands, ssem, rsem):
    x, y, c, _, _, _ = _place()
    res = []
    for t in range(len(grads)):
        r = grads[t].shape[1] // 2
        res.append(pltpu.make_async_remote_copy(src_ref=grads[t].at[:, pl.ds((1 - c) * r, r)], dst_ref=lands[t], send_sem=ssem.at[t],
                                                recv_sem=rsem.at[t], device_id=(x, y, 1 - c), device_id_type=MESH))
    return res


def _split_start(name, copies, arrays, n_sems, after=None):
    n = len(arrays)
    n_dep = 0 if after is None else 1

    def body(*refs):
        refs = refs[n + n_dep:]
        for cp in copies(refs[2:n + 2], refs[0], refs[1]):
            cp.start()
        refs[n + 2][...] = jnp.zeros((8, 128), F32)

    res = pl.pallas_call(
        body, name=name, in_specs=[HBM] * n + [pl.BlockSpec(memory_space=pl.ANY)] * n_dep,
        out_specs=[SEM, SEM] + [HBM] * n + [pl.BlockSpec(memory_space=pltpu.VMEM)],
        out_shape=[pltpu.SemaphoreType.DMA((n_sems,))] * 2 + [pltpu.HBM(a.shape, a.dtype) for a in arrays]
        + [jax.ShapeDtypeStruct((8, 128), F32)],
        input_output_aliases={t: 2 + t for t in range(n)}, compiler_params=SPLIT_COPY,
    )(*[pltpu.with_memory_space_constraint(a, pltpu.HBM) for a in arrays], *([] if after is None else [after]))
    return res[0], res[1], res[2:2 + n], res[-1]


def _split_wait(name, copies, ssem, rsem, arrays, after):
    n = len(arrays)

    def body(*refs):
        for cp in copies(refs[:n], refs[n], refs[n + 1]):
            cp.wait_send()
            cp.wait_recv()

    return pl.pallas_call(
        body, name=name, in_specs=[HBM] * n + [SEM, SEM, pl.BlockSpec(memory_space=pl.ANY)], out_specs=[HBM] * n,
        out_shape=[pltpu.HBM(a.shape, a.dtype) for a in arrays],
        input_output_aliases={t: t for t in range(n)}, compiler_params=SPLIT_COPY,
    )(*arrays, ssem, rsem, after)


def pair_start(name, grads):
    n = len(grads)
    lands = [lax.empty((a.shape[0], a.shape[1] // 2, a.shape[2]), a.dtype) for a in grads]
    ssem, rsem, arrays, token = _split_start(name, lambda a, s, r: _pair_copies(a[:n], a[n:], s, r), list(grads) + lands, n)
    return ssem, rsem, arrays[:n], arrays[n:], token


def pair_wait(name, ssem, rsem, grads, lands, after):
    n = len(grads)
    arrays = _split_wait(name, lambda a, s, r: _pair_copies(a[:n], a[n:], s, r), ssem, rsem, list(grads) + list(lands), after)
    return arrays[:n], arrays[n:]


def _reduce_copies(parts, lands, ssem, rsem):
    x, y, c, j_own, chips, js = _place()
    return [pltpu.make_async_remote_copy(src_ref=parts[t].at[js[k]], dst_ref=lands[t].at[k], send_sem=ssem.at[3 * t + k],
                                         recv_sem=rsem.at[3 * t + k], device_id=(px, py, c), device_id_type=MESH)
            for t in range(len(parts)) for k, (px, py) in enumerate(chips)]


def reduce_start(name, parts, after):
    n = len(parts)
    lands = [lax.empty((3,) + a.shape[1:], a.dtype) for a in parts]
    ssem, rsem, arrays, token = _split_start(name, lambda a, s, r: _reduce_copies(a[:n], a[n:], s, r), list(parts) + lands,
                                             3 * n, after)
    return ssem, rsem, arrays[:n], arrays[n:], token


def reduce_wait(name, ssem, rsem, parts, lands, after):
    n = len(parts)
    arrays = _split_wait(name, lambda a, s, r: _reduce_copies(a[:n], a[n:], s, r), ssem, rsem, list(parts) + list(lands), after)
    return arrays[:n], arrays[n:]


def share_halves(name, bufs):
    n = len(bufs)

    def body(*refs):
        outs = refs[n:2 * n]
        ssem, rsem = refs[2 * n:]
        x, y, c, _, _, _ = _place()

        def copy(t, l, half):
            blk = _half(outs[t], l, half)
            return pltpu.make_async_remote_copy(src_ref=blk, dst_ref=blk, send_sem=ssem.at[2 * t + l],
                                                recv_sem=rsem.at[2 * t + l], device_id=(x, y, 1 - c), device_id_type=MESH)

        for t in range(n):
            for l in range(2):
                copy(t, l, c).start()
        for t in range(n):
            for l in range(2):
                copy(t, l, 1 - c).wait_recv()
                copy(t, l, c).wait_send()

    return pl.pallas_call(
        body, name=name, in_specs=[HBM] * n, out_specs=[HBM] * n, input_output_aliases={t: t for t in range(n)},
        out_shape=[jax.ShapeDtypeStruct(a.shape, a.dtype) for a in bufs],
        scratch_shapes=[pltpu.SemaphoreType.DMA((2 * n,)), pltpu.SemaphoreType.DMA((2 * n,))],
    )(*bufs)


def allreduce_small(buf, after):
    r = buf.shape[0]
    n_dev = 8

    def body(in_ref, _, out_ref, gath, ssem, rsem):
        x, y, c = lax.axis_index("x"), lax.axis_index("y"), lax.axis_index("c")
        flip = lambda v, bit: 1 - v if bit else v
        me = 4 * x + 2 * y + c
        gath[me] = in_ref[...]
        peers = [(flip(x, k >> 2 & 1), flip(y, k >> 1 & 1), flip(c, k & 1)) for k in range(1, n_dev)]
        sends = []
        for k, peer in enumerate(peers):
            cp = pltpu.make_async_remote_copy(src_ref=in_ref, dst_ref=gath.at[me], send_sem=ssem.at[k], recv_sem=rsem.at[k],
                                              device_id=peer, device_id_type=MESH)
            cp.start()
            sends.append(cp)
        for k, (px, py, pc) in enumerate(peers):
            pltpu.make_async_remote_copy(src_ref=in_ref, dst_ref=gath.at[4 * px + 2 * py + pc], send_sem=ssem.at[k],
                                         recv_sem=rsem.at[k], device_id=(px, py, pc), device_id_type=MESH).wait_recv()
        for cp in sends:
            cp.wait_send()
        acc = gath[0]
        for i in range(1, n_dev):
            acc = acc + gath[i]
        out_ref[...] = acc

    return pl.pallas_call(
        body, name="allreduce_small",
        in_specs=[pl.BlockSpec(memory_space=pltpu.VMEM), pl.BlockSpec(memory_space=pl.ANY)],
        out_specs=pl.BlockSpec(memory_space=pltpu.VMEM), out_shape=jax.ShapeDtypeStruct(buf.shape, F32),
        scratch_shapes=[pltpu.VMEM((n_dev, r, 128), F32), pltpu.SemaphoreType.DMA((n_dev - 1,)), pltpu.SemaphoreType.DMA((n_dev - 1,))],
    )(buf, after)


def _rtile(r, pref=256):
    return _tile(r, pref, 16)


def chip_partial(name, grad, recv, where):
    _, rh, cdim = recv.shape
    tr = _rtile(rh, 512)
    nt = rh // tr
    return tile_call(name, lambda a, b: a + b, (N_SHARDS, nt),
                     [(grad, (None, tr, cdim), lambda j, i, p: (j, p[1] * nt + i, 0)), (recv, (None, tr, cdim), lambda j, i, p: (j, i, 0))],
                     [(recv.shape, BF16, (None, tr, cdim), lambda j, i, p: (j, i, 0), None)], prefetch=where)[0]


def sum_chips(name, parts, lands, where, layer, n_layers, into):
    _, rh, cdim = parts.shape
    tr = _rtile(rh, 512)
    nt = rh // tr
    up = lambda a: a.astype(F32)
    return tile_call(name, lambda own, rv: (up(own) + up(rv[0])) + (up(rv[1]) + up(rv[2])), (nt,),
                     [(parts, (None, tr, cdim), lambda i, p: (p[0], i, 0)), (lands, (3, tr, cdim), lambda i, p: (0, i, 0))],
                     [((n_layers, 2 * rh, cdim), F32, (None, tr, cdim), lambda i, p: (layer, p[1] * nt + i, 0), None)],
                     prefetch=where, into=into)[0]


def adamw_fn(w, g, m, v):
    m = ADAM_B1 * m + (1.0 - ADAM_B1) * g
    v = ADAM_B2 * v + (1.0 - ADAM_B2) * jnp.square(g)
    m_hat = m / (1.0 - ADAM_B1 ** ADAM_STEP)
    v_hat = v / (1.0 - ADAM_B2 ** ADAM_STEP)
    delta = -ADAM_LR * (m_hat / (jnp.sqrt(v_hat) + ADAM_EPS) + ADAM_WD * w)
    return delta, m, v, g


def adamw(name, w, g, m, v):
    nl, r, cdim = w.shape
    tr = _rtile(r, 256)
    spec = ((None, tr, cdim), lambda l, i: (l, i, 0))
    return tile_call(name, adamw_fn, (nl, r // tr), [(a,) + spec for a in (w, g, m, v)],
                     [(w.shape, F32) + spec + (None,)] * 4)


def _as3(a):
    return _comm_rows(a.reshape(a.shape[0], -1, a.shape[-1]))


COMM_GROUPS = ((0, ("ffn1",)), (0, ("ffn1d",)), (0, ("mixer",)), (0, ("xattn", "ffn2")), (1, ("ffn1", "ffn1d", "mixer")),
               (1, ("xattn", "ffn2")))


def _pack_rows(vals):
    rows = []
    for v in vals:
        v = v.reshape(-1).astype(F32)
        pad = (-v.shape[0]) % 128
        rows.append(jnp.pad(v, (0, pad)).reshape(-1, 128))
    out = jnp.concatenate(rows, axis=0)
    return jnp.pad(out, ((0, (-out.shape[0]) % 8), (0, 0)))


def _unpack_rows(buf, like):
    outs, r = [], 0
    for a in like:
        n = a.size
        nr = -(-n // 128)
        outs.append(buf[r:r + nr].reshape(-1)[:n].reshape(a.shape))
        r += nr
    return outs


def kernel(x, mem, ffn1_norm, ffn1_w_gate, ffn1_w_up, ffn1_w_down, mix_norm, w_in, conv_w, a_log, dt_bias, dn_out_norm, pool_w, pool_scale, w_out, xattn_norm, mem_norm, xattn_wq, xattn_wkv, xattn_wo, ffn2_norm, ffn2_w_gate, ffn2_w_up, ffn2_w_down, final_norm, loss_target, m_ffn1_norm, m_ffn1_w_gate, m_ffn1_w_up, m_ffn1_w_down, m_mix_norm, m_w_in, m_conv_w, m_a_log, m_dt_bias, m_dn_out_norm, m_pool_w, m_pool_scale, m_w_out, m_xattn_norm, m_mem_norm, m_xattn_wq, m_xattn_wkv, m_xattn_wo, m_ffn2_norm, m_ffn2_w_gate, m_ffn2_w_up, m_ffn2_w_down, m_final_norm, v_ffn1_norm, v_ffn1_w_gate, v_ffn1_w_up, v_ffn1_w_down, v_mix_norm, v_w_in, v_conv_w, v_a_log, v_dt_bias, v_dn_out_norm, v_pool_w, v_pool_scale, v_w_out, v_xattn_norm, v_mem_norm, v_xattn_wq, v_xattn_wkv, v_xattn_wo, v_ffn2_norm, v_ffn2_w_gate, v_ffn2_w_up, v_ffn2_w_down, v_final_norm):
    given = dict(locals())
    w = {n: given[n] for n in WEIGHTS}
    m = {n: given["m_" + n] for n in WEIGHTS}
    v = {n: given["v_" + n] for n in WEIGHTS}
    where = jnp.stack([2 * lax.axis_index("x") + lax.axis_index("y"), lax.axis_index("c")]).astype(jnp.int32)
    placed = {}
    for n in SHARDED:
        for l, buf in enumerate(place_shards("place_" + n, _as3(w[n]), where, F32 if n == "conv_w" else BF16)):
            placed[(l, n)] = buf
    keys, groups = [], []
    for l, entries in COMM_GROUPS:
        groups.append([])
        for e in entries:
            for n in GROUPS[e]:
                groups[-1].append(len(keys))
                keys.append((l, n))
    sems, bufs, _ = gather_start([placed[k] for k in keys], groups)
    fetched = {}

    def fetch(l, entry, after):
        gi = [i for i, (gl, entries) in enumerate(COMM_GROUPS) if gl == l and entry in entries][0]
        if gi not in fetched:
            landed = gather_wait("gather_wait%d" % gi, [bufs[i] for i in groups[gi]], sems[gi], after)
            fetched[gi] = {keys[i][1]: a for i, a in zip(groups[gi], gather_swap("gather_swap%d" % gi, landed))}
        return fetched[gi]

    swapping, travelling = [], []

    def to_ici(after):
        tag, names, (ssem, rsem, mine, theirs) = swapping.pop(0)
        mine, theirs = pair_wait("pair_wait" + tag, ssem, rsem, mine, theirs, after)
        parts = [chip_partial("partial%s_%s" % (tag, n), g, r, where) for n, g, r in zip(names, mine, theirs)]
        ssem, rsem, parts, lands, token = reduce_start("reduce_start" + tag, parts, after)
        travelling.append((tag, names, (ssem, rsem, parts, lands)))
        return token

    def grads_done(l, part, gb):
        tag, names = "%d%d" % (l, part), GRAD_PARTS[part]
        ssem, rsem, mine, theirs, token = pair_start("pair_start" + tag, [gb[n] for n in names])
        swapping.append((tag, names, (ssem, rsem, mine, theirs)))
        return to_ici(token) if len(swapping) > 1 else token

    rep = {n: w[n] for n in REPLICATED}
    loss_row, dx, big, small, d_final = model_grads(x[0], mem[0], loss_target[0], fetch, rep, grads_done)
    last_start = to_ici(dx)
    n_layers = len(big)
    sums, out_g, out_d, out_m, out_v = {}, {}, {}, {}, {}

    def land(after):
        tag, names, (ssem, rsem, parts, lands) = travelling.pop(0)
        parts, lands = reduce_wait("reduce_wait" + tag, ssem, rsem, parts, lands, after)
        for n, p, r in zip(names, parts, lands):
            sums[n] = sum_chips("sum%s_%s" % (tag, n), p, r, where, int(tag[0]), n_layers, sums.get(n))

    def finish(part):
        names = GRAD_PARTS[part]
        for n, g in zip(names, share_halves("share_halves%d" % part, [sums[n] for n in names])):
            d_, m_, v_, g_ = adamw("adamw_" + n, _as3(w[n]), g, _as3(m[n]), _as3(v[n]))
            out_g[n], out_d[n], out_m[n], out_v[n] = (a.reshape(w[n].shape) for a in (g_, d_, m_, v_))
        return out_d[names[-1]]

    n_parts = len(GRAD_PARTS)
    for _ in range(len(travelling) - 2):
        land(last_start)
    for part in range(n_parts - 3):
        finish(part)
    land(finish(n_parts - 3))
    land(finish(n_parts - 2))
    done = finish(n_parts - 1)
    rep_names = [n for n in REPLICATED if n != "final_norm"]
    g_rep = {n: jnp.stack([small[l][n][0, :w[n].shape[1]] for l in range(len(small))]) for n in rep_names}
    g_rep["final_norm"] = d_final[0]
    like = [w[n] for n in REPLICATED] + [jnp.zeros((1,), F32)]
    summed = allreduce_small(_pack_rows([g_rep[n] for n in REPLICATED] + [loss_row[0, :1]]), done)
    pk = lambda tree: _pack_rows([tree[n] for n in REPLICATED] + [jnp.zeros((1,), F32)])
    wp, mp, vp = pk(w), pk(m), pk(v)
    dp, mp2, vp2, _ = adamw("adamw_small", wp[None], summed[None], mp[None], vp[None])
    for buf, dst in ((summed, out_g), (dp[0], out_d), (mp2[0], out_m), (vp2[0], out_v)):
        for n, a in zip(REPLICATED, _unpack_rows(buf, like)):
            dst[n] = a
    loss = _unpack_rows(summed, like)[-1][0]
    return (loss, dx[None], *[out_g[n] for n in WEIGHTS], *[out_d[n] for n in WEIGHTS],
            *[out_m[n] for n in WEIGHTS], *[out_v[n] for n in WEIGHTS])
```

```python
import functools

import jax
import jax.numpy as jnp
from jax import lax
from jax.experimental import pallas as pl
from jax.experimental.pallas import tpu as pltpu

F32, BF16 = jnp.float32, jnp.bfloat16
HI = lax.Precision.HIGHEST
MESH = pl.DeviceIdType.MESH

EPS = 1e-6
DN_HEADS = 8
HEAD_DIM = 128
X_HEADS = 4
POOL_WINDOWS = (2, 4, 8, 16)
CONV_WIDTH = 4
CHUNK = 64
N_SHARDS = 4
AB_PAD = 128
ADAM_LR, ADAM_B1, ADAM_B2, ADAM_EPS, ADAM_WD, ADAM_STEP = 0.001, 0.9, 0.999, 1e-08, 0.01, 10
VMEM_LIMIT = 56 << 20
ROW_TILE = 512


def _tile(n, pref, unit=128):
    best = None
    for t in range(unit, min(n, pref) + 1, unit):
        if n % t == 0:
            best = t
    return best if best is not None else n


def _params():
    return pltpu.CompilerParams(vmem_limit_bytes=VMEM_LIMIT)


def _split(a):
    a = a.astype(F32)
    head = a.astype(BF16)
    return head, (a - head.astype(F32)).astype(BF16)


def _dg(a, b, ca, cb, hi):
    dims = (((ca,), (cb,)), ((), ()))
    dot = lambda u, v: lax.dot_general(u, v, dims, preferred_element_type=F32)
    if hi:
        (a0, a1), (b0, b1) = _split(a), _split(b)
        return dot(a0, b0) + (dot(a0, b1) + dot(a1, b0))
    return dot(a.astype(BF16), b.astype(BF16))


@functools.partial(jax.custom_vjp, nondiff_argnums=(2, 3, 4))
def mmul(a, b, ca, cb, hi):
    return _dg(a, b, ca, cb, hi)


def _mmul_fwd(a, b, ca, cb, hi):
    return _dg(a, b, ca, cb, hi), (a, b)


def _mmul_bwd(ca, cb, hi, res, g):
    a, b = res
    da = _dg(g, b, 1, 1 - cb, hi) if ca == 1 else _dg(b, g, 1 - cb, 1, hi)
    db = _dg(a, g, 1 - ca, 0, hi) if cb == 0 else _dg(g, a, 0, 1 - ca, hi)
    return da.astype(a.dtype), db.astype(b.dtype)


mmul.defvjp(_mmul_fwd, _mmul_bwd)


@functools.partial(jax.custom_vjp, nondiff_argnums=(1,))
def shift_down(x, s):
    t = lax.broadcasted_iota(jnp.int32, x.shape, 0)
    return jnp.where(t >= s, pltpu.roll(x, s, 0), 0.0)


def _shift_up(x, s):
    n = x.shape[0]
    t = lax.broadcasted_iota(jnp.int32, x.shape, 0)
    return jnp.where(t < n - s, pltpu.roll(x, n - s, 0), 0.0)


shift_down.defvjp(lambda x, s: (shift_down(x, s), None), lambda s, _, g: (_shift_up(g, s),))


def sigmoid(x):
    return 0.5 * (jnp.tanh(0.5 * x) + 1.0)


def silu(x):
    return x * sigmoid(x)


@jax.custom_vjp
def softplus(x):
    u = jnp.exp(-jnp.abs(x))
    w = 1.0 + u
    log1p = jnp.where(w == 1.0, u, jnp.log(w) * u / jnp.where(w == 1.0, 1.0, w - 1.0))
    return jnp.maximum(x, 0.0) + log1p


softplus.defvjp(lambda x: (softplus(x), x), lambda x, g: (g * sigmoid(x),))


def rms(x, g):
    x = x.astype(F32)
    return x * lax.rsqrt(jnp.mean(x * x, axis=-1, keepdims=True) + EPS) * g


def swiglu(gate, up):
    return silu(gate) * up


def vjp_of(fn, n_in, diff, has_pids=False):
    def g(*args):
        pids = None
        if has_pids:
            pids, args = args[0], args[1:]
        ins, cots = list(args[:n_in]), args[n_in:]

        def f(*d):
            full = list(ins)
            for i, v in zip(diff, d):
                full[i] = v
            return fn(pids, *full) if has_pids else fn(*full)

        out, pull = jax.vjp(f, *[ins[i].astype(F32) for i in diff])
        if isinstance(out, (tuple, list)):
            return pull(tuple(c.astype(o.dtype) for c, o in zip(cots, out)))
        return pull(cots[0].astype(out.dtype))
    return g


def tile_call(name, fn, grid, ins, outs, with_pids=False, prefetch=None, into=None):
    n_in = len(ins)
    n_into = 0 if into is None else 1

    def body(*refs):
        if prefetch is not None:
            refs = refs[1:]
        pids = tuple(pl.program_id(a) for a in range(len(grid)))
        vals = [r[...] for r in refs[:n_in]]
        res = fn(pids, *vals) if with_pids else fn(*vals)
        if not isinstance(res, (tuple, list)):
            res = (res,)
        for r, o, spec in zip(res, refs[n_in + n_into:], outs):
            acc = spec[4]
            if acc is None:
                o[...] = r.astype(o.dtype)
            else:
                first = functools.reduce(jnp.logical_and, [pids[a] == 0 for a in acc])

                @pl.when(first)
                def _():
                    o[...] = r.astype(o.dtype)

                @pl.when(jnp.logical_not(first))
                def _():
                    o[...] += r.astype(o.dtype)

    in_specs = [pl.BlockSpec(b, im) for _, b, im in ins] + [pl.BlockSpec(memory_space=pl.ANY)] * n_into
    out_specs = [pl.BlockSpec(s[2], s[3]) for s in outs]
    out_shape = [jax.ShapeDtypeStruct(s[0], s[1]) for s in outs]
    args = [a for a, _, _ in ins] + ([] if into is None else [into])
    if prefetch is None:
        return pl.pallas_call(body, name=name, grid=grid, in_specs=in_specs, out_specs=out_specs, out_shape=out_shape,
                              input_output_aliases={n_in: 0} if n_into else {}, compiler_params=_params())(*args)
    spec = pltpu.PrefetchScalarGridSpec(num_scalar_prefetch=1, grid=grid, in_specs=in_specs, out_specs=out_specs)
    return pl.pallas_call(body, name=name, grid_spec=spec, out_shape=out_shape,
                          input_output_aliases={n_in + 1: 0} if n_into else {}, compiler_params=_params())(prefetch, *args)


def mm_call(name, grid, ins, pairs, n_acc, acc_shape, outs, epilogue, extras=(), after=None):
    n_in, n_ex, nk = len(ins), len(extras), grid[2]
    n_dep = 0 if after is None else 1

    def finish(accs, ex_refs, out_refs):
        res = epilogue(accs, *[r[...] for r in ex_refs])
        if not isinstance(res, (tuple, list)):
            res = (res,)
        for r, o in zip(res, out_refs):
            o[...] = r.astype(o.dtype)

    def body(*refs):
        in_refs, ex_refs = refs[:n_in], refs[n_in:n_in + n_ex]
        refs = refs[n_in + n_ex + n_dep:]
        out_refs, accs = refs[:len(outs)], refs[len(outs):]
        if nk == 1:
            vals = [None] * n_acc
            for ia, ib, ca, cb, ai in pairs:
                d = _dg(in_refs[ia][...], in_refs[ib][...], ca, cb, False)
                vals[ai] = d if vals[ai] is None else vals[ai] + d
            finish(vals, ex_refs, out_refs)
            return
        k = pl.program_id(2)

        @pl.when(k == 0)
        def _():
            for a in accs:
                a[...] = jnp.zeros_like(a)

        for ia, ib, ca, cb, ai in pairs:
            accs[ai][...] += _dg(in_refs[ia][...], in_refs[ib][...], ca, cb, False)

        @pl.when(k == nk - 1)
        def _():
            finish([a[...] for a in accs], ex_refs, out_refs)

    return pl.pallas_call(
        body, name=name, grid=grid,
        in_specs=[pl.BlockSpec(b, im) for _, b, im in list(ins) + list(extras)] + [pl.BlockSpec(memory_space=pl.ANY)] * n_dep,
        out_specs=[pl.BlockSpec(s[2], s[3]) for s in outs],
        out_shape=[jax.ShapeDtypeStruct(s[0], s[1]) for s in outs],
        scratch_shapes=[pltpu.VMEM(acc_shape, F32) for _ in range(n_acc if nk > 1 else 0)],
        compiler_params=_params(),
    )(*[a for a, _, _ in list(ins) + list(extras)], *([] if after is None else [after]))


MM_VMEM_BUDGET = 40 << 20


def _mm_tiles(m, n, kk, sa, sb, so, has_res):
    tk = _tile(kk, 2048)
    best = None
    for tm in (1024, 512, 256, 128):
        for tn in (1024, 512, 256, 128):
            tm_, tn_ = _tile(m, tm), _tile(n, tn)
            need = 2 * (tm_ * tk * sa + tk * tn_ * sb + tm_ * tn_ * so) + (tm_ * tn_ * 4 if tk < kk else 0)
            need += 2 * tm_ * tn_ * 4 if has_res else 0
            if need <= MM_VMEM_BUDGET and (best is None or tm_ * tn_ > best[0] * best[1]):
                best = (tm_, tn_)
    return best + (tk,)


def mm2(name, a, b, ca, cb, res=None, scale=None, out_dtype=F32, after=None):
    m, kk, n = a.shape[1 - ca], a.shape[ca], b.shape[1 - cb]
    tm, tn, tk = _mm_tiles(m, n, kk, a.dtype.itemsize, b.dtype.itemsize, jnp.dtype(out_dtype).itemsize, res is not None)
    a_spec = ((tm, tk), lambda i, j, k: (i, k)) if ca == 1 else ((tk, tm), lambda i, j, k: (k, i))
    b_spec = ((tk, tn), lambda i, j, k: (k, j)) if cb == 0 else ((tn, tk), lambda i, j, k: (j, k))
    extras = [] if res is None else [(res, (tm, tn), lambda i, j, k: (i, j))]

    def epi(accs, *ex):
        r = accs[0] if scale is None else accs[0] * scale
        return r + ex[0] if ex else r

    return mm_call(name, (m // tm, n // tn, kk // tk), [(a,) + a_spec, (b,) + b_spec], [(0, 1, ca, cb, 0)], 1, (tm, tn),
                   [((m, n), out_dtype, (tm, tn), lambda i, j, k: (i, j))], epi, extras, after)[0]


def rows_call(name, fn, rows, consts, outs, acc_outs=(), tr=ROW_TILE):
    s = rows[0].shape[0]
    tr = _tile(s, tr, 8)
    ins = [(r, (tr, r.shape[1]), lambda i: (i, 0)) for r in rows]
    ins += [(c, c.shape, (lambda nd: (lambda i: (0,) * nd))(c.ndim)) for c in consts]
    o = [((s, c), dt, (tr, c), lambda i: (i, 0), None) for c, dt in outs]
    o += [(shp, F32, shp, (lambda nd: (lambda i: (0,) * nd))(len(shp)), (0,)) for shp in acc_outs]
    return tile_call(name, fn, (s // tr,), ins, o)


def _lane_pick(x, h):
    lane = lax.broadcasted_iota(jnp.int32, x.shape, x.ndim - 1)
    return jnp.sum(jnp.where(lane == h, x, 0.0), axis=-1, keepdims=True)


def gateprep_fn(ab, alog, dtb):
    t = ab.shape[0]
    gs, bs = [], []
    for h in range(DN_HEADS):
        a_h = _lane_pick(ab, h)
        b_h = _lane_pick(ab, DN_HEADS + h)
        g_h = -jnp.exp(_lane_pick(alog, h)) * softplus(a_h + _lane_pick(dtb, h))
        gs.append(jnp.broadcast_to(g_h, (t, HEAD_DIM)))
        bs.append(jnp.broadcast_to(sigmoid(b_h), (t, HEAD_DIM)))
    return jnp.concatenate(gs, axis=1), jnp.concatenate(bs, axis=1)


def conv_fn(pids, x, w):
    kind = pids[0] // DN_HEADS
    y = x * w[CONV_WIDTH - 1:CONV_WIDTH]
    for i in range(CONV_WIDTH - 1):
        y = y + shift_down(x, CONV_WIDTH - 1 - i) * w[i:i + 1]
    y = silu(y)
    n = y * lax.rsqrt(jnp.sum(y * y, axis=-1, keepdims=True) + EPS)
    n = n * jnp.where(kind == 0, HEAD_DIM ** -0.5, 1.0)
    return jnp.where(kind == 2, y, n)


def unit_lower_inverse(a):
    c = a.shape[0]
    r = lax.broadcasted_iota(jnp.int32, (c, c), 0)
    cc = lax.broadcasted_iota(jnp.int32, (c, c), 1)
    x = -a
    t = jnp.where(r == cc, 1.0, 0.0) + x
    p = 2
    while p < c:
        x = _dg(x, x, 1, 0, True)
        t = t + _dg(t, x, 1, 0, True)
        p *= 2
    return t


@jax.custom_vjp
def known_inverse(a, t):
    return t


known_inverse.defvjp(lambda a, t: (t, t),
                     lambda t, g: (-_dg(_dg(t, g, 0, 0, True), t, 1, 1, True), jnp.zeros_like(t)))


def intra_head(q, k, v, g, b, t_known):
    c = q.shape[0]
    r = lax.broadcasted_iota(jnp.int32, (c, c), 0)
    cc = lax.broadcasted_iota(jnp.int32, (c, c), 1)
    tril = (r >= cc).astype(F32)
    gc = mmul(tril, g, 1, 0, True)
    m = gc[:, :c]
    decay = jnp.exp(jnp.where(r >= cc, m - m.T, -1e30))
    kb = k * b
    a = jnp.where(r > cc, mmul(kb, k, 1, 1, False) * decay, 0.0)
    t = unit_lower_inverse(a) if t_known is None else known_inverse(a, t_known)
    e = jnp.exp(gc)
    u = mmul(t, v * b, 1, 0, True)
    w = mmul(t, kb * e, 1, 0, True)
    qk = mmul(q, k, 1, 1, False) * decay
    gl = gc[c - 1:c, :]
    kd = k * jnp.exp(gl - gc)
    outs = (u, w, qk, q * e, kd, jnp.broadcast_to(jnp.exp(gl), (8, HEAD_DIM)))
    return outs + (t,) if t_known is None else outs


def _heads(x, h):
    return x[:, h * HEAD_DIM:(h + 1) * HEAD_DIM]


def intra_fn(q, k, v, g, b):
    outs = [intra_head(_heads(q, h), _heads(k, h), _heads(v, h), _heads(g, h), _heads(b, h), None) for h in range(DN_HEADS)]
    cat = lambda i: jnp.concatenate([o[i] for o in outs], axis=1)
    stack = lambda i: jnp.stack([o[i] for o in outs], axis=0)
    return cat(0), cat(1), stack(2), cat(3), cat(4), cat(5), stack(6)


def intra_bwd_fn(q, k, v, g, b, tinv, du, dw, dqk, dqd, dkd, dgl):
    res = []
    for h in range(DN_HEADS):
        hs = lambda x: _heads(x, h)
        res.append(vjp_of(intra_head, 6, (0, 1, 2, 3, 4))(hs(q), hs(k), hs(v), hs(g), hs(b), tinv[h],
                                                         hs(du), hs(dw), dqk[h], hs(dqd), hs(dkd), hs(dgl)))
    cat = lambda i: jnp.concatenate([r[i] for r in res], axis=1)
    return jnp.concatenate([cat(0), cat(1), cat(2)], axis=1), cat(3), cat(4)


def scan_step(s, u, w, qk, qd, kd, gl):
    v_new = u - mmul(w, s, 1, 0, False)
    o = mmul(qd, s, 1, 0, False) + mmul(qk, v_new, 1, 0, False)
    return s * gl[0:1, :] + mmul(kd, v_new, 0, 0, False), o


def outgate_fn(o, z, g):
    return rms(o, g) * silu(z)


def pool_fn(pids, p):
    gid = pids[0]
    s = p.shape[0]
    t1 = (lax.broadcasted_iota(jnp.int32, p.shape, 0) + 1).astype(F32)
    acc, win, out = p, 1, None
    for gi, target in enumerate(POOL_WINDOWS):
        while win < target:
            acc = acc + shift_down(acc, win)
            win *= 2
        cand = acc / jnp.minimum(t1, float(target))
        out = cand if out is None else jnp.where(gid == gi, cand, out)
    return out - p


def poolmix_fn(pooled, pw, scale):
    return mmul(pooled, pw, 1, 0, False) * scale


def attn_fn(q, k, v):
    s = mmul(q, k, 1, 1, False) * (q.shape[1] ** -0.5)
    s = s - jnp.max(s, axis=-1, keepdims=True)
    e = jnp.exp(s)
    p = e / jnp.sum(e, axis=-1, keepdims=True)
    return mmul(p, v, 1, 0, False)


def rms_fwd(name, x, g):
    return rows_call(name, lambda a, b: rms(a, b), [x], [g], [(x.shape[1], BF16)])[0]


def rms_bwd(name, x, g, dy, dres):
    def fn(a, d, r, b):
        dx, dg = vjp_of(rms, 2, (0, 1))(a, b, d)
        return dx + r, dx + r, dg
    return rows_call(name, fn, [x, dy, dres], [g], [(x.shape[1], F32), (x.shape[1], BF16)], [g.shape])


def ffn_fwd(tag, x, g, wg, wu, wd):
    s, d = x.shape
    fj = wg.shape[-1]
    f = N_SHARDS * fj
    xn = rms_fwd(tag + "_norm", x, g)
    tm = _tile(s, 256)
    w_spec = ((None, d, fj), lambda j, i, k: (j, 0, 0))
    o_spec = ((tm, fj), lambda j, i, k: (i, j))
    gate, up, act = mm_call(
        tag + "_gu", (N_SHARDS, s // tm, 1),
        [(xn, (tm, d), lambda j, i, k: (i, 0)), (wg,) + w_spec, (wu,) + w_spec],
        [(0, 1, 1, 0, 0), (0, 2, 1, 0, 1)], 2, (tm, fj),
        [((s, f), F32) + o_spec, ((s, f), F32) + o_spec, ((s, f), BF16) + o_spec],
        lambda accs: (accs[0], accs[1], swiglu(accs[0], accs[1])))
    if callable(wd):
        wd = wd(act)
    tm, tn = _tile(s, 1024), _tile(d, 1024)
    out = mm_call(
        tag + "_down", (s // tm, d // tn, N_SHARDS),
        [(act, (tm, fj), lambda i, j, k: (i, k)), (wd, (None, fj, tn), lambda i, j, k: (k, 0, j))],
        [(0, 1, 1, 0, 0)], 1, (tm, tn),
        [((s, d), F32, (tm, tn), lambda i, j, k: (i, j))],
        lambda accs, r: r + 0.5 * accs[0], [(x, (tm, tn), lambda i, j, k: (i, j))])[0]
    return out, (x, xn, gate, up, act), wd


def ffn_bwd(tag, saved, g, wg, wu, wd, dout, dout_b, after=None):
    x, xn, gate, up, act = saved
    s, d = x.shape
    fj = wg.shape[-1]
    f = N_SHARDS * fj
    tm = _tile(s, 512)
    o_spec = ((tm, fj), lambda j, i, k: (i, j))

    def epi(accs, ga, u):
        dgate, dup = vjp_of(swiglu, 2, (0, 1))(ga, u, 0.5 * accs[0])
        return dgate, dup

    dgate, dup = mm_call(
        tag + "_dact", (N_SHARDS, s // tm, 1),
        [(dout_b, (tm, d), lambda j, i, k: (i, 0)), (wd, (None, fj, d), lambda j, i, k: (j, 0, 0))],
        [(0, 1, 1, 1, 0)], 1, (tm, fj),
        [((s, f), BF16) + o_spec, ((s, f), BF16) + o_spec], epi,
        [(gate,) + o_spec, (up,) + o_spec], after=after)
    tn = _tile(d, 1024)
    dwd = mm_call(
        tag + "_dwd", (N_SHARDS, d // tn, 1),
        [(act, (s, fj), lambda j, i, k: (0, j)), (dout_b, (s, tn), lambda j, i, k: (0, i))],
        [(0, 1, 0, 0, 0)], 1, (fj, tn),
        [((N_SHARDS, fj, d), F32, (None, fj, tn), lambda j, i, k: (j, 0, i))],
        lambda accs: 0.5 * accs[0])[0]
    td = _tile(d, 512)
    g_spec = ((s, fj), lambda j, i, k: (0, j))
    w_out = ((N_SHARDS, d, fj), F32, (None, td, fj), lambda j, i, k: (j, i, 0))
    dwg, dwu = mm_call(
        tag + "_dwgu", (N_SHARDS, d // td, 1),
        [(xn, (s, td), lambda j, i, k: (0, i)), (dgate,) + g_spec, (dup,) + g_spec],
        [(0, 1, 0, 0, 0), (0, 2, 0, 0, 1)], 2, (td, fj), [w_out, w_out], lambda accs: (accs[0], accs[1]))
    tm, tn = _tile(s, 1024), _tile(d, 1024)
    a_spec = ((tm, fj), lambda i, j, k: (i, k))
    wt_spec = ((None, tn, fj), lambda i, j, k: (k, j, 0))
    dxn = mm_call(
        tag + "_dxn", (s // tm, d // tn, N_SHARDS),
        [(dgate,) + a_spec, (wg,) + wt_spec, (dup,) + a_spec, (wu,) + wt_spec],
        [(0, 1, 1, 1, 0), (2, 3, 1, 1, 0)], 1, (tm, tn),
        [((s, d), F32, (tm, tn), lambda i, j, k: (i, j))], lambda accs: accs[0])[0]
    dx, dx_b, dg = rms_bwd(tag + "_dnorm", x, g, dxn, dout)
    return dx, dx_b, dg, dwg, dwu, dwd


def mixer_fwd(tag, h, wts):
    s, d = h.shape
    dnw = DN_HEADS * HEAD_DIM
    pw_ = d - dnw
    gdim = pw_ // len(POOL_WINDOWS)
    nc = s // CHUNK
    hn = rms_fwd(tag + "_norm", h, wts["mix_norm"])
    qkv = mm2(tag + "_qkv", hn, wts["w_qkv"], 1, 0)
    z = mm2(tag + "_z", hn, wts["w_z"], 1, 0)
    ab = mm2(tag + "_ab", hn, wts["w_ab"], 1, 0)
    p = mm2(tag + "_p", hn, wts["w_p"], 1, 0)
    qkvn = tile_call(tag + "_conv", conv_fn, (3 * DN_HEADS,),
                     [(qkv, (s, HEAD_DIM), lambda i: (0, i)), (wts["conv_w"], (CONV_WIDTH, HEAD_DIM), lambda i: (0, i))],
                     [((s, 3 * dnw), F32, (s, HEAD_DIM), lambda i: (0, i), None)], with_pids=True)[0]
    g_bc, b_bc = rows_call(tag + "_gates", gateprep_fn, [ab], [wts["a_log"], wts["dt_bias"]], [(dnw, F32), (dnw, F32)])
    cw = (CHUNK, dnw)
    sq = ((DN_HEADS, s, CHUNK), F32, (DN_HEADS, CHUNK, CHUNK), lambda n: (0, n, 0), None)
    u, w, qk, qd, kd, gl, tinv = tile_call(
        tag + "_intra", intra_fn, (nc,),
        [(qkvn, cw, lambda n: (n, 0)), (qkvn, cw, lambda n: (n, 1)), (qkvn, cw, lambda n: (n, 2)),
         (g_bc, cw, lambda n: (n, 0)), (b_bc, cw, lambda n: (n, 0))],
        [((s, dnw), F32, cw, lambda n: (n, 0), None), ((s, dnw), F32, cw, lambda n: (n, 0), None), sq,
         ((s, dnw), F32, cw, lambda n: (n, 0), None), ((s, dnw), F32, cw, lambda n: (n, 0), None),
         ((nc * 8, dnw), F32, (8, dnw), lambda n: (n, 0), None), sq])
    o, states = scan_fwd(tag + "_scan", u, w, qk, qd, kd, gl)
    th = _tile(s, 2 * ROW_TILE, 8)
    y_dn = tile_call(
        tag + "_outgate", outgate_fn, (DN_HEADS, s // th),
        [(o, (th, HEAD_DIM), lambda hh, i: (i, hh)), (z, (th, HEAD_DIM), lambda hh, i: (i, hh)),
         (wts["dn_out_norm"], (1, HEAD_DIM), lambda hh, i: (0, 0))],
        [((s, dnw), BF16, (th, HEAD_DIM), lambda hh, i: (i, hh), None)])[0]
    ng = len(POOL_WINDOWS)
    pooled = tile_call(tag + "_pool", pool_fn, (ng,), [(p, (s, gdim), lambda i: (0, i))],
                       [((s, pw_), BF16, (s, gdim), lambda i: (0, i), None)], with_pids=True)[0]
    tp = _tile(s, 512)
    y_pool = tile_call(
        tag + "_poolmix", poolmix_fn, (ng, s // tp),
        [(pooled, (tp, gdim), lambda gi, i: (i, gi)), (wts["pool_w"], (None, gdim, gdim), lambda gi, i: (gi, 0, 0)),
         (wts["pool_scale"], (1, gdim), lambda gi, i: (0, gi))],
        [((s, pw_), BF16, (tp, gdim), lambda gi, i: (i, gi), None)])[0]
    h1 = mm2(tag + "_out_a", y_dn, wts["w_out_a"], 1, 0, res=h)
    h2 = mm2(tag + "_out_b", y_pool, wts["w_out_b"], 1, 0, res=h1)
    saved = (h, hn, qkv, z, ab, p, qkvn, g_bc, b_bc, u, w, qk, qd, kd, gl, tinv, o, states, y_dn, pooled, y_pool)
    return h2, saved


def scan_fwd(name, u, w, qk, qd, kd, gl):
    s, dnw = u.shape
    nc = s // CHUNK
    cw = (CHUNK, dnw)

    def body(u_r, w_r, qk_r, qd_r, kd_r, gl_r, o_r, st_r, state):
        @pl.when(pl.program_id(0) == 0)
        def _():
            state[...] = jnp.zeros_like(state)

        st_r[...] = state[...]
        outs = []
        for h in range(DN_HEADS):
            hs = slice(h * HEAD_DIM, (h + 1) * HEAD_DIM)
            s_new, o_h = scan_step(state[hs, :], u_r[:, hs], w_r[:, hs], qk_r[h], qd_r[:, hs], kd_r[:, hs], gl_r[:, hs])
            state[hs, :] = s_new
            outs.append(o_h)
        o_r[...] = jnp.concatenate(outs, axis=1)

    row = lambda n: (n, 0)
    return pl.pallas_call(
        body, name=name, grid=(nc,),
        in_specs=[pl.BlockSpec(cw, row), pl.BlockSpec(cw, row), pl.BlockSpec((DN_HEADS, CHUNK, CHUNK), lambda n: (0, n, 0)),
                  pl.BlockSpec(cw, row), pl.BlockSpec(cw, row), pl.BlockSpec((8, dnw), row)],
        out_specs=[pl.BlockSpec(cw, row), pl.BlockSpec((None, dnw, HEAD_DIM), lambda n: (n, 0, 0))],
        out_shape=[jax.ShapeDtypeStruct((s, dnw), F32), jax.ShapeDtypeStruct((nc, dnw, HEAD_DIM), F32)],
        scratch_shapes=[pltpu.VMEM((dnw, HEAD_DIM), F32)],
        compiler_params=_params(),
    )(u, w, qk, qd, kd, gl)


def scan_bwd(name, states, u, w, qk, qd, kd, gl, do):
    s, dnw = u.shape
    nc = s // CHUNK
    cw = (CHUNK, dnw)

    def body(st_r, u_r, w_r, qk_r, qd_r, kd_r, gl_r, do_r, du_r, dw_r, dqk_r, dqd_r, dkd_r, dgl_r, dstate):
        @pl.when(pl.program_id(0) == 0)
        def _():
            dstate[...] = jnp.zeros_like(dstate)

        res = []
        for h in range(DN_HEADS):
            hs = slice(h * HEAD_DIM, (h + 1) * HEAD_DIM)
            r = vjp_of(scan_step, 7, tuple(range(7)))(
                st_r[hs, :], u_r[:, hs], w_r[:, hs], qk_r[h], qd_r[:, hs], kd_r[:, hs], gl_r[:, hs],
                dstate[hs, :], do_r[:, hs])
            dstate[hs, :] = r[0]
            res.append(r)
        cat = lambda i: jnp.concatenate([r[i] for r in res], axis=1)
        du_r[...] = cat(1)
        dw_r[...] = cat(2)
        dqk_r[...] = jnp.stack([r[3] for r in res], axis=0)
        dqd_r[...] = cat(4)
        dkd_r[...] = cat(5)
        dgl_r[...] = cat(6)

    row = lambda n: (nc - 1 - n, 0)
    qk_spec = pl.BlockSpec((DN_HEADS, CHUNK, CHUNK), lambda n: (0, nc - 1 - n, 0))
    return pl.pallas_call(
        body, name=name, grid=(nc,),
        in_specs=[pl.BlockSpec((None, dnw, HEAD_DIM), lambda n: (nc - 1 - n, 0, 0)), pl.BlockSpec(cw, row), pl.BlockSpec(cw, row),
                  qk_spec, pl.BlockSpec(cw, row), pl.BlockSpec(cw, row), pl.BlockSpec((8, dnw), row), pl.BlockSpec(cw, row)],
        out_specs=[pl.BlockSpec(cw, row), pl.BlockSpec(cw, row), qk_spec, pl.BlockSpec(cw, row), pl.BlockSpec(cw, row),
                   pl.BlockSpec((8, dnw), row)],
        out_shape=[jax.ShapeDtypeStruct((s, dnw), F32), jax.ShapeDtypeStruct((s, dnw), F32),
                   jax.ShapeDtypeStruct((DN_HEADS, s, CHUNK), F32), jax.ShapeDtypeStruct((s, dnw), F32),
                   jax.ShapeDtypeStruct((s, dnw), F32), jax.ShapeDtypeStruct((nc * 8, dnw), F32)],
        scratch_shapes=[pltpu.VMEM((dnw, HEAD_DIM), F32)],
        compiler_params=_params(),
    )(states, u, w, qk, qd, kd, gl, do)


def mixer_bwd(tag, saved, wts, dout, dout_b, after=None):
    h, hn, qkv, z, ab, p, qkvn, g_bc, b_bc, u, w, qk, qd, kd, gl, tinv, o, states, y_dn, pooled, y_pool = saved
    s, d = h.shape
    dnw = DN_HEADS * HEAD_DIM
    pw_ = d - dnw
    ng = len(POOL_WINDOWS)
    gdim = pw_ // ng
    nc = s // CHUNK
    gr = {}
    d_ydn = mm2(tag + "_dydn", dout_b, wts["w_out_a"], 1, 1, after=after)
    d_ypool = mm2(tag + "_dypool", dout_b, wts["w_out_b"], 1, 1)
    gr["w_out_a"] = mm2(tag + "_dwout_a", y_dn, dout_b, 0, 0)
    gr["w_out_b"] = mm2(tag + "_dwout_b", y_pool, dout_b, 0, 0)
    tp = _tile(s, 512)
    d_pooled, gr["pool_w"], gr["pool_scale"] = tile_call(
        tag + "_dpoolmix", vjp_of(poolmix_fn, 3, (0, 1, 2)), (ng, s // tp),
        [(pooled, (tp, gdim), lambda gi, i: (i, gi)), (wts["pool_w"], (None, gdim, gdim), lambda gi, i: (gi, 0, 0)),
         (wts["pool_scale"], (1, gdim), lambda gi, i: (0, gi)), (d_ypool, (tp, gdim), lambda gi, i: (i, gi))],
        [((s, pw_), F32, (tp, gdim), lambda gi, i: (i, gi), None),
         ((ng, gdim, gdim), F32, (None, gdim, gdim), lambda gi, i: (gi, 0, 0), (1,)),
         ((1, pw_), F32, (1, gdim), lambda gi, i: (0, gi), (1,))])
    d_p = tile_call(tag + "_dpool", vjp_of(pool_fn, 1, (0,), True), (ng,),
                    [(p, (s, gdim), lambda i: (0, i)), (d_pooled, (s, gdim), lambda i: (0, i))],
                    [((s, pw_), BF16, (s, gdim), lambda i: (0, i), None)], with_pids=True)[0]
    hb = (_tile(s, 2 * ROW_TILE, 8), HEAD_DIM)
    d_o, d_z, gr["dn_out_norm"] = tile_call(
        tag + "_doutgate", vjp_of(outgate_fn, 3, (0, 1, 2)), (DN_HEADS, s // hb[0]),
        [(o, hb, lambda hh, i: (i, hh)), (z, hb, lambda hh, i: (i, hh)), (wts["dn_out_norm"], (1, HEAD_DIM), lambda hh, i: (0, 0)),
         (d_ydn, hb, lambda hh, i: (i, hh))],
        [((s, dnw), F32, hb, lambda hh, i: (i, hh), None), ((s, dnw), BF16, hb, lambda hh, i: (i, hh), None),
         ((1, HEAD_DIM), F32, (1, HEAD_DIM), lambda hh, i: (0, 0), (0, 1))])
    du, dw, dqk, dqd, dkd, dgl = scan_bwd(tag + "_dscan", states, u, w, qk, qd, kd, gl, d_o)
    cw = (CHUNK, dnw)
    row = lambda n: (n, 0)
    d_qkvn, dg_bc, db_bc = tile_call(
        tag + "_dintra", intra_bwd_fn, (nc,),
        [(qkvn, cw, lambda n: (n, 0)), (qkvn, cw, lambda n: (n, 1)), (qkvn, cw, lambda n: (n, 2)),
         (g_bc, cw, row), (b_bc, cw, row), (tinv, (DN_HEADS, CHUNK, CHUNK), lambda n: (0, n, 0)), (du, cw, row), (dw, cw, row),
         (dqk, (DN_HEADS, CHUNK, CHUNK), lambda n: (0, n, 0)), (dqd, cw, row), (dkd, cw, row), (dgl, (8, dnw), row)],
        [((s, 3 * dnw), F32, (CHUNK, 3 * dnw), row, None)] + [((s, dnw), F32, cw, row, None)] * 2)
    d_ab, gr["a_log"], gr["dt_bias"] = rows_call(
        tag + "_dgates", lambda a, dg, db, al, dt: vjp_of(gateprep_fn, 3, (0, 1, 2))(a, al, dt, dg, db),
        [ab, dg_bc, db_bc], [wts["a_log"], wts["dt_bias"]], [(AB_PAD, BF16)], [(1, AB_PAD), (1, AB_PAD)])
    d_qkv, gr["conv_w"] = tile_call(
        tag + "_dconv", vjp_of(conv_fn, 2, (0, 1), True), (3 * DN_HEADS,),
        [(qkv, (s, HEAD_DIM), lambda i: (0, i)), (wts["conv_w"], (CONV_WIDTH, HEAD_DIM), lambda i: (0, i)),
         (d_qkvn, (s, HEAD_DIM), lambda i: (0, i))],
        [((s, 3 * dnw), BF16, (s, HEAD_DIM), lambda i: (0, i), None),
         ((CONV_WIDTH, 3 * dnw), F32, (CONV_WIDTH, HEAD_DIM), lambda i: (0, i), None)], with_pids=True)
    gr["w_qkv"] = mm2(tag + "_dwqkv", hn, d_qkv, 0, 0)
    gr["w_z"] = mm2(tag + "_dwz", hn, d_z, 0, 0)
    gr["w_ab"] = mm2(tag + "_dwab", hn, d_ab, 0, 0)
    gr["w_p"] = mm2(tag + "_dwp", hn, d_p, 0, 0)
    d_hn = mm2(tag + "_dhn1", d_qkv, wts["w_qkv"], 1, 1)
    d_hn = mm2(tag + "_dhn2", d_z, wts["w_z"], 1, 1, res=d_hn)
    d_hn = mm2(tag + "_dhn3", d_ab, wts["w_ab"], 1, 1, res=d_hn)
    d_hn = mm2(tag + "_dhn4", d_p, wts["w_p"], 1, 1, res=d_hn)
    dh, dh_b, gr["mix_norm"] = rms_bwd(tag + "_dnorm", h, wts["mix_norm"], d_hn, dout)
    return dh, dh_b, gr


def xattn_fwd(tag, h, memn, wts):
    s, d = h.shape
    m = memn.shape[0]
    dh_ = d // X_HEADS
    hn = rms_fwd(tag + "_norm", h, wts["xattn_norm"])
    q = mm2(tag + "_q", hn, wts["wq"], 1, 0, out_dtype=BF16)
    wkv = wts["wkv"]
    nj = wkv.shape[2]
    kv = mm_call(tag + "_kv", (1, N_SHARDS, 1), [(memn, (m, d), lambda i, j, k: (0, 0)), (wkv, (None, d, nj), lambda i, j, k: (j, 0, 0))],
                 [(0, 1, 1, 0, 0)], 1, (m, nj), [((m, 2 * d), BF16, (m, nj), lambda i, j, k: (0, j))], lambda accs: accs[0])[0]
    tq = _tile(s, 512)
    o = tile_call(
        tag + "_attn", attn_fn, (X_HEADS, s // tq),
        [(q, (tq, dh_), lambda hh, i: (i, hh)), (kv, (m, dh_), lambda hh, i: (0, hh)), (kv, (m, dh_), lambda hh, i: (0, X_HEADS + hh))],
        [((s, d), BF16, (tq, dh_), lambda hh, i: (i, hh), None)])[0]
    out = mm2(tag + "_o", o, wts["wo"], 1, 0, res=h)
    return out, (h, hn, q, kv, o)


def xattn_bwd(tag, saved, memn, mem, wts, dout, dout_b, after=None):
    h, hn, q, kv, o = saved
    s, d = h.shape
    m = memn.shape[0]
    dh_ = d // X_HEADS
    gr = {}
    d_o = mm2(tag + "_do", dout_b, wts["wo"], 1, 1, out_dtype=BF16, after=after)
    gr["wo"] = mm2(tag + "_dwo", o, dout_b, 0, 0)
    tq = _tile(s, 512)
    dq, dk, dv = tile_call(
        tag + "_dattn", vjp_of(attn_fn, 3, (0, 1, 2)), (X_HEADS, s // tq),
        [(q, (tq, dh_), lambda hh, i: (i, hh)), (kv, (m, dh_), lambda hh, i: (0, hh)), (kv, (m, dh_), lambda hh, i: (0, X_HEADS + hh)),
         (d_o, (tq, dh_), lambda hh, i: (i, hh))],
        [((s, d), BF16, (tq, dh_), lambda hh, i: (i, hh), None),
         ((m, d), F32, (m, dh_), lambda hh, i: (0, hh), (1,)), ((m, d), F32, (m, dh_), lambda hh, i: (0, hh), (1,))])
    dkv = jnp.concatenate([dk, dv], axis=1).astype(BF16)
    gr["wq"] = mm2(tag + "_dwq", hn, dq, 0, 0)
    wkv = wts["wkv"]
    nj = wkv.shape[2]
    td, tn = _tile(d, 1024), _tile(d, 1024)
    gr["wkv"] = mm_call(tag + "_dwkv", (d // td, N_SHARDS, 1),
                        [(memn, (m, td), lambda i, j, k: (0, i)), (dkv, (m, nj), lambda i, j, k: (0, j))], [(0, 1, 0, 0, 0)], 1, (td, nj),
                        [((N_SHARDS, d, nj), F32, (None, td, nj), lambda i, j, k: (j, i, 0))], lambda accs: accs[0])[0]
    d_memn = mm_call(tag + "_dmemn", (1, d // tn, N_SHARDS),
                     [(dkv, (m, nj), lambda i, j, k: (0, k)), (wkv, (None, tn, nj), lambda i, j, k: (k, j, 0))], [(0, 1, 1, 1, 0)], 1, (m, tn),
                     [((m, d), F32, (m, tn), lambda i, j, k: (0, j))], lambda accs: accs[0])[0]
    gr["mem_norm"] = rows_call(
        tag + "_dmemnorm", lambda a, dy, b: vjp_of(rms, 2, (1,))(a, b, dy)[0], [mem, d_memn], [wts["mem_norm"]], [],
        [wts["mem_norm"].shape], tr=128)[0]
    d_hn = mm2(tag + "_dhn", dq, wts["wq"], 1, 1)
    dh, dh_b, gr["xattn_norm"] = rms_bwd(tag + "_dnorm", h, wts["xattn_norm"], d_hn, dout)
    return dh, dh_b, gr


def final_loss(x, g, tgt):
    d = x.shape[1]

    def fn(a, t, b):
        def f(aa, bb):
            return 0.5 * jnp.sum(jnp.square(rms(aa, bb) - t)) / d
        loss, (dx, dg) = jax.value_and_grad(f, (0, 1))(a, b)
        lane = lax.broadcasted_iota(jnp.int32, (1, 128), 1)
        return dx, dx, dg, jnp.where(lane == 0, loss, 0.0)
    return rows_call("final_loss", fn, [x, tgt], [g], [(d, F32), (d, BF16)], [g.shape, (1, 128)])


SHARDED = ("ffn1_w_gate", "ffn1_w_up", "ffn1_w_down", "w_in", "conv_w", "pool_w", "w_out", "xattn_wq", "xattn_wkv",
           "xattn_wo", "ffn2_w_gate", "ffn2_w_up", "ffn2_w_down")
REPLICATED = ("ffn1_norm", "mix_norm", "a_log", "dt_bias", "dn_out_norm", "pool_scale", "xattn_norm", "mem_norm",
              "ffn2_norm", "final_norm")
WEIGHTS = ("ffn1_norm", "ffn1_w_gate", "ffn1_w_up", "ffn1_w_down", "mix_norm", "w_in", "conv_w", "a_log", "dt_bias",
           "dn_out_norm", "pool_w", "pool_scale", "w_out", "xattn_norm", "mem_norm", "xattn_wq", "xattn_wkv", "xattn_wo",
           "ffn2_norm", "ffn2_w_gate", "ffn2_w_up", "ffn2_w_down", "final_norm")


def _lane_pad(v, width=128):
    return jnp.pad(v, (0, width - v.shape[0]))[None, :]


GROUPS = {"ffn1": ("ffn1_w_gate", "ffn1_w_up"), "ffn1d": ("ffn1_w_down",), "mixer": ("w_in", "conv_w", "pool_w", "w_out"),
          "xattn": ("xattn_wq", "xattn_wkv", "xattn_wo"), "ffn2": ("ffn2_w_gate", "ffn2_w_up", "ffn2_w_down")}


def _cols(g):
    return jnp.transpose(g, (1, 0, 2)).reshape(g.shape[1], -1)


def _rows(g):
    return g.reshape(-1, g.shape[2])


def _rep_row(rep, name, l):
    return rep[name][l][None, :].astype(F32)


def mixer_weights(g, rep, l):
    dnw = DN_HEADS * HEAD_DIM
    w_in = _cols(g["w_in"])
    o_ab = 4 * dnw
    w_out = _rows(g["w_out"])
    gdim = g["pool_w"].shape[-1]
    pw = jnp.transpose(g["pool_w"].reshape(N_SHARDS, len(POOL_WINDOWS), gdim // N_SHARDS, gdim), (1, 0, 2, 3))
    return dict(
        mix_norm=_rep_row(rep, "mix_norm", l), w_qkv=w_in[:, :3 * dnw], w_z=w_in[:, 3 * dnw:o_ab],
        w_ab=jnp.pad(w_in[:, o_ab:o_ab + 2 * DN_HEADS], ((0, 0), (0, AB_PAD - 2 * DN_HEADS))),
        w_p=w_in[:, o_ab + 2 * DN_HEADS:], conv_w=_cols(g["conv_w"].reshape(N_SHARDS, CONV_WIDTH, -1)).astype(F32),
        a_log=_lane_pad(rep["a_log"][l].astype(F32)), dt_bias=_lane_pad(rep["dt_bias"][l].astype(F32)),
        dn_out_norm=_rep_row(rep, "dn_out_norm", l), pool_w=pw.reshape(len(POOL_WINDOWS), gdim, gdim),
        pool_scale=_rep_row(rep, "pool_scale", l), w_out_a=w_out[:dnw], w_out_b=w_out[dnw:])


def xattn_weights(g, rep, l):
    return dict(xattn_norm=_rep_row(rep, "xattn_norm", l), mem_norm=_rep_row(rep, "mem_norm", l), wq=_rows(g["xattn_wq"]),
                wkv=g["xattn_wkv"], wo=_rows(g["xattn_wo"]))


def _col_shards(g):
    k, n = g.shape
    return jnp.transpose(g.reshape(k, N_SHARDS, n // N_SHARDS), (1, 0, 2))


MIN_COMM_ROWS = 32


def _comm_rows(a):
    if a.shape[-2] >= MIN_COMM_ROWS:
        return a
    return a.reshape(a.shape[:-2] + (MIN_COMM_ROWS, -1))


GRAD_PARTS = (("ffn2_w_gate", "ffn2_w_up", "ffn2_w_down", "xattn_wq", "xattn_wkv", "xattn_wo"),
              ("w_in", "conv_w", "pool_w", "w_out"), ("ffn1_w_gate", "ffn1_w_up", "ffn1_w_down"))


def model_grads(x, mem, tgt, fetch, rep, grads_done=None):
    s, d = x.shape
    depth = rep["ffn1_norm"].shape[0]
    saved, wl = [], []
    h = x
    for l in range(depth):
        t = "l%d" % l
        f1 = dict(fetch(l, "ffn1", h))
        h, s1, f1["ffn1_w_down"] = ffn_fwd(t + "_ffn1", h, _rep_row(rep, "ffn1_norm", l), f1["ffn1_w_gate"], f1["ffn1_w_up"],
                                           lambda act: fetch(l, "ffn1d", act)["ffn1_w_down"])
        wm = mixer_weights(fetch(l, "mixer", h), rep, l)
        h, s2 = mixer_fwd(t + "_mix", h, wm)
        wx = xattn_weights(fetch(l, "xattn", h), rep, l)
        memn = rms_fwd(t + "_memnorm", mem, wx["mem_norm"])
        h, s3 = xattn_fwd(t + "_xattn", h, memn, wx)
        f2 = fetch(l, "ffn2", h)
        h, s4, _ = ffn_fwd(t + "_ffn2", h, _rep_row(rep, "ffn2_norm", l), f2["ffn2_w_gate"], f2["ffn2_w_up"], f2["ffn2_w_down"])
        saved.append((memn, s1, s2, s3, s4))
        wl.append((f1, wm, wx, f2))
    dh, dh_b, d_final, loss_row = final_loss(h, rep["final_norm"][None, :].astype(F32), tgt)
    big, small = [None] * depth, [None] * depth
    after = None
    done = (lambda l, part, gb: None) if grads_done is None else grads_done
    for l in reversed(range(depth)):
        f1, wm, wx, f2 = wl[l]
        t = "l%d" % l
        memn, s1, s2, s3, s4 = saved[l]
        gb, gs = {}, {}
        dh, dh_b, gs["ffn2_norm"], gb["ffn2_w_gate"], gb["ffn2_w_up"], gb["ffn2_w_down"] = ffn_bwd(
            t + "_ffn2", s4, _rep_row(rep, "ffn2_norm", l), f2["ffn2_w_gate"], f2["ffn2_w_up"], f2["ffn2_w_down"], dh, dh_b, after)
        dh, dh_b, gx = xattn_bwd(t + "_xattn", s3, memn, mem, wx, dh, dh_b)
        gb["xattn_wq"] = gx["wq"].reshape(N_SHARDS, d // N_SHARDS, d)
        gb["xattn_wo"] = gx["wo"].reshape(N_SHARDS, d // N_SHARDS, d)
        gb["xattn_wkv"] = gx["wkv"]
        after = done(l, 0, gb)
        dh, dh_b, gm = mixer_bwd(t + "_mix", s2, wm, dh, dh_b, after)
        gb["w_in"] = _col_shards(jnp.concatenate([gm["w_qkv"], gm["w_z"], gm["w_ab"][:, :2 * DN_HEADS], gm["w_p"]], axis=1))
        gb["conv_w"] = _comm_rows(_col_shards(gm["conv_w"]))
        gdim = gm["pool_w"].shape[-1]
        gb["pool_w"] = jnp.transpose(gm["pool_w"].reshape(len(POOL_WINDOWS), N_SHARDS, gdim // N_SHARDS, gdim),
                                     (1, 0, 2, 3)).reshape(N_SHARDS, gdim, gdim)
        gb["w_out"] = jnp.concatenate([gm["w_out_a"], gm["w_out_b"]], axis=0).reshape(N_SHARDS, d // N_SHARDS, d)
        after = done(l, 1, gb)
        dh, dh_b, gs["ffn1_norm"], gb["ffn1_w_gate"], gb["ffn1_w_up"], gb["ffn1_w_down"] = ffn_bwd(
            t + "_ffn1", s1, _rep_row(rep, "ffn1_norm", l), f1["ffn1_w_gate"], f1["ffn1_w_up"], f1["ffn1_w_down"], dh, dh_b, after)
        after = done(l, 2, gb)
        for n in ("xattn_norm", "mem_norm"):
            gs[n] = gx[n]
        for n in ("mix_norm", "a_log", "dt_bias", "dn_out_norm", "pool_scale"):
            gs[n] = gm[n]
        big[l], small[l] = gb, gs
    return loss_row, dh, big, small, d_final


HBM = pl.BlockSpec(memory_space=pltpu.HBM)


def _place():
    x, y, c = lax.axis_index("x"), lax.axis_index("y"), lax.axis_index("c")
    chips = [(1 - x, y), (x, 1 - y), (1 - x, 1 - y)]
    return x, y, c, 2 * x + y, chips, [2 * px + py for px, py in chips]


SEM = pl.BlockSpec(memory_space=pltpu.SEMAPHORE)
SPLIT_COPY = pltpu.CompilerParams(has_side_effects=pltpu.SideEffectType.DATAFLOW_SIDE_EFFECTING)


def _half(ref, lead, c):
    r = ref.shape[1] // 2
    return ref.at[lead, pl.ds(c * r, r)]


def place_shards(name, w, where, dtype):
    nl, r, cdim = w.shape
    tr = _rtile(r)
    out = ((N_SHARDS, r, cdim), dtype, (None, tr, cdim), lambda i, p: (p[0], i, 0), None)
    return tile_call(name, lambda a: tuple(a[l] for l in range(nl)), (r // tr,),
                     [(w, (nl, tr, cdim), lambda i, p: (0, i, 0))], [out] * nl, prefetch=where)


def _gather_copies(bufs, ssem, rsem):
    x, y, c, j_own, chips, js = _place()
    res = []
    for i, b in enumerate(bufs):
        for k, (px, py) in enumerate(chips):
            mk = lambda slot: pltpu.make_async_remote_copy(
                src_ref=_half(b, slot, c), dst_ref=_half(b, slot, c), send_sem=ssem.at[3 * i + k],
                recv_sem=rsem.at[3 * i + k], device_id=(px, py, c), device_id_type=MESH)
            res.append((mk(j_own), mk(js[k])))
    return res


def gather_start(bufs, groups):
    n, ng = len(bufs), len(groups)

    def body(*refs):
        sems, outs, token = refs[n:n + 2 * ng], refs[n + 2 * ng:2 * n + 2 * ng], refs[2 * n + 2 * ng]
        for gi, group in enumerate(groups):
            for mine, _ in _gather_copies([outs[t] for t in group], sems[2 * gi], sems[2 * gi + 1]):
                mine.start()
        token[...] = jnp.zeros_like(token)

    sem_shapes = []
    for group in groups:
        sem_shapes += [pltpu.SemaphoreType.DMA((3 * len(group),))] * 2
    res = pl.pallas_call(
        body, name="gather_start", in_specs=[HBM] * n,
        out_specs=[SEM] * (2 * ng) + [HBM] * n + [pl.BlockSpec(memory_space=pltpu.VMEM)],
        out_shape=sem_shapes + [pltpu.HBM(a.shape, a.dtype) for a in bufs] + [jax.ShapeDtypeStruct((8, 128), F32)],
        input_output_aliases={t: 2 * ng + t for t in range(n)}, compiler_params=SPLIT_COPY,
    )(*[pltpu.with_memory_space_constraint(a, pltpu.HBM) for a in bufs])
    sems = [(res[2 * gi], res[2 * gi + 1]) for gi in range(ng)]
    return sems, res[2 * ng:2 * ng + n], res[-1]


def gather_wait(name, bufs, sems, after):
    n = len(bufs)

    def body(*refs):
        ins, ssem, rsem = refs[:n], refs[n], refs[n + 1]
        for mine, theirs in _gather_copies(ins, ssem, rsem):
            mine.wait_send()
            theirs.wait_recv()

    return pl.pallas_call(
        body, name=name, in_specs=[HBM] * n + [SEM, SEM, pl.BlockSpec(memory_space=pl.ANY)], out_specs=[HBM] * n,
        out_shape=[pltpu.HBM(a.shape, a.dtype) for a in bufs],
        input_output_aliases={t: t for t in range(n)}, compiler_params=SPLIT_COPY,
    )(*bufs, sems[0], sems[1], after)


def gather_swap(name, bufs):
    n = len(bufs)

    def body(*refs):
        outs = refs[n:2 * n]
        ssem, rsem = refs[2 * n:]
        x, y, c, j_own, chips, js = _place()

        def copy(i, k, half):
            blk = _half(outs[i], js[k], half)
            return pltpu.make_async_remote_copy(src_ref=blk, dst_ref=blk, send_sem=ssem.at[3 * i + k],
                                                recv_sem=rsem.at[3 * i + k], device_id=(x, y, 1 - c), device_id_type=MESH)

        for i in range(n):
            for k in range(3):
                copy(i, k, c).start()
        for i in range(n):
            for k in range(3):
                copy(i, k, 1 - c).wait_recv()
                copy(i, k, c).wait_send()

    return pl.pallas_call(
        body, name=name, in_specs=[HBM] * n, out_specs=[HBM] * n, input_output_aliases={t: t for t in range(n)},
        out_shape=[jax.ShapeDtypeStruct(a.shape, a.dtype) for a in bufs],
        scratch_shapes=[pltpu.SemaphoreType.DMA((3 * n,)), pltpu.SemaphoreType.DMA((3 * n,))],
    )(*bufs)


def _pair_copies(grads, lands, ssem, rsem):
    x, y, c, _, _, _ = _place()
    res = []
    for t in range(len(grads)):
        r = grads[t].shape[1] // 2
        res.append(pltpu.make_async_remote_copy(src_ref=grads[t].at[:, pl.ds((1 - c) * r, r)], dst_ref=lands[t], send_sem=ssem.at[t],
                                                recv_sem=rsem.at[t], device_id=(x, y, 1 - c), device_id_type=MESH))
    return res


def _split_start(name, copies, arrays, n_sems, after=None):
    n = len(arrays)
    n_dep = 0 if after is None else 1

    def body(*refs):
        refs = refs[n + n_dep:]
        for cp in copies(refs[2:n + 2], refs[0], refs[1]):
            cp.start()
        refs[n + 2][...] = jnp.zeros((8, 128), F32)

    res = pl.pallas_call(
        body, name=name, in_specs=[HBM] * n + [pl.BlockSpec(memory_space=pl.ANY)] * n_dep,
        out_specs=[SEM, SEM] + [HBM] * n + [pl.BlockSpec(memory_space=pltpu.VMEM)],
        out_shape=[pltpu.SemaphoreType.DMA((n_sems,))] * 2 + [pltpu.HBM(a.shape, a.dtype) for a in arrays]
        + [jax.ShapeDtypeStruct((8, 128), F32)],
        input_output_aliases={t: 2 + t for t in range(n)}, compiler_params=SPLIT_COPY,
    )(*[pltpu.with_memory_space_constraint(a, pltpu.HBM) for a in arrays], *([] if after is None else [after]))
    return res[0], res[1], res[2:2 + n], res[-1]


def _split_wait(name, copies, ssem, rsem, arrays, after):
    n = len(arrays)

    def body(*refs):
        for cp in copies(refs[:n], refs[n], refs[n + 1]):
            cp.wait_send()
            cp.wait_recv()

    return pl.pallas_call(
        body, name=name, in_specs=[HBM] * n + [SEM, SEM, pl.BlockSpec(memory_space=pl.ANY)], out_specs=[HBM] * n,
        out_shape=[pltpu.HBM(a.shape, a.dtype) for a in arrays],
        input_output_aliases={t: t for t in range(n)}, compiler_params=SPLIT_COPY,
    )(*arrays, ssem, rsem, after)


def pair_start(name, grads):
    n = len(grads)
    lands = [lax.empty((a.shape[0], a.shape[1] // 2, a.shape[2]), a.dtype) for a in grads]
    ssem, rsem, arrays, token = _split_start(name, lambda a, s, r: _pair_copies(a[:n], a[n:], s, r), list(grads) + lands, n)
    return ssem, rsem, arrays[:n], arrays[n:], token


def pair_wait(name, ssem, rsem, grads, lands, after):
    n = len(grads)
    arrays = _split_wait(name, lambda a, s, r: _pair_copies(a[:n], a[n:], s, r), ssem, rsem, list(grads) + list(lands), after)
    return arrays[:n], arrays[n:]


def _reduce_copies(parts, lands, ssem, rsem):
    x, y, c, j_own, chips, js = _place()
    return [pltpu.make_async_remote_copy(src_ref=parts[t].at[js[k]], dst_ref=lands[t].at[k], send_sem=ssem.at[3 * t + k],
                                         recv_sem=rsem.at[3 * t + k], device_id=(px, py, c), device_id_type=MESH)
            for t in range(len(parts)) for k, (px, py) in enumerate(chips)]


def reduce_start(name, parts, after):
    n = len(parts)
    lands = [lax.empty((3,) + a.shape[1:], a.dtype) for a in parts]
    ssem, rsem, arrays, token = _split_start(name, lambda a, s, r: _reduce_copies(a[:n], a[n:], s, r), list(parts) + lands,
                                             3 * n, after)
    return ssem, rsem, arrays[:n], arrays[n:], token


def reduce_wait(name, ssem, rsem, parts, lands, after):
    n = len(parts)
    arrays = _split_wait(name, lambda a, s, r: _reduce_copies(a[:n], a[n:], s, r), ssem, rsem, list(parts) + list(lands), after)
    return arrays[:n], arrays[n:]


def share_halves(name, bufs):
    n = len(bufs)

    def body(*refs):
        outs = refs[n:2 * n]
        ssem, rsem = refs[2 * n:]
        x, y, c, _, _, _ = _place()

        def copy(t, l, half):
            blk = _half(outs[t], l, half)
            return pltpu.make_async_remote_copy(src_ref=blk, dst_ref=blk, send_sem=ssem.at[2 * t + l],
                                                recv_sem=rsem.at[2 * t + l], device_id=(x, y, 1 - c), device_id_type=MESH)

        for t in range(n):
            for l in range(2):
                copy(t, l, c).start()
        for t in range(n):
            for l in range(2):
                copy(t, l, 1 - c).wait_recv()
                copy(t, l, c).wait_send()

    return pl.pallas_call(
        body, name=name, in_specs=[HBM] * n, out_specs=[HBM] * n, input_output_aliases={t: t for t in range(n)},
        out_shape=[jax.ShapeDtypeStruct(a.shape, a.dtype) for a in bufs],
        scratch_shapes=[pltpu.SemaphoreType.DMA((2 * n,)), pltpu.SemaphoreType.DMA((2 * n,))],
    )(*bufs)


def allreduce_small(buf, after):
    r = buf.shape[0]
    n_dev = 8

    def body(in_ref, _, out_ref, gath, ssem, rsem):
        x, y, c = lax.axis_index("x"), lax.axis_index("y"), lax.axis_index("c")
        flip = lambda v, bit: 1 - v if bit else v
        me = 4 * x + 2 * y + c
        gath[me] = in_ref[...]
        peers = [(flip(x, k >> 2 & 1), flip(y, k >> 1 & 1), flip(c, k & 1)) for k in range(1, n_dev)]
        sends = []
        for k, peer in enumerate(peers):
            cp = pltpu.make_async_remote_copy(src_ref=in_ref, dst_ref=gath.at[me], send_sem=ssem.at[k], recv_sem=rsem.at[k],
                                              device_id=peer, device_id_type=MESH)
            cp.start()
            sends.append(cp)
        for k, (px, py, pc) in enumerate(peers):
            pltpu.make_async_remote_copy(src_ref=in_ref, dst_ref=gath.at[4 * px + 2 * py + pc], send_sem=ssem.at[k],
                                         recv_sem=rsem.at[k], device_id=(px, py, pc), device_id_type=MESH).wait_recv()
        for cp in sends:
            cp.wait_send()
        acc = gath[0]
        for i in range(1, n_dev):
            acc = acc + gath[i]
        out_ref[...] = acc

    return pl.pallas_call(
        body, name="allreduce_small",
        in_specs=[pl.BlockSpec(memory_space=pltpu.VMEM), pl.BlockSpec(memory_space=pl.ANY)],
        out_specs=pl.BlockSpec(memory_space=pltpu.VMEM), out_shape=jax.ShapeDtypeStruct(buf.shape, F32),
        scratch_shapes=[pltpu.VMEM((n_dev, r, 128), F32), pltpu.SemaphoreType.DMA((n_dev - 1,)), pltpu.SemaphoreType.DMA((n_dev - 1,))],
    )(buf, after)


def _rtile(r, pref=256):
    return _tile(r, pref, 16)


def chip_partial(name, grad, recv, where):
    _, rh, cdim = recv.shape
    tr = _rtile(rh, 512)
    nt = rh // tr
    return tile_call(name, lambda a, b: a + b, (N_SHARDS, nt),
                     [(grad, (None, tr, cdim), lambda j, i, p: (j, p[1] * nt + i, 0)), (recv, (None, tr, cdim), lambda j, i, p: (j, i, 0))],
                     [(recv.shape, BF16, (None, tr, cdim), lambda j, i, p: (j, i, 0), None)], prefetch=where)[0]


def sum_chips(name, parts, lands, where, layer, n_layers, into):
    _, rh, cdim = parts.shape
    tr = _rtile(rh, 512)
    nt = rh // tr
    up = lambda a: a.astype(F32)
    return tile_call(name, lambda own, rv: (up(own) + up(rv[0])) + (up(rv[1]) + up(rv[2])), (nt,),
                     [(parts, (None, tr, cdim), lambda i, p: (p[0], i, 0)), (lands, (3, tr, cdim), lambda i, p: (0, i, 0))],
                     [((n_layers, 2 * rh, cdim), F32, (None, tr, cdim), lambda i, p: (layer, p[1] * nt + i, 0), None)],
                     prefetch=where, into=into)[0]


def adamw_fn(w, g, m, v):
    m = ADAM_B1 * m + (1.0 - ADAM_B1) * g
    v = ADAM_B2 * v + (1.0 - ADAM_B2) * jnp.square(g)
    m_hat = m / (1.0 - ADAM_B1 ** ADAM_STEP)
    v_hat = v / (1.0 - ADAM_B2 ** ADAM_STEP)
    delta = -ADAM_LR * (m_hat / (jnp.sqrt(v_hat) + ADAM_EPS) + ADAM_WD * w)
    return delta, m, v, g


def adamw(name, w, g, m, v):
    nl, r, cdim = w.shape
    tr = _rtile(r, 256)
    spec = ((None, tr, cdim), lambda l, i: (l, i, 0))
    return tile_call(name, adamw_fn, (nl, r // tr), [(a,) + spec for a in (w, g, m, v)],
                     [(w.shape, F32) + spec + (None,)] * 4)


def _as3(a):
    return _comm_rows(a.reshape(a.shape[0], -1, a.shape[-1]))


COMM_GROUPS = ((0, ("ffn1",)), (0, ("ffn1d",)), (0, ("mixer",)), (0, ("xattn", "ffn2")), (1, ("ffn1", "ffn1d", "mixer")),
               (1, ("xattn", "ffn2")))


def _pack_rows(vals):
    rows = []
    for v in vals:
        v = v.reshape(-1).astype(F32)
        pad = (-v.shape[0]) % 128
        rows.append(jnp.pad(v, (0, pad)).reshape(-1, 128))
    out = jnp.concatenate(rows, axis=0)
    return jnp.pad(out, ((0, (-out.shape[0]) % 8), (0, 0)))


def _unpack_rows(buf, like):
    outs, r = [], 0
    for a in like:
        n = a.size
        nr = -(-n // 128)
        outs.append(buf[r:r + nr].reshape(-1)[:n].reshape(a.shape))
        r += nr
    return outs


def kernel(x, mem, ffn1_norm, ffn1_w_gate, ffn1_w_up, ffn1_w_down, mix_norm, w_in, conv_w, a_log, dt_bias, dn_out_norm, pool_w, pool_scale, w_out, xattn_norm, mem_norm, xattn_wq, xattn_wkv, xattn_wo, ffn2_norm, ffn2_w_gate, ffn2_w_up, ffn2_w_down, final_norm, loss_target, m_ffn1_norm, m_ffn1_w_gate, m_ffn1_w_up, m_ffn1_w_down, m_mix_norm, m_w_in, m_conv_w, m_a_log, m_dt_bias, m_dn_out_norm, m_pool_w, m_pool_scale, m_w_out, m_xattn_norm, m_mem_norm, m_xattn_wq, m_xattn_wkv, m_xattn_wo, m_ffn2_norm, m_ffn2_w_gate, m_ffn2_w_up, m_ffn2_w_down, m_final_norm, v_ffn1_norm, v_ffn1_w_gate, v_ffn1_w_up, v_ffn1_w_down, v_mix_norm, v_w_in, v_conv_w, v_a_log, v_dt_bias, v_dn_out_norm, v_pool_w, v_pool_scale, v_w_out, v_xattn_norm, v_mem_norm, v_xattn_wq, v_xattn_wkv, v_xattn_wo, v_ffn2_norm, v_ffn2_w_gate, v_ffn2_w_up, v_ffn2_w_down, v_final_norm):
    given = dict(locals())
    w = {n: given[n] for n in WEIGHTS}
    m = {n: given["m_" + n] for n in WEIGHTS}
    v = {n: given["v_" + n] for n in WEIGHTS}
    where = jnp.stack([2 * lax.axis_index("x") + lax.axis_index("y"), lax.axis_index("c")]).astype(jnp.int32)
    placed = {}
    for n in SHARDED:
        for l, buf in enumerate(place_shards("place_" + n, _as3(w[n]), where, F32 if n == "conv_w" else BF16)):
            placed[(l, n)] = buf
    keys, groups = [], []
    for l, entries in COMM_GROUPS:
        groups.append([])
        for e in entries:
            for n in GROUPS[e]:
                groups[-1].append(len(keys))
                keys.append((l, n))
    sems, bufs, _ = gather_start([placed[k] for k in keys], groups)
    fetched = {}

    def fetch(l, entry, after):
        gi = [i for i, (gl, entries) in enumerate(COMM_GROUPS) if gl == l and entry in entries][0]
        if gi not in fetched:
            landed = gather_wait("gather_wait%d" % gi, [bufs[i] for i in groups[gi]], sems[gi], after)
            fetched[gi] = {keys[i][1]: a for i, a in zip(groups[gi], gather_swap("gather_swap%d" % gi, landed))}
        return fetched[gi]

    swapping, travelling = [], []

    def to_ici(after):
        tag, names, (ssem, rsem, mine, theirs) = swapping.pop(0)
        mine, theirs = pair_wait("pair_wait" + tag, ssem, rsem, mine, theirs, after)
        parts = [chip_partial("partial%s_%s" % (tag, n), g, r, where) for n, g, r in zip(names, mine, theirs)]
        ssem, rsem, parts, lands, token = reduce_start("reduce_start" + tag, parts, after)
        travelling.append((tag, names, (ssem, rsem, parts, lands)))
        return token

    def grads_done(l, part, gb):
        tag, names = "%d%d" % (l, part), GRAD_PARTS[part]
        ssem, rsem, mine, theirs, token = pair_start("pair_start" + tag, [gb[n] for n in names])
        swapping.append((tag, names, (ssem, rsem, mine, theirs)))
        return to_ici(token) if len(swapping) > 1 else token

    rep = {n: w[n] for n in REPLICATED}
    loss_row, dx, big, small, d_final = model_grads(x[0], mem[0], loss_target[0], fetch, rep, grads_done)
    last_start = to_ici(dx)
    n_layers = len(big)
    sums, out_g, out_d, out_m, out_v = {}, {}, {}, {}, {}

    def land(after):
        tag, names, (ssem, rsem, parts, lands) = travelling.pop(0)
        parts, lands = reduce_wait("reduce_wait" + tag, ssem, rsem, parts, lands, after)
        for n, p, r in zip(names, parts, lands):
            sums[n] = sum_chips("sum%s_%s" % (tag, n), p, r, where, int(tag[0]), n_layers, sums.get(n))

    def finish(part):
        names = GRAD_PARTS[part]
        for n, g in zip(names, share_halves("share_halves%d" % part, [sums[n] for n in names])):
            d_, m_, v_, g_ = adamw("adamw_" + n, _as3(w[n]), g, _as3(m[n]), _as3(v[n]))
            out_g[n], out_d[n], out_m[n], out_v[n] = (a.reshape(w[n].shape) for a in (g_, d_, m_, v_))
        return out_d[names[-1]]

    n_parts = len(GRAD_PARTS)
    for _ in range(len(travelling) - 2):
        land(last_start)
    for part in range(n_parts - 3):
        finish(part)
    land(finish(n_parts - 3))
    land(finish(n_parts - 2))
    done = finish(n_parts - 1)
    rep_names = [n for n in REPLICATED if n != "final_norm"]
    g_rep = {n: jnp.stack([small[l][n][0, :w[n].shape[1]] for l in range(len(small))]) for n in rep_names}
    g_rep["final_norm"] = d_final[0]
    like = [w[n] for n in REPLICATED] + [jnp.zeros((1,), F32)]
    summed = allreduce_small(_pack_rows([g_rep[n] for n in REPLICATED] + [loss_row[0, :1]]), done)
    pk = lambda tree: _pack_rows([tree[n] for n in REPLICATED] + [jnp.zeros((1,), F32)])
    wp, mp, vp = pk(w), pk(m), pk(v)
    dp, mp2, vp2, _ = adamw("adamw_small", wp[None], summed[None], mp[None], vp[None])
    for buf, dst in ((summed, out_g), (dp[0], out_d), (mp2[0], out_m), (vp2[0], out_v)):
        for n, a in zip(REPLICATED, _unpack_rows(buf, like)):
            dst[n] = a
    loss = _unpack_rows(summed, like)[-1][0]
    return (loss, dx[None], *[out_g[n] for n in WEIGHTS], *[out_d[n] for n in WEIGHTS],
            *[out_m[n] for n in WEIGHTS], *[out_v[n] for n in WEIGHTS])
```

```python
import functools

import jax
import jax.numpy as jnp
from jax import lax
from jax.experimental import pallas as pl
from jax.experimental.pallas import tpu as pltpu

F32, BF16 = jnp.float32, jnp.bfloat16
HI = lax.Precision.HIGHEST
MESH = pl.DeviceIdType.MESH

EPS = 1e-6
DN_HEADS = 8
HEAD_DIM = 128
X_HEADS = 4
POOL_WINDOWS = (2, 4, 8, 16)
CONV_WIDTH = 4
CHUNK = 64
N_SHARDS = 4
AB_PAD = 128
ADAM_LR, ADAM_B1, ADAM_B2, ADAM_EPS, ADAM_WD, ADAM_STEP = 0.001, 0.9, 0.999, 1e-08, 0.01, 10
VMEM_LIMIT = 56 << 20
ROW_TILE = 512


def _tile(n, pref, unit=128):
    best = None
    for t in range(unit, min(n, pref) + 1, unit):
        if n % t == 0:
            best = t
    return best if best is not None else n


def _params():
    return pltpu.CompilerParams(vmem_limit_bytes=VMEM_LIMIT)


def _split(a):
    a = a.astype(F32)
    head = a.astype(BF16)
    return head, (a - head.astype(F32)).astype(BF16)


def _dg(a, b, ca, cb, hi):
    dims = (((ca,), (cb,)), ((), ()))
    dot = lambda u, v: lax.dot_general(u, v, dims, preferred_element_type=F32)
    if hi:
        (a0, a1), (b0, b1) = _split(a), _split(b)
        return dot(a0, b0) + (dot(a0, b1) + dot(a1, b0))
    return dot(a.astype(BF16), b.astype(BF16))


@functools.partial(jax.custom_vjp, nondiff_argnums=(2, 3, 4))
def mmul(a, b, ca, cb, hi):
    return _dg(a, b, ca, cb, hi)


def _mmul_fwd(a, b, ca, cb, hi):
    return _dg(a, b, ca, cb, hi), (a, b)


def _mmul_bwd(ca, cb, hi, res, g):
    a, b = res
    da = _dg(g, b, 1, 1 - cb, hi) if ca == 1 else _dg(b, g, 1 - cb, 1, hi)
    db = _dg(a, g, 1 - ca, 0, hi) if cb == 0 else _dg(g, a, 0, 1 - ca, hi)
    return da.astype(a.dtype), db.astype(b.dtype)


mmul.defvjp(_mmul_fwd, _mmul_bwd)


@functools.partial(jax.custom_vjp, nondiff_argnums=(1,))
def shift_down(x, s):
    t = lax.broadcasted_iota(jnp.int32, x.shape, 0)
    return jnp.where(t >= s, pltpu.roll(x, s, 0), 0.0)


def _shift_up(x, s):
    n = x.shape[0]
    t = lax.broadcasted_iota(jnp.int32, x.shape, 0)
    return jnp.where(t < n - s, pltpu.roll(x, n - s, 0), 0.0)


shift_down.defvjp(lambda x, s: (shift_down(x, s), None), lambda s, _, g: (_shift_up(g, s),))


def sigmoid(x):
    return 0.5 * (jnp.tanh(0.5 * x) + 1.0)


def silu(x):
    return x * sigmoid(x)


@jax.custom_vjp
def softplus(x):
    u = jnp.exp(-jnp.abs(x))
    w = 1.0 + u
    log1p = jnp.where(w == 1.0, u, jnp.log(w) * u / jnp.where(w == 1.0, 1.0, w - 1.0))
    return jnp.maximum(x, 0.0) + log1p


softplus.defvjp(lambda x: (softplus(x), x), lambda x, g: (g * sigmoid(x),))


def rms(x, g):
    x = x.astype(F32)
    return x * lax.rsqrt(jnp.mean(x * x, axis=-1, keepdims=True) + EPS) * g


def swiglu(gate, up):
    return silu(gate) * up


def vjp_of(fn, n_in, diff, has_pids=False):
    def g(*args):
        pids = None
        if has_pids:
            pids, args = args[0], args[1:]
        ins, cots = list(args[:n_in]), args[n_in:]

        def f(*d):
            full = list(ins)
            for i, v in zip(diff, d):
                full[i] = v
            return fn(pids, *full) if has_pids else fn(*full)

        out, pull = jax.vjp(f, *[ins[i].astype(F32) for i in diff])
        if isinstance(out, (tuple, list)):
            return pull(tuple(c.astype(o.dtype) for c, o in zip(cots, out)))
        return pull(cots[0].astype(out.dtype))
    return g


def tile_call(name, fn, grid, ins, outs, with_pids=False, prefetch=None, into=None):
    n_in = len(ins)
    n_into = 0 if into is None else 1

    def body(*refs):
        if prefetch is not None:
            refs = refs[1:]
        pids = tuple(pl.program_id(a) for a in range(len(grid)))
        vals = [r[...] for r in refs[:n_in]]
        res = fn(pids, *vals) if with_pids else fn(*vals)
        if not isinstance(res, (tuple, list)):
            res = (res,)
        for r, o, spec in zip(res, refs[n_in + n_into:], outs):
            acc = spec[4]
            if acc is None:
                o[...] = r.astype(o.dtype)
            else:
                first = functools.reduce(jnp.logical_and, [pids[a] == 0 for a in acc])

                @pl.when(first)
                def _():
                    o[...] = r.astype(o.dtype)

                @pl.when(jnp.logical_not(first))
                def _():
                    o[...] += r.astype(o.dtype)

    in_specs = [pl.BlockSpec(b, im) for _, b, im in ins] + [pl.BlockSpec(memory_space=pl.ANY)] * n_into
    out_specs = [pl.BlockSpec(s[2], s[3]) for s in outs]
    out_shape = [jax.ShapeDtypeStruct(s[0], s[1]) for s in outs]
    args = [a for a, _, _ in ins] + ([] if into is None else [into])
    if prefetch is None:
        return pl.pallas_call(body, name=name, grid=grid, in_specs=in_specs, out_specs=out_specs, out_shape=out_shape,
                              input_output_aliases={n_in: 0} if n_into else {}, compiler_params=_params())(*args)
    spec = pltpu.PrefetchScalarGridSpec(num_scalar_prefetch=1, grid=grid, in_specs=in_specs, out_specs=out_specs)
    return pl.pallas_call(body, name=name, grid_spec=spec, out_shape=out_shape,
                          input_output_aliases={n_in + 1: 0} if n_into else {}, compiler_params=_params())(prefetch, *args)


def mm_call(name, grid, ins, pairs, n_acc, acc_shape, outs, epilogue, extras=(), after=None):
    n_in, n_ex, nk = len(ins), len(extras), grid[2]
    n_dep = 0 if after is None else 1

    def finish(accs, ex_refs, out_refs):
        res = epilogue(accs, *[r[...] for r in ex_refs])
        if not isinstance(res, (tuple, list)):
            res = (res,)
        for r, o in zip(res, out_refs):
            o[...] = r.astype(o.dtype)

    def body(*refs):
        in_refs, ex_refs = refs[:n_in], refs[n_in:n_in + n_ex]
        refs = refs[n_in + n_ex + n_dep:]
        out_refs, accs = refs[:len(outs)], refs[len(outs):]
        if nk == 1:
            vals = [None] * n_acc
            for ia, ib, ca, cb, ai in pairs:
                d = _dg(in_refs[ia][...], in_refs[ib][...], ca, cb, False)
                vals[ai] = d if vals[ai] is None else vals[ai] + d
            finish(vals, ex_refs, out_refs)
            return
        k = pl.program_id(2)

        @pl.when(k == 0)
        def _():
            for a in accs:
                a[...] = jnp.zeros_like(a)

        for ia, ib, ca, cb, ai in pairs:
            accs[ai][...] += _dg(in_refs[ia][...], in_refs[ib][...], ca, cb, False)

        @pl.when(k == nk - 1)
        def _():
            finish([a[...] for a in accs], ex_refs, out_refs)

    return pl.pallas_call(
        body, name=name, grid=grid,
        in_specs=[pl.BlockSpec(b, im) for _, b, im in list(ins) + list(extras)] + [pl.BlockSpec(memory_space=pl.ANY)] * n_dep,
        out_specs=[pl.BlockSpec(s[2], s[3]) for s in outs],
        out_shape=[jax.ShapeDtypeStruct(s[0], s[1]) for s in outs],
        scratch_shapes=[pltpu.VMEM(acc_shape, F32) for _ in range(n_acc if nk > 1 else 0)],
        compiler_params=_params(),
    )(*[a for a, _, _ in list(ins) + list(extras)], *([] if after is None else [after]))


MM_VMEM_BUDGET = 40 << 20


def _mm_tiles(m, n, kk, sa, sb, so, has_res):
    tk = _tile(kk, 2048)
    best = None
    for tm in (1024, 512, 256, 128):
        for tn in (1024, 512, 256, 128):
            tm_, tn_ = _tile(m, tm), _tile(n, tn)
            need = 2 * (tm_ * tk * sa + tk * tn_ * sb + tm_ * tn_ * so) + (tm_ * tn_ * 4 if tk < kk else 0)
            need += 2 * tm_ * tn_ * 4 if has_res else 0
            if need <= MM_VMEM_BUDGET and (best is None or tm_ * tn_ > best[0] * best[1]):
                best = (tm_, tn_)
    return best + (tk,)


def mm2(name, a, b, ca, cb, res=None, scale=None, out_dtype=F32, after=None):
    m, kk, n = a.shape[1 - ca], a.shape[ca], b.shape[1 - cb]
    tm, tn, tk = _mm_tiles(m, n, kk, a.dtype.itemsize, b.dtype.itemsize, jnp.dtype(out_dtype).itemsize, res is not None)
    a_spec = ((tm, tk), lambda i, j, k: (i, k)) if ca == 1 else ((tk, tm), lambda i, j, k: (k, i))
    b_spec = ((tk, tn), lambda i, j, k: (k, j)) if cb == 0 else ((tn, tk), lambda i, j, k: (j, k))
    extras = [] if res is None else [(res, (tm, tn), lambda i, j, k: (i, j))]

    def epi(accs, *ex):
        r = accs[0] if scale is None else accs[0] * scale
        return r + ex[0] if ex else r

    return mm_call(name, (m // tm, n // tn, kk // tk), [(a,) + a_spec, (b,) + b_spec], [(0, 1, ca, cb, 0)], 1, (tm, tn),
                   [((m, n), out_dtype, (tm, tn), lambda i, j, k: (i, j))], epi, extras, after)[0]


def rows_call(name, fn, rows, consts, outs, acc_outs=(), tr=ROW_TILE):
    s = rows[0].shape[0]
    tr = _tile(s, tr, 8)
    ins = [(r, (tr, r.shape[1]), lambda i: (i, 0)) for r in rows]
    ins += [(c, c.shape, (lambda nd: (lambda i: (0,) * nd))(c.ndim)) for c in consts]
    o = [((s, c), dt, (tr, c), lambda i: (i, 0), None) for c, dt in outs]
    o += [(shp, F32, shp, (lambda nd: (lambda i: (0,) * nd))(len(shp)), (0,)) for shp in acc_outs]
    return tile_call(name, fn, (s // tr,), ins, o)


def _lane_pick(x, h):
    lane = lax.broadcasted_iota(jnp.int32, x.shape, x.ndim - 1)
    return jnp.sum(jnp.where(lane == h, x, 0.0), axis=-1, keepdims=True)


def gateprep_fn(ab, alog, dtb):
    t = ab.shape[0]
    gs, bs = [], []
    for h in range(DN_HEADS):
        a_h = _lane_pick(ab, h)
        b_h = _lane_pick(ab, DN_HEADS + h)
        g_h = -jnp.exp(_lane_pick(alog, h)) * softplus(a_h + _lane_pick(dtb, h))
        gs.append(jnp.broadcast_to(g_h, (t, HEAD_DIM)))
        bs.append(jnp.broadcast_to(sigmoid(b_h), (t, HEAD_DIM)))
    return jnp.concatenate(gs, axis=1), jnp.concatenate(bs, axis=1)


def conv_fn(pids, x, w):
    kind = pids[0] // DN_HEADS
    y = x * w[CONV_WIDTH - 1:CONV_WIDTH]
    for i in range(CONV_WIDTH - 1):
        y = y + shift_down(x, CONV_WIDTH - 1 - i) * w[i:i + 1]
    y = silu(y)
    n = y * lax.rsqrt(jnp.sum(y * y, axis=-1, keepdims=True) + EPS)
    n = n * jnp.where(kind == 0, HEAD_DIM ** -0.5, 1.0)
    return jnp.where(kind == 2, y, n)


def unit_lower_inverse(a):
    c = a.shape[0]
    r = lax.broadcasted_iota(jnp.int32, (c, c), 0)
    cc = lax.broadcasted_iota(jnp.int32, (c, c), 1)
    x = -a
    t = jnp.where(r == cc, 1.0, 0.0) + x
    p = 2
    while p < c:
        x = _dg(x, x, 1, 0, True)
        t = t + _dg(t, x, 1, 0, True)
        p *= 2
    return t


@jax.custom_vjp
def known_inverse(a, t):
    return t


known_inverse.defvjp(lambda a, t: (t, t),
                     lambda t, g: (-_dg(_dg(t, g, 0, 0, True), t, 1, 1, True), jnp.zeros_like(t)))


def intra_head(q, k, v, g, b, t_known):
    c = q.shape[0]
    r = lax.broadcasted_iota(jnp.int32, (c, c), 0)
    cc = lax.broadcasted_iota(jnp.int32, (c, c), 1)
    tril = (r >= cc).astype(F32)
    gc = mmul(tril, g, 1, 0, True)
    m = gc[:, :c]
    decay = jnp.exp(jnp.where(r >= cc, m - m.T, -1e30))
    kb = k * b
    a = jnp.where(r > cc, mmul(kb, k, 1, 1, False) * decay, 0.0)
    t = unit_lower_inverse(a) if t_known is None else known_inverse(a, t_known)
    e = jnp.exp(gc)
    u = mmul(t, v * b, 1, 0, True)
    w = mmul(t, kb * e, 1, 0, True)
    qk = mmul(q, k, 1, 1, False) * decay
    gl = gc[c - 1:c, :]
    kd = k * jnp.exp(gl - gc)
    outs = (u, w, qk, q * e, kd, jnp.broadcast_to(jnp.exp(gl), (8, HEAD_DIM)))
    return outs + (t,) if t_known is None else outs


def _heads(x, h):
    return x[:, h * HEAD_DIM:(h + 1) * HEAD_DIM]


def intra_fn(q, k, v, g, b):
    outs = [intra_head(_heads(q, h), _heads(k, h), _heads(v, h), _heads(g, h), _heads(b, h), None) for h in range(DN_HEADS)]
    cat = lambda i: jnp.concatenate([o[i] for o in outs], axis=1)
    stack = lambda i: jnp.stack([o[i] for o in outs], axis=0)
    return cat(0), cat(1), stack(2), cat(3), cat(4), cat(5), stack(6)


def intra_bwd_fn(q, k, v, g, b, tinv, du, dw, dqk, dqd, dkd, dgl):
    res = []
    for h in range(DN_HEADS):
        hs = lambda x: _heads(x, h)
        res.append(vjp_of(intra_head, 6, (0, 1, 2, 3, 4))(hs(q), hs(k), hs(v), hs(g), hs(b), tinv[h],
                                                         hs(du), hs(dw), dqk[h], hs(dqd), hs(dkd), hs(dgl)))
    cat = lambda i: jnp.concatenate([r[i] for r in res], axis=1)
    return jnp.concatenate([cat(0), cat(1), cat(2)], axis=1), cat(3), cat(4)


def scan_step(s, u, w, qk, qd, kd, gl):
    v_new = u - mmul(w, s, 1, 0, False)
    o = mmul(qd, s, 1, 0, False) + mmul(qk, v_new, 1, 0, False)
    return s * gl[0:1, :] + mmul(kd, v_new, 0, 0, False), o


def outgate_fn(o, z, g):
    return rms(o, g) * silu(z)


def pool_fn(pids, p):
    gid = pids[0]
    s = p.shape[0]
    t1 = (lax.broadcasted_iota(jnp.int32, p.shape, 0) + 1).astype(F32)
    acc, win, out = p, 1, None
    for gi, target in enumerate(POOL_WINDOWS):
        while win < target:
            acc = acc + shift_down(acc, win)
            win *= 2
        cand = acc / jnp.minimum(t1, float(target))
        out = cand if out is None else jnp.where(gid == gi, cand, out)
    return out - p


def poolmix_fn(pooled, pw, scale):
    return mmul(pooled, pw, 1, 0, False) * scale


def attn_fn(q, k, v):
    s = mmul(q, k, 1, 1, False) * (q.shape[1] ** -0.5)
    s = s - jnp.max(s, axis=-1, keepdims=True)
    e = jnp.exp(s)
    p = e / jnp.sum(e, axis=-1, keepdims=True)
    return mmul(p, v, 1, 0, False)


def rms_fwd(name, x, g):
    return rows_call(name, lambda a, b: rms(a, b), [x], [g], [(x.shape[1], BF16)])[0]


def rms_bwd(name, x, g, dy, dres):
    def fn(a, d, r, b):
        dx, dg = vjp_of(rms, 2, (0, 1))(a, b, d)
        return dx + r, dx + r, dg
    return rows_call(name, fn, [x, dy, dres], [g], [(x.shape[1], F32), (x.shape[1], BF16)], [g.shape])


def ffn_fwd(tag, x, g, wg, wu, wd):
    s, d = x.shape
    fj = wg.shape[-1]
    f = N_SHARDS * fj
    xn = rms_fwd(tag + "_norm", x, g)
    tm = _tile(s, 256)
    w_spec = ((None, d, fj), lambda j, i, k: (j, 0, 0))
    o_spec = ((tm, fj), lambda j, i, k: (i, j))
    gate, up, act = mm_call(
        tag + "_gu", (N_SHARDS, s // tm, 1),
        [(xn, (tm, d), lambda j, i, k: (i, 0)), (wg,) + w_spec, (wu,) + w_spec],
        [(0, 1, 1, 0, 0), (0, 2, 1, 0, 1)], 2, (tm, fj),
        [((s, f), F32) + o_spec, ((s, f), F32) + o_spec, ((s, f), BF16) + o_spec],
        lambda accs: (accs[0], accs[1], swiglu(accs[0], accs[1])))
    if callable(wd):
        wd = wd(act)
    tm, tn = _tile(s, 1024), _tile(d, 1024)
    out = mm_call(
        tag + "_down", (s // tm, d // tn, N_SHARDS),
        [(act, (tm, fj), lambda i, j, k: (i, k)), (wd, (None, fj, tn), lambda i, j, k: (k, 0, j))],
        [(0, 1, 1, 0, 0)], 1, (tm, tn),
        [((s, d), F32, (tm, tn), lambda i, j, k: (i, j))],
        lambda accs, r: r + 0.5 * accs[0], [(x, (tm, tn), lambda i, j, k: (i, j))])[0]
    return out, (x, xn, gate, up, act), wd


def ffn_bwd(tag, saved, g, wg, wu, wd, dout, dout_b, after=None):
    x, xn, gate, up, act = saved
    s, d = x.shape
    fj = wg.shape[-1]
    f = N_SHARDS * fj
    tm = _tile(s, 512)
    o_spec = ((tm, fj), lambda j, i, k: (i, j))

    def epi(accs, ga, u):
        dgate, dup = vjp_of(swiglu, 2, (0, 1))(ga, u, 0.5 * accs[0])
        return dgate, dup

    dgate, dup = mm_call(
        tag + "_dact", (N_SHARDS, s // tm, 1),
        [(dout_b, (tm, d), lambda j, i, k: (i, 0)), (wd, (None, fj, d), lambda j, i, k: (j, 0, 0))],
        [(0, 1, 1, 1, 0)], 1, (tm, fj),
        [((s, f), BF16) + o_spec, ((s, f), BF16) + o_spec], epi,
        [(gate,) + o_spec, (up,) + o_spec], after=after)
    tn = _tile(d, 1024)
    dwd = mm_call(
        tag + "_dwd", (N_SHARDS, d // tn, 1),
        [(act, (s, fj), lambda j, i, k: (0, j)), (dout_b, (s, tn), lambda j, i, k: (0, i))],
        [(0, 1, 0, 0, 0)], 1, (fj, tn),
        [((N_SHARDS, fj, d), F32, (None, fj, tn), lambda j, i, k: (j, 0, i))],
        lambda accs: 0.5 * accs[0])[0]
    td = _tile(d, 512)
    g_spec = ((s, fj), lambda j, i, k: (0, j))
    w_out = ((N_SHARDS, d, fj), F32, (None, td, fj), lambda j, i, k: (j, i, 0))
    dwg, dwu = mm_call(
        tag + "_dwgu", (N_SHARDS, d // td, 1),
        [(xn, (s, td), lambda j, i, k: (0, i)), (dgate,) + g_spec, (dup,) + g_spec],
        [(0, 1, 0, 0, 0), (0, 2, 0, 0, 1)], 2, (td, fj), [w_out, w_out], lambda accs: (accs[0], accs[1]))
    tm, tn = _tile(s, 1024), _tile(d, 1024)
    a_spec = ((tm, fj), lambda i, j, k: (i, k))
    wt_spec = ((None, tn, fj), lambda i, j, k: (k, j, 0))
    dxn = mm_call(
        tag + "_dxn", (s // tm, d // tn, N_SHARDS),
        [(dgate,) + a_spec, (wg,) + wt_spec, (dup,) + a_spec, (wu,) + wt_spec],
        [(0, 1, 1, 1, 0), (2, 3, 1, 1, 0)], 1, (tm, tn),
        [((s, d), F32, (tm, tn), lambda i, j, k: (i, j))], lambda accs: accs[0])[0]
    dx, dx_b, dg = rms_bwd(tag + "_dnorm", x, g, dxn, dout)
    return dx, dx_b, dg, dwg, dwu, dwd


def mixer_fwd(tag, h, wts):
    s, d = h.shape
    dnw = DN_HEADS * HEAD_DIM
    pw_ = d - dnw
    gdim = pw_ // len(POOL_WINDOWS)
    nc = s // CHUNK
    hn = rms_fwd(tag + "_norm", h, wts["mix_norm"])
    qkv = mm2(tag + "_qkv", hn, wts["w_qkv"], 1, 0)
    z = mm2(tag + "_z", hn, wts["w_z"], 1, 0)
    ab = mm2(tag + "_ab", hn, wts["w_ab"], 1, 0)
    p = mm2(tag + "_p", hn, wts["w_p"], 1, 0)
    qkvn = tile_call(tag + "_conv", conv_fn, (3 * DN_HEADS,),
                     [(qkv, (s, HEAD_DIM), lambda i: (0, i)), (wts["conv_w"], (CONV_WIDTH, HEAD_DIM), lambda i: (0, i))],
                     [((s, 3 * dnw), F32, (s, HEAD_DIM), lambda i: (0, i), None)], with_pids=True)[0]
    g_bc, b_bc = rows_call(tag + "_gates", gateprep_fn, [ab], [wts["a_log"], wts["dt_bias"]], [(dnw, F32), (dnw, F32)])
    cw = (CHUNK, dnw)
    sq = ((DN_HEADS, s, CHUNK), F32, (DN_HEADS, CHUNK, CHUNK), lambda n: (0, n, 0), None)
    u, w, qk, qd, kd, gl, tinv = tile_call(
        tag + "_intra", intra_fn, (nc,),
        [(qkvn, cw, lambda n: (n, 0)), (qkvn, cw, lambda n: (n, 1)), (qkvn, cw, lambda n: (n, 2)),
         (g_bc, cw, lambda n: (n, 0)), (b_bc, cw, lambda n: (n, 0))],
        [((s, dnw), F32, cw, lambda n: (n, 0), None), ((s, dnw), F32, cw, lambda n: (n, 0), None), sq,
         ((s, dnw), F32, cw, lambda n: (n, 0), None), ((s, dnw), F32, cw, lambda n: (n, 0), None),
         ((nc * 8, dnw), F32, (8, dnw), lambda n: (n, 0), None), sq])
    o, states = scan_fwd(tag + "_scan", u, w, qk, qd, kd, gl)
    th = _tile(s, 2 * ROW_TILE, 8)
    y_dn = tile_call(
        tag + "_outgate", outgate_fn, (DN_HEADS, s // th),
        [(o, (th, HEAD_DIM), lambda hh, i: (i, hh)), (z, (th, HEAD_DIM), lambda hh, i: (i, hh)),
         (wts["dn_out_norm"], (1, HEAD_DIM), lambda hh, i: (0, 0))],
        [((s, dnw), BF16, (th, HEAD_DIM), lambda hh, i: (i, hh), None)])[0]
    ng = len(POOL_WINDOWS)
    pooled = tile_call(tag + "_pool", pool_fn, (ng,), [(p, (s, gdim), lambda i: (0, i))],
                       [((s, pw_), BF16, (s, gdim), lambda i: (0, i), None)], with_pids=True)[0]
    tp = _tile(s, 512)
    y_pool = tile_call(
        tag + "_poolmix", poolmix_fn, (ng, s // tp),
        [(pooled, (tp, gdim), lambda gi, i: (i, gi)), (wts["pool_w"], (None, gdim, gdim), lambda gi, i: (gi, 0, 0)),
         (wts["pool_scale"], (1, gdim), lambda gi, i: (0, gi))],
        [((s, pw_), BF16, (tp, gdim), lambda gi, i: (i, gi), None)])[0]
    h1 = mm2(tag + "_out_a", y_dn, wts["w_out_a"], 1, 0, res=h)
    h2 = mm2(tag + "_out_b", y_pool, wts["w_out_b"], 1, 0, res=h1)
    saved = (h, hn, qkv, z, ab, p, qkvn, g_bc, b_bc, u, w, qk, qd, kd, gl, tinv, o, states, y_dn, pooled, y_pool)
    return h2, saved


def scan_fwd(name, u, w, qk, qd, kd, gl):
    s, dnw = u.shape
    nc = s // CHUNK
    cw = (CHUNK, dnw)

    def body(u_r, w_r, qk_r, qd_r, kd_r, gl_r, o_r, st_r, state):
        @pl.when(pl.program_id(0) == 0)
        def _():
            state[...] = jnp.zeros_like(state)

        st_r[...] = state[...]
        outs = []
        for h in range(DN_HEADS):
            hs = slice(h * HEAD_DIM, (h + 1) * HEAD_DIM)
            s_new, o_h = scan_step(state[hs, :], u_r[:, hs], w_r[:, hs], qk_r[h], qd_r[:, hs], kd_r[:, hs], gl_r[:, hs])
            state[hs, :] = s_new
            outs.append(o_h)
        o_r[...] = jnp.concatenate(outs, axis=1)

    row = lambda n: (n, 0)
    return pl.pallas_call(
        body, name=name, grid=(nc,),
        in_specs=[pl.BlockSpec(cw, row), pl.BlockSpec(cw, row), pl.BlockSpec((DN_HEADS, CHUNK, CHUNK), lambda n: (0, n, 0)),
                  pl.BlockSpec(cw, row), pl.BlockSpec(cw, row), pl.BlockSpec((8, dnw), row)],
        out_specs=[pl.BlockSpec(cw, row), pl.BlockSpec((None, dnw, HEAD_DIM), lambda n: (n, 0, 0))],
        out_shape=[jax.ShapeDtypeStruct((s, dnw), F32), jax.ShapeDtypeStruct((nc, dnw, HEAD_DIM), F32)],
        scratch_shapes=[pltpu.VMEM((dnw, HEAD_DIM), F32)],
        compiler_params=_params(),
    )(u, w, qk, qd, kd, gl)


def scan_bwd(name, states, u, w, qk, qd, kd, gl, do):
    s, dnw = u.shape
    nc = s // CHUNK
    cw = (CHUNK, dnw)

    def body(st_r, u_r, w_r, qk_r, qd_r, kd_r, gl_r, do_r, du_r, dw_r, dqk_r, dqd_r, dkd_r, dgl_r, dstate):
        @pl.when(pl.program_id(0) == 0)
        def _():
            dstate[...] = jnp.zeros_like(dstate)

        res = []
        for h in range(DN_HEADS):
            hs = slice(h * HEAD_DIM, (h + 1) * HEAD_DIM)
            r = vjp_of(scan_step, 7, tuple(range(7)))(
                st_r[hs, :], u_r[:, hs], w_r[:, hs], qk_r[h], qd_r[:, hs], kd_r[:, hs], gl_r[:, hs],
                dstate[hs, :], do_r[:, hs])
            dstate[hs, :] = r[0]
            res.append(r)
        cat = lambda i: jnp.concatenate([r[i] for r in res], axis=1)
        du_r[...] = cat(1)
        dw_r[...] = cat(2)
        dqk_r[...] = jnp.stack([r[3] for r in res], axis=0)
        dqd_r[...] = cat(4)
        dkd_r[...] = cat(5)
        dgl_r[...] = cat(6)

    row = lambda n: (nc - 1 - n, 0)
    qk_spec = pl.BlockSpec((DN_HEADS, CHUNK, CHUNK), lambda n: (0, nc - 1 - n, 0))
    return pl.pallas_call(
        body, name=name, grid=(nc,),
        in_specs=[pl.BlockSpec((None, dnw, HEAD_DIM), lambda n: (nc - 1 - n, 0, 0)), pl.BlockSpec(cw, row), pl.BlockSpec(cw, row),
                  qk_spec, pl.BlockSpec(cw, row), pl.BlockSpec(cw, row), pl.BlockSpec((8, dnw), row), pl.BlockSpec(cw, row)],
        out_specs=[pl.BlockSpec(cw, row), pl.BlockSpec(cw, row), qk_spec, pl.BlockSpec(cw, row), pl.BlockSpec(cw, row),
                   pl.BlockSpec((8, dnw), row)],
        out_shape=[jax.ShapeDtypeStruct((s, dnw), F32), jax.ShapeDtypeStruct((s, dnw), F32),
                   jax.ShapeDtypeStruct((DN_HEADS, s, CHUNK), F32), jax.ShapeDtypeStruct((s, dnw), F32),
                   jax.ShapeDtypeStruct((s, dnw), F32), jax.ShapeDtypeStruct((nc * 8, dnw), F32)],
        scratch_shapes=[pltpu.VMEM((dnw, HEAD_DIM), F32)],
        compiler_params=_params(),
    )(states, u, w, qk, qd, kd, gl, do)


def mixer_bwd(tag, saved, wts, dout, dout_b, after=None):
    h, hn, qkv, z, ab, p, qkvn, g_bc, b_bc, u, w, qk, qd, kd, gl, tinv, o, states, y_dn, pooled, y_pool = saved
    s, d = h.shape
    dnw = DN_HEADS * HEAD_DIM
    pw_ = d - dnw
    ng = len(POOL_WINDOWS)
    gdim = pw_ // ng
    nc = s // CHUNK
    gr = {}
    d_ydn = mm2(tag + "_dydn", dout_b, wts["w_out_a"], 1, 1, after=after)
    d_ypool = mm2(tag + "_dypool", dout_b, wts["w_out_b"], 1, 1)
    gr["w_out_a"] = mm2(tag + "_dwout_a", y_dn, dout_b, 0, 0)
    gr["w_out_b"] = mm2(tag + "_dwout_b", y_pool, dout_b, 0, 0)
    tp = _tile(s, 512)
    d_pooled, gr["pool_w"], gr["pool_scale"] = tile_call(
        tag + "_dpoolmix", vjp_of(poolmix_fn, 3, (0, 1, 2)), (ng, s // tp),
        [(pooled, (tp, gdim), lambda gi, i: (i, gi)), (wts["pool_w"], (None, gdim, gdim), lambda gi, i: (gi, 0, 0)),
         (wts["pool_scale"], (1, gdim), lambda gi, i: (0, gi)), (d_ypool, (tp, gdim), lambda gi, i: (i, gi))],
        [((s, pw_), F32, (tp, gdim), lambda gi, i: (i, gi), None),
         ((ng, gdim, gdim), F32, (None, gdim, gdim), lambda gi, i: (gi, 0, 0), (1,)),
         ((1, pw_), F32, (1, gdim), lambda gi, i: (0, gi), (1,))])
    d_p = tile_call(tag + "_dpool", vjp_of(pool_fn, 1, (0,), True), (ng,),
                    [(p, (s, gdim), lambda i: (0, i)), (d_pooled, (s, gdim), lambda i: (0, i))],
                    [((s, pw_), BF16, (s, gdim), lambda i: (0, i), None)], with_pids=True)[0]
    hb = (_tile(s, 2 * ROW_TILE, 8), HEAD_DIM)
    d_o, d_z, gr["dn_out_norm"] = tile_call(
        tag + "_doutgate", vjp_of(outgate_fn, 3, (0, 1, 2)), (DN_HEADS, s // hb[0]),
        [(o, hb, lambda hh, i: (i, hh)), (z, hb, lambda hh, i: (i, hh)), (wts["dn_out_norm"], (1, HEAD_DIM), lambda hh, i: (0, 0)),
         (d_ydn, hb, lambda hh, i: (i, hh))],
        [((s, dnw), F32, hb, lambda hh, i: (i, hh), None), ((s, dnw), BF16, hb, lambda hh, i: (i, hh), None),
         ((1, HEAD_DIM), F32, (1, HEAD_DIM), lambda hh, i: (0, 0), (0, 1))])
    du, dw, dqk, dqd, dkd, dgl = scan_bwd(tag + "_dscan", states, u, w, qk, qd, kd, gl, d_o)
    cw = (CHUNK, dnw)
    row = lambda n: (n, 0)
    d_qkvn, dg_bc, db_bc = tile_call(
        tag + "_dintra", intra_bwd_fn, (nc,),
        [(qkvn, cw, lambda n: (n, 0)), (qkvn, cw, lambda n: (n, 1)), (qkvn, cw, lambda n: (n, 2)),
         (g_bc, cw, row), (b_bc, cw, row), (tinv, (DN_HEADS, CHUNK, CHUNK), lambda n: (0, n, 0)), (du, cw, row), (dw, cw, row),
         (dqk, (DN_HEADS, CHUNK, CHUNK), lambda n: (0, n, 0)), (dqd, cw, row), (dkd, cw, row), (dgl, (8, dnw), row)],
        [((s, 3 * dnw), F32, (CHUNK, 3 * dnw), row, None)] + [((s, dnw), F32, cw, row, None)] * 2)
    d_ab, gr["a_log"], gr["dt_bias"] = rows_call(
        tag + "_dgates", lambda a, dg, db, al, dt: vjp_of(gateprep_fn, 3, (0, 1, 2))(a, al, dt, dg, db),
        [ab, dg_bc, db_bc], [wts["a_log"], wts["dt_bias"]], [(AB_PAD, BF16)], [(1, AB_PAD), (1, AB_PAD)])
    d_qkv, gr["conv_w"] = tile_call(
        tag + "_dconv", vjp_of(conv_fn, 2, (0, 1), True), (3 * DN_HEADS,),
        [(qkv, (s, HEAD_DIM), lambda i: (0, i)), (wts["conv_w"], (CONV_WIDTH, HEAD_DIM), lambda i: (0, i)),
         (d_qkvn, (s, HEAD_DIM), lambda i: (0, i))],
        [((s, 3 * dnw), BF16, (s, HEAD_DIM), lambda i: (0, i), None),
         ((CONV_WIDTH, 3 * dnw), F32, (CONV_WIDTH, HEAD_DIM), lambda i: (0, i), None)], with_pids=True)
    gr["w_qkv"] = mm2(tag + "_dwqkv", hn, d_qkv, 0, 0)
    gr["w_z"] = mm2(tag + "_dwz", hn, d_z, 0, 0)
    gr["w_ab"] = mm2(tag + "_dwab", hn, d_ab, 0, 0)
    gr["w_p"] = mm2(tag + "_dwp", hn, d_p, 0, 0)
    d_hn = mm2(tag + "_dhn1", d_qkv, wts["w_qkv"], 1, 1)
    d_hn = mm2(tag + "_dhn2", d_z, wts["w_z"], 1, 1, res=d_hn)
    d_hn = mm2(tag + "_dhn3", d_ab, wts["w_ab"], 1, 1, res=d_hn)
    d_hn = mm2(tag + "_dhn4", d_p, wts["w_p"], 1, 1, res=d_hn)
    dh, dh_b, gr["mix_norm"] = rms_bwd(tag + "_dnorm", h, wts["mix_norm"], d_hn, dout)
    return dh, dh_b, gr


def xattn_fwd(tag, h, memn, wts):
    s, d = h.shape
    m = memn.shape[0]
    dh_ = d // X_HEADS
    hn = rms_fwd(tag + "_norm", h, wts["xattn_norm"])
    q = mm2(tag + "_q", hn, wts["wq"], 1, 0, out_dtype=BF16)
    wkv = wts["wkv"]
    nj = wkv.shape[2]
    kv = mm_call(tag + "_kv", (1, N_SHARDS, 1), [(memn, (m, d), lambda i, j, k: (0, 0)), (wkv, (None, d, nj), lambda i, j, k: (j, 0, 0))],
                 [(0, 1, 1, 0, 0)], 1, (m, nj), [((m, 2 * d), BF16, (m, nj), lambda i, j, k: (0, j))], lambda accs: accs[0])[0]
    tq = _tile(s, 512)
    o = tile_call(
        tag + "_attn", attn_fn, (X_HEADS, s // tq),
        [(q, (tq, dh_), lambda hh, i: (i, hh)), (kv, (m, dh_), lambda hh, i: (0, hh)), (kv, (m, dh_), lambda hh, i: (0, X_HEADS + hh))],
        [((s, d), BF16, (tq, dh_), lambda hh, i: (i, hh), None)])[0]
    out = mm2(tag + "_o", o, wts["wo"], 1, 0, res=h)
    return out, (h, hn, q, kv, o)


def xattn_bwd(tag, saved, memn, mem, wts, dout, dout_b, after=None):
    h, hn, q, kv, o = saved
    s, d = h.shape
    m = memn.shape[0]
    dh_ = d // X_HEADS
    gr = {}
    d_o = mm2(tag + "_do", dout_b, wts["wo"], 1, 1, out_dtype=BF16, after=after)
    gr["wo"] = mm2(tag + "_dwo", o, dout_b, 0, 0)
    tq = _tile(s, 512)
    dq, dk, dv = tile_call(
        tag + "_dattn", vjp_of(attn_fn, 3, (0, 1, 2)), (X_HEADS, s // tq),
        [(q, (tq, dh_), lambda hh, i: (i, hh)), (kv, (m, dh_), lambda hh, i: (0, hh)), (kv, (m, dh_), lambda hh, i: (0, X_HEADS + hh)),
         (d_o, (tq, dh_), lambda hh, i: (i, hh))],
        [((s, d), BF16, (tq, dh_), lambda hh, i: (i, hh), None),
         ((m, d), F32, (m, dh_), lambda hh, i: (0, hh), (1,)), ((m, d), F32, (m, dh_), lambda hh, i: (0, hh), (1,))])
    dkv = jnp.concatenate([dk, dv], axis=1).astype(BF16)
    gr["wq"] = mm2(tag + "_dwq", hn, dq, 0, 0)
    wkv = wts["wkv"]
    nj = wkv.shape[2]
    td, tn = _tile(d, 1024), _tile(d, 1024)
    gr["wkv"] = mm_call(tag + "_dwkv", (d // td, N_SHARDS, 1),
                        [(memn, (m, td), lambda i, j, k: (0, i)), (dkv, (m, nj), lambda i, j, k: (0, j))], [(0, 1, 0, 0, 0)], 1, (td, nj),
                        [((N_SHARDS, d, nj), F32, (None, td, nj), lambda i, j, k: (j, i, 0))], lambda accs: accs[0])[0]
    d_memn = mm_call(tag + "_dmemn", (1, d // tn, N_SHARDS),
                     [(dkv, (m, nj), lambda i, j, k: (0, k)), (wkv, (None, tn, nj), lambda i, j, k: (k, j, 0))], [(0, 1, 1, 1, 0)], 1, (m, tn),
                     [((m, d), F32, (m, tn), lambda i, j, k: (0, j))], lambda accs: accs[0])[0]
    gr["mem_norm"] = rows_call(
        tag + "_dmemnorm", lambda a, dy, b: vjp_of(rms, 2, (1,))(a, b, dy)[0], [mem, d_memn], [wts["mem_norm"]], [],
        [wts["mem_norm"].shape], tr=128)[0]
    d_hn = mm2(tag + "_dhn", dq, wts["wq"], 1, 1)
    dh, dh_b, gr["xattn_norm"] = rms_bwd(tag + "_dnorm", h, wts["xattn_norm"], d_hn, dout)
    return dh, dh_b, gr


def final_loss(x, g, tgt):
    d = x.shape[1]

    def fn(a, t, b):
        def f(aa, bb):
            return 0.5 * jnp.sum(jnp.square(rms(aa, bb) - t)) / d
        loss, (dx, dg) = jax.value_and_grad(f, (0, 1))(a, b)
        lane = lax.broadcasted_iota(jnp.int32, (1, 128), 1)
        return dx, dx, dg, jnp.where(lane == 0, loss, 0.0)
    return rows_call("final_loss", fn, [x, tgt], [g], [(d, F32), (d, BF16)], [g.shape, (1, 128)])


SHARDED = ("ffn1_w_gate", "ffn1_w_up", "ffn1_w_down", "w_in", "conv_w", "pool_w", "w_out", "xattn_wq", "xattn_wkv",
           "xattn_wo", "ffn2_w_gate", "ffn2_w_up", "ffn2_w_down")
REPLICATED = ("ffn1_norm", "mix_norm", "a_log", "dt_bias", "dn_out_norm", "pool_scale", "xattn_norm", "mem_norm",
              "ffn2_norm", "final_norm")
WEIGHTS = ("ffn1_norm", "ffn1_w_gate", "ffn1_w_up", "ffn1_w_down", "mix_norm", "w_in", "conv_w", "a_log", "dt_bias",
           "dn_out_norm", "pool_w", "pool_scale", "w_out", "xattn_norm", "mem_norm", "xattn_wq", "xattn_wkv", "xattn_wo",
           "ffn2_norm", "ffn2_w_gate", "ffn2_w_up", "ffn2_w_down", "final_norm")


def _lane_pad(v, width=128):
    return jnp.pad(v, (0, width - v.shape[0]))[None, :]


GROUPS = {"ffn1": ("ffn1_w_gate", "ffn1_w_up"), "ffn1d": ("ffn1_w_down",), "mixer": ("w_in", "conv_w", "pool_w", "w_out"),
          "xattn": ("xattn_wq", "xattn_wkv", "xattn_wo"), "ffn2": ("ffn2_w_gate", "ffn2_w_up", "ffn2_w_down")}


def _cols(g):
    return jnp.transpose(g, (1, 0, 2)).reshape(g.shape[1], -1)


def _rows(g):
    return g.reshape(-1, g.shape[2])


def _rep_row(rep, name, l):
    return rep[name][l][None, :].astype(F32)


def mixer_weights(g, rep, l):
    dnw = DN_HEADS * HEAD_DIM
    w_in = _cols(g["w_in"])
    o_ab = 4 * dnw
    w_out = _rows(g["w_out"])
    gdim = g["pool_w"].shape[-1]
    pw = jnp.transpose(g["pool_w"].reshape(N_SHARDS, len(POOL_WINDOWS), gdim // N_SHARDS, gdim), (1, 0, 2, 3))
    return dict(
        mix_norm=_rep_row(rep, "mix_norm", l), w_qkv=w_in[:, :3 * dnw], w_z=w_in[:, 3 * dnw:o_ab],
        w_ab=jnp.pad(w_in[:, o_ab:o_ab + 2 * DN_HEADS], ((0, 0), (0, AB_PAD - 2 * DN_HEADS))),
        w_p=w_in[:, o_ab + 2 * DN_HEADS:], conv_w=_cols(g["conv_w"].reshape(N_SHARDS, CONV_WIDTH, -1)).astype(F32),
        a_log=_lane_pad(rep["a_log"][l].astype(F32)), dt_bias=_lane_pad(rep["dt_bias"][l].astype(F32)),
        dn_out_norm=_rep_row(rep, "dn_out_norm", l), pool_w=pw.reshape(len(POOL_WINDOWS), gdim, gdim),
        pool_scale=_rep_row(rep, "pool_scale", l), w_out_a=w_out[:dnw], w_out_b=w_out[dnw:])


def xattn_weights(g, rep, l):
    return dict(xattn_norm=_rep_row(rep, "xattn_norm", l), mem_norm=_rep_row(rep, "mem_norm", l), wq=_rows(g["xattn_wq"]),
                wkv=g["xattn_wkv"], wo=_rows(g["xattn_wo"]))


def _col_shards(g):
    k, n = g.shape
    return jnp.transpose(g.reshape(k, N_SHARDS, n // N_SHARDS), (1, 0, 2))


MIN_COMM_ROWS = 32


def _comm_rows(a):
    if a.shape[-2] >= MIN_COMM_ROWS:
        return a
    return a.reshape(a.shape[:-2] + (MIN_COMM_ROWS, -1))


GRAD_PARTS = (("ffn2_w_gate", "ffn2_w_up", "ffn2_w_down", "xattn_wq", "xattn_wkv", "xattn_wo"),
              ("w_in", "conv_w", "pool_w", "w_out"), ("ffn1_w_gate", "ffn1_w_up", "ffn1_w_down"))


def model_grads(x, mem, tgt, fetch, rep, grads_done=None):
    s, d = x.shape
    depth = rep["ffn1_norm"].shape[0]
    saved, wl = [], []
    h = x
    for l in range(depth):
        t = "l%d" % l
        f1 = dict(fetch(l, "ffn1", h))
        h, s1, f1["ffn1_w_down"] = ffn_fwd(t + "_ffn1", h, _rep_row(rep, "ffn1_norm", l), f1["ffn1_w_gate"], f1["ffn1_w_up"],
                                           lambda act: fetch(l, "ffn1d", act)["ffn1_w_down"])
        wm = mixer_weights(fetch(l, "mixer", h), rep, l)
        h, s2 = mixer_fwd(t + "_mix", h, wm)
        wx = xattn_weights(fetch(l, "xattn", h), rep, l)
        memn = rms_fwd(t + "_memnorm", mem, wx["mem_norm"])
        h, s3 = xattn_fwd(t + "_xattn", h, memn, wx)
        f2 = fetch(l, "ffn2", h)
        h, s4, _ = ffn_fwd(t + "_ffn2", h, _rep_row(rep, "ffn2_norm", l), f2["ffn2_w_gate"], f2["ffn2_w_up"], f2["ffn2_w_down"])
        saved.append((memn, s1, s2, s3, s4))
        wl.append((f1, wm, wx, f2))
    dh, dh_b, d_final, loss_row = final_loss(h, rep["final_norm"][None, :].astype(F32), tgt)
    big, small = [None] * depth, [None] * depth
    after = None
    done = (lambda l, part, gb: None) if grads_done is None else grads_done
    for l in reversed(range(depth)):
        f1, wm, wx, f2 = wl[l]
        t = "l%d" % l
        memn, s1, s2, s3, s4 = saved[l]
        gb, gs = {}, {}
        dh, dh_b, gs["ffn2_norm"], gb["ffn2_w_gate"], gb["ffn2_w_up"], gb["ffn2_w_down"] = ffn_bwd(
            t + "_ffn2", s4, _rep_row(rep, "ffn2_norm", l), f2["ffn2_w_gate"], f2["ffn2_w_up"], f2["ffn2_w_down"], dh, dh_b, after)
        dh, dh_b, gx = xattn_bwd(t + "_xattn", s3, memn, mem, wx, dh, dh_b)
        gb["xattn_wq"] = gx["wq"].reshape(N_SHARDS, d // N_SHARDS, d)
        gb["xattn_wo"] = gx["wo"].reshape(N_SHARDS, d // N_SHARDS, d)
        gb["xattn_wkv"] = gx["wkv"]
        after = done(l, 0, gb)
        dh, dh_b, gm = mixer_bwd(t + "_mix", s2, wm, dh, dh_b, after)
        gb["w_in"] = _col_shards(jnp.concatenate([gm["w_qkv"], gm["w_z"], gm["w_ab"][:, :2 * DN_HEADS], gm["w_p"]], axis=1))
        gb["conv_w"] = _comm_rows(_col_shards(gm["conv_w"]))
        gdim = gm["pool_w"].shape[-1]
        gb["pool_w"] = jnp.transpose(gm["pool_w"].reshape(len(POOL_WINDOWS), N_SHARDS, gdim // N_SHARDS, gdim),
                                     (1, 0, 2, 3)).reshape(N_SHARDS, gdim, gdim)
        gb["w_out"] = jnp.concatenate([gm["w_out_a"], gm["w_out_b"]], axis=0).reshape(N_SHARDS, d // N_SHARDS, d)
        after = done(l, 1, gb)
        dh, dh_b, gs["ffn1_norm"], gb["ffn1_w_gate"], gb["ffn1_w_up"], gb["ffn1_w_down"] = ffn_bwd(
            t + "_ffn1", s1, _rep_row(rep, "ffn1_norm", l), f1["ffn1_w_gate"], f1["ffn1_w_up"], f1["ffn1_w_down"], dh, dh_b, after)
        after = done(l, 2, gb)
        for n in ("xattn_norm", "mem_norm"):
            gs[n] = gx[n]
        for n in ("mix_norm", "a_log", "dt_bias", "dn_out_norm", "pool_scale"):
            gs[n] = gm[n]
        big[l], small[l] = gb, gs
    return loss_row, dh, big, small, d_final


HBM = pl.BlockSpec(memory_space=pltpu.HBM)


def _place():
    x, y, c = lax.axis_index("x"), lax.axis_index("y"), lax.axis_index("c")
    chips = [(1 - x, y), (x, 1 - y), (1 - x, 1 - y)]
    return x, y, c, 2 * x + y, chips, [2 * px + py for px, py in chips]


SEM = pl.BlockSpec(memory_space=pltpu.SEMAPHORE)
SPLIT_COPY = pltpu.CompilerParams(has_side_effects=pltpu.SideEffectType.DATAFLOW_SIDE_EFFECTING)


def _half(ref, lead, c):
    r = ref.shape[1] // 2
    return ref.at[lead, pl.ds(c * r, r)]


def place_shards(name, w, where, dtype):
    nl, r, cdim = w.shape
    tr = _rtile(r)
    out = ((N_SHARDS, r, cdim), dtype, (None, tr, cdim), lambda i, p: (p[0], i, 0), None)
    return tile_call(name, lambda a: tuple(a[l] for l in range(nl)), (r // tr,),
                     [(w, (nl, tr, cdim), lambda i, p: (0, i, 0))], [out] * nl, prefetch=where)


def _gather_copies(bufs, ssem, rsem):
    x, y, c, j_own, chips, js = _place()
    res = []
    for i, b in enumerate(bufs):
        for k, (px, py) in enumerate(chips):
            mk = lambda slot: pltpu.make_async_remote_copy(
                src_ref=_half(b, slot, c), dst_ref=_half(b, slot, c), send_sem=ssem.at[3 * i + k],
                recv_sem=rsem.at[3 * i + k], device_id=(px, py, c), device_id_type=MESH)
            res.append((mk(j_own), mk(js[k])))
    return res


def gather_start(bufs, groups):
    n, ng = len(bufs), len(groups)

    def body(*refs):
        sems, outs, token = refs[n:n + 2 * ng], refs[n + 2 * ng:2 * n + 2 * ng], refs[2 * n + 2 * ng]
        for gi, group in enumerate(groups):
            for mine, _ in _gather_copies([outs[t] for t in group], sems[2 * gi], sems[2 * gi + 1]):
                mine.start()
        token[...] = jnp.zeros_like(token)

    sem_shapes = []
    for group in groups:
        sem_shapes += [pltpu.SemaphoreType.DMA((3 * len(group),))] * 2
    res = pl.pallas_call(
        body, name="gather_start", in_specs=[HBM] * n,
        out_specs=[SEM] * (2 * ng) + [HBM] * n + [pl.BlockSpec(memory_space=pltpu.VMEM)],
        out_shape=sem_shapes + [pltpu.HBM(a.shape, a.dtype) for a in bufs] + [jax.ShapeDtypeStruct((8, 128), F32)],
        input_output_aliases={t: 2 * ng + t for t in range(n)}, compiler_params=SPLIT_COPY,
    )(*[pltpu.with_memory_space_constraint(a, pltpu.HBM) for a in bufs])
    sems = [(res[2 * gi], res[2 * gi + 1]) for gi in range(ng)]
    return sems, res[2 * ng:2 * ng + n], res[-1]


def gather_wait(name, bufs, sems, after):
    n = len(bufs)

    def body(*refs):
        ins, ssem, rsem = refs[:n], refs[n], refs[n + 1]
        for mine, theirs in _gather_copies(ins, ssem, rsem):
            mine.wait_send()
            theirs.wait_recv()

    return pl.pallas_call(
        body, name=name, in_specs=[HBM] * n + [SEM, SEM, pl.BlockSpec(memory_space=pl.ANY)], out_specs=[HBM] * n,
        out_shape=[pltpu.HBM(a.shape, a.dtype) for a in bufs],
        input_output_aliases={t: t for t in range(n)}, compiler_params=SPLIT_COPY,
    )(*bufs, sems[0], sems[1], after)


def gather_swap(name, bufs):
    n = len(bufs)

    def body(*refs):
        outs = refs[n:2 * n]
        ssem, rsem = refs[2 * n:]
        x, y, c, j_own, chips, js = _place()

        def copy(i, k, half):
            blk = _half(outs[i], js[k], half)
            return pltpu.make_async_remote_copy(src_ref=blk, dst_ref=blk, send_sem=ssem.at[3 * i + k],
                                                recv_sem=rsem.at[3 * i + k], device_id=(x, y, 1 - c), device_id_type=MESH)

        for i in range(n):
            for k in range(3):
                copy(i, k, c).start()
        for i in range(n):
            for k in range(3):
                copy(i, k, 1 - c).wait_recv()
                copy(i, k, c).wait_send()

    return pl.pallas_call(
        body, name=name, in_specs=[HBM] * n, out_specs=[HBM] * n, input_output_aliases={t: t for t in range(n)},
        out_shape=[jax.ShapeDtypeStruct(a.shape, a.dtype) for a in bufs],
        scratch_shapes=[pltpu.SemaphoreType.DMA((3 * n,)), pltpu.SemaphoreType.DMA((3 * n,))],
    )(*bufs)


def _pair_copies(grads, lands, ssem, rsem):
    x, y, c, _, _, _ = _place()
    res = []
    for t in range(len(grads)):
        r = grads[t].shape[1] // 2
        res.append(pltpu.make_async_remote_copy(src_ref=grads[t].at[:, pl.ds((1 - c) * r, r)], dst_ref=lands[t], send_sem=ssem.at[t],
                                                recv_sem=rsem.at[t], device_id=(x, y, 1 - c), device_id_type=MESH))
    return res


def _split_start(name, copies, arrays, n_sems, after=None):
    n = len(arrays)
    n_dep = 0 if after is None else 1

    def body(*refs):
        refs = refs[n + n_dep:]
        for cp in copies(refs[2:n + 2], refs[0], refs[1]):
            cp.start()
        refs[n + 2][...] = jnp.zeros((8, 128), F32)

    res = pl.pallas_call(
        body, name=name, in_specs=[HBM] * n + [pl.BlockSpec(memory_space=pl.ANY)] * n_dep,
        out_specs=[SEM, SEM] + [HBM] * n + [pl.BlockSpec(memory_space=pltpu.VMEM)],
        out_shape=[pltpu.SemaphoreType.DMA((n_sems,))] * 2 + [pltpu.HBM(a.shape, a.dtype) for a in arrays]
        + [jax.ShapeDtypeStruct((8, 128), F32)],
        input_output_aliases={t: 2 + t for t in range(n)}, compiler_params=SPLIT_COPY,
    )(*[pltpu.with_memory_space_constraint(a, pltpu.HBM) for a in arrays], *([] if after is None else [after]))
    return res[0], res[1], res[2:2 + n], res[-1]


def _split_wait(name, copies, ssem, rsem, arrays, after):
    n = len(arrays)

    def body(*refs):
        for cp in copies(refs[:n], refs[n], refs[n + 1]):
            cp.wait_send()
            cp.wait_recv()

    return pl.pallas_call(
        body, name=name, in_specs=[HBM] * n + [SEM, SEM, pl.BlockSpec(memory_space=pl.ANY)], out_specs=[HBM] * n,
        out_shape=[pltpu.HBM(a.shape, a.dtype) for a in arrays],
        input_output_aliases={t: t for t in range(n)}, compiler_params=SPLIT_COPY,
    )(*arrays, ssem, rsem, after)


def pair_start(name, grads):
    n = len(grads)
    lands = [lax.empty((a.shape[0], a.shape[1] // 2, a.shape[2]), a.dtype) for a in grads]
    ssem, rsem, arrays, token = _split_start(name, lambda a, s, r: _pair_copies(a[:n], a[n:], s, r), list(grads) + lands, n)
    return ssem, rsem, arrays[:n], arrays[n:], token


def pair_wait(name, ssem, rsem, grads, lands, after):
    n = len(grads)
    arrays = _split_wait(name, lambda a, s, r: _pair_copies(a[:n], a[n:], s, r), ssem, rsem, list(grads) + list(lands), after)
    return arrays[:n], arrays[n:]


def _reduce_copies(parts, lands, ssem, rsem):
    x, y, c, j_own, chips, js = _place()
    return [pltpu.make_async_remote_copy(src_ref=parts[t].at[js[k]], dst_ref=lands[t].at[k], send_sem=ssem.at[3 * t + k],
                                         recv_sem=rsem.at[3 * t + k], device_id=(px, py, c), device_id_type=MESH)
            for t in range(len(parts)) for k, (px, py) in enumerate(chips)]


def reduce_start(name, parts, after):
    n = len(parts)
    lands = [lax.empty((3,) + a.shape[1:], a.dtype) for a in parts]
    ssem, rsem, arrays, token = _split_start(name, lambda a, s, r: _reduce_copies(a[:n], a[n:], s, r), list(parts) + lands,
                                             3 * n, after)
    return ssem, rsem, arrays[:n], arrays[n:], token


def reduce_wait(name, ssem, rsem, parts, lands, after):
    n = len(parts)
    arrays = _split_wait(name, lambda a, s, r: _reduce_copies(a[:n], a[n:], s, r), ssem, rsem, list(parts) + list(lands), after)
    return arrays[:n], arrays[n:]


def share_halves(name, bufs):
    n = len(bufs)

    def body(*refs):
        outs = refs[n:2 * n]
        ssem, rsem = refs[2 * n:]
        x, y, c, _, _, _ = _place()

        def copy(t, l, half):
            blk = _half(outs[t], l, half)
            return pltpu.make_async_remote_copy(src_ref=blk, dst_ref=blk, send_sem=ssem.at[2 * t + l],
                                                recv_sem=rsem.at[2 * t + l], device_id=(x, y, 1 - c), device_id_type=MESH)

        for t in range(n):
            for l in range(2):
                copy(t, l, c).start()
        for t in range(n):
            for l in range(2):
                copy(t, l, 1 - c).wait_recv()
                copy(t, l, c).wait_send()

    return pl.pallas_call(
        body, name=name, in_specs=[HBM] * n, out_specs=[HBM] * n, input_output_aliases={t: t for t in range(n)},
        out_shape=[jax.ShapeDtypeStruct(a.shape, a.dtype) for a in bufs],
        scratch_shapes=[pltpu.SemaphoreType.DMA((2 * n,)), pltpu.SemaphoreType.DMA((2 * n,))],
    )(*bufs)


def allreduce_small(buf, after):
    r = buf.shape[0]
    n_dev = 8

    def body(in_ref, _, out_ref, gath, ssem, rsem):
        x, y, c = lax.axis_index("x"), lax.axis_index("y"), lax.axis_index("c")
        flip = lambda v, bit: 1 - v if bit else v
        me = 4 * x + 2 * y + c
        gath[me] = in_ref[...]
        peers = [(flip(x, k >> 2 & 1), flip(y, k >> 1 & 1), flip(c, k & 1)) for k in range(1, n_dev)]
        sends = []
        for k, peer in enumerate(peers):
            cp = pltpu.make_async_remote_copy(src_ref=in_ref, dst_ref=gath.at[me], send_sem=ssem.at[k], recv_sem=rsem.at[k],
                                              device_id=peer, device_id_type=MESH)
            cp.start()
            sends.append(cp)
        for k, (px, py, pc) in enumerate(peers):
            pltpu.make_async_remote_copy(src_ref=in_ref, dst_ref=gath.at[4 * px + 2 * py + pc], send_sem=ssem.at[k],
                                         recv_sem=rsem.at[k], device_id=(px, py, pc), device_id_type=MESH).wait_recv()
        for cp in sends:
            cp.wait_send()
        acc = gath[0]
        for i in range(1, n_dev):
            acc = acc + gath[i]
        out_ref[...] = acc

    return pl.pallas_call(
        body, name="allreduce_small",
        in_specs=[pl.BlockSpec(memory_space=pltpu.VMEM), pl.BlockSpec(memory_space=pl.ANY)],
        out_specs=pl.BlockSpec(memory_space=pltpu.VMEM), out_shape=jax.ShapeDtypeStruct(buf.shape, F32),
        scratch_shapes=[pltpu.VMEM((n_dev, r, 128), F32), pltpu.SemaphoreType.DMA((n_dev - 1,)), pltpu.SemaphoreType.DMA((n_dev - 1,))],
    )(buf, after)


def _rtile(r, pref=256):
    return _tile(r, pref, 16)


def chip_partial(name, grad, recv, where):
    _, rh, cdim = recv.shape
    tr = _rtile(rh, 512)
    nt = rh // tr
    return tile_call(name, lambda a, b: a + b, (N_SHARDS, nt),
                     [(grad, (None, tr, cdim), lambda j, i, p: (j, p[1] * nt + i, 0)), (recv, (None, tr, cdim), lambda j, i, p: (j, i, 0))],
                     [(recv.shape, BF16, (None, tr, cdim), lambda j, i, p: (j, i, 0), None)], prefetch=where)[0]


def sum_chips(name, parts, lands, where, layer, n_layers, into):
    _, rh, cdim = parts.shape
    tr = _rtile(rh, 512)
    nt = rh // tr
    up = lambda a: a.astype(F32)
    return tile_call(name, lambda own, rv: (up(own) + up(rv[0])) + (up(rv[1]) + up(rv[2])), (nt,),
                     [(parts, (None, tr, cdim), lambda i, p: (p[0], i, 0)), (lands, (3, tr, cdim), lambda i, p: (0, i, 0))],
                     [((n_layers, 2 * rh, cdim), F32, (None, tr, cdim), lambda i, p: (layer, p[1] * nt + i, 0), None)],
                     prefetch=where, into=into)[0]


def adamw_fn(w, g, m, v):
    m = ADAM_B1 * m + (1.0 - ADAM_B1) * g
    v = ADAM_B2 * v + (1.0 - ADAM_B2) * jnp.square(g)
    m_hat = m / (1.0 - ADAM_B1 ** ADAM_STEP)
    v_hat = v / (1.0 - ADAM_B2 ** ADAM_STEP)
    delta = -ADAM_LR * (m_hat / (jnp.sqrt(v_hat) + ADAM_EPS) + ADAM_WD * w)
    return delta, m, v, g


def adamw(name, w, g, m, v):
    nl, r, cdim = w.shape
    tr = _rtile(r, 256)
    spec = ((None, tr, cdim), lambda l, i: (l, i, 0))
    return tile_call(name, adamw_fn, (nl, r // tr), [(a,) + spec for a in (w, g, m, v)],
                     [(w.shape, F32) + spec + (None,)] * 4)


def _as3(a):
    return _comm_rows(a.reshape(a.shape[0], -1, a.shape[-1]))


COMM_GROUPS = ((0, ("ffn1",)), (0, ("ffn1d",)), (0, ("mixer",)), (0, ("xattn", "ffn2")), (1, ("ffn1", "ffn1d", "mixer")),
               (1, ("xattn", "ffn2")))


def _pack_rows(vals):
    rows = []
    for v in vals:
        v = v.reshape(-1).astype(F32)
        pad = (-v.shape[0]) % 128
        rows.append(jnp.pad(v, (0, pad)).reshape(-1, 128))
    out = jnp.concatenate(rows, axis=0)
    return jnp.pad(out, ((0, (-out.shape[0]) % 8), (0, 0)))


def _unpack_rows(buf, like):
    outs, r = [], 0
    for a in like:
        n = a.size
        nr = -(-n // 128)
        outs.append(buf[r:r + nr].reshape(-1)[:n].reshape(a.shape))
        r += nr
    return outs


def kernel(x, mem, ffn1_norm, ffn1_w_gate, ffn1_w_up, ffn1_w_down, mix_norm, w_in, conv_w, a_log, dt_bias, dn_out_norm, pool_w, pool_scale, w_out, xattn_norm, mem_norm, xattn_wq, xattn_wkv, xattn_wo, ffn2_norm, ffn2_w_gate, ffn2_w_up, ffn2_w_down, final_norm, loss_target, m_ffn1_norm, m_ffn1_w_gate, m_ffn1_w_up, m_ffn1_w_down, m_mix_norm, m_w_in, m_conv_w, m_a_log, m_dt_bias, m_dn_out_norm, m_pool_w, m_pool_scale, m_w_out, m_xattn_norm, m_mem_norm, m_xattn_wq, m_xattn_wkv, m_xattn_wo, m_ffn2_norm, m_ffn2_w_gate, m_ffn2_w_up, m_ffn2_w_down, m_final_norm, v_ffn1_norm, v_ffn1_w_gate, v_ffn1_w_up, v_ffn1_w_down, v_mix_norm, v_w_in, v_conv_w, v_a_log, v_dt_bias, v_dn_out_norm, v_pool_w, v_pool_scale, v_w_out, v_xattn_norm, v_mem_norm, v_xattn_wq, v_xattn_wkv, v_xattn_wo, v_ffn2_norm, v_ffn2_w_gate, v_ffn2_w_up, v_ffn2_w_down, v_final_norm):
    given = dict(locals())
    w = {n: given[n] for n in WEIGHTS}
    m = {n: given["m_" + n] for n in WEIGHTS}
    v = {n: given["v_" + n] for n in WEIGHTS}
    where = jnp.stack([2 * lax.axis_index("x") + lax.axis_index("y"), lax.axis_index("c")]).astype(jnp.int32)
    placed = {}
    for n in SHARDED:
        for l, buf in enumerate(place_shards("place_" + n, _as3(w[n]), where, F32 if n == "conv_w" else BF16)):
            placed[(l, n)] = buf
    keys, groups = [], []
    for l, entries in COMM_GROUPS:
        groups.append([])
        for e in entries:
            for n in GROUPS[e]:
                groups[-1].append(len(keys))
                keys.append((l, n))
    sems, bufs, _ = gather_start([placed[k] for k in keys], groups)
    fetched = {}

    def fetch(l, entry, after):
        gi = [i for i, (gl, entries) in enumerate(COMM_GROUPS) if gl == l and entry in entries][0]
        if gi not in fetched:
            landed = gather_wait("gather_wait%d" % gi, [bufs[i] for i in groups[gi]], sems[gi], after)
            fetched[gi] = {keys[i][1]: a for i, a in zip(groups[gi], gather_swap("gather_swap%d" % gi, landed))}
        return fetched[gi]

    swapping, travelling = [], []

    def to_ici(after):
        tag, names, (ssem, rsem, mine, theirs) = swapping.pop(0)
        mine, theirs = pair_wait("pair_wait" + tag, ssem, rsem, mine, theirs, after)
        parts = [chip_partial("partial%s_%s" % (tag, n), g, r, where) for n, g, r in zip(names, mine, theirs)]
        ssem, rsem, parts, lands, token = reduce_start("reduce_start" + tag, parts, after)
        travelling.append((tag, names, (ssem, rsem, parts, lands)))
        return token

    def grads_done(l, part, gb):
        tag, names = "%d%d" % (l, part), GRAD_PARTS[part]
        ssem, rsem, mine, theirs, token = pair_start("pair_start" + tag, [gb[n] for n in names])
        swapping.append((tag, names, (ssem, rsem, mine, theirs)))
        return to_ici(token) if len(swapping) > 1 else token

    rep = {n: w[n] for n in REPLICATED}
    loss_row, dx, big, small, d_final = model_grads(x[0], mem[0], loss_target[0], fetch, rep, grads_done)
    last_start = to_ici(dx)
    n_layers = len(big)
    sums, out_g, out_d, out_m, out_v = {}, {}, {}, {}, {}

    def land(after):
        tag, names, (ssem, rsem, parts, lands) = travelling.pop(0)
        parts, lands = reduce_wait("reduce_wait" + tag, ssem, rsem, parts, lands, after)
        for n, p, r in zip(names, parts, lands):
            sums[n] = sum_chips("sum%s_%s" % (tag, n), p, r, where, int(tag[0]), n_layers, sums.get(n))

    def finish(part):
        names = GRAD_PARTS[part]
        for n, g in zip(names, share_halves("share_halves%d" % part, [sums[n] for n in names])):
            d_, m_, v_, g_ = adamw("adamw_" + n, _as3(w[n]), g, _as3(m[n]), _as3(v[n]))
            out_g[n], out_d[n], out_m[n], out_v[n] = (a.reshape(w[n].shape) for a in (g_, d_, m_, v_))
        return out_d[names[-1]]

    n_parts = len(GRAD_PARTS)
    early = len(travelling) - 3
    for i in range(len(travelling) - 2):
        land(dx if i < early else last_start)
    for part in range(n_parts - 3):
        finish(part)
    land(finish(n_parts - 3))
    land(finish(n_parts - 2))
    done = finish(n_parts - 1)
    rep_names = [n for n in REPLICATED if n != "final_norm"]
    g_rep = {n: jnp.stack([small[l][n][0, :w[n].shape[1]] for l in range(len(small))]) for n in rep_names}
    g_rep["final_norm"] = d_final[0]
    like = [w[n] for n in REPLICATED] + [jnp.zeros((1,), F32)]
    summed = allreduce_small(_pack_rows([g_rep[n] for n in REPLICATED] + [loss_row[0, :1]]), done)
    pk = lambda tree: _pack_rows([tree[n] for n in REPLICATED] + [jnp.zeros((1,), F32)])
    wp, mp, vp = pk(w), pk(m), pk(v)
    dp, mp2, vp2, _ = adamw("adamw_small", wp[None], summed[None], mp[None], vp[None])
    for buf, dst in ((summed, out_g), (dp[0], out_d), (mp2[0], out_m), (vp2[0], out_v)):
        for n, a in zip(REPLICATED, _unpack_rows(buf, like)):
            dst[n] = a
    loss = _unpack_rows(summed, like)[-1][0]
    return (loss, dx[None], *[out_g[n] for n in WEIGHTS], *[out_d[n] for n in WEIGHTS],
            *[out_m[n] for n in WEIGHTS], *[out_v[n] for n in WEIGHTS])
```
